```python
import math
import jax, jax.numpy as jnp
from jax import lax
import numpy as np

D_MODEL = 2048
BATCH = 2
SEQ = 4096
DEPTH = 2
DEC_BATCH = 16
DEC_SEQ = 16
PAST_LEN = 2048

CHUNK = 64
N_MIXERS = 2
N_CONV_LAYERS = (DEPTH + 1) // 2
N_ATTN_LAYERS = DEPTH // 2
CONV_WIDTH = 31
CONV_STATE = CONV_WIDTH - 1
HEAD_DIM = 64
N_HEADS = D_MODEL // HEAD_DIM
N_KV_HEADS = N_HEADS // 8
GROUP = N_HEADS // N_KV_HEADS
QKV_DIM = (N_HEADS + 2 * N_KV_HEADS) * HEAD_DIM
WINDOW = 128
WIN_CHUNKS = WINDOW // CHUNK
BAND = WINDOW + CHUNK
N_BUCKETS = 32
MAX_DISTANCE = 128
D_FF = -(-8 * D_MODEL // (3 * 256)) * 256
EPS = 1e-6

kernel_name = "streaming_conformer_swa_sink_hybrid_step"


def rms_norm(x, g):
    xf = x.astype(jnp.float32)
    y = xf * lax.rsqrt(jnp.mean(xf * xf, axis=-1, keepdims=True) + EPS)
    return (y * g.astype(jnp.float32)).astype(x.dtype)


def layer_norm(x, g, b):
    xf = x.astype(jnp.float32)
    mu = jnp.mean(xf, axis=-1, keepdims=True)
    xc = xf - mu
    y = xc * lax.rsqrt(jnp.mean(xc * xc, axis=-1, keepdims=True) + EPS)
    return (y * g.astype(jnp.float32) + b.astype(jnp.float32)).astype(x.dtype)


def modulate(h, shift, scale):
    return h * (1 + scale[:, None, :]) + shift[:, None, :]


def t5_bucket(rel):
    nb = N_BUCKETS // 2
    max_exact = nb // 2
    ret = jnp.where(rel > 0, nb, 0)
    n = jnp.abs(rel)
    nf = jnp.maximum(n, 1).astype(jnp.float32)
    large = max_exact + (jnp.log(nf / max_exact) / math.log(MAX_DISTANCE / max_exact)
                         * (nb - max_exact)).astype(jnp.int32)
    large = jnp.minimum(large, nb - 1)
    return ret + jnp.where(n < max_exact, n, large)


def rel_bias(table, q_pos, k_pos):
    bucket = t5_bucket(k_pos[None, :] - q_pos[:, None])
    b = jnp.take(table.astype(jnp.float32), bucket, axis=1)
    return b.reshape(N_KV_HEADS, GROUP, q_pos.shape[0], k_pos.shape[0])


def sink_attention(q, k, v, bias, valid, sinks):
    s = jnp.einsum('bnqhgd,bnkhd->bnhgqk', q, k,
                   preferred_element_type=jnp.float32) * (HEAD_DIM ** -0.5)
    s = s + bias
    if valid is not None:
        s = jnp.where(valid, s, -jnp.inf)
    sink = jnp.broadcast_to(sinks.astype(jnp.float32).reshape(N_KV_HEADS, GROUP, 1, 1),
                            s.shape[:-1] + (1,))
    p = jax.nn.softmax(jnp.concatenate([s, sink], axis=-1), axis=-1)[..., :-1]
    return jnp.einsum('bnhgqk,bnkhd->bnqhgd', p.astype(v.dtype), v)


def qkv_proj(h, w_qkv, b_qkv):
    B, T, _ = h.shape
    qkv = h @ w_qkv + b_qkv
    qd = N_HEADS * HEAD_DIM
    kd = N_KV_HEADS * HEAD_DIM
    q = qkv[..., :qd].reshape(B, T, N_KV_HEADS, GROUP, HEAD_DIM)
    k = qkv[..., qd:qd + kd].reshape(B, T, N_KV_HEADS, HEAD_DIM)
    v = qkv[..., qd + kd:].reshape(B, T, N_KV_HEADS, HEAD_DIM)
    return q, k, v


def attn_prompt(h, w_qkv, b_qkv, w_o, b_o, sinks, table):
    B, S, _ = h.shape
    NC = S // CHUNK
    q, k, v = qkv_proj(h, w_qkv, b_qkv)
    qc = q.reshape(B, NC, CHUNK, N_KV_HEADS, GROUP, HEAD_DIM)
    pad = ((0, 0), (WIN_CHUNKS, 0), (0, 0), (0, 0), (0, 0))
    kp = jnp.pad(k.reshape(B, NC, CHUNK, N_KV_HEADS, HEAD_DIM), pad)
    vp = jnp.pad(v.reshape(B, NC, CHUNK, N_KV_HEADS, HEAD_DIM), pad)
    kb = jnp.concatenate([kp[:, j:j + NC] for j in range(WIN_CHUNKS + 1)], axis=2)
    vb = jnp.concatenate([vp[:, j:j + NC] for j in range(WIN_CHUNKS + 1)], axis=2)
    q_pos = jnp.arange(CHUNK)
    k_pos = jnp.arange(BAND) - WINDOW
    bias = rel_bias(table, q_pos, k_pos)
    key_abs = jnp.arange(NC)[:, None] * CHUNK + k_pos[None, :]
    valid = (key_abs >= 0).reshape(1, NC, 1, 1, 1, BAND)
    o = sink_attention(qc, kb, vb, bias, valid, sinks)
    y = o.reshape(B, S, N_HEADS * HEAD_DIM) @ w_o + b_o
    return y, k[:, S - WINDOW:], v[:, S - WINDOW:]


def attn_sample(h, cache_k, cache_v, w_qkv, b_qkv, w_o, b_o, sinks, table):
    B, T, _ = h.shape
    q, k, v = qkv_proj(h, w_qkv, b_qkv)
    k_all = jnp.concatenate([cache_k.astype(k.dtype), k], axis=1)
    v_all = jnp.concatenate([cache_v.astype(v.dtype), v], axis=1)
    q_pos = jnp.arange(T)
    k_pos = jnp.arange(WINDOW + T) - WINDOW
    bias = rel_bias(table, q_pos, k_pos)
    o = sink_attention(q[:, None], k_all[:, None], v_all[:, None], bias, None, sinks)
    y = o.reshape(B, T, N_HEADS * HEAD_DIM) @ w_o + b_o
    return y, k_all[:, T:], v_all[:, T:]


def conv_module(h, state, w_pw1, b_pw1, w_dw, b_dw, ln_g, ln_b, w_pw2, b_pw2):
    T = h.shape[1]
    a, gt = jnp.split(h @ w_pw1 + b_pw1, 2, axis=-1)
    u = a * jax.nn.sigmoid(gt)
    up = jnp.concatenate([state.astype(u.dtype), u], axis=1)
    z = lax.conv_general_dilated(up, w_dw[:, None, :].astype(up.dtype), window_strides=(1,),
                                 padding='VALID', dimension_numbers=('NWC', 'WIO', 'NWC'),
                                 feature_group_count=D_MODEL) + b_dw
    z = jax.nn.silu(layer_norm(z, ln_g, ln_b))
    return z @ w_pw2 + b_pw2, up[:, T:]


def swiglu(h, w_gu, w_down):
    g, u = jnp.split(h @ w_gu, 2, axis=-1)
    return (jax.nn.silu(g) * u) @ w_down


def run_trunk(x, c, conv_states, win_k, win_v, p):
    new_conv, new_k, new_v = [], [], []
    for i in range(DEPTH):
        mod = jax.nn.silu(c) @ p['w_mod'][i] + p['b_mod'][i]
        sh1, sc1, g1, sh2, sc2, g2 = jnp.split(mod, 6, axis=-1)
        h = modulate(rms_norm(x, p['norm_mix'][i]), sh1, sc1)
        j = i // N_MIXERS
        if i % N_MIXERS == 0:
            st = (jnp.zeros((x.shape[0], CONV_STATE, D_MODEL), x.dtype)
                  if conv_states is None else conv_states[j])
            y, s_new = conv_module(h, st, p['w_pw1'][j], p['b_pw1'][j], p['w_dw'][j], p['b_dw'][j],
                                   p['conv_ln_g'][j], p['conv_ln_b'][j], p['w_pw2'][j], p['b_pw2'][j])
            new_conv.append(s_new)
        else:
            if win_k is None:
                y, k_new, v_new = attn_prompt(h, p['w_qkv'][j], p['b_qkv'][j], p['w_o'][j],
                                              p['b_o'][j], p['attn_sinks'][j], p['rel_bias_table'])
            else:
                y, k_new, v_new = attn_sample(h, win_k[j], win_v[j], p['w_qkv'][j], p['b_qkv'][j],
                                              p['w_o'][j], p['b_o'][j], p['attn_sinks'][j],
                                              p['rel_bias_table'])
            new_k.append(k_new)
            new_v.append(v_new)
        x = x + g1[:, None, :] * y
        h = modulate(rms_norm(x, p['norm_ffn'][i]), sh2, sc2)
        x = x + g2[:, None, :] * swiglu(h, p['w_gu'][i], p['w_down'][i])
    return rms_norm(x, p['norm_out']), jnp.stack(new_conv), jnp.stack(new_k), jnp.stack(new_v)


def setup_inputs(seed: int = 0) -> dict:
    key = jax.random.key(seed)
    ks = jax.random.split(key, 32)

    def nrm(k, shape, scale):
        return jax.random.normal(k, shape, jnp.float32) * scale

    D = D_MODEL
    return {
        "x_prompt": nrm(ks[0], (BATCH, SEQ, D), 1.0),
        "x_sample": nrm(ks[1], (DEC_BATCH, DEC_SEQ, D), 1.0),
        "c_prompt": nrm(ks[2], (BATCH, D), 1.0),
        "c_sample": nrm(ks[3], (DEC_BATCH, D), 1.0),
        "state_conv": nrm(ks[4], (N_CONV_LAYERS, DEC_BATCH, CONV_STATE, D), 0.5),
        "cache_win_k": nrm(ks[5], (N_ATTN_LAYERS, DEC_BATCH, WINDOW, N_KV_HEADS, HEAD_DIM), 1.0),
        "cache_win_v": nrm(ks[6], (N_ATTN_LAYERS, DEC_BATCH, WINDOW, N_KV_HEADS, HEAD_DIM), 1.0),
        "w_mod": nrm(ks[7], (DEPTH, D, 6 * D), 0.5 * D ** -0.5),
        "b_mod": nrm(ks[8], (DEPTH, 6 * D), 0.02),
        "norm_mix": 1.0 + nrm(ks[9], (DEPTH, D), 0.02),
        "norm_ffn": 1.0 + nrm(ks[10], (DEPTH, D), 0.02),
        "w_pw1": nrm(ks[11], (N_CONV_LAYERS, D, 2 * D), D ** -0.5),
        "b_pw1": nrm(ks[12], (N_CONV_LAYERS, 2 * D), 0.02),
        "w_dw": nrm(ks[13], (N_CONV_LAYERS, CONV_WIDTH, D), CONV_WIDTH ** -0.5),
        "b_dw": nrm(ks[14], (N_CONV_LAYERS, D), 0.02),
        "conv_ln_g": 1.0 + nrm(ks[15], (N_CONV_LAYERS, D), 0.02),
        "conv_ln_b": nrm(ks[16], (N_CONV_LAYERS, D), 0.02),
        "w_pw2": nrm(ks[17], (N_CONV_LAYERS, D, D), D ** -0.5),
        "b_pw2": nrm(ks[18], (N_CONV_LAYERS, D), 0.02),
        "w_qkv": nrm(ks[19], (N_ATTN_LAYERS, D, QKV_DIM), D ** -0.5),
        "b_qkv": nrm(ks[20], (N_ATTN_LAYERS, QKV_DIM), 0.02),
        "w_o": nrm(ks[21], (N_ATTN_LAYERS, N_HEADS * HEAD_DIM, D), (N_HEADS * HEAD_DIM) ** -0.5),
        "b_o": nrm(ks[22], (N_ATTN_LAYERS, D), 0.02),
        "attn_sinks": nrm(ks[23], (N_ATTN_LAYERS, N_HEADS), 1.0),
        "rel_bias_table": nrm(ks[24], (N_HEADS, N_BUCKETS), 0.5),
        "w_gu": nrm(ks[25], (DEPTH, D, 2 * D_FF), D ** -0.5),
        "w_down": nrm(ks[26], (DEPTH, D_FF, D), D_FF ** -0.5),
        "norm_out": 1.0 + nrm(ks[27], (D,), 0.02),
    }


def reference(x_prompt, x_sample, c_prompt, c_sample, state_conv, cache_win_k, cache_win_v,
              w_mod, b_mod, norm_mix, norm_ffn, w_pw1, b_pw1, w_dw, b_dw, conv_ln_g, conv_ln_b,
              w_pw2, b_pw2, w_qkv, b_qkv, w_o, b_o, attn_sinks, rel_bias_table, w_gu, w_down,
              norm_out):
    p = {
        'w_mod': w_mod, 'b_mod': b_mod, 'norm_mix': norm_mix, 'norm_ffn': norm_ffn,
        'w_pw1': w_pw1, 'b_pw1': b_pw1, 'w_dw': w_dw, 'b_dw': b_dw,
        'conv_ln_g': conv_ln_g, 'conv_ln_b': conv_ln_b, 'w_pw2': w_pw2, 'b_pw2': b_pw2,
        'w_qkv': w_qkv, 'b_qkv': b_qkv, 'w_o': w_o, 'b_o': b_o, 'attn_sinks': attn_sinks,
        'rel_bias_table': rel_bias_table, 'w_gu': w_gu, 'w_down': w_down, 'norm_out': norm_out,
    }
    y_prompt, conv_prompt, win_k_prompt, win_v_prompt = run_trunk(x_prompt, c_prompt, None, None, None, p)
    y_sample, conv_sample, win_k_sample, win_v_sample = run_trunk(
        x_sample, c_sample, state_conv, cache_win_k, cache_win_v, p)
    return (y_prompt, y_sample, conv_prompt, win_k_prompt, win_v_prompt,
            conv_sample, win_k_sample, win_v_sample)
```

```python
import functools
import math

import numpy as np
import jax
import jax.numpy as jnp
from jax import lax
from jax.experimental import pallas as pl
from jax.experimental.pallas import tpu as pltpu

F32 = jnp.float32
BF16 = jnp.bfloat16

CHUNK = 64
WINDOW = 128
HEAD_DIM = 64
MAX_DISTANCE = 128
EPS = 1e-6

V7X_VMEM_BYTES = 64 * 2**20
V7X_VMEM_CAP = V7X_VMEM_BYTES - 8 * 2**20
SUBLANES = 8
LANES = 128


def _nbytes(shape, dtype):
    return math.prod(shape) * jnp.dtype(dtype).itemsize


def _vmem_limit(blocks, scratch=()):
    est = 2 * sum(_nbytes(s, d) for s, d in blocks) + sum(_nbytes(s, d) for s, d in scratch)
    return int(min(V7X_VMEM_CAP, max(32 * 2**20, 2 * est)))


def _tile(dim, pref, mult=LANES):
    if dim <= pref:
        return dim
    t = (pref // mult) * mult
    while t >= mult:
        if dim % t == 0:
            return t
        t -= mult
    return dim


def _sigmoid(x):
    return 1.0 / (1.0 + jnp.exp(-x))


def _mod_body(c_ref, w_ref, b_ref, o_ref):
    c = c_ref[...]
    a = (c * _sigmoid(c)).astype(BF16)
    o_ref[...] = jnp.dot(a, w_ref[...].astype(BF16), preferred_element_type=F32) + b_ref[...]


def _mod_call(c_all, w_mod, b_mod):
    depth, d, n = w_mod.shape
    mp = c_all.shape[0]
    tn = _tile(n, 1024)
    blocks = [((mp, d), F32), ((d, tn), F32), ((1, tn), F32), ((mp, tn), F32)]
    return pl.pallas_call(
        _mod_body,
        out_shape=jax.ShapeDtypeStruct((depth, mp, n), F32),
        grid=(depth, n // tn),
        in_specs=[
            pl.BlockSpec((mp, d), lambda l, j: (0, 0)),
            pl.BlockSpec((None, d, tn), lambda l, j: (l, 0, j)),
            pl.BlockSpec((None, 1, tn), lambda l, j: (l, 0, j)),
        ],
        out_specs=pl.BlockSpec((None, mp, tn), lambda l, j: (l, 0, j)),
        compiler_params=pltpu.CompilerParams(
            dimension_semantics=("arbitrary", "arbitrary"),
            vmem_limit_bytes=_vmem_limit(blocks, [((d, tn), BF16)])),
        name="adaln_mod",
    )(c_all, w_mod, b_mod.reshape(depth, 1, n))


class _Mod:
    def __init__(self, arr, rows_per_group):
        self.arr = arr
        self.rows_per_group = rows_per_group

    def spec(self, seg, d, tn, tm, order):
        r = self.arr.shape[1]
        per = self.rows_per_group // tm
        nb = d // tn
        if order == "ji":
            return pl.BlockSpec((None, r, tn), lambda j, i: (i // per, 0, seg * nb + j))
        return pl.BlockSpec((None, r, tn), lambda i: (i // per, 0, seg))


def _norm_mod_body(x_ref, g_ref, sh_ref, sc_ref, o_ref):
    x = x_ref[...]
    y = x * lax.rsqrt(jnp.mean(x * x, axis=-1, keepdims=True) + EPS) * g_ref[...]
    o_ref[...] = (y * (1.0 + sc_ref[...]) + sh_ref[...]).astype(o_ref.dtype)


def _norm_mod_call(x, g, mod, seg_shift, seg_scale, tm):
    m, d = x.shape
    r = mod.arr.shape[1]
    blocks = [((tm, d), F32), ((1, d), F32), ((r, d), F32), ((r, d), F32), ((tm, d), BF16)]
    return pl.pallas_call(
        _norm_mod_body,
        out_shape=jax.ShapeDtypeStruct((m, d), BF16),
        grid=(m // tm,),
        in_specs=[
            pl.BlockSpec((tm, d), lambda i: (i, 0)),
            pl.BlockSpec((1, d), lambda i: (0, 0)),
            mod.spec(seg_shift, d, d, tm, "i"),
            mod.spec(seg_scale, d, d, tm, "i"),
        ],
        out_specs=pl.BlockSpec((tm, d), lambda i: (i, 0)),
        compiler_params=pltpu.CompilerParams(
            dimension_semantics=("parallel",), vmem_limit_bytes=_vmem_limit(blocks)),
        name="rmsnorm_modulate",
    )(x, g.reshape(1, d), mod.arr, mod.arr)


def _final_norm_body(x_ref, g_ref, o_ref):
    x = x_ref[...]
    o_ref[...] = x * lax.rsqrt(jnp.mean(x * x, axis=-1, keepdims=True) + EPS) * g_ref[...]


def _final_norm_call(x, g, tm):
    m, d = x.shape
    blocks = [((tm, d), F32), ((1, d), F32), ((tm, d), F32)]
    return pl.pallas_call(
        _final_norm_body,
        out_shape=jax.ShapeDtypeStruct((m, d), F32),
        grid=(m // tm,),
        in_specs=[pl.BlockSpec((tm, d), lambda i: (i, 0)), pl.BlockSpec((1, d), lambda i: (0, 0))],
        out_specs=pl.BlockSpec((tm, d), lambda i: (i, 0)),
        compiler_params=pltpu.CompilerParams(
            dimension_semantics=("parallel",), vmem_limit_bytes=_vmem_limit(blocks)),
        name="final_rmsnorm",
    )(x, g.reshape(1, d))


def _mm_body(x_ref, *refs, n_w, has_bias, has_res, act, scale):
    ws = refs[:n_w]
    refs = refs[n_w:]
    bs = refs[:n_w] if has_bias else ()
    refs = refs[len(bs):]
    if has_res:
        res_ref, gate_ref = refs[:2]
        refs = refs[2:]
    o_ref = refs[0]
    wbs = refs[1:]

    @pl.when(pl.program_id(1) == 0)
    def _():
        for w, wb in zip(ws, wbs):
            wb[...] = w[...].astype(BF16)

    x = x_ref[...]
    ps = []
    for i in range(n_w):
        p = jnp.dot(x, wbs[i][...], preferred_element_type=F32)
        if has_bias:
            p = p + bs[i][...]
        ps.append(p)
    if act == "glu":
        y = ps[0] * _sigmoid(ps[1])
    elif act == "swiglu":
        y = ps[0] * _sigmoid(ps[0]) * ps[1]
    else:
        y = ps[0]
    if scale != 1.0:
        y = y * scale
    if has_res:
        y = res_ref[...] + gate_ref[...] * y
    o_ref[...] = y.astype(o_ref.dtype)


def _mm_call(x, w, layer, col_starts, n_out, *, bias=None, act=None, res=None, gate=None,
             gate_seg=0, scale=1.0, out_dtype=F32, tm=512, tn=512, name="matmul"):
    m, k = x.shape
    tm = _tile(m if gate is None else gate.rows_per_group, tm, SUBLANES)
    tn = _tile(n_out, tn)
    n_w = len(col_starts)
    assert all(c % tn == 0 for c in col_starts) and m % tm == 0 and n_out % tn == 0
    has_bias, has_res = bias is not None, res is not None

    in_specs = [pl.BlockSpec((tm, k), lambda j, i: (i, 0))]
    args = [x]
    blocks = [((tm, k), x.dtype), ((tm, tn), out_dtype)]
    for c in col_starts:
        off = c // tn
        in_specs.append(pl.BlockSpec((None, k, tn), lambda j, i, off=off: (layer, 0, off + j)))
        args.append(w)
        blocks.append(((k, tn), F32))
    if has_bias:
        b3 = bias.reshape(bias.shape[0], 1, bias.shape[1])
        for c in col_starts:
            off = c // tn
            in_specs.append(pl.BlockSpec((None, 1, tn), lambda j, i, off=off: (layer, 0, off + j)))
            args.append(b3)
    if has_res:
        in_specs.append(pl.BlockSpec((tm, tn), lambda j, i: (i, j)))
        args.append(res)
        in_specs.append(gate.spec(gate_seg, n_out, tn, tm, "ji"))
        args.append(gate.arr)
        blocks += [((tm, tn), F32), ((gate.arr.shape[1], tn), F32)]
    scratch = [((k, tn), BF16)] * n_w
    body = functools.partial(_mm_body, n_w=n_w, has_bias=has_bias, has_res=has_res, act=act,
                             scale=scale)
    return pl.pallas_call(
        body,
        out_shape=jax.ShapeDtypeStruct((m, n_out), out_dtype),
        grid=(n_out // tn, m // tm),
        in_specs=in_specs,
        out_specs=pl.BlockSpec((tm, tn), lambda j, i: (i, j)),
        scratch_shapes=[pltpu.VMEM(s, d) for s, d in scratch],
        compiler_params=pltpu.CompilerParams(
            dimension_semantics=("arbitrary", "arbitrary"),
            vmem_limit_bytes=_vmem_limit(blocks, scratch)),
        name=name,
    )(*args)


CONV_ROWS = 32
CONV_COLS = 256


def _conv_ln_swish(up_ref, pad, tt, w_ref, bdw_ref, lg_ref, lb_ref, acc_ref, o_ref):
    kw, d = w_ref.shape
    rows = min(CONV_ROWS, tt)
    cols = min(CONV_COLS, d)
    for c0 in range(0, d, cols):
        wk = [w_ref[k:k + 1, c0:c0 + cols] for k in range(kw)]
        for r0 in range(0, tt, rows):
            acc = jnp.zeros((rows, cols), F32)
            for k in range(kw):
                a = pad + r0 + k
                acc = acc + up_ref[a:a + rows, c0:c0 + cols] * wk[k]
            acc_ref[r0:r0 + rows, c0:c0 + cols] = acc + bdw_ref[:, c0:c0 + cols]
    z = acc_ref[...]
    mu = jnp.mean(z, axis=-1, keepdims=True)
    zc = z - mu
    y = zc * lax.rsqrt(jnp.mean(zc * zc, axis=-1, keepdims=True) + EPS)
    y = y * lg_ref[...] + lb_ref[...]
    o_ref[...] = (y * _sigmoid(y)).astype(o_ref.dtype)


def _conv_prompt_body(main_ref, halo_ref, w_ref, bdw_ref, lg_ref, lb_ref, o_ref, up_ref, acc_ref,
                      *, halo, tt):
    kw = w_ref.shape[0]

    @pl.when(pl.program_id(1) == 0)
    def _():
        up_ref[0:halo, :] = jnp.zeros((halo, up_ref.shape[1]), F32)

    @pl.when(pl.program_id(1) > 0)
    def _():
        up_ref[0:halo, :] = halo_ref[...]

    up_ref[halo:halo + tt, :] = main_ref[...]
    _conv_ln_swish(up_ref, halo - (kw - 1), tt, w_ref, bdw_ref, lg_ref, lb_ref, acc_ref, o_ref)


def _conv_prompt_call(u, w_dw, b_dw, ln_g, ln_b, layer, tt):
    b, t, d = u.shape
    kw = w_dw.shape[1]
    halo = -(-(kw - 1) // SUBLANES) * SUBLANES
    tt = _tile(t, tt, halo)
    hb = tt // halo
    vec = lambda: pl.BlockSpec((None, 1, d), lambda bi, ti: (layer, 0, 0))
    blocks = [((tt, d), F32), ((halo, d), F32), ((kw, d), F32), ((tt, d), BF16)]
    scratch = [((halo + tt, d), F32), ((tt, d), F32)]
    return pl.pallas_call(
        functools.partial(_conv_prompt_body, halo=halo, tt=tt),
        out_shape=jax.ShapeDtypeStruct((b, t, d), BF16),
        grid=(b, t // tt),
        in_specs=[
            pl.BlockSpec((None, tt, d), lambda bi, ti: (bi, ti, 0)),
            pl.BlockSpec((None, halo, d), lambda bi, ti: (bi, jnp.maximum(ti * hb - 1, 0), 0)),
            pl.BlockSpec((None, kw, d), lambda bi, ti: (layer, 0, 0)),
            vec(), vec(), vec(),
        ],
        out_specs=pl.BlockSpec((None, tt, d), lambda bi, ti: (bi, ti, 0)),
        scratch_shapes=[pltpu.VMEM(s, dt) for s, dt in scratch],
        compiler_params=pltpu.CompilerParams(
            dimension_semantics=("parallel", "arbitrary"),
            vmem_limit_bytes=_vmem_limit(blocks, scratch)),
        name="dwconv_ln_swish_prompt",
    )(u, u, w_dw, b_dw.reshape(-1, 1, d), ln_g.reshape(-1, 1, d), ln_b.reshape(-1, 1, d))


def _conv_sample_body(up_ref, w_ref, bdw_ref, lg_ref, lb_ref, o_ref, acc_ref, *, pad, tt):
    _conv_ln_swish(up_ref, pad, tt, w_ref, bdw_ref, lg_ref, lb_ref, acc_ref, o_ref)


def _conv_sample_call(up, pad, tt, w_dw, b_dw, ln_g, ln_b, layer):
    b, rows, d = up.shape
    kw = w_dw.shape[1]
    vec = lambda: pl.BlockSpec((None, 1, d), lambda bi: (layer, 0, 0))
    blocks = [((rows, d), F32), ((kw, d), F32), ((tt, d), BF16)]
    scratch = [((tt, d), F32)]
    return pl.pallas_call(
        functools.partial(_conv_sample_body, pad=pad, tt=tt),
        out_shape=jax.ShapeDtypeStruct((b, tt, d), BF16),
        grid=(b,),
        in_specs=[
            pl.BlockSpec((None, rows, d), lambda bi: (bi, 0, 0)),
            pl.BlockSpec((None, kw, d), lambda bi: (layer, 0, 0)),
            vec(), vec(), vec(),
        ],
        out_specs=pl.BlockSpec((None, tt, d), lambda bi: (bi, 0, 0)),
        scratch_shapes=[pltpu.VMEM(s, dt) for s, dt in scratch],
        compiler_params=pltpu.CompilerParams(
            dimension_semantics=("parallel",), vmem_limit_bytes=_vmem_limit(blocks, scratch)),
        name="dwconv_ln_swish_sample",
    )(up, w_dw, b_dw.reshape(-1, 1, d), ln_g.reshape(-1, 1, d), ln_b.reshape(-1, 1, d))


def _bucket_codes(n_q, n_k, k_off, n_k_pad, n_buckets):
    rel = (np.arange(n_k) + k_off)[None, :] - np.arange(n_q)[:, None]
    nb = n_buckets // 2
    max_exact = nb // 2
    ret = np.where(rel > 0, nb, 0)
    n = np.abs(rel)
    nf = np.maximum(n, 1).astype(np.float32)
    large = max_exact + (np.log(nf / np.float32(max_exact))
                         / np.float32(math.log(MAX_DISTANCE / max_exact))
                         * np.float32(nb - max_exact)).astype(np.int32)
    large = np.minimum(large, nb - 1)
    code = np.full((n_q, n_k_pad), n_buckets + 1, np.int32)
    code[:, :n_k] = ret + np.where(n < max_exact, n, large)
    code[:, n_k] = n_buckets
    return code


def _bias_body(code_ref, table_ref, sink_ref, o_ref, *, n_buckets):
    h = pl.program_id(0)
    code = code_ref[...]
    out = jnp.full(code.shape, -jnp.inf, F32)
    for b in range(n_buckets):
        out = jnp.where(code == b, table_ref[h, b], out)
    o_ref[...] = jnp.where(code == n_buckets, sink_ref[h], out)


def _bias_call(code, table, sinks):
    n_heads, n_buckets = table.shape
    n_q, n_kp = code.shape
    return pl.pallas_call(
        functools.partial(_bias_body, n_buckets=n_buckets),
        out_shape=jax.ShapeDtypeStruct((n_heads, n_q, n_kp), F32),
        grid=(n_heads,),
        in_specs=[
            pl.BlockSpec((n_q, n_kp), lambda h: (0, 0)),
            pl.BlockSpec(memory_space=pltpu.SMEM),
            pl.BlockSpec(memory_space=pltpu.SMEM),
        ],
        out_specs=pl.BlockSpec((None, n_q, n_kp), lambda h: (h, 0, 0)),
        compiler_params=pltpu.CompilerParams(dimension_semantics=("arbitrary",)),
        name="rel_bias_sink",
    )(jnp.asarray(code), table, sinks)


def _attn_core(q, kb_ref, bias_ref, maskrow, o_ref, *, kvh, group):
    hd = HEAD_DIM
    outs = []
    for h in range(kvh):
        qs = jnp.concatenate(
            [q[:, (h * group + g) * hd:(h * group + g + 1) * hd] for g in range(group)], axis=0)
        kh = kb_ref[:, h * hd:(h + 1) * hd].astype(BF16)
        vh = kb_ref[:, (kvh + h) * hd:(kvh + h + 1) * hd].astype(BF16)
        s = lax.dot_general(qs.astype(BF16), kh, (((1,), (1,)), ((), ())),
                            preferred_element_type=F32)
        s = s + bias_ref[h]
        if maskrow is not None:
            s = s + maskrow
        m = jnp.max(s, axis=-1, keepdims=True)
        e = jnp.exp(s - m)
        l = jnp.sum(e, axis=-1, keepdims=True)
        o = jnp.dot(e.astype(BF16), vh, preferred_element_type=F32) * (1.0 / l)
        tq = o.shape[0] // group
        outs += [o[g * tq:(g + 1) * tq, :] for g in range(group)]
    o_ref[...] = jnp.concatenate(outs, axis=1).astype(o_ref.dtype)


def _attn_prompt_body(q_ref, *refs, n_band, kvh, group):
    kv_refs = refs[:n_band]
    bias_ref, o_ref, kb_ref = refs[n_band:]
    c = pl.program_id(1)
    nkp, w = kb_ref.shape
    for j, r in enumerate(kv_refs):
        kb_ref[j * CHUNK:(j + 1) * CHUNK, :] = r[...]
    kb_ref[n_band * CHUNK:nkp, :] = jnp.zeros((nkp - n_band * CHUNK, w), F32)
    col = lax.broadcasted_iota(jnp.int32, (1, nkp), 1)
    maskrow = jnp.where(col < (n_band - 1 - c) * CHUNK, -jnp.inf, 0.0).astype(F32)
    _attn_core(q_ref[...].astype(F32), kb_ref, bias_ref, maskrow, o_ref, kvh=kvh, group=group)


def _attn_prompt_call(q, kv, bias, kvh, group):
    b, s, dq = q.shape
    dkv = kv.shape[2]
    n_band = WINDOW // CHUNK + 1
    nkp = bias.shape[2]
    nc = s // CHUNK
    kv_spec = lambda back: pl.BlockSpec(
        (None, CHUNK, dkv), lambda bi, ci: (bi, jnp.maximum(ci - back, 0), 0))
    blocks = [((CHUNK, dq), BF16)] * 2 + [((CHUNK, dkv), F32)] * n_band + [(bias.shape, F32)]
    scratch = [((nkp, dkv), F32)]
    return pl.pallas_call(
        functools.partial(_attn_prompt_body, n_band=n_band, kvh=kvh, group=group),
        out_shape=jax.ShapeDtypeStruct((b, s, dq), BF16),
        grid=(b, nc),
        in_specs=[pl.BlockSpec((None, CHUNK, dq), lambda bi, ci: (bi, ci, 0))]
        + [kv_spec(n_band - 1 - j) for j in range(n_band)]
        + [pl.BlockSpec(bias.shape, lambda bi, ci: (0, 0, 0))],
        out_specs=pl.BlockSpec((None, CHUNK, dq), lambda bi, ci: (bi, ci, 0)),
        scratch_shapes=[pltpu.VMEM(s_, d_) for s_, d_ in scratch],
        compiler_params=pltpu.CompilerParams(
            dimension_semantics=("parallel", "arbitrary"),
            vmem_limit_bytes=_vmem_limit(blocks, scratch)),
        name="swa_sink_attention_prompt",
    )(q, *([kv] * n_band), bias)


def _attn_sample_body(q_ref, k_ref, v_ref, bias_ref, o_ref, kb_ref, *, kvh, group):
    nk, dk = k_ref.shape
    nkp, w = kb_ref.shape
    kb_ref[0:nk, 0:dk] = k_ref[...]
    kb_ref[0:nk, dk:2 * dk] = v_ref[...]
    kb_ref[nk:nkp, :] = jnp.zeros((nkp - nk, w), F32)
    _attn_core(q_ref[...].astype(F32), kb_ref, bias_ref, None, o_ref, kvh=kvh, group=group)


def _attn_sample_call(q, k_all, v_all, bias, kvh, group):
    b, t, dq = q.shape
    nk, dk = k_all.shape[1:]
    nkp = bias.shape[2]
    blocks = [((t, dq), BF16)] * 2 + [((nk, dk), F32)] * 2 + [(bias.shape, F32)]
    scratch = [((nkp, 2 * dk), F32)]
    return pl.pallas_call(
        functools.partial(_attn_sample_body, kvh=kvh, group=group),
        out_shape=jax.ShapeDtypeStruct((b, t, dq), BF16),
        grid=(b,),
        in_specs=[
            pl.BlockSpec((None, t, dq), lambda bi: (bi, 0, 0)),
            pl.BlockSpec((None, nk, dk), lambda bi: (bi, 0, 0)),
            pl.BlockSpec((None, nk, dk), lambda bi: (bi, 0, 0)),
            pl.BlockSpec(bias.shape, lambda bi: (0, 0, 0)),
        ],
        out_specs=pl.BlockSpec((None, t, dq), lambda bi: (bi, 0, 0)),
        scratch_shapes=[pltpu.VMEM(s_, d_) for s_, d_ in scratch],
        compiler_params=pltpu.CompilerParams(
            dimension_semantics=("parallel",), vmem_limit_bytes=_vmem_limit(blocks, scratch)),
        name="swa_sink_attention_sample",
    )(q, k_all, v_all, bias)


def _pad_keys(n):
    return -(-(n + 1) // LANES) * LANES


def _trunk(x3, mods, state_conv, win_k, win_v, p, tm):
    b, t, d = x3.shape
    m = b * t
    x = x3.reshape(m, d)
    depth = p["w_mod"].shape[0]
    d_ff = p["w_gu"].shape[2] // 2
    n_heads = p["attn_sinks"].shape[1]
    dq = n_heads * HEAD_DIM
    kvh = (p["w_qkv"].shape[2] - dq) // (2 * HEAD_DIM)
    group = n_heads // kvh
    dkv = kvh * HEAD_DIM
    kw = p["w_dw"].shape[1]
    n_buckets = p["rel_bias_table"].shape[1]
    new_conv, new_k, new_v = [], [], []

    for i in range(depth):
        mod = mods[i]
        j = i // 2
        h = _norm_mod_call(x, p["norm_mix"][i], mod, 0, 1, tm)
        if i % 2 == 0:
            u = _mm_call(h, p["w_pw1"], j, (0, d), d, bias=p["b_pw1"], act="glu",
                         tm=1024, tn=512, name="pw1_glu")
            u3 = u.reshape(b, t, d)
            if state_conv is None:
                z = _conv_prompt_call(u3, p["w_dw"], p["b_dw"], p["conv_ln_g"], p["conv_ln_b"],
                                      j, 128)
                new_conv.append(u3[:, t - (kw - 1):])
            else:
                st = state_conv[j]
                pad = (-(kw - 1 + t)) % SUBLANES
                up = jnp.concatenate([jnp.zeros((b, pad, d), F32), st, u3], axis=1)
                z = _conv_sample_call(up, pad, t, p["w_dw"], p["b_dw"], p["conv_ln_g"],
                                      p["conv_ln_b"], j)
                new_conv.append(up[:, pad + t:])
            x = _mm_call(z.reshape(m, d), p["w_pw2"], j, (0,), d, bias=p["b_pw2"], res=x,
                         gate=mod, gate_seg=2, tm=1024, tn=512, name="pw2_residual")
        else:
            q = _mm_call(h, p["w_qkv"], j, (0,), dq, bias=p["b_qkv"], scale=HEAD_DIM ** -0.5,
                         out_dtype=BF16, tm=1024, tn=512, name="q_proj")
            kv = _mm_call(h, p["w_qkv"], j, (dq,), 2 * dkv, bias=p["b_qkv"],
                          tm=1024, tn=2 * dkv, name="kv_proj")
            sinks = p["attn_sinks"][j]
            if win_k is None:
                n_keys = WINDOW + CHUNK
                code = _bucket_codes(CHUNK, n_keys, -WINDOW, _pad_keys(n_keys), n_buckets)
                bias = _bias_call(code, p["rel_bias_table"], sinks)
                bias = bias.reshape(kvh, group * CHUNK, bias.shape[2])
                o = _attn_prompt_call(q.reshape(b, t, dq), kv.reshape(b, t, 2 * dkv), bias,
                                      kvh, group)
                kv3 = kv.reshape(b, t, 2, kvh, HEAD_DIM)
                new_k.append(kv3[:, t - WINDOW:, 0])
                new_v.append(kv3[:, t - WINDOW:, 1])
            else:
                kv3 = kv.reshape(b, t, 2, kvh, HEAD_DIM)
                k_all = jnp.concatenate([win_k[j], kv3[:, :, 0]], axis=1)
                v_all = jnp.concatenate([win_v[j], kv3[:, :, 1]], axis=1)
                n_keys = WINDOW + t
                code = _bucket_codes(t, n_keys, -WINDOW, _pad_keys(n_keys), n_buckets)
                bias = _bias_call(code, p["rel_bias_table"], sinks)
                bias = bias.reshape(kvh, group * t, bias.shape[2])
                o = _attn_sample_call(q.reshape(b, t, dq), k_all.reshape(b, n_keys, dkv),
                                      v_all.reshape(b, n_keys, dkv), bias, kvh, group)
                new_k.append(k_all[:, t:])
                new_v.append(v_all[:, t:])
            x = _mm_call(o.reshape(m, dq), p["w_o"], j, (0,), d, bias=p["b_o"], res=x,
                         gate=mod, gate_seg=2, tm=1024, tn=512, name="wo_residual")
        h = _norm_mod_call(x, p["norm_ffn"][i], mod, 3, 4, tm)
        a = _mm_call(h, p["w_gu"], i, (0, d_ff), d_ff, act="swiglu", out_dtype=BF16,
                     tm=1024, tn=512, name="ffn_gate_up")
        x = _mm_call(a, p["w_down"], i, (0,), d, res=x, gate=mod, gate_seg=5,
                     tm=512, tn=512, name="ffn_down_residual")
    y = _final_norm_call(x, p["norm_out"], tm)
    return y.reshape(b, t, d), jnp.stack(new_conv), jnp.stack(new_k), jnp.stack(new_v)


def kernel(x_prompt, x_sample, c_prompt, c_sample, state_conv, cache_win_k, cache_win_v, w_mod, b_mod, norm_mix, norm_ffn, w_pw1, b_pw1, w_dw, b_dw, conv_ln_g, conv_ln_b, w_pw2, b_pw2, w_qkv, b_qkv, w_o, b_o, attn_sinks, rel_bias_table, w_gu, w_down, norm_out):
    p = dict(w_mod=w_mod, b_mod=b_mod, norm_mix=norm_mix, norm_ffn=norm_ffn, w_pw1=w_pw1,
             b_pw1=b_pw1, w_dw=w_dw, b_dw=b_dw, conv_ln_g=conv_ln_g, conv_ln_b=conv_ln_b,
             w_pw2=w_pw2, b_pw2=b_pw2, w_qkv=w_qkv, b_qkv=b_qkv, w_o=w_o, b_o=b_o,
             attn_sinks=attn_sinks, rel_bias_table=rel_bias_table, w_gu=w_gu, w_down=w_down,
             norm_out=norm_out)
    bp, sp, d = x_prompt.shape
    bs, ts, _ = x_sample.shape
    depth = w_mod.shape[0]

    n_c = bp + bs
    c_all = jnp.concatenate(
        [c_prompt, c_sample, jnp.zeros((-n_c % (2 * SUBLANES), d), F32)], axis=0)
    mod_all = _mod_call(c_all, w_mod, b_mod)

    tm_p = _tile(sp, 256, SUBLANES)
    mods_p = [_Mod(mod_all[l, :bp].reshape(bp, 1, 6 * d), sp) for l in range(depth)]
    ms = bs * ts
    mods_s = [_Mod(jnp.repeat(mod_all[l, bp:n_c], ts, axis=0).reshape(1, ms, 6 * d), ms)
              for l in range(depth)]

    y_p, conv_p, k_p, v_p = _trunk(x_prompt, mods_p, None, None, None, p, tm_p)
    y_s, conv_s, k_s, v_s = _trunk(x_sample, mods_s, state_conv, cache_win_k, cache_win_v, p, ms)
    return (y_p, y_s, conv_p, k_p, v_p, conv_s, k_s, v_s)
```

```python
import functools
import math

import numpy as np
import jax
import jax.numpy as jnp
from jax import lax
from jax.experimental import pallas as pl
from jax.experimental.pallas import tpu as pltpu

F32 = jnp.float32
BF16 = jnp.bfloat16

CHUNK = 64
WINDOW = 128
HEAD_DIM = 64
MAX_DISTANCE = 128
EPS = 1e-6

V7X_VMEM_BYTES = 64 * 2**20
V7X_VMEM_CAP = V7X_VMEM_BYTES - 8 * 2**20
SUBLANES = 8
LANES = 128


def _nbytes(shape, dtype):
    return math.prod(shape) * jnp.dtype(dtype).itemsize


def _vmem_limit(blocks, scratch=()):
    est = 2 * sum(_nbytes(s, d) for s, d in blocks) + sum(_nbytes(s, d) for s, d in scratch)
    return int(min(V7X_VMEM_CAP, max(32 * 2**20, 2 * est)))


def _tile(dim, pref, mult=LANES):
    if dim <= pref:
        return dim
    t = (pref // mult) * mult
    while t >= mult:
        if dim % t == 0:
            return t
        t -= mult
    return dim


def _sigmoid(x):
    return 1.0 / (1.0 + jnp.exp(-x))


def _mod_body(c_ref, w_ref, b_ref, o_ref):
    c = c_ref[...]
    a = (c * _sigmoid(c)).astype(BF16)
    o_ref[...] = jnp.dot(a, w_ref[...].astype(BF16), preferred_element_type=F32) + b_ref[...]


def _mod_call(c_all, w_mod, b_mod):
    depth, d, n = w_mod.shape
    mp = c_all.shape[0]
    tn = _tile(n, 1024)
    blocks = [((mp, d), F32), ((d, tn), F32), ((1, tn), F32), ((mp, tn), F32)]
    return pl.pallas_call(
        _mod_body,
        out_shape=jax.ShapeDtypeStruct((depth, mp, n), F32),
        grid=(depth, n // tn),
        in_specs=[
            pl.BlockSpec((mp, d), lambda l, j: (0, 0)),
            pl.BlockSpec((None, d, tn), lambda l, j: (l, 0, j)),
            pl.BlockSpec((None, 1, tn), lambda l, j: (l, 0, j)),
        ],
        out_specs=pl.BlockSpec((None, mp, tn), lambda l, j: (l, 0, j)),
        compiler_params=pltpu.CompilerParams(
            dimension_semantics=("arbitrary", "arbitrary"),
            vmem_limit_bytes=_vmem_limit(blocks, [((d, tn), BF16)])),
        name="adaln_mod",
    )(c_all, w_mod, b_mod.reshape(depth, 1, n))


class _Mod:
    def __init__(self, arr, rows_per_group):
        self.arr = arr
        self.rows_per_group = rows_per_group

    def spec(self, seg, d, tn, tm, order):
        r = self.arr.shape[1]
        per = self.rows_per_group // tm
        nb = d // tn
        if order == "ji":
            return pl.BlockSpec((None, r, tn), lambda j, i: (i // per, 0, seg * nb + j))
        return pl.BlockSpec((None, r, tn), lambda i: (i // per, 0, seg))


def _norm_mod_body(x_ref, g_ref, sh_ref, sc_ref, o_ref):
    x = x_ref[...]
    y = x * lax.rsqrt(jnp.mean(x * x, axis=-1, keepdims=True) + EPS) * g_ref[...]
    o_ref[...] = (y * (1.0 + sc_ref[...]) + sh_ref[...]).astype(o_ref.dtype)


def _norm_mod_call(x, g, mod, seg_shift, seg_scale, tm):
    m, d = x.shape
    r = mod.arr.shape[1]
    blocks = [((tm, d), F32), ((1, d), F32), ((r, d), F32), ((r, d), F32), ((tm, d), BF16)]
    return pl.pallas_call(
        _norm_mod_body,
        out_shape=jax.ShapeDtypeStruct((m, d), BF16),
        grid=(m // tm,),
        in_specs=[
            pl.BlockSpec((tm, d), lambda i: (i, 0)),
            pl.BlockSpec((1, d), lambda i: (0, 0)),
            mod.spec(seg_shift, d, d, tm, "i"),
            mod.spec(seg_scale, d, d, tm, "i"),
        ],
        out_specs=pl.BlockSpec((tm, d), lambda i: (i, 0)),
        compiler_params=pltpu.CompilerParams(
            dimension_semantics=("parallel",), vmem_limit_bytes=_vmem_limit(blocks)),
        name="rmsnorm_modulate",
    )(x, g.reshape(1, d), mod.arr, mod.arr)


def _final_norm_body(x_ref, g_ref, o_ref):
    x = x_ref[...]
    o_ref[...] = x * lax.rsqrt(jnp.mean(x * x, axis=-1, keepdims=True) + EPS) * g_ref[...]


def _final_norm_call(x, g, tm):
    m, d = x.shape
    blocks = [((tm, d), F32), ((1, d), F32), ((tm, d), F32)]
    return pl.pallas_call(
        _final_norm_body,
        out_shape=jax.ShapeDtypeStruct((m, d), F32),
        grid=(m // tm,),
        in_specs=[pl.BlockSpec((tm, d), lambda i: (i, 0)), pl.BlockSpec((1, d), lambda i: (0, 0))],
        out_specs=pl.BlockSpec((tm, d), lambda i: (i, 0)),
        compiler_params=pltpu.CompilerParams(
            dimension_semantics=("parallel",), vmem_limit_bytes=_vmem_limit(blocks)),
        name="final_rmsnorm",
    )(x, g.reshape(1, d))


def _mm_body(x_ref, *refs, n_w, has_bias, has_res, act, scale):
    ws = refs[:n_w]
    refs = refs[n_w:]
    bs = refs[:n_w] if has_bias else ()
    refs = refs[len(bs):]
    if has_res:
        res_ref, gate_ref = refs[:2]
        refs = refs[2:]
    o_ref = refs[0]
    wbs = refs[1:]

    @pl.when(pl.program_id(1) == 0)
    def _():
        for w, wb in zip(ws, wbs):
            wb[...] = w[...].astype(BF16)

    x = x_ref[...]
    ps = []
    for i in range(n_w):
        p = jnp.dot(x, wbs[i][...], preferred_element_type=F32)
        if has_bias:
            p = p + bs[i][...]
        ps.append(p)
    if act == "glu":
        y = ps[0] * _sigmoid(ps[1])
    elif act == "swiglu":
        y = ps[0] * _sigmoid(ps[0]) * ps[1]
    else:
        y = ps[0]
    if scale != 1.0:
        y = y * scale
    if has_res:
        y = res_ref[...] + gate_ref[...] * y
    o_ref[...] = y.astype(o_ref.dtype)


def _mm_call(x, w, layer, col_starts, n_out, *, bias=None, act=None, res=None, gate=None,
             gate_seg=0, scale=1.0, out_dtype=F32, tm=512, tn=512, name="matmul"):
    m, k = x.shape
    tm = _tile(m if gate is None else gate.rows_per_group, tm, SUBLANES)
    tn = _tile(n_out, tn)
    n_w = len(col_starts)
    assert all(c % tn == 0 for c in col_starts) and m % tm == 0 and n_out % tn == 0
    has_bias, has_res = bias is not None, res is not None

    in_specs = [pl.BlockSpec((tm, k), lambda j, i: (i, 0))]
    args = [x]
    blocks = [((tm, k), x.dtype), ((tm, tn), out_dtype)]
    for c in col_starts:
        off = c // tn
        in_specs.append(pl.BlockSpec((None, k, tn), lambda j, i, off=off: (layer, 0, off + j)))
        args.append(w)
        blocks.append(((k, tn), F32))
    if has_bias:
        b3 = bias.reshape(bias.shape[0], 1, bias.shape[1])
        for c in col_starts:
            off = c // tn
            in_specs.append(pl.BlockSpec((None, 1, tn), lambda j, i, off=off: (layer, 0, off + j)))
            args.append(b3)
    if has_res:
        in_specs.append(pl.BlockSpec((tm, tn), lambda j, i: (i, j)))
        args.append(res)
        in_specs.append(gate.spec(gate_seg, n_out, tn, tm, "ji"))
        args.append(gate.arr)
        blocks += [((tm, tn), F32), ((gate.arr.shape[1], tn), F32)]
    scratch = [((k, tn), BF16)] * n_w
    body = functools.partial(_mm_body, n_w=n_w, has_bias=has_bias, has_res=has_res, act=act,
                             scale=scale)
    return pl.pallas_call(
        body,
        out_shape=jax.ShapeDtypeStruct((m, n_out), out_dtype),
        grid=(n_out // tn, m // tm),
        in_specs=in_specs,
        out_specs=pl.BlockSpec((tm, tn), lambda j, i: (i, j)),
        scratch_shapes=[pltpu.VMEM(s, d) for s, d in scratch],
        compiler_params=pltpu.CompilerParams(
            dimension_semantics=("arbitrary", "arbitrary"),
            vmem_limit_bytes=_vmem_limit(blocks, scratch)),
        name=name,
    )(*args)


def _conv_ln_swish(up_ref, pad, tt, w_ref, bdw_ref, lg_ref, lb_ref, acc_ref, o_ref):
    kw, d = w_ref.shape
    by_shift = [[(a, SUBLANES * a + s - pad) for a in range((pad + kw - 1) // SUBLANES + 1)
                 if 0 <= SUBLANES * a + s - pad < kw] for s in range(SUBLANES)]

    def strip(c, carry):
        cols = pl.ds(pl.multiple_of(c * LANES, LANES), LANES)
        z = bdw_ref[:, cols]
        for s, taps in enumerate(by_shift):
            n = tt + SUBLANES if s else tt
            q = None
            for a, k in taps:
                term = up_ref[pl.ds(SUBLANES * a, n), cols] * w_ref[pl.ds(k, 1), cols]
                q = term if q is None else q + term
            if q is not None:
                z = z + q[s:s + tt]
        acc_ref[:, cols] = z
        return carry

    lax.fori_loop(0, d // LANES, strip, 0)
    z = acc_ref[...]
    mu = jnp.mean(z, axis=-1, keepdims=True)
    zc = z - mu
    y = zc * lax.rsqrt(jnp.mean(zc * zc, axis=-1, keepdims=True) + EPS)
    y = y * lg_ref[...] + lb_ref[...]
    o_ref[...] = (y * _sigmoid(y)).astype(o_ref.dtype)


def _conv_prompt_body(main_ref, halo_ref, w_ref, bdw_ref, lg_ref, lb_ref, o_ref, up_ref, acc_ref,
                      *, halo, tt):
    kw = w_ref.shape[0]

    @pl.when(pl.program_id(1) == 0)
    def _():
        up_ref[0:halo, :] = jnp.zeros((halo, up_ref.shape[1]), F32)

    @pl.when(pl.program_id(1) > 0)
    def _():
        up_ref[0:halo, :] = halo_ref[...]

    up_ref[halo:halo + tt, :] = main_ref[...]
    _conv_ln_swish(up_ref, halo - (kw - 1), tt, w_ref, bdw_ref, lg_ref, lb_ref, acc_ref, o_ref)


def _conv_prompt_call(u, w_dw, b_dw, ln_g, ln_b, layer, tt):
    b, t, d = u.shape
    kw = w_dw.shape[1]
    halo = -(-(kw - 1) // SUBLANES) * SUBLANES
    tt = _tile(t, tt, halo)
    hb = tt // halo
    vec = lambda: pl.BlockSpec((None, 1, d), lambda bi, ti: (layer, 0, 0))
    blocks = [((tt, d), F32), ((halo, d), F32), ((kw, d), F32), ((tt, d), BF16)]
    scratch = [((halo + tt, d), F32), ((tt, d), F32)]
    return pl.pallas_call(
        functools.partial(_conv_prompt_body, halo=halo, tt=tt),
        out_shape=jax.ShapeDtypeStruct((b, t, d), BF16),
        grid=(b, t // tt),
        in_specs=[
            pl.BlockSpec((None, tt, d), lambda bi, ti: (bi, ti, 0)),
            pl.BlockSpec((None, halo, d), lambda bi, ti: (bi, jnp.maximum(ti * hb - 1, 0), 0)),
            pl.BlockSpec((None, kw, d), lambda bi, ti: (layer, 0, 0)),
            vec(), vec(), vec(),
        ],
        out_specs=pl.BlockSpec((None, tt, d), lambda bi, ti: (bi, ti, 0)),
        scratch_shapes=[pltpu.VMEM(s, dt) for s, dt in scratch],
        compiler_params=pltpu.CompilerParams(
            dimension_semantics=("parallel", "arbitrary"),
            vmem_limit_bytes=_vmem_limit(blocks, scratch)),
        name="dwconv_ln_swish_prompt",
    )(u, u, w_dw, b_dw.reshape(-1, 1, d), ln_g.reshape(-1, 1, d), ln_b.reshape(-1, 1, d))


def _conv_sample_body(up_ref, w_ref, bdw_ref, lg_ref, lb_ref, o_ref, acc_ref, *, pad, tt):
    _conv_ln_swish(up_ref, pad, tt, w_ref, bdw_ref, lg_ref, lb_ref, acc_ref, o_ref)


def _conv_sample_call(up, pad, tt, w_dw, b_dw, ln_g, ln_b, layer):
    b, rows, d = up.shape
    kw = w_dw.shape[1]
    vec = lambda: pl.BlockSpec((None, 1, d), lambda bi: (layer, 0, 0))
    blocks = [((rows, d), F32), ((kw, d), F32), ((tt, d), BF16)]
    scratch = [((tt, d), F32)]
    return pl.pallas_call(
        functools.partial(_conv_sample_body, pad=pad, tt=tt),
        out_shape=jax.ShapeDtypeStruct((b, tt, d), BF16),
        grid=(b,),
        in_specs=[
            pl.BlockSpec((None, rows, d), lambda bi: (bi, 0, 0)),
            pl.BlockSpec((None, kw, d), lambda bi: (layer, 0, 0)),
            vec(), vec(), vec(),
        ],
        out_specs=pl.BlockSpec((None, tt, d), lambda bi: (bi, 0, 0)),
        scratch_shapes=[pltpu.VMEM(s, dt) for s, dt in scratch],
        compiler_params=pltpu.CompilerParams(
            dimension_semantics=("parallel",), vmem_limit_bytes=_vmem_limit(blocks, scratch)),
        name="dwconv_ln_swish_sample",
    )(up, w_dw, b_dw.reshape(-1, 1, d), ln_g.reshape(-1, 1, d), ln_b.reshape(-1, 1, d))


def _bucket_codes(n_q, n_k, k_off, n_buckets):
    rel = (np.arange(n_k) + k_off)[None, :] - np.arange(n_q)[:, None]
    nb = n_buckets // 2
    max_exact = nb // 2
    ret = np.where(rel > 0, nb, 0)
    n = np.abs(rel)
    nf = np.maximum(n, 1).astype(np.float32)
    large = max_exact + (np.log(nf / np.float32(max_exact))
                         / np.float32(math.log(MAX_DISTANCE / max_exact))
                         * np.float32(nb - max_exact)).astype(np.int32)
    large = np.minimum(large, nb - 1)
    return (ret + np.where(n < max_exact, n, large)).astype(np.int32)


def _bias_body(code_ref, table_ref, o_ref, *, n_buckets):
    code = code_ref[...]

    def head(h, carry):
        out = jnp.zeros(code.shape, F32)
        for b in range(n_buckets):
            out = jnp.where(code == b, table_ref[h, b], out)
        o_ref[h] = out
        return carry

    lax.fori_loop(0, o_ref.shape[0], head, 0)


def _bias_call(code, table):
    n_heads, n_buckets = table.shape
    return pl.pallas_call(
        functools.partial(_bias_body, n_buckets=n_buckets),
        out_shape=jax.ShapeDtypeStruct((n_heads,) + code.shape, F32),
        in_specs=[pl.BlockSpec(memory_space=pltpu.VMEM), pl.BlockSpec(memory_space=pltpu.SMEM)],
        out_specs=pl.BlockSpec(memory_space=pltpu.VMEM),
        name="rel_bias",
    )(jnp.asarray(code), table)


def _sink_attention_unit(qs, kh, vh, bias, sink_col, maskrow):
    s = lax.dot_general(qs, kh, (((1,), (1,)), ((), ())), preferred_element_type=F32) + bias
    if maskrow is not None:
        s = s + maskrow
    m = jnp.maximum(jnp.max(s, axis=-1, keepdims=True), sink_col)
    e = jnp.exp(s - m)
    l = jnp.sum(e, axis=-1, keepdims=True) + jnp.exp(sink_col - m)
    return jnp.dot(e.astype(BF16), vh, preferred_element_type=F32) * (1.0 / l)


def _stack_heads(q, h, group):
    hd = HEAD_DIM
    return jnp.concatenate(
        [q[:, (h * group + g) * hd:(h * group + g + 1) * hd] for g in range(group)], axis=0)


def _unstack_heads(o, group):
    tq = o.shape[0] // group
    return [o[g * tq:(g + 1) * tq, :] for g in range(group)]


ATTN_Q_ROWS = 256


def _attn_prompt_body(q_ref, kvp_ref, kvc_ref, bias_ref, sink_ref, o_ref, ks_ref, *, kvh, group):
    i = pl.program_id(1)
    tq = q_ref.shape[0]
    n_sub = tq // CHUNK
    n_back = WINDOW // CHUNK
    band = WINDOW + CHUNK
    hd = HEAD_DIM
    for hh in range(2 * kvh):
        ks_ref[hh, 0:WINDOW, :] = kvp_ref[:, hh * hd:(hh + 1) * hd].astype(BF16)
        ks_ref[hh, WINDOW:WINDOW + tq, :] = kvc_ref[:, hh * hd:(hh + 1) * hd].astype(BF16)
    col = lax.broadcasted_iota(jnp.int32, (1, band), 1)
    for j in range(n_sub):
        r0 = j * CHUNK
        q = q_ref[r0:r0 + CHUNK, :].astype(F32)
        maskrow = None
        if j < n_back:
            maskrow = jnp.where(col < (n_back - (i * n_sub + j)) * CHUNK, -jnp.inf, 0.0).astype(F32)
        outs = []
        for h in range(kvh):
            o = _sink_attention_unit(
                _stack_heads(q, h, group).astype(BF16),
                ks_ref[h, r0:r0 + band, :], ks_ref[kvh + h, r0:r0 + band, :],
                bias_ref[h], sink_ref[h], maskrow)
            outs += _unstack_heads(o, group)
        o_ref[r0:r0 + CHUNK, :] = jnp.concatenate(outs, axis=1).astype(o_ref.dtype)


def _attn_prompt_call(q, kv, bias, sink_col, kvh, group):
    b, s, dq = q.shape
    dkv2 = kv.shape[2]
    tq = _tile(s, ATTN_Q_ROWS, WINDOW)
    per = tq // WINDOW
    blocks = [((tq, dq), BF16)] * 2 + [((WINDOW, dkv2), F32), ((tq, dkv2), F32),
                                       (bias.shape, F32), (sink_col.shape[:2] + (LANES,), F32)]
    scratch = [((2 * kvh, WINDOW + tq, LANES), BF16)]
    return pl.pallas_call(
        functools.partial(_attn_prompt_body, kvh=kvh, group=group),
        out_shape=jax.ShapeDtypeStruct((b, s, dq), BF16),
        grid=(b, s // tq),
        in_specs=[
            pl.BlockSpec((None, tq, dq), lambda bi, i: (bi, i, 0)),
            pl.BlockSpec((None, WINDOW, dkv2), lambda bi, i: (bi, jnp.maximum(i * per - 1, 0), 0)),
            pl.BlockSpec((None, tq, dkv2), lambda bi, i: (bi, i, 0)),
            pl.BlockSpec(bias.shape, lambda bi, i: (0, 0, 0)),
            pl.BlockSpec(sink_col.shape, lambda bi, i: (0, 0, 0)),
        ],
        out_specs=pl.BlockSpec((None, tq, dq), lambda bi, i: (bi, i, 0)),
        scratch_shapes=[pltpu.VMEM((2 * kvh, WINDOW + tq, HEAD_DIM), BF16)],
        compiler_params=pltpu.CompilerParams(
            dimension_semantics=("parallel", "arbitrary"),
            vmem_limit_bytes=_vmem_limit(blocks, scratch)),
        name="swa_sink_attention_prompt",
    )(q, kv, kv, bias, sink_col)


def _attn_sample_body(q_ref, k_ref, v_ref, bias_ref, sink_ref, o_ref, *, kvh, group):
    hd = HEAD_DIM
    q = q_ref[...].astype(F32)
    outs = []
    for h in range(kvh):
        o = _sink_attention_unit(
            _stack_heads(q, h, group).astype(BF16),
            k_ref[:, h * hd:(h + 1) * hd].astype(BF16), v_ref[:, h * hd:(h + 1) * hd].astype(BF16),
            bias_ref[h], sink_ref[h], None)
        outs += _unstack_heads(o, group)
    o_ref[...] = jnp.concatenate(outs, axis=1).astype(o_ref.dtype)


def _attn_sample_call(q, k_all, v_all, bias, sink_col, kvh, group):
    b, t, dq = q.shape
    nk, dk = k_all.shape[1:]
    blocks = [((t, dq), BF16)] * 2 + [((nk, dk), F32)] * 2 + [
        (bias.shape[:2] + (2 * LANES,), F32), (sink_col.shape[:2] + (LANES,), F32)]
    return pl.pallas_call(
        functools.partial(_attn_sample_body, kvh=kvh, group=group),
        out_shape=jax.ShapeDtypeStruct((b, t, dq), BF16),
        grid=(b,),
        in_specs=[
            pl.BlockSpec((None, t, dq), lambda bi: (bi, 0, 0)),
            pl.BlockSpec((None, nk, dk), lambda bi: (bi, 0, 0)),
            pl.BlockSpec((None, nk, dk), lambda bi: (bi, 0, 0)),
            pl.BlockSpec(bias.shape, lambda bi: (0, 0, 0)),
            pl.BlockSpec(sink_col.shape, lambda bi: (0, 0, 0)),
        ],
        out_specs=pl.BlockSpec((None, t, dq), lambda bi: (bi, 0, 0)),
        compiler_params=pltpu.CompilerParams(
            dimension_semantics=("parallel",), vmem_limit_bytes=_vmem_limit(blocks)),
        name="swa_sink_attention_sample",
    )(q, k_all, v_all, bias, sink_col)


def _trunk(x3, mods, state_conv, win_k, win_v, p, tm):
    b, t, d = x3.shape
    m = b * t
    x = x3.reshape(m, d)
    depth = p["w_mod"].shape[0]
    d_ff = p["w_gu"].shape[2] // 2
    n_heads = p["attn_sinks"].shape[1]
    dq = n_heads * HEAD_DIM
    kvh = (p["w_qkv"].shape[2] - dq) // (2 * HEAD_DIM)
    group = n_heads // kvh
    dkv = kvh * HEAD_DIM
    kw = p["w_dw"].shape[1]
    n_buckets = p["rel_bias_table"].shape[1]
    new_conv, new_k, new_v = [], [], []

    for i in range(depth):
        mod = mods[i]
        j = i // 2
        h = _norm_mod_call(x, p["norm_mix"][i], mod, 0, 1, tm)
        if i % 2 == 0:
            u = _mm_call(h, p["w_pw1"], j, (0, d), d, bias=p["b_pw1"], act="glu",
                         tm=1024, tn=512, name="pw1_glu")
            u3 = u.reshape(b, t, d)
            if state_conv is None:
                z = _conv_prompt_call(u3, p["w_dw"], p["b_dw"], p["conv_ln_g"], p["conv_ln_b"],
                                      j, 128)
                new_conv.append(u3[:, t - (kw - 1):])
            else:
                st = state_conv[j]
                pad = (-(kw - 1 + t)) % SUBLANES
                up = jnp.concatenate([jnp.zeros((b, pad, d), F32), st, u3], axis=1)
                z = _conv_sample_call(up, pad, t, p["w_dw"], p["b_dw"], p["conv_ln_g"],
                                      p["conv_ln_b"], j)
                new_conv.append(up[:, pad + t:])
            x = _mm_call(z.reshape(m, d), p["w_pw2"], j, (0,), d, bias=p["b_pw2"], res=x,
                         gate=mod, gate_seg=2, tm=1024, tn=512, name="pw2_residual")
        else:
            q = _mm_call(h, p["w_qkv"], j, (0,), dq, bias=p["b_qkv"], scale=HEAD_DIM ** -0.5,
                         out_dtype=BF16, tm=1024, tn=512, name="q_proj")
            kv = _mm_call(h, p["w_qkv"], j, (dq,), 2 * dkv, bias=p["b_qkv"],
                          tm=1024, tn=2 * dkv, name="kv_proj")
            sinks = p["attn_sinks"][j].reshape(kvh, group, 1)
            if win_k is None:
                n_keys = WINDOW + CHUNK
                bias = _bias_call(_bucket_codes(CHUNK, n_keys, -WINDOW, n_buckets),
                                  p["rel_bias_table"]).reshape(kvh, group * CHUNK, n_keys)
                sink_col = jnp.repeat(sinks, CHUNK, axis=1)
                o = _attn_prompt_call(q.reshape(b, t, dq), kv.reshape(b, t, 2 * dkv), bias,
                                      sink_col, kvh, group)
                kv_tail = kv.reshape(b, t, 2 * dkv)[:, t - WINDOW:]
                new_k.append(kv_tail[..., :dkv].reshape(b, WINDOW, kvh, HEAD_DIM))
                new_v.append(kv_tail[..., dkv:].reshape(b, WINDOW, kvh, HEAD_DIM))
            else:
                kv3 = kv.reshape(b, t, 2, kvh, HEAD_DIM)
                k_all = jnp.concatenate([win_k[j], kv3[:, :, 0]], axis=1)
                v_all = jnp.concatenate([win_v[j], kv3[:, :, 1]], axis=1)
                n_keys = WINDOW + t
                bias = _bias_call(_bucket_codes(t, n_keys, -WINDOW, n_buckets),
                                  p["rel_bias_table"]).reshape(kvh, group * t, n_keys)
                sink_col = jnp.repeat(sinks, t, axis=1)
                o = _attn_sample_call(q.reshape(b, t, dq), k_all.reshape(b, n_keys, dkv),
                                      v_all.reshape(b, n_keys, dkv), bias, sink_col, kvh, group)
                new_k.append(k_all[:, t:])
                new_v.append(v_all[:, t:])
            x = _mm_call(o.reshape(m, dq), p["w_o"], j, (0,), d, bias=p["b_o"], res=x,
                         gate=mod, gate_seg=2, tm=1024, tn=512, name="wo_residual")
        h = _norm_mod_call(x, p["norm_ffn"][i], mod, 3, 4, tm)
        a = _mm_call(h, p["w_gu"], i, (0, d_ff), d_ff, act="swiglu", out_dtype=BF16,
                     tm=1024, tn=512, name="ffn_gate_up")
        x = _mm_call(a, p["w_down"], i, (0,), d, res=x, gate=mod, gate_seg=5,
                     tm=512, tn=512, name="ffn_down_residual")
    y = _final_norm_call(x, p["norm_out"], tm)
    return y.reshape(b, t, d), jnp.stack(new_conv), jnp.stack(new_k), jnp.stack(new_v)


def kernel(x_prompt, x_sample, c_prompt, c_sample, state_conv, cache_win_k, cache_win_v, w_mod, b_mod, norm_mix, norm_ffn, w_pw1, b_pw1, w_dw, b_dw, conv_ln_g, conv_ln_b, w_pw2, b_pw2, w_qkv, b_qkv, w_o, b_o, attn_sinks, rel_bias_table, w_gu, w_down, norm_out):
    p = dict(w_mod=w_mod, b_mod=b_mod, norm_mix=norm_mix, norm_ffn=norm_ffn, w_pw1=w_pw1,
             b_pw1=b_pw1, w_dw=w_dw, b_dw=b_dw, conv_ln_g=conv_ln_g, conv_ln_b=conv_ln_b,
             w_pw2=w_pw2, b_pw2=b_pw2, w_qkv=w_qkv, b_qkv=b_qkv, w_o=w_o, b_o=b_o,
             attn_sinks=attn_sinks, rel_bias_table=rel_bias_table, w_gu=w_gu, w_down=w_down,
             norm_out=norm_out)
    bp, sp, d = x_prompt.shape
    bs, ts, _ = x_sample.shape
    depth = w_mod.shape[0]

    n_c = bp + bs
    c_all = jnp.concatenate(
        [c_prompt, c_sample, jnp.zeros((-n_c % (2 * SUBLANES), d), F32)], axis=0)
    mod_all = _mod_call(c_all, w_mod, b_mod)

    tm_p = _tile(sp, 512, SUBLANES)
    mods_p = [_Mod(mod_all[l, :bp].reshape(bp, 1, 6 * d), sp) for l in range(depth)]
    ms = bs * ts
    mods_s = [_Mod(jnp.repeat(mod_all[l, bp:n_c], ts, axis=0).reshape(1, ms, 6 * d), ms)
              for l in range(depth)]

    y_p, conv_p, k_p, v_p = _trunk(x_prompt, mods_p, None, None, None, p, tm_p)
    y_s, conv_s, k_s, v_s = _trunk(x_sample, mods_s, state_conv, cache_win_k, cache_win_v, p, ms)
    return (y_p, y_s, conv_p, k_p, v_p, conv_s, k_s, v_s)
```

```python
import functools
import math

import numpy as np
import jax
import jax.numpy as jnp
from jax import lax
from jax.experimental import pallas as pl
from jax.experimental.pallas import tpu as pltpu

F32 = jnp.float32
BF16 = jnp.bfloat16

CHUNK = 64
WINDOW = 128
HEAD_DIM = 64
MAX_DISTANCE = 128
EPS = 1e-6

V7X_VMEM_BYTES = 64 * 2**20
V7X_VMEM_CAP = V7X_VMEM_BYTES - 8 * 2**20
SUBLANES = 8
LANES = 128


def _nbytes(shape, dtype):
    return math.prod(shape) * jnp.dtype(dtype).itemsize


def _vmem_limit(blocks, scratch=()):
    est = 2 * sum(_nbytes(s, d) for s, d in blocks) + sum(_nbytes(s, d) for s, d in scratch)
    return int(min(V7X_VMEM_CAP, max(32 * 2**20, 2 * est)))


def _tile(dim, pref, mult=LANES):
    if dim <= pref:
        return dim
    t = (pref // mult) * mult
    while t >= mult:
        if dim % t == 0:
            return t
        t -= mult
    return dim


def _sigmoid(x):
    return 1.0 / (1.0 + jnp.exp(-x))


def _mod_body(c_ref, w_ref, b_ref, o_ref):
    c = c_ref[...]
    a = (c * _sigmoid(c)).astype(BF16)
    o_ref[...] = jnp.dot(a, w_ref[...].astype(BF16), preferred_element_type=F32) + b_ref[...]


def _mod_call(c_all, w_mod, b_mod):
    depth, d, n = w_mod.shape
    mp = c_all.shape[0]
    tn = _tile(n, 1024)
    blocks = [((mp, d), F32), ((d, tn), F32), ((1, tn), F32), ((mp, tn), F32)]
    return pl.pallas_call(
        _mod_body,
        out_shape=jax.ShapeDtypeStruct((depth, mp, n), F32),
        grid=(depth, n // tn),
        in_specs=[
            pl.BlockSpec((mp, d), lambda l, j: (0, 0)),
            pl.BlockSpec((None, d, tn), lambda l, j: (l, 0, j)),
            pl.BlockSpec((None, 1, tn), lambda l, j: (l, 0, j)),
        ],
        out_specs=pl.BlockSpec((None, mp, tn), lambda l, j: (l, 0, j)),
        compiler_params=pltpu.CompilerParams(
            dimension_semantics=("arbitrary", "arbitrary"),
            vmem_limit_bytes=_vmem_limit(blocks, [((d, tn), BF16)])),
        name="adaln_mod",
    )(c_all, w_mod, b_mod.reshape(depth, 1, n))


class _Mod:
    def __init__(self, arr, rows_per_group):
        self.arr = arr
        self.rows_per_group = rows_per_group

    def spec(self, seg, d, tn, tm, order):
        r = self.arr.shape[1]
        per = self.rows_per_group // tm
        nb = d // tn
        if order == "ji":
            return pl.BlockSpec((None, r, tn), lambda j, i: (i // per, 0, seg * nb + j))
        return pl.BlockSpec((None, r, tn), lambda i: (i // per, 0, seg))


def _norm_mod_body(x_ref, g_ref, sh_ref, sc_ref, o_ref):
    x = x_ref[...]
    y = x * lax.rsqrt(jnp.mean(x * x, axis=-1, keepdims=True) + EPS) * g_ref[...]
    o_ref[...] = (y * (1.0 + sc_ref[...]) + sh_ref[...]).astype(o_ref.dtype)


def _norm_mod_call(x, g, mod, seg_shift, seg_scale, tm):
    m, d = x.shape
    r = mod.arr.shape[1]
    blocks = [((tm, d), F32), ((1, d), F32), ((r, d), F32), ((r, d), F32), ((tm, d), BF16)]
    return pl.pallas_call(
        _norm_mod_body,
        out_shape=jax.ShapeDtypeStruct((m, d), BF16),
        grid=(m // tm,),
        in_specs=[
            pl.BlockSpec((tm, d), lambda i: (i, 0)),
            pl.BlockSpec((1, d), lambda i: (0, 0)),
            mod.spec(seg_shift, d, d, tm, "i"),
            mod.spec(seg_scale, d, d, tm, "i"),
        ],
        out_specs=pl.BlockSpec((tm, d), lambda i: (i, 0)),
        compiler_params=pltpu.CompilerParams(
            dimension_semantics=("parallel",), vmem_limit_bytes=_vmem_limit(blocks)),
        name="rmsnorm_modulate",
    )(x, g.reshape(1, d), mod.arr, mod.arr)


def _final_norm_body(x_ref, g_ref, o_ref):
    x = x_ref[...]
    o_ref[...] = x * lax.rsqrt(jnp.mean(x * x, axis=-1, keepdims=True) + EPS) * g_ref[...]


def _final_norm_call(x, g, tm):
    m, d = x.shape
    blocks = [((tm, d), F32), ((1, d), F32), ((tm, d), F32)]
    return pl.pallas_call(
        _final_norm_body,
        out_shape=jax.ShapeDtypeStruct((m, d), F32),
        grid=(m // tm,),
        in_specs=[pl.BlockSpec((tm, d), lambda i: (i, 0)), pl.BlockSpec((1, d), lambda i: (0, 0))],
        out_specs=pl.BlockSpec((tm, d), lambda i: (i, 0)),
        compiler_params=pltpu.CompilerParams(
            dimension_semantics=("parallel",), vmem_limit_bytes=_vmem_limit(blocks)),
        name="final_rmsnorm",
    )(x, g.reshape(1, d))


def _mm_body(x_ref, *refs, n_w, has_bias, has_res, act, scale, x_t, out_mode):
    ws = refs[:n_w]
    refs = refs[n_w:]
    bs = refs[:n_w] if has_bias else ()
    refs = refs[len(bs):]
    if has_res:
        res_ref, gate_ref = refs[:2]
        refs = refs[2:]
    n_o = 2 if out_mode == "split_t" else 1
    o_refs = refs[:n_o]
    wbs = refs[n_o:]

    @pl.when(pl.program_id(1) == 0)
    def _():
        for w, wb in zip(ws, wbs):
            wb[...] = w[...].astype(BF16)

    x = x_ref[...].T if x_t else x_ref[...]
    ps = []
    for i in range(n_w):
        p = jnp.dot(x, wbs[i][...], preferred_element_type=F32)
        if has_bias:
            p = p + bs[i][...]
        ps.append(p)
    if act == "glu":
        y = ps[0] * _sigmoid(ps[1])
    elif act == "swiglu":
        y = ps[0] * _sigmoid(ps[0]) * ps[1]
    else:
        y = ps[0]
    if scale != 1.0:
        y = y * scale
    if has_res:
        y = res_ref[...] + gate_ref[...] * y
    if out_mode == "plain":
        o_refs[0][...] = y.astype(o_refs[0].dtype)
    elif out_mode == "t":
        o_refs[0][...] = y.T.astype(o_refs[0].dtype)
    else:
        half = y.shape[1] // 2
        o_refs[0][...] = y[:, :half].astype(o_refs[0].dtype)
        o_refs[1][...] = y[:, half:].T.astype(o_refs[1].dtype)


def _mm_call(x, w, layer, col_starts, n_out, *, bias=None, act=None, res=None, gate=None,
             gate_seg=0, scale=1.0, out_dtype=F32, tm=512, tn=512, name="matmul",
             x_t=False, out_mode="plain", seq=None):
    if x_t:
        nb, k, seq = x.shape
        m = nb * seq
    else:
        m, k = x.shape
    row_span = m if gate is None else gate.rows_per_group
    if x_t or out_mode != "plain":
        row_span = min(row_span, seq)
    tm = _tile(row_span, tm, SUBLANES if out_mode == "plain" and not x_t else LANES)
    tn = _tile(n_out, tn)
    n_w = len(col_starts)
    assert all(c % tn == 0 for c in col_starts) and m % tm == 0 and n_out % tn == 0
    has_bias, has_res = bias is not None, res is not None
    per_b = None if seq is None else seq // tm

    if x_t:
        in_specs = [pl.BlockSpec((None, k, tm), lambda j, i: (i // per_b, 0, i % per_b))]
    else:
        in_specs = [pl.BlockSpec((tm, k), lambda j, i: (i, 0))]
    args = [x]
    blocks = [((tm, k), x.dtype), ((tm, tn), out_dtype)]
    for c in col_starts:
        off = c // tn
        in_specs.append(pl.BlockSpec((None, k, tn), lambda j, i, off=off: (layer, 0, off + j)))
        args.append(w)
        blocks.append(((k, tn), F32))
    if has_bias:
        b3 = bias.reshape(bias.shape[0], 1, bias.shape[1])
        for c in col_starts:
            off = c // tn
            in_specs.append(pl.BlockSpec((None, 1, tn), lambda j, i, off=off: (layer, 0, off + j)))
            args.append(b3)
    if has_res:
        in_specs.append(pl.BlockSpec((tm, tn), lambda j, i: (i, j)))
        args.append(res)
        in_specs.append(gate.spec(gate_seg, n_out, tn, tm, "ji"))
        args.append(gate.arr)
        blocks += [((tm, tn), F32), ((gate.arr.shape[1], tn), F32)]
    scratch = [((k, tn), BF16)] * n_w
    body = functools.partial(_mm_body, n_w=n_w, has_bias=has_bias, has_res=has_res, act=act,
                             scale=scale, x_t=x_t, out_mode=out_mode)
    if out_mode == "plain":
        out_shape = jax.ShapeDtypeStruct((m, n_out), out_dtype)
        out_specs = pl.BlockSpec((tm, tn), lambda j, i: (i, j))
    elif out_mode == "t":
        out_shape = jax.ShapeDtypeStruct((m // seq, n_out, seq), out_dtype)
        out_specs = pl.BlockSpec((None, tn, tm), lambda j, i: (i // per_b, j, i % per_b))
    else:
        assert tn == n_out
        half = n_out // 2
        out_shape = [jax.ShapeDtypeStruct((m, half), out_dtype),
                     jax.ShapeDtypeStruct((m // seq, half, seq), out_dtype)]
        out_specs = [pl.BlockSpec((tm, half), lambda j, i: (i, 0)),
                     pl.BlockSpec((None, half, tm), lambda j, i: (i // per_b, 0, i % per_b))]
    return pl.pallas_call(
        body,
        out_shape=out_shape,
        grid=(n_out // tn, m // tm),
        in_specs=in_specs,
        out_specs=out_specs,
        scratch_shapes=[pltpu.VMEM(s, d) for s, d in scratch],
        compiler_params=pltpu.CompilerParams(
            dimension_semantics=("arbitrary", "arbitrary"),
            vmem_limit_bytes=_vmem_limit(blocks, scratch)),
        name=name,
    )(*args)


def _conv_ln_swish(up_ref, pad, tt, w_ref, bdw_ref, lg_ref, lb_ref, acc_ref, o_ref):
    kw, d = w_ref.shape
    by_shift = [[(a, SUBLANES * a + s - pad) for a in range((pad + kw - 1) // SUBLANES + 1)
                 if 0 <= SUBLANES * a + s - pad < kw] for s in range(SUBLANES)]

    def strip(c, carry):
        cols = pl.ds(pl.multiple_of(c * LANES, LANES), LANES)
        z = bdw_ref[:, cols]
        for s, taps in enumerate(by_shift):
            n = tt + SUBLANES if s else tt
            q = None
            for a, k in taps:
                term = up_ref[pl.ds(SUBLANES * a, n), cols] * w_ref[pl.ds(k, 1), cols]
                q = term if q is None else q + term
            if q is not None:
                z = z + q[s:s + tt]
        acc_ref[:, cols] = z
        return carry

    lax.fori_loop(0, d // LANES, strip, 0)
    z = acc_ref[...]
    mu = jnp.mean(z, axis=-1, keepdims=True)
    zc = z - mu
    y = zc * lax.rsqrt(jnp.mean(zc * zc, axis=-1, keepdims=True) + EPS)
    y = y * lg_ref[...] + lb_ref[...]
    o_ref[...] = (y * _sigmoid(y)).astype(o_ref.dtype)


def _conv_prompt_body(main_ref, halo_ref, w_ref, bdw_ref, lg_ref, lb_ref, o_ref, up_ref, acc_ref,
                      *, halo, tt):
    kw = w_ref.shape[0]

    @pl.when(pl.program_id(1) == 0)
    def _():
        up_ref[0:halo, :] = jnp.zeros((halo, up_ref.shape[1]), F32)

    @pl.when(pl.program_id(1) > 0)
    def _():
        up_ref[0:halo, :] = halo_ref[...]

    up_ref[halo:halo + tt, :] = main_ref[...]
    _conv_ln_swish(up_ref, halo - (kw - 1), tt, w_ref, bdw_ref, lg_ref, lb_ref, acc_ref, o_ref)


def _conv_prompt_call(u, w_dw, b_dw, ln_g, ln_b, layer, tt):
    b, t, d = u.shape
    kw = w_dw.shape[1]
    halo = -(-(kw - 1) // SUBLANES) * SUBLANES
    tt = _tile(t, tt, halo)
    hb = tt // halo
    vec = lambda: pl.BlockSpec((None, 1, d), lambda bi, ti: (layer, 0, 0))
    blocks = [((tt, d), F32), ((halo, d), F32), ((kw, d), F32), ((tt, d), BF16)]
    scratch = [((halo + tt, d), F32), ((tt, d), F32)]
    return pl.pallas_call(
        functools.partial(_conv_prompt_body, halo=halo, tt=tt),
        out_shape=jax.ShapeDtypeStruct((b, t, d), BF16),
        grid=(b, t // tt),
        in_specs=[
            pl.BlockSpec((None, tt, d), lambda bi, ti: (bi, ti, 0)),
            pl.BlockSpec((None, halo, d), lambda bi, ti: (bi, jnp.maximum(ti * hb - 1, 0), 0)),
            pl.BlockSpec((None, kw, d), lambda bi, ti: (layer, 0, 0)),
            vec(), vec(), vec(),
        ],
        out_specs=pl.BlockSpec((None, tt, d), lambda bi, ti: (bi, ti, 0)),
        scratch_shapes=[pltpu.VMEM(s, dt) for s, dt in scratch],
        compiler_params=pltpu.CompilerParams(
            dimension_semantics=("parallel", "arbitrary"),
            vmem_limit_bytes=_vmem_limit(blocks, scratch)),
        name="dwconv_ln_swish_prompt",
    )(u, u, w_dw, b_dw.reshape(-1, 1, d), ln_g.reshape(-1, 1, d), ln_b.reshape(-1, 1, d))


def _conv_sample_body(up_ref, w_ref, bdw_ref, lg_ref, lb_ref, o_ref, acc_ref, *, pad, tt):
    _conv_ln_swish(up_ref, pad, tt, w_ref, bdw_ref, lg_ref, lb_ref, acc_ref, o_ref)


def _conv_sample_call(up, pad, tt, w_dw, b_dw, ln_g, ln_b, layer):
    b, rows, d = up.shape
    kw = w_dw.shape[1]
    vec = lambda: pl.BlockSpec((None, 1, d), lambda bi: (layer, 0, 0))
    blocks = [((rows, d), F32), ((kw, d), F32), ((tt, d), BF16)]
    scratch = [((tt, d), F32)]
    return pl.pallas_call(
        functools.partial(_conv_sample_body, pad=pad, tt=tt),
        out_shape=jax.ShapeDtypeStruct((b, tt, d), BF16),
        grid=(b,),
        in_specs=[
            pl.BlockSpec((None, rows, d), lambda bi: (bi, 0, 0)),
            pl.BlockSpec((None, kw, d), lambda bi: (layer, 0, 0)),
            vec(), vec(), vec(),
        ],
        out_specs=pl.BlockSpec((None, tt, d), lambda bi: (bi, 0, 0)),
        scratch_shapes=[pltpu.VMEM(s, dt) for s, dt in scratch],
        compiler_params=pltpu.CompilerParams(
            dimension_semantics=("parallel",), vmem_limit_bytes=_vmem_limit(blocks, scratch)),
        name="dwconv_ln_swish_sample",
    )(up, w_dw, b_dw.reshape(-1, 1, d), ln_g.reshape(-1, 1, d), ln_b.reshape(-1, 1, d))


def _bucket_codes(n_q, n_k, k_off, n_buckets):
    rel = (np.arange(n_k) + k_off)[None, :] - np.arange(n_q)[:, None]
    nb = n_buckets // 2
    max_exact = nb // 2
    ret = np.where(rel > 0, nb, 0)
    n = np.abs(rel)
    nf = np.maximum(n, 1).astype(np.float32)
    large = max_exact + (np.log(nf / np.float32(max_exact))
                         / np.float32(math.log(MAX_DISTANCE / max_exact))
                         * np.float32(nb - max_exact)).astype(np.int32)
    large = np.minimum(large, nb - 1)
    return (ret + np.where(n < max_exact, n, large)).astype(np.int32)


def _bias_body(code_ref, table_ref, o_ref, *, n_buckets):
    code = code_ref[...]

    def head(h, carry):
        out = jnp.full(code.shape, -jnp.inf, F32)
        for b in range(n_buckets):
            out = jnp.where(code == b, table_ref[h, b], out)
        o_ref[h] = out
        return carry

    lax.fori_loop(0, o_ref.shape[0], head, 0)


def _bias_call(code, table):
    n_heads, n_buckets = table.shape
    return pl.pallas_call(
        functools.partial(_bias_body, n_buckets=n_buckets),
        out_shape=jax.ShapeDtypeStruct((n_heads,) + code.shape, F32),
        in_specs=[pl.BlockSpec(memory_space=pltpu.VMEM), pl.BlockSpec(memory_space=pltpu.SMEM)],
        out_specs=pl.BlockSpec(memory_space=pltpu.VMEM),
        name="rel_bias",
    )(jnp.asarray(code), table)


def _sink_attention_unit(qs, kh, vh, bias, sink_col, maskrow):
    s = lax.dot_general(qs, kh, (((1,), (1,)), ((), ())), preferred_element_type=F32) + bias
    if maskrow is not None:
        s = s + maskrow
    m = jnp.maximum(jnp.max(s, axis=-1, keepdims=True), sink_col)
    e = jnp.exp(s - m)
    l = jnp.sum(e, axis=-1, keepdims=True) + jnp.exp(sink_col - m)
    return jnp.dot(e.astype(BF16), vh, preferred_element_type=F32) * (1.0 / l)


def _stack_heads(q, h, group):
    hd = HEAD_DIM
    return jnp.concatenate(
        [q[:, (h * group + g) * hd:(h * group + g + 1) * hd] for g in range(group)], axis=0)


def _unstack_heads(o, group):
    tq = o.shape[0] // group
    return [o[g * tq:(g + 1) * tq, :] for g in range(group)]


ATTN_Q_COLS = 256
PAIR = 2 * CHUNK


def _pair_codes(n_buckets):
    code = _bucket_codes(PAIR, WINDOW + PAIR, -WINDOW, n_buckets).T
    key_chunk = np.arange(WINDOW + PAIR)[:, None] // CHUNK
    q_chunk = np.arange(PAIR)[None, :] // CHUNK
    visible = (key_chunk >= q_chunk) & (key_chunk <= q_chunk + WINDOW // CHUNK)
    return np.where(visible, code, n_buckets).astype(np.int32)


def _attn_prompt_body(qT_ref, kp_ref, kc_ref, vTp_ref, vTc_ref, bias_ref, sink_ref, oT_ref,
                      k_scr, vT_scr, *, kvh, group):
    i = pl.program_id(1)
    tq = qT_ref.shape[1]
    nk = WINDOW + PAIR
    hd = HEAD_DIM
    for h in range(kvh):
        k_scr[h, 0:WINDOW, :] = kp_ref[:, h * hd:(h + 1) * hd].astype(BF16)
        k_scr[h, WINDOW:WINDOW + tq, :] = kc_ref[:, h * hd:(h + 1) * hd].astype(BF16)
    vT_scr[:, 0:WINDOW] = vTp_ref[...].astype(BF16)
    vT_scr[:, WINDOW:WINDOW + tq] = vTc_ref[...].astype(BF16)
    row = lax.broadcasted_iota(jnp.int32, (nk, group * PAIR), 0)
    for p in range(tq // PAIR):
        c0 = p * PAIR
        start_mask = None
        if p == 0:
            start_mask = jnp.where(row < jnp.where(i == 0, WINDOW, 0), -jnp.inf, 0.0).astype(F32)
        for h in range(kvh):
            heads = range(h * group, (h + 1) * group)
            qsT = jnp.concatenate(
                [qT_ref[hh * hd:(hh + 1) * hd, c0:c0 + PAIR] for hh in heads], axis=1)
            s = jnp.dot(k_scr[h, c0:c0 + nk, :], qsT, preferred_element_type=F32) + bias_ref[h]
            if start_mask is not None:
                s = s + start_mask
            sink = sink_ref[h]
            m = jnp.maximum(jnp.max(s, axis=0, keepdims=True), sink)
            e = jnp.exp(s - m)
            l = jnp.sum(e, axis=0, keepdims=True) + jnp.exp(sink - m)
            oT = jnp.dot(vT_scr[h * hd:(h + 1) * hd, c0:c0 + nk], e.astype(BF16),
                         preferred_element_type=F32) * (1.0 / l)
            for g, hh in enumerate(heads):
                oT_ref[hh * hd:(hh + 1) * hd, c0:c0 + PAIR] = (
                    oT[:, g * PAIR:(g + 1) * PAIR].astype(oT_ref.dtype))


def _attn_prompt_call(qT, k, vT, bias, sink_row, kvh, group):
    b, dq, s = qT.shape
    dk = k.shape[2]
    tq = _tile(s, ATTN_Q_COLS, PAIR)
    per = tq // WINDOW
    prev = lambda i: jnp.maximum(i * per - 1, 0)
    blocks = [((dq, tq), BF16)] * 2 + [((WINDOW + tq, dk), F32)] * 2 + [
        (bias.shape, F32), (sink_row.shape[:1] + (SUBLANES, sink_row.shape[2]), F32)]
    scratch = [((kvh, WINDOW + tq, LANES), BF16), ((dk, WINDOW + tq), BF16)]
    return pl.pallas_call(
        functools.partial(_attn_prompt_body, kvh=kvh, group=group),
        out_shape=jax.ShapeDtypeStruct((b, dq, s), BF16),
        grid=(b, s // tq),
        in_specs=[
            pl.BlockSpec((None, dq, tq), lambda bi, i: (bi, 0, i)),
            pl.BlockSpec((None, WINDOW, dk), lambda bi, i: (bi, prev(i), 0)),
            pl.BlockSpec((None, tq, dk), lambda bi, i: (bi, i, 0)),
            pl.BlockSpec((None, dk, WINDOW), lambda bi, i: (bi, 0, prev(i))),
            pl.BlockSpec((None, dk, tq), lambda bi, i: (bi, 0, i)),
            pl.BlockSpec(bias.shape, lambda bi, i: (0, 0, 0)),
            pl.BlockSpec(sink_row.shape, lambda bi, i: (0, 0, 0)),
        ],
        out_specs=pl.BlockSpec((None, dq, tq), lambda bi, i: (bi, 0, i)),
        scratch_shapes=[pltpu.VMEM((kvh, WINDOW + tq, HEAD_DIM), BF16),
                        pltpu.VMEM((dk, WINDOW + tq), BF16)],
        compiler_params=pltpu.CompilerParams(
            dimension_semantics=("parallel", "arbitrary"),
            vmem_limit_bytes=_vmem_limit(blocks, scratch)),
        name="swa_sink_attention_prompt",
    )(qT, k, k, vT, vT, bias, sink_row)


def _attn_sample_body(q_ref, k_ref, v_ref, bias_ref, sink_ref, o_ref, *, kvh, group):
    hd = HEAD_DIM
    q = q_ref[...].astype(F32)
    outs = []
    for h in range(kvh):
        o = _sink_attention_unit(
            _stack_heads(q, h, group).astype(BF16),
            k_ref[:, h * hd:(h + 1) * hd].astype(BF16), v_ref[:, h * hd:(h + 1) * hd].astype(BF16),
            bias_ref[h], sink_ref[h], None)
        outs += _unstack_heads(o, group)
    o_ref[...] = jnp.concatenate(outs, axis=1).astype(o_ref.dtype)


def _attn_sample_call(q, k_all, v_all, bias, sink_col, kvh, group):
    b, t, dq = q.shape
    nk, dk = k_all.shape[1:]
    blocks = [((t, dq), BF16)] * 2 + [((nk, dk), F32)] * 2 + [
        (bias.shape[:2] + (2 * LANES,), F32), (sink_col.shape[:2] + (LANES,), F32)]
    return pl.pallas_call(
        functools.partial(_attn_sample_body, kvh=kvh, group=group),
        out_shape=jax.ShapeDtypeStruct((b, t, dq), BF16),
        grid=(b,),
        in_specs=[
            pl.BlockSpec((None, t, dq), lambda bi: (bi, 0, 0)),
            pl.BlockSpec((None, nk, dk), lambda bi: (bi, 0, 0)),
            pl.BlockSpec((None, nk, dk), lambda bi: (bi, 0, 0)),
            pl.BlockSpec(bias.shape, lambda bi: (0, 0, 0)),
            pl.BlockSpec(sink_col.shape, lambda bi: (0, 0, 0)),
        ],
        out_specs=pl.BlockSpec((None, t, dq), lambda bi: (bi, 0, 0)),
        compiler_params=pltpu.CompilerParams(
            dimension_semantics=("parallel",), vmem_limit_bytes=_vmem_limit(blocks)),
        name="swa_sink_attention_sample",
    )(q, k_all, v_all, bias, sink_col)


def _trunk(x3, mods, state_conv, win_k, win_v, p, tm):
    b, t, d = x3.shape
    m = b * t
    x = x3.reshape(m, d)
    depth = p["w_mod"].shape[0]
    d_ff = p["w_gu"].shape[2] // 2
    n_heads = p["attn_sinks"].shape[1]
    dq = n_heads * HEAD_DIM
    kvh = (p["w_qkv"].shape[2] - dq) // (2 * HEAD_DIM)
    group = n_heads // kvh
    dkv = kvh * HEAD_DIM
    kw = p["w_dw"].shape[1]
    n_buckets = p["rel_bias_table"].shape[1]
    new_conv, new_k, new_v = [], [], []

    for i in range(depth):
        mod = mods[i]
        j = i // 2
        h = _norm_mod_call(x, p["norm_mix"][i], mod, 0, 1, tm)
        if i % 2 == 0:
            u = _mm_call(h, p["w_pw1"], j, (0, d), d, bias=p["b_pw1"], act="glu",
                         tm=1024, tn=512, name="pw1_glu")
            u3 = u.reshape(b, t, d)
            if state_conv is None:
                z = _conv_prompt_call(u3, p["w_dw"], p["b_dw"], p["conv_ln_g"], p["conv_ln_b"],
                                      j, 128)
                new_conv.append(u3[:, t - (kw - 1):])
            else:
                st = state_conv[j]
                pad = (-(kw - 1 + t)) % SUBLANES
                up = jnp.concatenate([jnp.zeros((b, pad, d), F32), st, u3], axis=1)
                z = _conv_sample_call(up, pad, t, p["w_dw"], p["b_dw"], p["conv_ln_g"],
                                      p["conv_ln_b"], j)
                new_conv.append(up[:, pad + t:])
            x = _mm_call(z.reshape(m, d), p["w_pw2"], j, (0,), d, bias=p["b_pw2"], res=x,
                         gate=mod, gate_seg=2, tm=1024, tn=512, name="pw2_residual")
        else:
            sinks = p["attn_sinks"][j].reshape(kvh, group, 1)
            prompt = win_k is None
            q = _mm_call(h, p["w_qkv"], j, (0,), dq, bias=p["b_qkv"], scale=HEAD_DIM ** -0.5,
                         out_dtype=BF16, tm=1024, tn=512, name="q_proj",
                         out_mode="t" if prompt else "plain", seq=t)
            kv = _mm_call(h, p["w_qkv"], j, (dq,), 2 * dkv, bias=p["b_qkv"],
                          tm=1024, tn=2 * dkv, name="kv_proj",
                          out_mode="split_t" if prompt else "plain", seq=t)
            if prompt:
                k, vT = kv[0].reshape(b, t, dkv), kv[1]
                nk = WINDOW + PAIR
                bias = _bias_call(_pair_codes(n_buckets), p["rel_bias_table"])
                bias = bias.reshape(kvh, group, nk, PAIR).transpose(0, 2, 1, 3)
                sink_row = jnp.repeat(sinks, PAIR, axis=2).reshape(kvh, 1, group * PAIR)
                o = _attn_prompt_call(q, k, vT, bias.reshape(kvh, nk, group * PAIR), sink_row,
                                      kvh, group)
                new_k.append(k[:, t - WINDOW:].reshape(b, WINDOW, kvh, HEAD_DIM))
                new_v.append(jnp.swapaxes(vT[:, :, t - WINDOW:], 1, 2)
                             .reshape(b, WINDOW, kvh, HEAD_DIM))
            else:
                kv3 = kv.reshape(b, t, 2, kvh, HEAD_DIM)
                k_all = jnp.concatenate([win_k[j], kv3[:, :, 0]], axis=1)
                v_all = jnp.concatenate([win_v[j], kv3[:, :, 1]], axis=1)
                n_keys = WINDOW + t
                bias = _bias_call(_bucket_codes(t, n_keys, -WINDOW, n_buckets),
                                  p["rel_bias_table"]).reshape(kvh, group * t, n_keys)
                sink_col = jnp.repeat(sinks, t, axis=1)
                o = _attn_sample_call(q.reshape(b, t, dq), k_all.reshape(b, n_keys, dkv),
                                      v_all.reshape(b, n_keys, dkv), bias, sink_col, kvh, group)
                new_k.append(k_all[:, t:])
                new_v.append(v_all[:, t:])
            x = _mm_call(o if prompt else o.reshape(m, dq), p["w_o"], j, (0,), d, bias=p["b_o"],
                         res=x, gate=mod, gate_seg=2, tm=1024, tn=512, name="wo_residual",
                         x_t=prompt)
        h = _norm_mod_call(x, p["norm_ffn"][i], mod, 3, 4, tm)
        a = _mm_call(h, p["w_gu"], i, (0, d_ff), d_ff, act="swiglu", out_dtype=BF16,
                     tm=1024, tn=512, name="ffn_gate_up")
        x = _mm_call(a, p["w_down"], i, (0,), d, res=x, gate=mod, gate_seg=5,
                     tm=512, tn=512, name="ffn_down_residual")
    y = _final_norm_call(x, p["norm_out"], tm)
    return y.reshape(b, t, d), jnp.stack(new_conv), jnp.stack(new_k), jnp.stack(new_v)


def kernel(x_prompt, x_sample, c_prompt, c_sample, state_conv, cache_win_k, cache_win_v, w_mod, b_mod, norm_mix, norm_ffn, w_pw1, b_pw1, w_dw, b_dw, conv_ln_g, conv_ln_b, w_pw2, b_pw2, w_qkv, b_qkv, w_o, b_o, attn_sinks, rel_bias_table, w_gu, w_down, norm_out):
    p = dict(w_mod=w_mod, b_mod=b_mod, norm_mix=norm_mix, norm_ffn=norm_ffn, w_pw1=w_pw1,
             b_pw1=b_pw1, w_dw=w_dw, b_dw=b_dw, conv_ln_g=conv_ln_g, conv_ln_b=conv_ln_b,
             w_pw2=w_pw2, b_pw2=b_pw2, w_qkv=w_qkv, b_qkv=b_qkv, w_o=w_o, b_o=b_o,
             attn_sinks=attn_sinks, rel_bias_table=rel_bias_table, w_gu=w_gu, w_down=w_down,
             norm_out=norm_out)
    bp, sp, d = x_prompt.shape
    bs, ts, _ = x_sample.shape
    depth = w_mod.shape[0]

    n_c = bp + bs
    c_all = jnp.concatenate(
        [c_prompt, c_sample, jnp.zeros((-n_c % (2 * SUBLANES), d), F32)], axis=0)
    mod_all = _mod_call(c_all, w_mod, b_mod)

    tm_p = _tile(sp, 512, SUBLANES)
    mods_p = [_Mod(mod_all[l, :bp].reshape(bp, 1, 6 * d), sp) for l in range(depth)]
    ms = bs * ts
    mods_s = [_Mod(jnp.repeat(mod_all[l, bp:n_c], ts, axis=0).reshape(1, ms, 6 * d), ms)
              for l in range(depth)]

    y_p, conv_p, k_p, v_p = _trunk(x_prompt, mods_p, None, None, None, p, tm_p)
    y_s, conv_s, k_s, v_s = _trunk(x_sample, mods_s, state_conv, cache_win_k, cache_win_v, p, ms)
    return (y_p, y_s, conv_p, k_p, v_p, conv_s, k_s, v_s)
```

```python
import functools
import math

import numpy as np
import jax
import jax.numpy as jnp
from jax import lax
from jax.experimental import pallas as pl
from jax.experimental.pallas import tpu as pltpu

F32 = jnp.float32
BF16 = jnp.bfloat16

CHUNK = 64
WINDOW = 128
HEAD_DIM = 64
MAX_DISTANCE = 128
EPS = 1e-6

V7X_VMEM_BYTES = 64 * 2**20
V7X_VMEM_CAP = V7X_VMEM_BYTES - 8 * 2**20
SUBLANES = 8
LANES = 128


def _nbytes(shape, dtype):
    return math.prod(shape) * jnp.dtype(dtype).itemsize


def _vmem_limit(blocks, scratch=()):
    est = 2 * sum(_nbytes(s, d) for s, d in blocks) + sum(_nbytes(s, d) for s, d in scratch)
    return int(min(V7X_VMEM_CAP, max(32 * 2**20, 2 * est)))


def _tile(dim, pref, mult=LANES):
    if dim <= pref:
        return dim
    t = (pref // mult) * mult
    while t >= mult:
        if dim % t == 0:
            return t
        t -= mult
    return dim


def _sigmoid(x):
    return 1.0 / (1.0 + jnp.exp(-x))


def _mod_body(c_ref, w_ref, b_ref, o_ref):
    c = c_ref[...]
    a = (c * _sigmoid(c)).astype(BF16)
    o_ref[...] = jnp.dot(a, w_ref[...].astype(BF16), preferred_element_type=F32) + b_ref[...]


def _mod_call(c_all, w_mod, b_mod):
    depth, d, n = w_mod.shape
    mp = c_all.shape[0]
    tn = _tile(n, 1024)
    blocks = [((mp, d), F32), ((d, tn), F32), ((1, tn), F32), ((mp, tn), F32)]
    return pl.pallas_call(
        _mod_body,
        out_shape=jax.ShapeDtypeStruct((depth, mp, n), F32),
        grid=(depth, n // tn),
        in_specs=[
            pl.BlockSpec((mp, d), lambda l, j: (0, 0)),
            pl.BlockSpec((None, d, tn), lambda l, j: (l, 0, j)),
            pl.BlockSpec((None, 1, tn), lambda l, j: (l, 0, j)),
        ],
        out_specs=pl.BlockSpec((None, mp, tn), lambda l, j: (l, 0, j)),
        compiler_params=pltpu.CompilerParams(
            dimension_semantics=("arbitrary", "arbitrary"),
            vmem_limit_bytes=_vmem_limit(blocks, [((d, tn), BF16)])),
        name="adaln_mod",
    )(c_all, w_mod, b_mod.reshape(depth, 1, n))


class _Mod:
    def __init__(self, arr, rows_per_group):
        self.arr = arr
        self.rows_per_group = rows_per_group

    def spec(self, seg, d, tm):
        per = self.rows_per_group // tm
        if self.arr.shape[1] == 1:
            return pl.BlockSpec((None, 1, d), lambda i: (i // per, 0, seg))
        return pl.BlockSpec((None, tm, d), lambda i: (i // per, i % per, seg))


def _norm_mod_body(x_ref, g_ref, sh_ref, sc_ref, o_ref):
    x = x_ref[...]
    y = x * lax.rsqrt(jnp.mean(x * x, axis=-1, keepdims=True) + EPS) * g_ref[...]
    o_ref[...] = (y * (1.0 + sc_ref[...]) + sh_ref[...]).astype(o_ref.dtype)


def _norm_mod_call(x, g, mod, seg_shift, seg_scale, tm):
    m, d = x.shape
    r = mod.arr.shape[1]
    blocks = [((tm, d), F32), ((1, d), F32), ((r, d), F32), ((r, d), F32), ((tm, d), BF16)]
    return pl.pallas_call(
        _norm_mod_body,
        out_shape=jax.ShapeDtypeStruct((m, d), BF16),
        grid=(m // tm,),
        in_specs=[
            pl.BlockSpec((tm, d), lambda i: (i, 0)),
            pl.BlockSpec((1, d), lambda i: (0, 0)),
            mod.spec(seg_shift, d, tm),
            mod.spec(seg_scale, d, tm),
        ],
        out_specs=pl.BlockSpec((tm, d), lambda i: (i, 0)),
        compiler_params=pltpu.CompilerParams(
            dimension_semantics=("parallel",), vmem_limit_bytes=_vmem_limit(blocks)),
        name="rmsnorm_modulate",
    )(x, g.reshape(1, d), mod.arr, mod.arr)


def _mm_body(x_ref, *refs, n_w, has_bias, act, scale, out_mode):
    ws = refs[:n_w]
    refs = refs[n_w:]
    bs = refs[:n_w] if has_bias else ()
    refs = refs[len(bs):]
    n_o = 2 if out_mode == "split_t" else 1
    o_refs = refs[:n_o]
    wbs = refs[n_o:]

    @pl.when(pl.program_id(1) == 0)
    def _():
        for w, wb in zip(ws, wbs):
            wb[...] = w[...].astype(BF16)

    x = x_ref[...]
    ps = []
    for i in range(n_w):
        p = jnp.dot(x, wbs[i][...], preferred_element_type=F32)
        if has_bias:
            p = p + bs[i][...]
        ps.append(p)
    if act == "glu":
        y = ps[0] * _sigmoid(ps[1])
    elif act == "swiglu":
        y = ps[0] * _sigmoid(ps[0]) * ps[1]
    else:
        y = ps[0]
    if scale != 1.0:
        y = y * scale
    if out_mode == "plain":
        o_refs[0][...] = y.astype(o_refs[0].dtype)
    elif out_mode == "t":
        o_refs[0][...] = y.T.astype(o_refs[0].dtype)
    else:
        half = y.shape[1] // 2
        o_refs[0][...] = y[:, :half].astype(o_refs[0].dtype)
        o_refs[1][...] = y[:, half:].T.astype(o_refs[1].dtype)


def _mm_call(x, w, layer, col_starts, n_out, *, bias=None, act=None, scale=1.0, out_dtype=F32,
             tm=512, tn=512, name="matmul", out_mode="plain", seq=None):
    m, k = x.shape
    if out_mode == "plain":
        tm = _tile(m, tm, SUBLANES)
    else:
        tm = _tile(seq, tm, LANES)
    tn = _tile(n_out, tn)
    n_w = len(col_starts)
    assert all(c % tn == 0 for c in col_starts) and m % tm == 0 and n_out % tn == 0
    has_bias = bias is not None
    per_b = None if seq is None else seq // tm

    in_specs = [pl.BlockSpec((tm, k), lambda j, i: (i, 0))]
    args = [x]
    blocks = [((tm, k), x.dtype), ((tm, tn), out_dtype)]
    for c in col_starts:
        off = c // tn
        in_specs.append(pl.BlockSpec((None, k, tn), lambda j, i, off=off: (layer, 0, off + j)))
        args.append(w)
        blocks.append(((k, tn), F32))
    if has_bias:
        b3 = bias.reshape(bias.shape[0], 1, bias.shape[1])
        for c in col_starts:
            off = c // tn
            in_specs.append(pl.BlockSpec((None, 1, tn), lambda j, i, off=off: (layer, 0, off + j)))
            args.append(b3)
    scratch = [((k, tn), BF16)] * n_w
    body = functools.partial(_mm_body, n_w=n_w, has_bias=has_bias, act=act, scale=scale,
                             out_mode=out_mode)
    if out_mode == "plain":
        out_shape = jax.ShapeDtypeStruct((m, n_out), out_dtype)
        out_specs = pl.BlockSpec((tm, tn), lambda j, i: (i, j))
    elif out_mode == "t":
        out_shape = jax.ShapeDtypeStruct((m // seq, n_out, seq), out_dtype)
        out_specs = pl.BlockSpec((None, tn, tm), lambda j, i: (i // per_b, j, i % per_b))
    else:
        assert tn == n_out
        half = n_out // 2
        out_shape = [jax.ShapeDtypeStruct((m, half), out_dtype),
                     jax.ShapeDtypeStruct((m // seq, half, seq), out_dtype)]
        out_specs = [pl.BlockSpec((tm, half), lambda j, i: (i, 0)),
                     pl.BlockSpec((None, half, tm), lambda j, i: (i // per_b, 0, i % per_b))]
    return pl.pallas_call(
        body,
        out_shape=out_shape,
        grid=(n_out // tn, m // tm),
        in_specs=in_specs,
        out_specs=out_specs,
        scratch_shapes=[pltpu.VMEM(s, d) for s, d in scratch],
        compiler_params=pltpu.CompilerParams(
            dimension_semantics=("arbitrary", "arbitrary"),
            vmem_limit_bytes=_vmem_limit(blocks, scratch)),
        name=name,
    )(*args)


WEIGHT_CHUNK_ROWS = 256


def _rowmm_body(*refs, layer, kc, has_bias, x_t, final):
    x_ref, w_hbm = refs[:2]
    refs = refs[2:]
    if has_bias:
        b_ref, refs = refs[0], refs[1:]
    res_ref, gate_ref, ng_ref = refs[:3]
    refs = refs[3:]
    if not final:
        sh_ref, sc_ref = refs[:2]
        refs = refs[2:]
    n_o = 1 if final else 2
    o_refs = refs[:n_o]
    w_res, stage, sem = refs[n_o:]
    n_chunks = w_res.shape[0] // kc

    def chunk_copy(c):
        return pltpu.make_async_copy(w_hbm.at[layer, pl.ds(c * kc, kc), :], stage.at[c % 2],
                                     sem.at[c % 2])

    @pl.when(pl.program_id(0) == 0)
    def _():
        chunk_copy(0).start()
        for c in range(n_chunks):
            if c + 1 < n_chunks:
                chunk_copy(c + 1).start()
            chunk_copy(c).wait()
            w_res[c * kc:(c + 1) * kc, :] = stage[c % 2].astype(BF16)

    x = x_ref[...].T if x_t else x_ref[...]
    y = jnp.dot(x, w_res[...], preferred_element_type=F32)
    if has_bias:
        y = y + b_ref[...]
    xn = res_ref[...] + gate_ref[...] * y
    r = xn * lax.rsqrt(jnp.mean(xn * xn, axis=-1, keepdims=True) + EPS) * ng_ref[...]
    if final:
        o_refs[0][...] = r
    else:
        o_refs[0][...] = xn
        o_refs[1][...] = (r * (1.0 + sc_ref[...]) + sh_ref[...]).astype(o_refs[1].dtype)


def _rowmm_call(x, w, layer, res, gate, gate_seg, nxt, *, bias=None, x_t=False, tm=512,
                name="rowmm"):
    if x_t:
        nb, k, seq = x.shape
        m = nb * seq
        row_span = min(gate.rows_per_group, seq)
    else:
        m, k = x.shape
        row_span = gate.rows_per_group
    d = w.shape[2]
    final = len(nxt) == 1
    if not final:
        row_span = min(row_span, nxt[1].rows_per_group)
    kc = _tile(k, WEIGHT_CHUNK_ROWS, SUBLANES)
    scratch = [((k, d), BF16), ((2, kc, d), F32)]
    mod_rows = [gate.arr.shape[1]] + ([] if final else [nxt[1].arr.shape[1]] * 2)

    def vmem_need(rows):
        blocks = [((rows, k), BF16), ((rows, d), F32), ((rows, d), F32 if final else BF16)]
        blocks += [((rows, d), F32)] * (0 if final else 1)
        blocks += [((SUBLANES if r == 1 else rows, d), F32) for r in mod_rows]
        temporaries = 4 * _nbytes((rows, d), F32)
        return (2 * sum(_nbytes(s, t) for s, t in blocks) + sum(_nbytes(s, t) for s, t in scratch)
                + temporaries)

    mult = LANES if x_t else SUBLANES
    tm = _tile(row_span, tm, mult)
    while vmem_need(tm) > V7X_VMEM_CAP and tm % (2 * mult) == 0:
        tm //= 2
    has_bias = bias is not None
    row = lambda i: (i, 0)
    if x_t:
        per_b = seq // tm
        in_specs = [pl.BlockSpec((None, k, tm), lambda i: (i // per_b, 0, i % per_b))]
    else:
        in_specs = [pl.BlockSpec((tm, k), row)]
    in_specs.append(pl.BlockSpec(memory_space=pl.ANY))
    args = [x, w]
    if has_bias:
        in_specs.append(pl.BlockSpec((None, 1, d), lambda i: (layer, 0, 0)))
        args.append(bias.reshape(bias.shape[0], 1, d))
    in_specs += [pl.BlockSpec((tm, d), row), gate.spec(gate_seg, d, tm),
                 pl.BlockSpec((1, d), lambda i: (0, 0))]
    args += [res, gate.arr, nxt[0].reshape(1, d)]
    if final:
        out_shape = jax.ShapeDtypeStruct((m, d), F32)
        out_specs = pl.BlockSpec((tm, d), row)
    else:
        _, nmod, seg_shift, seg_scale = nxt
        in_specs += [nmod.spec(seg_shift, d, tm), nmod.spec(seg_scale, d, tm)]
        args += [nmod.arr, nmod.arr]
        out_shape = [jax.ShapeDtypeStruct((m, d), F32), jax.ShapeDtypeStruct((m, d), BF16)]
        out_specs = [pl.BlockSpec((tm, d), row), pl.BlockSpec((tm, d), row)]
    return pl.pallas_call(
        functools.partial(_rowmm_body, layer=layer, kc=kc, has_bias=has_bias, x_t=x_t,
                          final=final),
        out_shape=out_shape,
        grid=(m // tm,),
        in_specs=in_specs,
        out_specs=out_specs,
        scratch_shapes=[pltpu.VMEM(s, dt) for s, dt in scratch] + [pltpu.SemaphoreType.DMA((2,))],
        compiler_params=pltpu.CompilerParams(
            dimension_semantics=("arbitrary",),
            vmem_limit_bytes=min(V7X_VMEM_CAP, max(32 * 2**20, vmem_need(tm) + 4 * 2**20))),
        name=name,
    )(*args)


def _conv_ln_swish(up_ref, pad, tt, w_ref, bdw_ref, lg_ref, lb_ref, acc_ref, o_ref):
    kw, d = w_ref.shape
    by_shift = [[(a, SUBLANES * a + s - pad) for a in range((pad + kw - 1) // SUBLANES + 1)
                 if 0 <= SUBLANES * a + s - pad < kw] for s in range(SUBLANES)]

    def strip(c, carry):
        cols = pl.ds(pl.multiple_of(c * LANES, LANES), LANES)
        z = bdw_ref[:, cols]
        for s, taps in enumerate(by_shift):
            n = tt + SUBLANES if s else tt
            q = None
            for a, k in taps:
                term = up_ref[pl.ds(SUBLANES * a, n), cols] * w_ref[pl.ds(k, 1), cols]
                q = term if q is None else q + term
            if q is not None:
                z = z + q[s:s + tt]
        acc_ref[:, cols] = z
        return carry

    lax.fori_loop(0, d // LANES, strip, 0)
    z = acc_ref[...]
    mu = jnp.mean(z, axis=-1, keepdims=True)
    zc = z - mu
    y = zc * lax.rsqrt(jnp.mean(zc * zc, axis=-1, keepdims=True) + EPS)
    y = y * lg_ref[...] + lb_ref[...]
    o_ref[...] = (y * _sigmoid(y)).astype(o_ref.dtype)


def _conv_prompt_body(main_ref, halo_ref, w_ref, bdw_ref, lg_ref, lb_ref, o_ref, up_ref, acc_ref,
                      *, halo, tt):
    kw = w_ref.shape[0]

    @pl.when(pl.program_id(1) == 0)
    def _():
        up_ref[0:halo, :] = jnp.zeros((halo, up_ref.shape[1]), F32)

    @pl.when(pl.program_id(1) > 0)
    def _():
        up_ref[0:halo, :] = halo_ref[...]

    up_ref[halo:halo + tt, :] = main_ref[...]
    _conv_ln_swish(up_ref, halo - (kw - 1), tt, w_ref, bdw_ref, lg_ref, lb_ref, acc_ref, o_ref)


def _conv_prompt_call(u, w_dw, b_dw, ln_g, ln_b, layer, tt):
    b, t, d = u.shape
    kw = w_dw.shape[1]
    halo = -(-(kw - 1) // SUBLANES) * SUBLANES
    tt = _tile(t, tt, halo)
    hb = tt // halo
    vec = lambda: pl.BlockSpec((None, 1, d), lambda bi, ti: (layer, 0, 0))
    blocks = [((tt, d), F32), ((halo, d), F32), ((kw, d), F32), ((tt, d), BF16)]
    scratch = [((halo + tt, d), F32), ((tt, d), F32)]
    return pl.pallas_call(
        functools.partial(_conv_prompt_body, halo=halo, tt=tt),
        out_shape=jax.ShapeDtypeStruct((b, t, d), BF16),
        grid=(b, t // tt),
        in_specs=[
            pl.BlockSpec((None, tt, d), lambda bi, ti: (bi, ti, 0)),
            pl.BlockSpec((None, halo, d), lambda bi, ti: (bi, jnp.maximum(ti * hb - 1, 0), 0)),
            pl.BlockSpec((None, kw, d), lambda bi, ti: (layer, 0, 0)),
            vec(), vec(), vec(),
        ],
        out_specs=pl.BlockSpec((None, tt, d), lambda bi, ti: (bi, ti, 0)),
        scratch_shapes=[pltpu.VMEM(s, dt) for s, dt in scratch],
        compiler_params=pltpu.CompilerParams(
            dimension_semantics=("parallel", "arbitrary"),
            vmem_limit_bytes=_vmem_limit(blocks, scratch)),
        name="dwconv_ln_swish_prompt",
    )(u, u, w_dw, b_dw.reshape(-1, 1, d), ln_g.reshape(-1, 1, d), ln_b.reshape(-1, 1, d))


def _conv_sample_body(up_ref, w_ref, bdw_ref, lg_ref, lb_ref, o_ref, acc_ref, *, pad, tt):
    _conv_ln_swish(up_ref, pad, tt, w_ref, bdw_ref, lg_ref, lb_ref, acc_ref, o_ref)


def _conv_sample_call(up, pad, tt, w_dw, b_dw, ln_g, ln_b, layer):
    b, rows, d = up.shape
    kw = w_dw.shape[1]
    vec = lambda: pl.BlockSpec((None, 1, d), lambda bi: (layer, 0, 0))
    blocks = [((rows, d), F32), ((kw, d), F32), ((tt, d), BF16)]
    scratch = [((tt, d), F32)]
    return pl.pallas_call(
        functools.partial(_conv_sample_body, pad=pad, tt=tt),
        out_shape=jax.ShapeDtypeStruct((b, tt, d), BF16),
        grid=(b,),
        in_specs=[
            pl.BlockSpec((None, rows, d), lambda bi: (bi, 0, 0)),
            pl.BlockSpec((None, kw, d), lambda bi: (layer, 0, 0)),
            vec(), vec(), vec(),
        ],
        out_specs=pl.BlockSpec((None, tt, d), lambda bi: (bi, 0, 0)),
        scratch_shapes=[pltpu.VMEM(s, dt) for s, dt in scratch],
        compiler_params=pltpu.CompilerParams(
            dimension_semantics=("parallel",), vmem_limit_bytes=_vmem_limit(blocks, scratch)),
        name="dwconv_ln_swish_sample",
    )(up, w_dw, b_dw.reshape(-1, 1, d), ln_g.reshape(-1, 1, d), ln_b.reshape(-1, 1, d))


def _bucket_codes(n_q, n_k, k_off, n_buckets):
    rel = (np.arange(n_k) + k_off)[None, :] - np.arange(n_q)[:, None]
    nb = n_buckets // 2
    max_exact = nb // 2
    ret = np.where(rel > 0, nb, 0)
    n = np.abs(rel)
    nf = np.maximum(n, 1).astype(np.float32)
    large = max_exact + (np.log(nf / np.float32(max_exact))
                         / np.float32(math.log(MAX_DISTANCE / max_exact))
                         * np.float32(nb - max_exact)).astype(np.int32)
    large = np.minimum(large, nb - 1)
    return (ret + np.where(n < max_exact, n, large)).astype(np.int32)


def _bias_body(code_ref, table_ref, o_ref, *, n_buckets):
    code = code_ref[...]

    def head(h, carry):
        out = jnp.full(code.shape, -jnp.inf, F32)
        for b in range(n_buckets):
            out = jnp.where(code == b, table_ref[h, b], out)
        o_ref[h] = out
        return carry

    lax.fori_loop(0, o_ref.shape[0], head, 0)


def _bias_call(code, table):
    n_heads, n_buckets = table.shape
    return pl.pallas_call(
        functools.partial(_bias_body, n_buckets=n_buckets),
        out_shape=jax.ShapeDtypeStruct((n_heads,) + code.shape, F32),
        in_specs=[pl.BlockSpec(memory_space=pltpu.VMEM), pl.BlockSpec(memory_space=pltpu.SMEM)],
        out_specs=pl.BlockSpec(memory_space=pltpu.VMEM),
        name="rel_bias",
    )(jnp.asarray(code), table)


def _sink_attention_unit(qs, kh, vh, bias, sink_col, maskrow):
    s = lax.dot_general(qs, kh, (((1,), (1,)), ((), ())), preferred_element_type=F32) + bias
    if maskrow is not None:
        s = s + maskrow
    m = jnp.maximum(jnp.max(s, axis=-1, keepdims=True), sink_col)
    e = jnp.exp(s - m)
    l = jnp.sum(e, axis=-1, keepdims=True) + jnp.exp(sink_col - m)
    return jnp.dot(e.astype(BF16), vh, preferred_element_type=F32) * (1.0 / l)


def _stack_heads(q, h, group):
    hd = HEAD_DIM
    return jnp.concatenate(
        [q[:, (h * group + g) * hd:(h * group + g + 1) * hd] for g in range(group)], axis=0)


def _unstack_heads(o, group):
    tq = o.shape[0] // group
    return [o[g * tq:(g + 1) * tq, :] for g in range(group)]


ATTN_Q_COLS = 256
PAIR = 2 * CHUNK


def _pair_codes(n_buckets):
    code = _bucket_codes(PAIR, WINDOW + PAIR, -WINDOW, n_buckets).T
    key_chunk = np.arange(WINDOW + PAIR)[:, None] // CHUNK
    q_chunk = np.arange(PAIR)[None, :] // CHUNK
    visible = (key_chunk >= q_chunk) & (key_chunk <= q_chunk + WINDOW // CHUNK)
    return np.where(visible, code, n_buckets).astype(np.int32)


def _attn_prompt_body(qT_ref, kp_ref, kc_ref, vTp_ref, vTc_ref, bias_ref, sink_ref, oT_ref,
                      k_scr, vT_scr, *, kvh, group):
    i = pl.program_id(1)
    tq = qT_ref.shape[1]
    nk = WINDOW + PAIR
    hd = HEAD_DIM
    for h in range(kvh):
        k_scr[h, 0:WINDOW, :] = kp_ref[:, h * hd:(h + 1) * hd].astype(BF16)
        k_scr[h, WINDOW:WINDOW + tq, :] = kc_ref[:, h * hd:(h + 1) * hd].astype(BF16)
    vT_scr[:, 0:WINDOW] = vTp_ref[...].astype(BF16)
    vT_scr[:, WINDOW:WINDOW + tq] = vTc_ref[...].astype(BF16)
    row = lax.broadcasted_iota(jnp.int32, (nk, group * PAIR), 0)
    for p in range(tq // PAIR):
        c0 = p * PAIR
        start_mask = None
        if p == 0:
            start_mask = jnp.where(row < jnp.where(i == 0, WINDOW, 0), -jnp.inf, 0.0).astype(F32)
        for h in range(kvh):
            heads = range(h * group, (h + 1) * group)
            qsT = jnp.concatenate(
                [qT_ref[hh * hd:(hh + 1) * hd, c0:c0 + PAIR] for hh in heads], axis=1)
            s = jnp.dot(k_scr[h, c0:c0 + nk, :], qsT, preferred_element_type=F32) + bias_ref[h]
            if start_mask is not None:
                s = s + start_mask
            sink = sink_ref[h]
            m = jnp.maximum(jnp.max(s, axis=0, keepdims=True), sink)
            e = jnp.exp(s - m)
            l = jnp.sum(e, axis=0, keepdims=True) + jnp.exp(sink - m)
            oT = jnp.dot(vT_scr[h * hd:(h + 1) * hd, c0:c0 + nk], e.astype(BF16),
                         preferred_element_type=F32) * (1.0 / l)
            for g, hh in enumerate(heads):
                oT_ref[hh * hd:(hh + 1) * hd, c0:c0 + PAIR] = (
                    oT[:, g * PAIR:(g + 1) * PAIR].astype(oT_ref.dtype))


def _attn_prompt_call(qT, k, vT, bias, sink_row, kvh, group):
    b, dq, s = qT.shape
    dk = k.shape[2]
    tq = _tile(s, ATTN_Q_COLS, PAIR)
    per = tq // WINDOW
    prev = lambda i: jnp.maximum(i * per - 1, 0)
    blocks = [((dq, tq), BF16)] * 2 + [((WINDOW + tq, dk), F32)] * 2 + [
        (bias.shape, F32), (sink_row.shape[:1] + (SUBLANES, sink_row.shape[2]), F32)]
    scratch = [((kvh, WINDOW + tq, LANES), BF16), ((dk, WINDOW + tq), BF16)]
    return pl.pallas_call(
        functools.partial(_attn_prompt_body, kvh=kvh, group=group),
        out_shape=jax.ShapeDtypeStruct((b, dq, s), BF16),
        grid=(b, s // tq),
        in_specs=[
            pl.BlockSpec((None, dq, tq), lambda bi, i: (bi, 0, i)),
            pl.BlockSpec((None, WINDOW, dk), lambda bi, i: (bi, prev(i), 0)),
            pl.BlockSpec((None, tq, dk), lambda bi, i: (bi, i, 0)),
            pl.BlockSpec((None, dk, WINDOW), lambda bi, i: (bi, 0, prev(i))),
            pl.BlockSpec((None, dk, tq), lambda bi, i: (bi, 0, i)),
            pl.BlockSpec(bias.shape, lambda bi, i: (0, 0, 0)),
            pl.BlockSpec(sink_row.shape, lambda bi, i: (0, 0, 0)),
        ],
        out_specs=pl.BlockSpec((None, dq, tq), lambda bi, i: (bi, 0, i)),
        scratch_shapes=[pltpu.VMEM((kvh, WINDOW + tq, HEAD_DIM), BF16),
                        pltpu.VMEM((dk, WINDOW + tq), BF16)],
        compiler_params=pltpu.CompilerParams(
            dimension_semantics=("parallel", "arbitrary"),
            vmem_limit_bytes=_vmem_limit(blocks, scratch)),
        name="swa_sink_attention_prompt",
    )(qT, k, k, vT, vT, bias, sink_row)


def _attn_sample_body(q_ref, k_ref, v_ref, bias_ref, sink_ref, o_ref, *, kvh, group):
    hd = HEAD_DIM
    q = q_ref[...].astype(F32)
    outs = []
    for h in range(kvh):
        o = _sink_attention_unit(
            _stack_heads(q, h, group).astype(BF16),
            k_ref[:, h * hd:(h + 1) * hd].astype(BF16), v_ref[:, h * hd:(h + 1) * hd].astype(BF16),
            bias_ref[h], sink_ref[h], None)
        outs += _unstack_heads(o, group)
    o_ref[...] = jnp.concatenate(outs, axis=1).astype(o_ref.dtype)


def _attn_sample_call(q, k_all, v_all, bias, sink_col, kvh, group):
    b, t, dq = q.shape
    nk, dk = k_all.shape[1:]
    blocks = [((t, dq), BF16)] * 2 + [((nk, dk), F32)] * 2 + [
        (bias.shape[:2] + (2 * LANES,), F32), (sink_col.shape[:2] + (LANES,), F32)]
    return pl.pallas_call(
        functools.partial(_attn_sample_body, kvh=kvh, group=group),
        out_shape=jax.ShapeDtypeStruct((b, t, dq), BF16),
        grid=(b,),
        in_specs=[
            pl.BlockSpec((None, t, dq), lambda bi: (bi, 0, 0)),
            pl.BlockSpec((None, nk, dk), lambda bi: (bi, 0, 0)),
            pl.BlockSpec((None, nk, dk), lambda bi: (bi, 0, 0)),
            pl.BlockSpec(bias.shape, lambda bi: (0, 0, 0)),
            pl.BlockSpec(sink_col.shape, lambda bi: (0, 0, 0)),
        ],
        out_specs=pl.BlockSpec((None, t, dq), lambda bi: (bi, 0, 0)),
        compiler_params=pltpu.CompilerParams(
            dimension_semantics=("parallel",), vmem_limit_bytes=_vmem_limit(blocks)),
        name="swa_sink_attention_sample",
    )(q, k_all, v_all, bias, sink_col)


def _trunk(x3, mods, state_conv, win_k, win_v, p, tm):
    b, t, d = x3.shape
    m = b * t
    x = x3.reshape(m, d)
    depth = p["w_mod"].shape[0]
    d_ff = p["w_gu"].shape[2] // 2
    n_heads = p["attn_sinks"].shape[1]
    dq = n_heads * HEAD_DIM
    kvh = (p["w_qkv"].shape[2] - dq) // (2 * HEAD_DIM)
    group = n_heads // kvh
    dkv = kvh * HEAD_DIM
    kw = p["w_dw"].shape[1]
    n_buckets = p["rel_bias_table"].shape[1]
    new_conv, new_k, new_v = [], [], []

    h = _norm_mod_call(x, p["norm_mix"][0], mods[0], 0, 1, tm)
    for i in range(depth):
        mod = mods[i]
        j = i // 2
        ffn_norm = (p["norm_ffn"][i], mod, 3, 4)
        if i % 2 == 0:
            u = _mm_call(h, p["w_pw1"], j, (0, d), d, bias=p["b_pw1"], act="glu",
                         tm=1024, tn=512, name="pw1_glu")
            u3 = u.reshape(b, t, d)
            if state_conv is None:
                z = _conv_prompt_call(u3, p["w_dw"], p["b_dw"], p["conv_ln_g"], p["conv_ln_b"],
                                      j, 128)
                new_conv.append(u3[:, t - (kw - 1):])
            else:
                st = state_conv[j]
                pad = (-(kw - 1 + t)) % SUBLANES
                up = jnp.concatenate([jnp.zeros((b, pad, d), F32), st, u3], axis=1)
                z = _conv_sample_call(up, pad, t, p["w_dw"], p["b_dw"], p["conv_ln_g"],
                                      p["conv_ln_b"], j)
                new_conv.append(up[:, pad + t:])
            x, h = _rowmm_call(z.reshape(m, d), p["w_pw2"], j, x, mod, 2, ffn_norm,
                               bias=p["b_pw2"], tm=512, name="pw2_residual_norm")
        else:
            sinks = p["attn_sinks"][j].reshape(kvh, group, 1)
            prompt = win_k is None
            q = _mm_call(h, p["w_qkv"], j, (0,), dq, bias=p["b_qkv"], scale=HEAD_DIM ** -0.5,
                         out_dtype=BF16, tm=1024, tn=512, name="q_proj",
                         out_mode="t" if prompt else "plain", seq=t)
            kv = _mm_call(h, p["w_qkv"], j, (dq,), 2 * dkv, bias=p["b_qkv"],
                          tm=1024, tn=2 * dkv, name="kv_proj",
                          out_mode="split_t" if prompt else "plain", seq=t)
            if prompt:
                k, vT = kv[0].reshape(b, t, dkv), kv[1]
                nk = WINDOW + PAIR
                bias = _bias_call(_pair_codes(n_buckets), p["rel_bias_table"])
                bias = bias.reshape(kvh, group, nk, PAIR).transpose(0, 2, 1, 3)
                sink_row = jnp.repeat(sinks, PAIR, axis=2).reshape(kvh, 1, group * PAIR)
                o = _attn_prompt_call(q, k, vT, bias.reshape(kvh, nk, group * PAIR), sink_row,
                                      kvh, group)
                new_k.append(k[:, t - WINDOW:].reshape(b, WINDOW, kvh, HEAD_DIM))
                new_v.append(jnp.swapaxes(vT[:, :, t - WINDOW:], 1, 2)
                             .reshape(b, WINDOW, kvh, HEAD_DIM))
            else:
                kv3 = kv.reshape(b, t, 2, kvh, HEAD_DIM)
                k_all = jnp.concatenate([win_k[j], kv3[:, :, 0]], axis=1)
                v_all = jnp.concatenate([win_v[j], kv3[:, :, 1]], axis=1)
                n_keys = WINDOW + t
                bias = _bias_call(_bucket_codes(t, n_keys, -WINDOW, n_buckets),
                                  p["rel_bias_table"]).reshape(kvh, group * t, n_keys)
                sink_col = jnp.repeat(sinks, t, axis=1)
                o = _attn_sample_call(q.reshape(b, t, dq), k_all.reshape(b, n_keys, dkv),
                                      v_all.reshape(b, n_keys, dkv), bias, sink_col, kvh, group)
                new_k.append(k_all[:, t:])
                new_v.append(v_all[:, t:])
            x, h = _rowmm_call(o if prompt else o.reshape(m, dq), p["w_o"], j, x, mod, 2, ffn_norm,
                               bias=p["b_o"], x_t=prompt, tm=512, name="wo_residual_norm")
        a = _mm_call(h, p["w_gu"], i, (0, d_ff), d_ff, act="swiglu", out_dtype=BF16,
                     tm=1024, tn=512, name="ffn_gate_up")
        if i + 1 < depth:
            x, h = _rowmm_call(a, p["w_down"], i, x, mod, 5,
                               (p["norm_mix"][i + 1], mods[i + 1], 0, 1), tm=256,
                               name="ffn_down_residual_norm")
        else:
            y = _rowmm_call(a, p["w_down"], i, x, mod, 5, (p["norm_out"],), tm=256,
                            name="ffn_down_residual_final_norm")
    return y.reshape(b, t, d), jnp.stack(new_conv), jnp.stack(new_k), jnp.stack(new_v)


def kernel(x_prompt, x_sample, c_prompt, c_sample, state_conv, cache_win_k, cache_win_v, w_mod, b_mod, norm_mix, norm_ffn, w_pw1, b_pw1, w_dw, b_dw, conv_ln_g, conv_ln_b, w_pw2, b_pw2, w_qkv, b_qkv, w_o, b_o, attn_sinks, rel_bias_table, w_gu, w_down, norm_out):
    p = dict(w_mod=w_mod, b_mod=b_mod, norm_mix=norm_mix, norm_ffn=norm_ffn, w_pw1=w_pw1,
             b_pw1=b_pw1, w_dw=w_dw, b_dw=b_dw, conv_ln_g=conv_ln_g, conv_ln_b=conv_ln_b,
             w_pw2=w_pw2, b_pw2=b_pw2, w_qkv=w_qkv, b_qkv=b_qkv, w_o=w_o, b_o=b_o,
             attn_sinks=attn_sinks, rel_bias_table=rel_bias_table, w_gu=w_gu, w_down=w_down,
             norm_out=norm_out)
    bp, sp, d = x_prompt.shape
    bs, ts, _ = x_sample.shape
    depth = w_mod.shape[0]

    n_c = bp + bs
    c_all = jnp.concatenate(
        [c_prompt, c_sample, jnp.zeros((-n_c % (2 * SUBLANES), d), F32)], axis=0)
    mod_all = _mod_call(c_all, w_mod, b_mod)

    tm_p = _tile(sp, 512, SUBLANES)
    mods_p = [_Mod(mod_all[l, :bp].reshape(bp, 1, 6 * d), sp) for l in range(depth)]
    ms = bs * ts
    mods_s = [_Mod(jnp.repeat(mod_all[l, bp:n_c], ts, axis=0).reshape(1, ms, 6 * d), ms)
              for l in range(depth)]

    y_p, conv_p, k_p, v_p = _trunk(x_prompt, mods_p, None, None, None, p, tm_p)
    y_s, conv_s, k_s, v_s = _trunk(x_sample, mods_s, state_conv, cache_win_k, cache_win_v, p, ms)
    return (y_p, y_s, conv_p, k_p, v_p, conv_s, k_s, v_s)
```

```python
import functools
import math

import numpy as np
import jax
import jax.numpy as jnp
from jax import lax
from jax.experimental import pallas as pl
from jax.experimental.pallas import tpu as pltpu

F32 = jnp.float32
BF16 = jnp.bfloat16

CHUNK = 64
WINDOW = 128
HEAD_DIM = 64
MAX_DISTANCE = 128
EPS = 1e-6

V7X_VMEM_BYTES = 64 * 2**20
V7X_VMEM_CAP = V7X_VMEM_BYTES - 8 * 2**20
SUBLANES = 8
LANES = 128


def _nbytes(shape, dtype):
    return math.prod(shape) * jnp.dtype(dtype).itemsize


def _vmem_limit(blocks, scratch=()):
    est = 2 * sum(_nbytes(s, d) for s, d in blocks) + sum(_nbytes(s, d) for s, d in scratch)
    return int(min(V7X_VMEM_CAP, max(32 * 2**20, 2 * est)))


def _tile(dim, pref, mult=LANES):
    if dim <= pref:
        return dim
    t = (pref // mult) * mult
    while t >= mult:
        if dim % t == 0:
            return t
        t -= mult
    return dim


def _sigmoid(x):
    return 1.0 / (1.0 + jnp.exp(-x))


def _mod_body(c_ref, w_ref, b_ref, o_ref):
    c = c_ref[...]
    a = (c * _sigmoid(c)).astype(BF16)
    o_ref[...] = jnp.dot(a, w_ref[...].astype(BF16), preferred_element_type=F32) + b_ref[...]


def _mod_call(c_all, w_mod, b_mod):
    depth, d, n = w_mod.shape
    mp = c_all.shape[0]
    tn = _tile(n, 1024)
    blocks = [((mp, d), F32), ((d, tn), F32), ((1, tn), F32), ((mp, tn), F32)]
    return pl.pallas_call(
        _mod_body,
        out_shape=jax.ShapeDtypeStruct((depth, mp, n), F32),
        grid=(depth, n // tn),
        in_specs=[
            pl.BlockSpec((mp, d), lambda l, j: (0, 0)),
            pl.BlockSpec((None, d, tn), lambda l, j: (l, 0, j)),
            pl.BlockSpec((None, 1, tn), lambda l, j: (l, 0, j)),
        ],
        out_specs=pl.BlockSpec((None, mp, tn), lambda l, j: (l, 0, j)),
        compiler_params=pltpu.CompilerParams(
            dimension_semantics=("arbitrary", "arbitrary"),
            vmem_limit_bytes=_vmem_limit(blocks, [((d, tn), BF16)])),
        name="adaln_mod",
    )(c_all, w_mod, b_mod.reshape(depth, 1, n))


class _Mod:
    def __init__(self, arr, rows_per_group):
        self.arr = arr
        self.rows_per_group = rows_per_group

    def spec(self, seg, d, tm):
        per = self.rows_per_group // tm
        if self.arr.shape[1] == 1:
            return pl.BlockSpec((None, 1, d), lambda i: (i // per, 0, seg))
        return pl.BlockSpec((None, tm, d), lambda i: (i // per, i % per, seg))


def _norm_mod_body(x_ref, g_ref, sh_ref, sc_ref, o_ref):
    x = x_ref[...]
    y = x * lax.rsqrt(jnp.mean(x * x, axis=-1, keepdims=True) + EPS) * g_ref[...]
    o_ref[...] = (y * (1.0 + sc_ref[...]) + sh_ref[...]).astype(o_ref.dtype)


def _norm_mod_call(x, g, mod, seg_shift, seg_scale, tm):
    m, d = x.shape
    r = mod.arr.shape[1]
    blocks = [((tm, d), F32), ((1, d), F32), ((r, d), F32), ((r, d), F32), ((tm, d), BF16)]
    return pl.pallas_call(
        _norm_mod_body,
        out_shape=jax.ShapeDtypeStruct((m, d), BF16),
        grid=(m // tm,),
        in_specs=[
            pl.BlockSpec((tm, d), lambda i: (i, 0)),
            pl.BlockSpec((1, d), lambda i: (0, 0)),
            mod.spec(seg_shift, d, tm),
            mod.spec(seg_scale, d, tm),
        ],
        out_specs=pl.BlockSpec((tm, d), lambda i: (i, 0)),
        compiler_params=pltpu.CompilerParams(
            dimension_semantics=("parallel",), vmem_limit_bytes=_vmem_limit(blocks)),
        name="rmsnorm_modulate",
    )(x, g.reshape(1, d), mod.arr, mod.arr)


def _mm_body(x_ref, *refs, n_w, has_bias, act, scale, out_mode, has_x2):
    if has_x2:
        x2_ref, refs = refs[0], refs[1:]
    ws = refs[:n_w]
    refs = refs[n_w:]
    bs = refs[:n_w] if has_bias else ()
    refs = refs[len(bs):]
    n_o = 2 if out_mode == "split_t" else 1
    o_refs = refs[:n_o]
    refs = refs[n_o:]
    if has_x2:
        o2_ref, refs = refs[0], refs[1:]
    wbs = refs

    def compute(x):
        ps = []
        for i in range(n_w):
            p = jnp.dot(x, wbs[i][...], preferred_element_type=F32)
            if has_bias:
                p = p + bs[i][...]
            ps.append(p)
        if act == "glu":
            y = ps[0] * _sigmoid(ps[1])
        elif act == "swiglu":
            y = ps[0] * _sigmoid(ps[0]) * ps[1]
        else:
            y = ps[0]
        return y * scale if scale != 1.0 else y

    @pl.when(pl.program_id(1) == 0)
    def _():
        for w, wb in zip(ws, wbs):
            wb[...] = w[...].astype(BF16)
        if has_x2:
            o2_ref[...] = compute(x2_ref[...]).astype(o2_ref.dtype)

    y = compute(x_ref[...])
    if out_mode == "plain":
        o_refs[0][...] = y.astype(o_refs[0].dtype)
    elif out_mode == "t":
        o_refs[0][...] = y.T.astype(o_refs[0].dtype)
    else:
        half = y.shape[1] // 2
        o_refs[0][...] = y[:, :half].astype(o_refs[0].dtype)
        o_refs[1][...] = y[:, half:].T.astype(o_refs[1].dtype)


def _mm_call(x, w, layer, col_starts, n_out, *, x2=None, bias=None, act=None, scale=1.0,
             out_dtype=F32, tm=512, tn=512, name="matmul", out_mode="plain", seq=None):
    m, k = x.shape
    if out_mode == "plain":
        tm = _tile(m, tm, SUBLANES)
    else:
        tm = _tile(seq, tm, LANES)
    tn = _tile(n_out, tn)
    n_w = len(col_starts)
    assert all(c % tn == 0 for c in col_starts) and m % tm == 0 and n_out % tn == 0
    has_bias = bias is not None
    per_b = None if seq is None else seq // tm

    in_specs = [pl.BlockSpec((tm, k), lambda j, i: (i, 0))]
    args = [x]
    blocks = [((tm, k), x.dtype), ((tm, tn), out_dtype)]
    if x2 is not None:
        m2 = x2.shape[0]
        in_specs.append(pl.BlockSpec((m2, k), lambda j, i: (0, 0)))
        args.append(x2)
        blocks += [((m2, k), x2.dtype), ((m2, tn), out_dtype)]
    for c in col_starts:
        off = c // tn
        in_specs.append(pl.BlockSpec((None, k, tn), lambda j, i, off=off: (layer, 0, off + j)))
        args.append(w)
        blocks.append(((k, tn), F32))
    if has_bias:
        b3 = bias.reshape(bias.shape[0], 1, bias.shape[1])
        for c in col_starts:
            off = c // tn
            in_specs.append(pl.BlockSpec((None, 1, tn), lambda j, i, off=off: (layer, 0, off + j)))
            args.append(b3)
    scratch = [((k, tn), BF16)] * n_w
    body = functools.partial(_mm_body, n_w=n_w, has_bias=has_bias, act=act, scale=scale,
                             out_mode=out_mode, has_x2=x2 is not None)
    if out_mode == "plain":
        out_shape = [jax.ShapeDtypeStruct((m, n_out), out_dtype)]
        out_specs = [pl.BlockSpec((tm, tn), lambda j, i: (i, j))]
    elif out_mode == "t":
        out_shape = [jax.ShapeDtypeStruct((m // seq, n_out, seq), out_dtype)]
        out_specs = [pl.BlockSpec((None, tn, tm), lambda j, i: (i // per_b, j, i % per_b))]
    else:
        assert tn == n_out
        half = n_out // 2
        out_shape = [jax.ShapeDtypeStruct((m, half), out_dtype),
                     jax.ShapeDtypeStruct((m // seq, half, seq), out_dtype)]
        out_specs = [pl.BlockSpec((tm, half), lambda j, i: (i, 0)),
                     pl.BlockSpec((None, half, tm), lambda j, i: (i // per_b, 0, i % per_b))]
    if x2 is not None:
        out_shape.append(jax.ShapeDtypeStruct((m2, n_out), out_dtype))
        out_specs.append(pl.BlockSpec((m2, tn), lambda j, i: (0, j)))
    outs = pl.pallas_call(
        body,
        out_shape=out_shape,
        grid=(n_out // tn, m // tm),
        in_specs=in_specs,
        out_specs=out_specs,
        scratch_shapes=[pltpu.VMEM(s, d) for s, d in scratch],
        compiler_params=pltpu.CompilerParams(
            dimension_semantics=("arbitrary", "arbitrary"),
            vmem_limit_bytes=_vmem_limit(blocks, scratch)),
        name=name,
    )(*args)
    n_primary = 2 if out_mode == "split_t" else 1
    primary = outs[0] if n_primary == 1 else outs[:n_primary]
    return primary if x2 is None else (primary, outs[n_primary])


WEIGHT_CHUNK_ROWS = 256
ROWMM_SUB_ROWS = 256


def _rowmm_body(*refs, layer, kc, has_bias, x_t, final):
    x_ref, w_hbm = refs[:2]
    refs = refs[2:]
    if has_bias:
        b_ref, refs = refs[0], refs[1:]
    res_ref, gate_ref, ng_ref = refs[:3]
    refs = refs[3:]
    if not final:
        sh_ref, sc_ref = refs[:2]
        refs = refs[2:]
    n_o = 1 if final else 2
    o_refs = refs[:n_o]
    w_res, stage, sem = refs[n_o:]
    n_chunks = w_res.shape[0] // kc

    def chunk_copy(c):
        return pltpu.make_async_copy(w_hbm.at[layer, pl.ds(c * kc, kc), :], stage.at[c % 2],
                                     sem.at[c % 2])

    @pl.when(pl.program_id(0) == 0)
    def _():
        chunk_copy(0).start()
        for c in range(n_chunks):
            if c + 1 < n_chunks:
                chunk_copy(c + 1).start()
            chunk_copy(c).wait()
            w_res[c * kc:(c + 1) * kc, :] = stage[c % 2].astype(BF16)

    tm = res_ref.shape[0]
    sub = min(tm, ROWMM_SUB_ROWS)

    def rows_of(ref, r0):
        return ref[...] if ref.shape[0] == 1 else ref[r0:r0 + sub, :]

    for r0 in range(0, tm, sub):
        x = x_ref[:, r0:r0 + sub].T if x_t else x_ref[r0:r0 + sub, :]
        y = jnp.dot(x, w_res[...], preferred_element_type=F32)
        if has_bias:
            y = y + b_ref[...]
        xn = res_ref[r0:r0 + sub, :] + rows_of(gate_ref, r0) * y
        r = xn * lax.rsqrt(jnp.mean(xn * xn, axis=-1, keepdims=True) + EPS) * ng_ref[...]
        if final:
            o_refs[0][r0:r0 + sub, :] = r
        else:
            o_refs[0][r0:r0 + sub, :] = xn
            o_refs[1][r0:r0 + sub, :] = (
                r * (1.0 + rows_of(sc_ref, r0)) + rows_of(sh_ref, r0)).astype(o_refs[1].dtype)


def _rowmm_call(x, w, layer, res, gate, gate_seg, nxt, *, bias=None, x_t=False, tm=512,
                name="rowmm"):
    if x_t:
        nb, k, seq = x.shape
        m = nb * seq
        row_span = min(gate.rows_per_group, seq)
    else:
        m, k = x.shape
        row_span = gate.rows_per_group
    d = w.shape[2]
    final = len(nxt) == 1
    if not final:
        row_span = min(row_span, nxt[1].rows_per_group)
    kc = _tile(k, WEIGHT_CHUNK_ROWS, SUBLANES)
    scratch = [((k, d), BF16), ((2, kc, d), F32)]
    mod_rows = [gate.arr.shape[1]] + ([] if final else [nxt[1].arr.shape[1]] * 2)

    def vmem_need(rows):
        blocks = [((rows, k), BF16), ((rows, d), F32), ((rows, d), F32 if final else BF16)]
        blocks += [((rows, d), F32)] * (0 if final else 1)
        blocks += [((SUBLANES if r == 1 else rows, d), F32) for r in mod_rows]
        temporaries = 4 * _nbytes((rows, d), F32)
        return (2 * sum(_nbytes(s, t) for s, t in blocks) + sum(_nbytes(s, t) for s, t in scratch)
                + temporaries)

    mult = LANES if x_t else SUBLANES
    tm = _tile(row_span, tm, mult)
    while vmem_need(tm) > V7X_VMEM_CAP and tm % (2 * mult) == 0:
        tm //= 2
    has_bias = bias is not None
    row = lambda i: (i, 0)
    if x_t:
        per_b = seq // tm
        in_specs = [pl.BlockSpec((None, k, tm), lambda i: (i // per_b, 0, i % per_b))]
    else:
        in_specs = [pl.BlockSpec((tm, k), row)]
    in_specs.append(pl.BlockSpec(memory_space=pl.ANY))
    args = [x, w]
    if has_bias:
        in_specs.append(pl.BlockSpec((None, 1, d), lambda i: (layer, 0, 0)))
        args.append(bias.reshape(bias.shape[0], 1, d))
    in_specs += [pl.BlockSpec((tm, d), row), gate.spec(gate_seg, d, tm),
                 pl.BlockSpec((1, d), lambda i: (0, 0))]
    args += [res, gate.arr, nxt[0].reshape(1, d)]
    if final:
        out_shape = jax.ShapeDtypeStruct((m, d), F32)
        out_specs = pl.BlockSpec((tm, d), row)
    else:
        _, nmod, seg_shift, seg_scale = nxt
        in_specs += [nmod.spec(seg_shift, d, tm), nmod.spec(seg_scale, d, tm)]
        args += [nmod.arr, nmod.arr]
        out_shape = [jax.ShapeDtypeStruct((m, d), F32), jax.ShapeDtypeStruct((m, d), BF16)]
        out_specs = [pl.BlockSpec((tm, d), row), pl.BlockSpec((tm, d), row)]
    return pl.pallas_call(
        functools.partial(_rowmm_body, layer=layer, kc=kc, has_bias=has_bias, x_t=x_t,
                          final=final),
        out_shape=out_shape,
        grid=(m // tm,),
        in_specs=in_specs,
        out_specs=out_specs,
        scratch_shapes=[pltpu.VMEM(s, dt) for s, dt in scratch] + [pltpu.SemaphoreType.DMA((2,))],
        compiler_params=pltpu.CompilerParams(
            dimension_semantics=("arbitrary",),
            vmem_limit_bytes=min(V7X_VMEM_CAP, max(32 * 2**20, vmem_need(tm) + 4 * 2**20))),
        name=name,
    )(*args)


def _conv_ln_swish(up_ref, pad, tt, w_ref, bdw_ref, lg_ref, lb_ref, acc_ref, o_ref):
    kw, d = w_ref.shape
    by_shift = [[(a, SUBLANES * a + s - pad) for a in range((pad + kw - 1) // SUBLANES + 1)
                 if 0 <= SUBLANES * a + s - pad < kw] for s in range(SUBLANES)]

    def strip(c, carry):
        cols = pl.ds(pl.multiple_of(c * LANES, LANES), LANES)
        z = bdw_ref[:, cols]
        for s, taps in enumerate(by_shift):
            n = tt + SUBLANES if s else tt
            q = None
            for a, k in taps:
                term = up_ref[pl.ds(SUBLANES * a, n), cols] * w_ref[pl.ds(k, 1), cols]
                q = term if q is None else q + term
            if q is not None:
                z = z + q[s:s + tt]
        acc_ref[:, cols] = z
        return carry

    lax.fori_loop(0, d // LANES, strip, 0)
    z = acc_ref[...]
    mu = jnp.mean(z, axis=-1, keepdims=True)
    zc = z - mu
    y = zc * lax.rsqrt(jnp.mean(zc * zc, axis=-1, keepdims=True) + EPS)
    y = y * lg_ref[...] + lb_ref[...]
    o_ref[...] = (y * _sigmoid(y)).astype(o_ref.dtype)


def _conv_prompt_body(main_ref, halo_ref, w_ref, bdw_ref, lg_ref, lb_ref, o_ref, up_ref, acc_ref,
                      *, halo, tt):
    kw = w_ref.shape[0]

    @pl.when(pl.program_id(1) == 0)
    def _():
        up_ref[0:halo, :] = jnp.zeros((halo, up_ref.shape[1]), F32)

    @pl.when(pl.program_id(1) > 0)
    def _():
        up_ref[0:halo, :] = halo_ref[...]

    up_ref[halo:halo + tt, :] = main_ref[...]
    _conv_ln_swish(up_ref, halo - (kw - 1), tt, w_ref, bdw_ref, lg_ref, lb_ref, acc_ref, o_ref)


def _conv_prompt_call(u, w_dw, b_dw, ln_g, ln_b, layer, tt):
    b, t, d = u.shape
    kw = w_dw.shape[1]
    halo = -(-(kw - 1) // SUBLANES) * SUBLANES
    tt = _tile(t, tt, halo)
    hb = tt // halo
    vec = lambda: pl.BlockSpec((None, 1, d), lambda bi, ti: (layer, 0, 0))
    blocks = [((tt, d), F32), ((halo, d), F32), ((kw, d), F32), ((tt, d), BF16)]
    scratch = [((halo + tt, d), F32), ((tt, d), F32)]
    return pl.pallas_call(
        functools.partial(_conv_prompt_body, halo=halo, tt=tt),
        out_shape=jax.ShapeDtypeStruct((b, t, d), BF16),
        grid=(b, t // tt),
        in_specs=[
            pl.BlockSpec((None, tt, d), lambda bi, ti: (bi, ti, 0)),
            pl.BlockSpec((None, halo, d), lambda bi, ti: (bi, jnp.maximum(ti * hb - 1, 0), 0)),
            pl.BlockSpec((None, kw, d), lambda bi, ti: (layer, 0, 0)),
            vec(), vec(), vec(),
        ],
        out_specs=pl.BlockSpec((None, tt, d), lambda bi, ti: (bi, ti, 0)),
        scratch_shapes=[pltpu.VMEM(s, dt) for s, dt in scratch],
        compiler_params=pltpu.CompilerParams(
            dimension_semantics=("parallel", "arbitrary"),
            vmem_limit_bytes=_vmem_limit(blocks, scratch)),
        name="dwconv_ln_swish_prompt",
    )(u, u, w_dw, b_dw.reshape(-1, 1, d), ln_g.reshape(-1, 1, d), ln_b.reshape(-1, 1, d))


def _conv_sample_body(up_ref, w_ref, bdw_ref, lg_ref, lb_ref, o_ref, acc_ref, *, pad, tt):
    _conv_ln_swish(up_ref, pad, tt, w_ref, bdw_ref, lg_ref, lb_ref, acc_ref, o_ref)


def _conv_sample_call(up, pad, tt, w_dw, b_dw, ln_g, ln_b, layer):
    b, rows, d = up.shape
    kw = w_dw.shape[1]
    vec = lambda: pl.BlockSpec((None, 1, d), lambda bi: (layer, 0, 0))
    blocks = [((rows, d), F32), ((kw, d), F32), ((tt, d), BF16)]
    scratch = [((tt, d), F32)]
    return pl.pallas_call(
        functools.partial(_conv_sample_body, pad=pad, tt=tt),
        out_shape=jax.ShapeDtypeStruct((b, tt, d), BF16),
        grid=(b,),
        in_specs=[
            pl.BlockSpec((None, rows, d), lambda bi: (bi, 0, 0)),
            pl.BlockSpec((None, kw, d), lambda bi: (layer, 0, 0)),
            vec(), vec(), vec(),
        ],
        out_specs=pl.BlockSpec((None, tt, d), lambda bi: (bi, 0, 0)),
        scratch_shapes=[pltpu.VMEM(s, dt) for s, dt in scratch],
        compiler_params=pltpu.CompilerParams(
            dimension_semantics=("parallel",), vmem_limit_bytes=_vmem_limit(blocks, scratch)),
        name="dwconv_ln_swish_sample",
    )(up, w_dw, b_dw.reshape(-1, 1, d), ln_g.reshape(-1, 1, d), ln_b.reshape(-1, 1, d))


def _bucket_codes(n_q, n_k, k_off, n_buckets):
    rel = (np.arange(n_k) + k_off)[None, :] - np.arange(n_q)[:, None]
    nb = n_buckets // 2
    max_exact = nb // 2
    ret = np.where(rel > 0, nb, 0)
    n = np.abs(rel)
    nf = np.maximum(n, 1).astype(np.float32)
    large = max_exact + (np.log(nf / np.float32(max_exact))
                         / np.float32(math.log(MAX_DISTANCE / max_exact))
                         * np.float32(nb - max_exact)).astype(np.int32)
    large = np.minimum(large, nb - 1)
    return (ret + np.where(n < max_exact, n, large)).astype(np.int32)


def _bias_body(code_ref, table_ref, o_ref, *, n_buckets):
    code = code_ref[...]

    def head(h, carry):
        out = jnp.full(code.shape, -jnp.inf, F32)
        for b in range(n_buckets):
            out = jnp.where(code == b, table_ref[h, b], out)
        o_ref[h] = out
        return carry

    lax.fori_loop(0, o_ref.shape[0], head, 0)


def _bias_call(code, table):
    n_heads, n_buckets = table.shape
    return pl.pallas_call(
        functools.partial(_bias_body, n_buckets=n_buckets),
        out_shape=jax.ShapeDtypeStruct((n_heads,) + code.shape, F32),
        in_specs=[pl.BlockSpec(memory_space=pltpu.VMEM), pl.BlockSpec(memory_space=pltpu.SMEM)],
        out_specs=pl.BlockSpec(memory_space=pltpu.VMEM),
        name="rel_bias",
    )(jnp.asarray(code), table)


def _sink_attention_unit(qs, kh, vh, bias, sink_col, maskrow):
    s = lax.dot_general(qs, kh, (((1,), (1,)), ((), ())), preferred_element_type=F32) + bias
    if maskrow is not None:
        s = s + maskrow
    m = jnp.maximum(jnp.max(s, axis=-1, keepdims=True), sink_col)
    e = jnp.exp(s - m)
    l = jnp.sum(e, axis=-1, keepdims=True) + jnp.exp(sink_col - m)
    return jnp.dot(e.astype(BF16), vh, preferred_element_type=F32) * (1.0 / l)


def _stack_heads(q, h, group):
    hd = HEAD_DIM
    return jnp.concatenate(
        [q[:, (h * group + g) * hd:(h * group + g + 1) * hd] for g in range(group)], axis=0)


def _unstack_heads(o, group):
    tq = o.shape[0] // group
    return [o[g * tq:(g + 1) * tq, :] for g in range(group)]


ATTN_Q_COLS = 256
PAIR = 2 * CHUNK


def _pair_codes(n_buckets):
    code = _bucket_codes(PAIR, WINDOW + PAIR, -WINDOW, n_buckets).T
    key_chunk = np.arange(WINDOW + PAIR)[:, None] // CHUNK
    q_chunk = np.arange(PAIR)[None, :] // CHUNK
    visible = (key_chunk >= q_chunk) & (key_chunk <= q_chunk + WINDOW // CHUNK)
    return np.where(visible, code, n_buckets).astype(np.int32)


def _attn_prompt_body(qT_ref, kp_ref, kc_ref, vTp_ref, vTc_ref, bias_ref, sink_ref, oT_ref,
                      k_scr, vT_scr, *, kvh, group):
    i = pl.program_id(1)
    tq = qT_ref.shape[1]
    nk = WINDOW + PAIR
    hd = HEAD_DIM
    for h in range(kvh):
        k_scr[h, 0:WINDOW, :] = kp_ref[:, h * hd:(h + 1) * hd].astype(BF16)
        k_scr[h, WINDOW:WINDOW + tq, :] = kc_ref[:, h * hd:(h + 1) * hd].astype(BF16)
    vT_scr[:, 0:WINDOW] = vTp_ref[...].astype(BF16)
    vT_scr[:, WINDOW:WINDOW + tq] = vTc_ref[...].astype(BF16)
    row = lax.broadcasted_iota(jnp.int32, (nk, group * PAIR), 0)
    for p in range(tq // PAIR):
        c0 = p * PAIR
        start_mask = None
        if p == 0:
            start_mask = jnp.where(row < jnp.where(i == 0, WINDOW, 0), -jnp.inf, 0.0).astype(F32)
        for h in range(kvh):
            heads = range(h * group, (h + 1) * group)
            qsT = jnp.concatenate(
                [qT_ref[hh * hd:(hh + 1) * hd, c0:c0 + PAIR] for hh in heads], axis=1)
            s = jnp.dot(k_scr[h, c0:c0 + nk, :], qsT, preferred_element_type=F32) + bias_ref[h]
            if start_mask is not None:
                s = s + start_mask
            sink = sink_ref[h]
            m = jnp.maximum(jnp.max(s, axis=0, keepdims=True), sink)
            e = jnp.exp(s - m)
            l = jnp.sum(e, axis=0, keepdims=True) + jnp.exp(sink - m)
            oT = jnp.dot(vT_scr[h * hd:(h + 1) * hd, c0:c0 + nk], e.astype(BF16),
                         preferred_element_type=F32) * (1.0 / l)
            for g, hh in enumerate(heads):
                oT_ref[hh * hd:(hh + 1) * hd, c0:c0 + PAIR] = (
                    oT[:, g * PAIR:(g + 1) * PAIR].astype(oT_ref.dtype))


def _attn_prompt_call(qT, k, vT, bias, sink_row, kvh, group):
    b, dq, s = qT.shape
    dk = k.shape[2]
    tq = _tile(s, ATTN_Q_COLS, PAIR)
    per = tq // WINDOW
    prev = lambda i: jnp.maximum(i * per - 1, 0)
    blocks = [((dq, tq), BF16)] * 2 + [((WINDOW + tq, dk), F32)] * 2 + [
        (bias.shape, F32), (sink_row.shape[:1] + (SUBLANES, sink_row.shape[2]), F32)]
    scratch = [((kvh, WINDOW + tq, LANES), BF16), ((dk, WINDOW + tq), BF16)]
    return pl.pallas_call(
        functools.partial(_attn_prompt_body, kvh=kvh, group=group),
        out_shape=jax.ShapeDtypeStruct((b, dq, s), BF16),
        grid=(b, s // tq),
        in_specs=[
            pl.BlockSpec((None, dq, tq), lambda bi, i: (bi, 0, i)),
            pl.BlockSpec((None, WINDOW, dk), lambda bi, i: (bi, prev(i), 0)),
            pl.BlockSpec((None, tq, dk), lambda bi, i: (bi, i, 0)),
            pl.BlockSpec((None, dk, WINDOW), lambda bi, i: (bi, 0, prev(i))),
            pl.BlockSpec((None, dk, tq), lambda bi, i: (bi, 0, i)),
            pl.BlockSpec(bias.shape, lambda bi, i: (0, 0, 0)),
            pl.BlockSpec(sink_row.shape, lambda bi, i: (0, 0, 0)),
        ],
        out_specs=pl.BlockSpec((None, dq, tq), lambda bi, i: (bi, 0, i)),
        scratch_shapes=[pltpu.VMEM((kvh, WINDOW + tq, HEAD_DIM), BF16),
                        pltpu.VMEM((dk, WINDOW + tq), BF16)],
        compiler_params=pltpu.CompilerParams(
            dimension_semantics=("parallel", "arbitrary"),
            vmem_limit_bytes=_vmem_limit(blocks, scratch)),
        name="swa_sink_attention_prompt",
    )(qT, k, k, vT, vT, bias, sink_row)


def _attn_sample_body(q_ref, k_ref, v_ref, bias_ref, sink_ref, o_ref, *, kvh, group):
    hd = HEAD_DIM
    q = q_ref[...].astype(F32)
    outs = []
    for h in range(kvh):
        o = _sink_attention_unit(
            _stack_heads(q, h, group).astype(BF16),
            k_ref[:, h * hd:(h + 1) * hd].astype(BF16), v_ref[:, h * hd:(h + 1) * hd].astype(BF16),
            bias_ref[h], sink_ref[h], None)
        outs += _unstack_heads(o, group)
    o_ref[...] = jnp.concatenate(outs, axis=1).astype(o_ref.dtype)


def _attn_sample_call(q, k_all, v_all, bias, sink_col, kvh, group):
    b, t, dq = q.shape
    nk, dk = k_all.shape[1:]
    blocks = [((t, dq), BF16)] * 2 + [((nk, dk), F32)] * 2 + [
        (bias.shape[:2] + (2 * LANES,), F32), (sink_col.shape[:2] + (LANES,), F32)]
    return pl.pallas_call(
        functools.partial(_attn_sample_body, kvh=kvh, group=group),
        out_shape=jax.ShapeDtypeStruct((b, t, dq), BF16),
        grid=(b,),
        in_specs=[
            pl.BlockSpec((None, t, dq), lambda bi: (bi, 0, 0)),
            pl.BlockSpec((None, nk, dk), lambda bi: (bi, 0, 0)),
            pl.BlockSpec((None, nk, dk), lambda bi: (bi, 0, 0)),
            pl.BlockSpec(bias.shape, lambda bi: (0, 0, 0)),
            pl.BlockSpec(sink_col.shape, lambda bi: (0, 0, 0)),
        ],
        out_specs=pl.BlockSpec((None, t, dq), lambda bi: (bi, 0, 0)),
        compiler_params=pltpu.CompilerParams(
            dimension_semantics=("parallel",), vmem_limit_bytes=_vmem_limit(blocks)),
        name="swa_sink_attention_sample",
    )(q, k_all, v_all, bias, sink_col)


def _trunks(xp3, xs3, mods_p, mods_s, state_conv, win_k, win_v, p):
    bp, tp, d = xp3.shape
    bs, ts, _ = xs3.shape
    mp, ms = bp * tp, bs * ts
    depth = p["w_mod"].shape[0]
    d_ff = p["w_gu"].shape[2] // 2
    n_heads = p["attn_sinks"].shape[1]
    dq = n_heads * HEAD_DIM
    kvh = (p["w_qkv"].shape[2] - dq) // (2 * HEAD_DIM)
    group = n_heads // kvh
    dkv = kvh * HEAD_DIM
    kw = p["w_dw"].shape[1]
    n_buckets = p["rel_bias_table"].shape[1]
    conv_p, k_p, v_p, conv_s, k_s, v_s = [], [], [], [], [], []

    xp, xs = xp3.reshape(mp, d), xs3.reshape(ms, d)
    hp = _norm_mod_call(xp, p["norm_mix"][0], mods_p[0], 0, 1, _tile(tp, 512, SUBLANES))
    hs = _norm_mod_call(xs, p["norm_mix"][0], mods_s[0], 0, 1, ms)
    for i in range(depth):
        j = i // 2
        ffn_norm = lambda mods: (p["norm_ffn"][i], mods[i], 3, 4)
        if i % 2 == 0:
            u_p, u_s = _mm_call(hp, p["w_pw1"], j, (0, d), d, x2=hs, bias=p["b_pw1"], act="glu",
                                tm=1024, tn=512, name="pw1_glu")
            u_p, u_s = u_p.reshape(bp, tp, d), u_s.reshape(bs, ts, d)
            conv_w = (p["w_dw"], p["b_dw"], p["conv_ln_g"], p["conv_ln_b"], j)
            z_p = _conv_prompt_call(u_p, *conv_w, 128)
            conv_p.append(u_p[:, tp - (kw - 1):])
            pad = (-(kw - 1 + ts)) % SUBLANES
            up = jnp.concatenate([jnp.zeros((bs, pad, d), F32), state_conv[j], u_s], axis=1)
            z_s = _conv_sample_call(up, pad, ts, *conv_w)
            conv_s.append(up[:, pad + ts:])
            xp, hp = _rowmm_call(z_p.reshape(mp, d), p["w_pw2"], j, xp, mods_p[i], 2,
                                 ffn_norm(mods_p), bias=p["b_pw2"], tm=512,
                                 name="pw2_residual_norm")
            xs, hs = _rowmm_call(z_s.reshape(ms, d), p["w_pw2"], j, xs, mods_s[i], 2,
                                 ffn_norm(mods_s), bias=p["b_pw2"], tm=512,
                                 name="pw2_residual_norm")
        else:
            sinks = p["attn_sinks"][j].reshape(kvh, group, 1)
            qT, q_s = _mm_call(hp, p["w_qkv"], j, (0,), dq, x2=hs, bias=p["b_qkv"],
                               scale=HEAD_DIM ** -0.5, out_dtype=BF16, tm=1024, tn=1024,
                               name="q_proj", out_mode="t", seq=tp)
            (k, vT), kv_s = _mm_call(hp, p["w_qkv"], j, (dq,), 2 * dkv, x2=hs, bias=p["b_qkv"],
                                     tm=1024, tn=2 * dkv, name="kv_proj", out_mode="split_t",
                                     seq=tp)
            k = k.reshape(bp, tp, dkv)
            nk = WINDOW + PAIR
            bias = _bias_call(_pair_codes(n_buckets), p["rel_bias_table"])
            bias = bias.reshape(kvh, group, nk, PAIR).transpose(0, 2, 1, 3)
            sink_row = jnp.repeat(sinks, PAIR, axis=2).reshape(kvh, 1, group * PAIR)
            oT = _attn_prompt_call(qT, k, vT, bias.reshape(kvh, nk, group * PAIR), sink_row,
                                   kvh, group)
            k_p.append(k[:, tp - WINDOW:].reshape(bp, WINDOW, kvh, HEAD_DIM))
            v_p.append(jnp.swapaxes(vT[:, :, tp - WINDOW:], 1, 2)
                       .reshape(bp, WINDOW, kvh, HEAD_DIM))
            kv3 = kv_s.reshape(bs, ts, 2, kvh, HEAD_DIM)
            k_all = jnp.concatenate([win_k[j], kv3[:, :, 0]], axis=1)
            v_all = jnp.concatenate([win_v[j], kv3[:, :, 1]], axis=1)
            n_keys = WINDOW + ts
            bias = _bias_call(_bucket_codes(ts, n_keys, -WINDOW, n_buckets),
                              p["rel_bias_table"]).reshape(kvh, group * ts, n_keys)
            o_s = _attn_sample_call(q_s.reshape(bs, ts, dq), k_all.reshape(bs, n_keys, dkv),
                                    v_all.reshape(bs, n_keys, dkv), bias,
                                    jnp.repeat(sinks, ts, axis=1), kvh, group)
            k_s.append(k_all[:, ts:])
            v_s.append(v_all[:, ts:])
            xp, hp = _rowmm_call(oT, p["w_o"], j, xp, mods_p[i], 2, ffn_norm(mods_p),
                                 bias=p["b_o"], x_t=True, tm=512, name="wo_residual_norm")
            xs, hs = _rowmm_call(o_s.reshape(ms, dq), p["w_o"], j, xs, mods_s[i], 2,
                                 ffn_norm(mods_s), bias=p["b_o"], tm=512,
                                 name="wo_residual_norm")
        a_p, a_s = _mm_call(hp, p["w_gu"], i, (0, d_ff), d_ff, x2=hs, act="swiglu",
                            out_dtype=BF16, tm=1024, tn=512, name="ffn_gate_up")
        if i + 1 < depth:
            nxt = lambda mods: (p["norm_mix"][i + 1], mods[i + 1], 0, 1)
            xp, hp = _rowmm_call(a_p, p["w_down"], i, xp, mods_p[i], 5, nxt(mods_p), tm=256,
                                 name="ffn_down_residual_norm")
            xs, hs = _rowmm_call(a_s, p["w_down"], i, xs, mods_s[i], 5, nxt(mods_s), tm=256,
                                 name="ffn_down_residual_norm")
        else:
            y_p = _rowmm_call(a_p, p["w_down"], i, xp, mods_p[i], 5, (p["norm_out"],), tm=256,
                              name="ffn_down_residual_final_norm")
            y_s = _rowmm_call(a_s, p["w_down"], i, xs, mods_s[i], 5, (p["norm_out"],), tm=256,
                              name="ffn_down_residual_final_norm")
    stack = jnp.stack
    return (y_p.reshape(bp, tp, d), y_s.reshape(bs, ts, d), stack(conv_p), stack(k_p), stack(v_p),
            stack(conv_s), stack(k_s), stack(v_s))


def kernel(x_prompt, x_sample, c_prompt, c_sample, state_conv, cache_win_k, cache_win_v, w_mod, b_mod, norm_mix, norm_ffn, w_pw1, b_pw1, w_dw, b_dw, conv_ln_g, conv_ln_b, w_pw2, b_pw2, w_qkv, b_qkv, w_o, b_o, attn_sinks, rel_bias_table, w_gu, w_down, norm_out):
    p = dict(w_mod=w_mod, b_mod=b_mod, norm_mix=norm_mix, norm_ffn=norm_ffn, w_pw1=w_pw1,
             b_pw1=b_pw1, w_dw=w_dw, b_dw=b_dw, conv_ln_g=conv_ln_g, conv_ln_b=conv_ln_b,
             w_pw2=w_pw2, b_pw2=b_pw2, w_qkv=w_qkv, b_qkv=b_qkv, w_o=w_o, b_o=b_o,
             attn_sinks=attn_sinks, rel_bias_table=rel_bias_table, w_gu=w_gu, w_down=w_down,
             norm_out=norm_out)
    bp, sp, d = x_prompt.shape
    bs, ts, _ = x_sample.shape
    depth = w_mod.shape[0]

    n_c = bp + bs
    c_all = jnp.concatenate(
        [c_prompt, c_sample, jnp.zeros((-n_c % (2 * SUBLANES), d), F32)], axis=0)
    mod_all = _mod_call(c_all, w_mod, b_mod)

    mods_p = [_Mod(mod_all[l, :bp].reshape(bp, 1, 6 * d), sp) for l in range(depth)]
    ms = bs * ts
    mods_s = [_Mod(jnp.repeat(mod_all[l, bp:n_c], ts, axis=0).reshape(1, ms, 6 * d), ms)
              for l in range(depth)]

    return _trunks(x_prompt, x_sample, mods_p, mods_s, state_conv, cache_win_k, cache_win_v, p)
```

```python
import functools
import math

import numpy as np
import jax
import jax.numpy as jnp
from jax import lax
from jax.experimental import pallas as pl
from jax.experimental.pallas import tpu as pltpu

F32 = jnp.float32
BF16 = jnp.bfloat16

CHUNK = 64
WINDOW = 128
HEAD_DIM = 64
MAX_DISTANCE = 128
EPS = 1e-6
LOG2E = math.log2(math.e)

V7X_VMEM_BYTES = 64 * 2**20
V7X_VMEM_CAP = V7X_VMEM_BYTES - 8 * 2**20
SUBLANES = 8
LANES = 128


def _nbytes(shape, dtype):
    return math.prod(shape) * jnp.dtype(dtype).itemsize


def _vmem_limit(blocks, scratch=()):
    est = 2 * sum(_nbytes(s, d) for s, d in blocks) + sum(_nbytes(s, d) for s, d in scratch)
    return int(min(V7X_VMEM_CAP, max(32 * 2**20, 2 * est)))


def _tile(dim, pref, mult=LANES):
    if dim <= pref:
        return dim
    t = (pref // mult) * mult
    while t >= mult:
        if dim % t == 0:
            return t
        t -= mult
    return dim


def _sigmoid(x):
    return 1.0 / (1.0 + jnp.exp(-x))


def _mod_body(c_ref, w_ref, b_ref, o_ref):
    c = c_ref[...]
    a = (c * _sigmoid(c)).astype(BF16)
    o_ref[...] = jnp.dot(a, w_ref[...].astype(BF16), preferred_element_type=F32) + b_ref[...]


def _mod_call(c_all, w_mod, b_mod):
    depth, d, n = w_mod.shape
    mp = c_all.shape[0]
    tn = _tile(n, 1024)
    blocks = [((mp, d), F32), ((d, tn), F32), ((1, tn), F32), ((mp, tn), F32)]
    return pl.pallas_call(
        _mod_body,
        out_shape=jax.ShapeDtypeStruct((depth, mp, n), F32),
        grid=(depth, n // tn),
        in_specs=[
            pl.BlockSpec((mp, d), lambda l, j: (0, 0)),
            pl.BlockSpec((None, d, tn), lambda l, j: (l, 0, j)),
            pl.BlockSpec((None, 1, tn), lambda l, j: (l, 0, j)),
        ],
        out_specs=pl.BlockSpec((None, mp, tn), lambda l, j: (l, 0, j)),
        compiler_params=pltpu.CompilerParams(
            dimension_semantics=("arbitrary", "arbitrary"),
            vmem_limit_bytes=_vmem_limit(blocks, [((d, tn), BF16)])),
        name="adaln_mod",
    )(c_all, w_mod, b_mod.reshape(depth, 1, n))


class _Mod:
    def __init__(self, arr, rows_per_group):
        self.arr = arr
        self.rows_per_group = rows_per_group

    def spec(self, seg, d, tm):
        per = self.rows_per_group // tm
        if self.arr.shape[1] == 1:
            return pl.BlockSpec((None, 1, d), lambda i: (i // per, 0, seg))
        return pl.BlockSpec((None, tm, d), lambda i: (i // per, i % per, seg))


def _norm_mod_body(x_ref, g_ref, sh_ref, sc_ref, o_ref):
    x = x_ref[...]
    y = x * lax.rsqrt(jnp.mean(x * x, axis=-1, keepdims=True) + EPS) * g_ref[...]
    o_ref[...] = (y * (1.0 + sc_ref[...]) + sh_ref[...]).astype(o_ref.dtype)


def _norm_mod_call(x, g, mod, seg_shift, seg_scale, tm):
    m, d = x.shape
    r = mod.arr.shape[1]
    blocks = [((tm, d), F32), ((1, d), F32), ((r, d), F32), ((r, d), F32), ((tm, d), BF16)]
    return pl.pallas_call(
        _norm_mod_body,
        out_shape=jax.ShapeDtypeStruct((m, d), BF16),
        grid=(m // tm,),
        in_specs=[
            pl.BlockSpec((tm, d), lambda i: (i, 0)),
            pl.BlockSpec((1, d), lambda i: (0, 0)),
            mod.spec(seg_shift, d, tm),
            mod.spec(seg_scale, d, tm),
        ],
        out_specs=pl.BlockSpec((tm, d), lambda i: (i, 0)),
        compiler_params=pltpu.CompilerParams(
            dimension_semantics=("parallel",), vmem_limit_bytes=_vmem_limit(blocks)),
        name="rmsnorm_modulate",
    )(x, g.reshape(1, d), mod.arr, mod.arr)


def _mm_body(x_ref, *refs, n_w, has_bias, act, scale, out_mode, has_x2):
    if has_x2:
        x2_ref, refs = refs[0], refs[1:]
    ws = refs[:n_w]
    refs = refs[n_w:]
    bs = refs[:n_w] if has_bias else ()
    refs = refs[len(bs):]
    n_o = 2 if out_mode == "split_t" else 1
    o_refs = refs[:n_o]
    refs = refs[n_o:]
    if has_x2:
        o2_ref, refs = refs[0], refs[1:]
    wbs = refs

    def compute(x):
        ps = []
        for i in range(n_w):
            p = jnp.dot(x, wbs[i][...], preferred_element_type=F32)
            if has_bias:
                p = p + bs[i][...]
            ps.append(p)
        if act == "glu":
            y = ps[0] * _sigmoid(ps[1])
        elif act == "swiglu":
            y = ps[0] * _sigmoid(ps[0]) * ps[1]
        else:
            y = ps[0]
        return y * scale if scale != 1.0 else y

    @pl.when(pl.program_id(1) == 0)
    def _():
        for w, wb in zip(ws, wbs):
            wb[...] = w[...].astype(BF16)
        if has_x2:
            o2_ref[...] = compute(x2_ref[...]).astype(o2_ref.dtype)

    y = compute(x_ref[...])
    if out_mode == "plain":
        o_refs[0][...] = y.astype(o_refs[0].dtype)
    elif out_mode == "t":
        o_refs[0][...] = y.T.astype(o_refs[0].dtype)
    else:
        half = y.shape[1] // 2
        o_refs[0][...] = y[:, :half].astype(o_refs[0].dtype)
        o_refs[1][...] = y[:, half:].T.astype(o_refs[1].dtype)


def _mm_call(x, w, layer, col_starts, n_out, *, x2=None, bias=None, act=None, scale=1.0,
             out_dtype=F32, tm=512, tn=512, name="matmul", out_mode="plain", seq=None):
    m, k = x.shape
    if out_mode == "plain":
        tm = _tile(m, tm, SUBLANES)
    else:
        tm = _tile(seq, tm, LANES)
    tn = _tile(n_out, tn)
    n_w = len(col_starts)
    assert all(c % tn == 0 for c in col_starts) and m % tm == 0 and n_out % tn == 0
    has_bias = bias is not None
    per_b = None if seq is None else seq // tm

    in_specs = [pl.BlockSpec((tm, k), lambda j, i: (i, 0))]
    args = [x]
    blocks = [((tm, k), x.dtype), ((tm, tn), out_dtype)]
    if x2 is not None:
        m2 = x2.shape[0]
        in_specs.append(pl.BlockSpec((m2, k), lambda j, i: (0, 0)))
        args.append(x2)
        blocks += [((m2, k), x2.dtype), ((m2, tn), out_dtype)]
    for c in col_starts:
        off = c // tn
        in_specs.append(pl.BlockSpec((None, k, tn), lambda j, i, off=off: (layer, 0, off + j)))
        args.append(w)
        blocks.append(((k, tn), F32))
    if has_bias:
        b3 = bias.reshape(bias.shape[0], 1, bias.shape[1])
        for c in col_starts:
            off = c // tn
            in_specs.append(pl.BlockSpec((None, 1, tn), lambda j, i, off=off: (layer, 0, off + j)))
            args.append(b3)
    scratch = [((k, tn), BF16)] * n_w
    body = functools.partial(_mm_body, n_w=n_w, has_bias=has_bias, act=act, scale=scale,
                             out_mode=out_mode, has_x2=x2 is not None)
    if out_mode == "plain":
        out_shape = [jax.ShapeDtypeStruct((m, n_out), out_dtype)]
        out_specs = [pl.BlockSpec((tm, tn), lambda j, i: (i, j))]
    elif out_mode == "t":
        out_shape = [jax.ShapeDtypeStruct((m // seq, n_out, seq), out_dtype)]
        out_specs = [pl.BlockSpec((None, tn, tm), lambda j, i: (i // per_b, j, i % per_b))]
    else:
        assert tn == n_out
        half = n_out // 2
        out_shape = [jax.ShapeDtypeStruct((m, half), out_dtype),
                     jax.ShapeDtypeStruct((m // seq, half, seq), out_dtype)]
        out_specs = [pl.BlockSpec((tm, half), lambda j, i: (i, 0)),
                     pl.BlockSpec((None, half, tm), lambda j, i: (i // per_b, 0, i % per_b))]
    if x2 is not None:
        out_shape.append(jax.ShapeDtypeStruct((m2, n_out), out_dtype))
        out_specs.append(pl.BlockSpec((m2, tn), lambda j, i: (0, j)))
    outs = pl.pallas_call(
        body,
        out_shape=out_shape,
        grid=(n_out // tn, m // tm),
        in_specs=in_specs,
        out_specs=out_specs,
        scratch_shapes=[pltpu.VMEM(s, d) for s, d in scratch],
        compiler_params=pltpu.CompilerParams(
            dimension_semantics=("arbitrary", "arbitrary"),
            vmem_limit_bytes=_vmem_limit(blocks, scratch)),
        name=name,
    )(*args)
    n_primary = 2 if out_mode == "split_t" else 1
    primary = outs[0] if n_primary == 1 else outs[:n_primary]
    return primary if x2 is None else (primary, outs[n_primary])


WEIGHT_CHUNK_ROWS = 256
ROWMM_SUB_ROWS = 256


def _rowmm_body(*refs, layer, kc, has_bias, x_t, final, w_is_bf16):
    x_ref, w_hbm = refs[:2]
    refs = refs[2:]
    if has_bias:
        b_ref, refs = refs[0], refs[1:]
    res_ref, gate_ref, ng_ref = refs[:3]
    refs = refs[3:]
    if not final:
        sh_ref, sc_ref = refs[:2]
        refs = refs[2:]
    n_o = 1 if final else 2
    o_refs = refs[:n_o]
    refs = refs[n_o:]
    if w_is_bf16:
        w_res, sem = refs

        @pl.when(pl.program_id(0) == 0)
        def _():
            whole = pltpu.make_async_copy(w_hbm, w_res, sem.at[0])
            whole.start()
            whole.wait()
    else:
        wq_hbm, w_res, stage, sem = refs
        n_chunks = w_res.shape[0] // kc
        write_back = pltpu.make_async_copy(w_res, wq_hbm, sem.at[2])

        def chunk_copy(c):
            return pltpu.make_async_copy(w_hbm.at[layer, pl.ds(c * kc, kc), :], stage.at[c % 2],
                                         sem.at[c % 2])

        @pl.when(pl.program_id(0) == 0)
        def _():
            chunk_copy(0).start()
            for c in range(n_chunks):
                if c + 1 < n_chunks:
                    chunk_copy(c + 1).start()
                chunk_copy(c).wait()
                w_res[c * kc:(c + 1) * kc, :] = stage[c % 2].astype(BF16)
            write_back.start()

        @pl.when(pl.program_id(0) == pl.num_programs(0) - 1)
        def _():
            write_back.wait()

    tm = res_ref.shape[0]
    sub = min(tm, ROWMM_SUB_ROWS)

    def rows_of(ref, r0):
        return ref[...] if ref.shape[0] == 1 else ref[r0:r0 + sub, :]

    for r0 in range(0, tm, sub):
        x = x_ref[:, r0:r0 + sub].T if x_t else x_ref[r0:r0 + sub, :]
        y = jnp.dot(x, w_res[...], preferred_element_type=F32)
        if has_bias:
            y = y + b_ref[...]
        xn = res_ref[r0:r0 + sub, :] + rows_of(gate_ref, r0) * y
        r = xn * lax.rsqrt(jnp.mean(xn * xn, axis=-1, keepdims=True) + EPS) * ng_ref[...]
        if final:
            o_refs[0][r0:r0 + sub, :] = r
        else:
            o_refs[0][r0:r0 + sub, :] = xn
            o_refs[1][r0:r0 + sub, :] = (
                r * (1.0 + rows_of(sc_ref, r0)) + rows_of(sh_ref, r0)).astype(o_refs[1].dtype)


def _rowmm_call(x, w, layer, res, gate, gate_seg, nxt, *, bias=None, x_t=False, tm=512,
                name="rowmm"):
    w_is_bf16 = w.ndim == 2
    if x_t:
        nb, k, seq = x.shape
        m = nb * seq
        row_span = min(gate.rows_per_group, seq)
    else:
        m, k = x.shape
        row_span = gate.rows_per_group
    d = w.shape[-1]
    final = len(nxt) == 1
    if not final:
        row_span = min(row_span, nxt[1].rows_per_group)
    kc = _tile(k, WEIGHT_CHUNK_ROWS, SUBLANES)
    scratch = [((k, d), BF16)] + ([] if w_is_bf16 else [((2, kc, d), F32)])
    mod_rows = [gate.arr.shape[1]] + ([] if final else [nxt[1].arr.shape[1]] * 2)

    def vmem_need(rows):
        blocks = [((rows, k), BF16), ((rows, d), F32), ((rows, d), F32 if final else BF16)]
        blocks += [((rows, d), F32)] * (0 if final else 1)
        blocks += [((SUBLANES if r == 1 else rows, d), F32) for r in mod_rows]
        temporaries = 4 * _nbytes((rows, d), F32)
        return (2 * sum(_nbytes(s, t) for s, t in blocks) + sum(_nbytes(s, t) for s, t in scratch)
                + temporaries)

    mult = LANES if x_t else SUBLANES
    tm = _tile(row_span, tm, mult)
    while vmem_need(tm) > V7X_VMEM_CAP and tm % (2 * mult) == 0:
        tm //= 2
    has_bias = bias is not None
    row = lambda i: (i, 0)
    if x_t:
        per_b = seq // tm
        in_specs = [pl.BlockSpec((None, k, tm), lambda i: (i // per_b, 0, i % per_b))]
    else:
        in_specs = [pl.BlockSpec((tm, k), row)]
    in_specs.append(pl.BlockSpec(memory_space=pl.ANY))
    args = [x, w]
    if has_bias:
        in_specs.append(pl.BlockSpec((None, 1, d), lambda i: (layer, 0, 0)))
        args.append(bias.reshape(bias.shape[0], 1, d))
    in_specs += [pl.BlockSpec((tm, d), row), gate.spec(gate_seg, d, tm),
                 pl.BlockSpec((1, d), lambda i: (0, 0))]
    args += [res, gate.arr, nxt[0].reshape(1, d)]
    if final:
        out_shape = jax.ShapeDtypeStruct((m, d), F32)
        out_specs = pl.BlockSpec((tm, d), row)
    else:
        _, nmod, seg_shift, seg_scale = nxt
        in_specs += [nmod.spec(seg_shift, d, tm), nmod.spec(seg_scale, d, tm)]
        args += [nmod.arr, nmod.arr]
        out_shape = [jax.ShapeDtypeStruct((m, d), F32), jax.ShapeDtypeStruct((m, d), BF16)]
        out_specs = [pl.BlockSpec((tm, d), row), pl.BlockSpec((tm, d), row)]
    if final:
        out_shape, out_specs = [out_shape], [out_specs]
    if not w_is_bf16:
        out_shape.append(jax.ShapeDtypeStruct((k, d), BF16))
        out_specs.append(pl.BlockSpec(memory_space=pl.ANY))
    outs = pl.pallas_call(
        functools.partial(_rowmm_body, layer=layer, kc=kc, has_bias=has_bias, x_t=x_t,
                          final=final, w_is_bf16=w_is_bf16),
        out_shape=out_shape,
        grid=(m // tm,),
        in_specs=in_specs,
        out_specs=out_specs,
        scratch_shapes=[pltpu.VMEM(s, dt) for s, dt in scratch] + [pltpu.SemaphoreType.DMA((3,))],
        compiler_params=pltpu.CompilerParams(
            dimension_semantics=("arbitrary",),
            vmem_limit_bytes=min(V7X_VMEM_CAP, max(32 * 2**20, vmem_need(tm) + 4 * 2**20))),
        name=name,
    )(*args)
    return outs[0] if len(outs) == 1 else outs


def _conv_ln_swish(up_ref, pad, tt, w_ref, bdw_ref, lg_ref, lb_ref, acc_ref, o_ref):
    kw, d = w_ref.shape
    by_shift = [[(a, SUBLANES * a + s - pad) for a in range((pad + kw - 1) // SUBLANES + 1)
                 if 0 <= SUBLANES * a + s - pad < kw] for s in range(SUBLANES)]

    def strip(c, carry):
        cols = pl.ds(pl.multiple_of(c * LANES, LANES), LANES)
        z = bdw_ref[:, cols]
        for s, taps in enumerate(by_shift):
            n = tt + SUBLANES if s else tt
            q = None
            for a, k in taps:
                term = up_ref[pl.ds(SUBLANES * a, n), cols] * w_ref[pl.ds(k, 1), cols]
                q = term if q is None else q + term
            if q is not None:
                z = z + q[s:s + tt]
        acc_ref[:, cols] = z
        return carry

    lax.fori_loop(0, d // LANES, strip, 0)
    z = acc_ref[...]
    mu = jnp.mean(z, axis=-1, keepdims=True)
    zc = z - mu
    y = zc * lax.rsqrt(jnp.mean(zc * zc, axis=-1, keepdims=True) + EPS)
    y = y * lg_ref[...] + lb_ref[...]
    o_ref[...] = (y * _sigmoid(y)).astype(o_ref.dtype)


def _conv_prompt_body(main_ref, halo_ref, w_ref, bdw_ref, lg_ref, lb_ref, o_ref, up_ref, acc_ref,
                      *, halo, tt):
    kw = w_ref.shape[0]

    @pl.when(pl.program_id(1) == 0)
    def _():
        up_ref[0:halo, :] = jnp.zeros((halo, up_ref.shape[1]), F32)

    @pl.when(pl.program_id(1) > 0)
    def _():
        up_ref[0:halo, :] = halo_ref[...]

    up_ref[halo:halo + tt, :] = main_ref[...]
    _conv_ln_swish(up_ref, halo - (kw - 1), tt, w_ref, bdw_ref, lg_ref, lb_ref, acc_ref, o_ref)


def _conv_prompt_call(u, w_dw, b_dw, ln_g, ln_b, layer, tt):
    b, t, d = u.shape
    kw = w_dw.shape[1]
    halo = -(-(kw - 1) // SUBLANES) * SUBLANES
    tt = _tile(t, tt, halo)
    hb = tt // halo
    vec = lambda: pl.BlockSpec((None, 1, d), lambda bi, ti: (layer, 0, 0))
    blocks = [((tt, d), F32), ((halo, d), F32), ((kw, d), F32), ((tt, d), BF16)]
    scratch = [((halo + tt, d), F32), ((tt, d), F32)]
    return pl.pallas_call(
        functools.partial(_conv_prompt_body, halo=halo, tt=tt),
        out_shape=jax.ShapeDtypeStruct((b, t, d), BF16),
        grid=(b, t // tt),
        in_specs=[
            pl.BlockSpec((None, tt, d), lambda bi, ti: (bi, ti, 0)),
            pl.BlockSpec((None, halo, d), lambda bi, ti: (bi, jnp.maximum(ti * hb - 1, 0), 0)),
            pl.BlockSpec((None, kw, d), lambda bi, ti: (layer, 0, 0)),
            vec(), vec(), vec(),
        ],
        out_specs=pl.BlockSpec((None, tt, d), lambda bi, ti: (bi, ti, 0)),
        scratch_shapes=[pltpu.VMEM(s, dt) for s, dt in scratch],
        compiler_params=pltpu.CompilerParams(
            dimension_semantics=("parallel", "arbitrary"),
            vmem_limit_bytes=_vmem_limit(blocks, scratch)),
        name="dwconv_ln_swish_prompt",
    )(u, u, w_dw, b_dw.reshape(-1, 1, d), ln_g.reshape(-1, 1, d), ln_b.reshape(-1, 1, d))


def _conv_sample_body(up_ref, w_ref, bdw_ref, lg_ref, lb_ref, o_ref, acc_ref, *, pad, tt):
    _conv_ln_swish(up_ref, pad, tt, w_ref, bdw_ref, lg_ref, lb_ref, acc_ref, o_ref)


def _conv_sample_call(up, pad, tt, w_dw, b_dw, ln_g, ln_b, layer):
    b, rows, d = up.shape
    kw = w_dw.shape[1]
    vec = lambda: pl.BlockSpec((None, 1, d), lambda bi: (layer, 0, 0))
    blocks = [((rows, d), F32), ((kw, d), F32), ((tt, d), BF16)]
    scratch = [((tt, d), F32)]
    return pl.pallas_call(
        functools.partial(_conv_sample_body, pad=pad, tt=tt),
        out_shape=jax.ShapeDtypeStruct((b, tt, d), BF16),
        grid=(b,),
        in_specs=[
            pl.BlockSpec((None, rows, d), lambda bi: (bi, 0, 0)),
            pl.BlockSpec((None, kw, d), lambda bi: (layer, 0, 0)),
            vec(), vec(), vec(),
        ],
        out_specs=pl.BlockSpec((None, tt, d), lambda bi: (bi, 0, 0)),
        scratch_shapes=[pltpu.VMEM(s, dt) for s, dt in scratch],
        compiler_params=pltpu.CompilerParams(
            dimension_semantics=("parallel",), vmem_limit_bytes=_vmem_limit(blocks, scratch)),
        name="dwconv_ln_swish_sample",
    )(up, w_dw, b_dw.reshape(-1, 1, d), ln_g.reshape(-1, 1, d), ln_b.reshape(-1, 1, d))


def _bucket_codes(n_q, n_k, k_off, n_buckets):
    rel = (np.arange(n_k) + k_off)[None, :] - np.arange(n_q)[:, None]
    nb = n_buckets // 2
    max_exact = nb // 2
    ret = np.where(rel > 0, nb, 0)
    n = np.abs(rel)
    nf = np.maximum(n, 1).astype(np.float32)
    large = max_exact + (np.log(nf / np.float32(max_exact))
                         / np.float32(math.log(MAX_DISTANCE / max_exact))
                         * np.float32(nb - max_exact)).astype(np.int32)
    large = np.minimum(large, nb - 1)
    return (ret + np.where(n < max_exact, n, large)).astype(np.int32)


def _bias_body(code_ref, table_ref, o_ref, *, n_buckets):
    code = code_ref[...]

    def head(h, carry):
        out = jnp.full(code.shape, -jnp.inf, F32)
        for b in range(n_buckets):
            out = jnp.where(code == b, table_ref[h, b] * LOG2E, out)
        o_ref[h] = out
        return carry

    lax.fori_loop(0, o_ref.shape[0], head, 0)


def _bias_call(code, table):
    n_heads, n_buckets = table.shape
    return pl.pallas_call(
        functools.partial(_bias_body, n_buckets=n_buckets),
        out_shape=jax.ShapeDtypeStruct((n_heads,) + code.shape, F32),
        in_specs=[pl.BlockSpec(memory_space=pltpu.VMEM), pl.BlockSpec(memory_space=pltpu.SMEM)],
        out_specs=pl.BlockSpec(memory_space=pltpu.VMEM),
        name="rel_bias",
    )(jnp.asarray(code), table)


def _sink_attention_unit(qs, kh, vh, bias, sink_col, maskrow):
    s = lax.dot_general(qs, kh, (((1,), (1,)), ((), ())), preferred_element_type=F32) + bias
    if maskrow is not None:
        s = s + maskrow
    sink_col = sink_col * LOG2E
    m = jnp.maximum(jnp.max(s, axis=-1, keepdims=True), sink_col)
    e = jnp.exp2(s - m)
    l = jnp.sum(e, axis=-1, keepdims=True) + jnp.exp2(sink_col - m)
    return jnp.dot(e.astype(BF16), vh, preferred_element_type=F32) * (1.0 / l)


def _stack_heads(q, h, group):
    hd = HEAD_DIM
    return jnp.concatenate(
        [q[:, (h * group + g) * hd:(h * group + g + 1) * hd] for g in range(group)], axis=0)


def _unstack_heads(o, group):
    tq = o.shape[0] // group
    return [o[g * tq:(g + 1) * tq, :] for g in range(group)]


ATTN_Q_COLS = 256
PAIR = 2 * CHUNK


def _pair_codes(n_buckets):
    code = _bucket_codes(PAIR, WINDOW + PAIR, -WINDOW, n_buckets).T
    key_chunk = np.arange(WINDOW + PAIR)[:, None] // CHUNK
    q_chunk = np.arange(PAIR)[None, :] // CHUNK
    visible = (key_chunk >= q_chunk) & (key_chunk <= q_chunk + WINDOW // CHUNK)
    return np.where(visible, code, n_buckets).astype(np.int32)


def _attn_prompt_body(qT_ref, kp_ref, kc_ref, vTp_ref, vTc_ref, bias_ref, sink_ref, oT_ref,
                      k_scr, vT_scr, *, kvh, group):
    i = pl.program_id(1)
    tq = qT_ref.shape[1]
    nk = WINDOW + PAIR
    hd = HEAD_DIM
    for h in range(kvh):
        k_scr[h, 0:WINDOW, :] = kp_ref[:, h * hd:(h + 1) * hd].astype(BF16)
        k_scr[h, WINDOW:WINDOW + tq, :] = kc_ref[:, h * hd:(h + 1) * hd].astype(BF16)
    vT_scr[:, 0:WINDOW] = vTp_ref[...].astype(BF16)
    vT_scr[:, WINDOW:WINDOW + tq] = vTc_ref[...].astype(BF16)
    row = lax.broadcasted_iota(jnp.int32, (nk, group * PAIR), 0)
    for p in range(tq // PAIR):
        c0 = p * PAIR
        start_mask = None
        if p == 0:
            start_mask = jnp.where(row < jnp.where(i == 0, WINDOW, 0), -jnp.inf, 0.0).astype(F32)
        for h in range(kvh):
            heads = range(h * group, (h + 1) * group)
            qsT = jnp.concatenate(
                [qT_ref[hh * hd:(hh + 1) * hd, c0:c0 + PAIR] for hh in heads], axis=1)
            s = jnp.dot(k_scr[h, c0:c0 + nk, :], qsT, preferred_element_type=F32) + bias_ref[h]
            if start_mask is not None:
                s = s + start_mask
            sink = sink_ref[h] * LOG2E
            m = jnp.maximum(jnp.max(s, axis=0, keepdims=True), sink)
            e = jnp.exp2(s - m)
            l = jnp.sum(e, axis=0, keepdims=True) + jnp.exp2(sink - m)
            oT = jnp.dot(vT_scr[h * hd:(h + 1) * hd, c0:c0 + nk], e.astype(BF16),
                         preferred_element_type=F32) * (1.0 / l)
            for g, hh in enumerate(heads):
                oT_ref[hh * hd:(hh + 1) * hd, c0:c0 + PAIR] = (
                    oT[:, g * PAIR:(g + 1) * PAIR].astype(oT_ref.dtype))


def _attn_prompt_call(qT, k, vT, bias, sink_row, kvh, group):
    b, dq, s = qT.shape
    dk = k.shape[2]
    tq = _tile(s, ATTN_Q_COLS, PAIR)
    per = tq // WINDOW
    prev = lambda i: jnp.maximum(i * per - 1, 0)
    blocks = [((dq, tq), BF16)] * 2 + [((WINDOW + tq, dk), F32)] * 2 + [
        (bias.shape, F32), (sink_row.shape[:1] + (SUBLANES, sink_row.shape[2]), F32)]
    scratch = [((kvh, WINDOW + tq, LANES), BF16), ((dk, WINDOW + tq), BF16)]
    return pl.pallas_call(
        functools.partial(_attn_prompt_body, kvh=kvh, group=group),
        out_shape=jax.ShapeDtypeStruct((b, dq, s), BF16),
        grid=(b, s // tq),
        in_specs=[
            pl.BlockSpec((None, dq, tq), lambda bi, i: (bi, 0, i)),
            pl.BlockSpec((None, WINDOW, dk), lambda bi, i: (bi, prev(i), 0)),
            pl.BlockSpec((None, tq, dk), lambda bi, i: (bi, i, 0)),
            pl.BlockSpec((None, dk, WINDOW), lambda bi, i: (bi, 0, prev(i))),
            pl.BlockSpec((None, dk, tq), lambda bi, i: (bi, 0, i)),
            pl.BlockSpec(bias.shape, lambda bi, i: (0, 0, 0)),
            pl.BlockSpec(sink_row.shape, lambda bi, i: (0, 0, 0)),
        ],
        out_specs=pl.BlockSpec((None, dq, tq), lambda bi, i: (bi, 0, i)),
        scratch_shapes=[pltpu.VMEM((kvh, WINDOW + tq, HEAD_DIM), BF16),
                        pltpu.VMEM((dk, WINDOW + tq), BF16)],
        compiler_params=pltpu.CompilerParams(
            dimension_semantics=("parallel", "arbitrary"),
            vmem_limit_bytes=_vmem_limit(blocks, scratch)),
        name="swa_sink_attention_prompt",
    )(qT, k, k, vT, vT, bias, sink_row)


def _attn_sample_body(q_ref, k_ref, v_ref, bias_ref, sink_ref, o_ref, *, kvh, group):
    hd = HEAD_DIM
    q = q_ref[...].astype(F32)
    outs = []
    for h in range(kvh):
        o = _sink_attention_unit(
            _stack_heads(q, h, group).astype(BF16),
            k_ref[:, h * hd:(h + 1) * hd].astype(BF16), v_ref[:, h * hd:(h + 1) * hd].astype(BF16),
            bias_ref[h], sink_ref[h], None)
        outs += _unstack_heads(o, group)
    o_ref[...] = jnp.concatenate(outs, axis=1).astype(o_ref.dtype)


def _attn_sample_call(q, k_all, v_all, bias, sink_col, kvh, group):
    b, t, dq = q.shape
    nk, dk = k_all.shape[1:]
    blocks = [((t, dq), BF16)] * 2 + [((nk, dk), F32)] * 2 + [
        (bias.shape[:2] + (2 * LANES,), F32), (sink_col.shape[:2] + (LANES,), F32)]
    return pl.pallas_call(
        functools.partial(_attn_sample_body, kvh=kvh, group=group),
        out_shape=jax.ShapeDtypeStruct((b, t, dq), BF16),
        grid=(b,),
        in_specs=[
            pl.BlockSpec((None, t, dq), lambda bi: (bi, 0, 0)),
            pl.BlockSpec((None, nk, dk), lambda bi: (bi, 0, 0)),
            pl.BlockSpec((None, nk, dk), lambda bi: (bi, 0, 0)),
            pl.BlockSpec(bias.shape, lambda bi: (0, 0, 0)),
            pl.BlockSpec(sink_col.shape, lambda bi: (0, 0, 0)),
        ],
        out_specs=pl.BlockSpec((None, t, dq), lambda bi: (bi, 0, 0)),
        compiler_params=pltpu.CompilerParams(
            dimension_semantics=("parallel",), vmem_limit_bytes=_vmem_limit(blocks)),
        name="swa_sink_attention_sample",
    )(q, k_all, v_all, bias, sink_col)


def _trunks(xp3, xs3, mods_p, mods_s, state_conv, win_k, win_v, p):
    bp, tp, d = xp3.shape
    bs, ts, _ = xs3.shape
    mp, ms = bp * tp, bs * ts
    depth = p["w_mod"].shape[0]
    d_ff = p["w_gu"].shape[2] // 2
    n_heads = p["attn_sinks"].shape[1]
    dq = n_heads * HEAD_DIM
    kvh = (p["w_qkv"].shape[2] - dq) // (2 * HEAD_DIM)
    group = n_heads // kvh
    dkv = kvh * HEAD_DIM
    kw = p["w_dw"].shape[1]
    n_buckets = p["rel_bias_table"].shape[1]
    conv_p, k_p, v_p, conv_s, k_s, v_s = [], [], [], [], [], []

    xp, xs = xp3.reshape(mp, d), xs3.reshape(ms, d)
    hp = _norm_mod_call(xp, p["norm_mix"][0], mods_p[0], 0, 1, _tile(tp, 512, SUBLANES))
    hs = _norm_mod_call(xs, p["norm_mix"][0], mods_s[0], 0, 1, ms)
    for i in range(depth):
        j = i // 2
        ffn_norm = lambda mods: (p["norm_ffn"][i], mods[i], 3, 4)
        if i % 2 == 0:
            u_p, u_s = _mm_call(hp, p["w_pw1"], j, (0, d), d, x2=hs, bias=p["b_pw1"], act="glu",
                                tm=1024, tn=512, name="pw1_glu")
            u_p, u_s = u_p.reshape(bp, tp, d), u_s.reshape(bs, ts, d)
            conv_w = (p["w_dw"], p["b_dw"], p["conv_ln_g"], p["conv_ln_b"], j)
            z_p = _conv_prompt_call(u_p, *conv_w, 128)
            conv_p.append(u_p[:, tp - (kw - 1):])
            pad = (-(kw - 1 + ts)) % SUBLANES
            up = jnp.concatenate([jnp.zeros((bs, pad, d), F32), state_conv[j], u_s], axis=1)
            z_s = _conv_sample_call(up, pad, ts, *conv_w)
            conv_s.append(up[:, pad + ts:])
            xp, hp, wq = _rowmm_call(z_p.reshape(mp, d), p["w_pw2"], j, xp, mods_p[i], 2,
                                     ffn_norm(mods_p), bias=p["b_pw2"], tm=512,
                                     name="pw2_residual_norm")
            xs, hs = _rowmm_call(z_s.reshape(ms, d), wq, j, xs, mods_s[i], 2,
                                 ffn_norm(mods_s), bias=p["b_pw2"], tm=512,
                                 name="pw2_residual_norm")
        else:
            sinks = p["attn_sinks"][j].reshape(kvh, group, 1)
            qT, q_s = _mm_call(hp, p["w_qkv"], j, (0,), dq, x2=hs, bias=p["b_qkv"],
                               scale=LOG2E * HEAD_DIM ** -0.5, out_dtype=BF16, tm=1024, tn=1024,
                               name="q_proj", out_mode="t", seq=tp)
            (k, vT), kv_s = _mm_call(hp, p["w_qkv"], j, (dq,), 2 * dkv, x2=hs, bias=p["b_qkv"],
                                     tm=1024, tn=2 * dkv, name="kv_proj", out_mode="split_t",
                                     seq=tp)
            k = k.reshape(bp, tp, dkv)
            nk = WINDOW + PAIR
            bias = _bias_call(_pair_codes(n_buckets), p["rel_bias_table"])
            bias = bias.reshape(kvh, group, nk, PAIR).transpose(0, 2, 1, 3)
            sink_row = jnp.repeat(sinks, PAIR, axis=2).reshape(kvh, 1, group * PAIR)
            oT = _attn_prompt_call(qT, k, vT, bias.reshape(kvh, nk, group * PAIR), sink_row,
                                   kvh, group)
            k_p.append(k[:, tp - WINDOW:].reshape(bp, WINDOW, kvh, HEAD_DIM))
            v_p.append(jnp.swapaxes(vT[:, :, tp - WINDOW:], 1, 2)
                       .reshape(bp, WINDOW, kvh, HEAD_DIM))
            kv3 = kv_s.reshape(bs, ts, 2, kvh, HEAD_DIM)
            k_all = jnp.concatenate([win_k[j], kv3[:, :, 0]], axis=1)
            v_all = jnp.concatenate([win_v[j], kv3[:, :, 1]], axis=1)
            n_keys = WINDOW + ts
            bias = _bias_call(_bucket_codes(ts, n_keys, -WINDOW, n_buckets),
                              p["rel_bias_table"]).reshape(kvh, group * ts, n_keys)
            o_s = _attn_sample_call(q_s.reshape(bs, ts, dq), k_all.reshape(bs, n_keys, dkv),
                                    v_all.reshape(bs, n_keys, dkv), bias,
                                    jnp.repeat(sinks, ts, axis=1), kvh, group)
            k_s.append(k_all[:, ts:])
            v_s.append(v_all[:, ts:])
            xp, hp, wq = _rowmm_call(oT, p["w_o"], j, xp, mods_p[i], 2, ffn_norm(mods_p),
                                     bias=p["b_o"], x_t=True, tm=512, name="wo_residual_norm")
            xs, hs = _rowmm_call(o_s.reshape(ms, dq), wq, j, xs, mods_s[i], 2,
                                 ffn_norm(mods_s), bias=p["b_o"], tm=512,
                                 name="wo_residual_norm")
        a_p, a_s = _mm_call(hp, p["w_gu"], i, (0, d_ff), d_ff, x2=hs, act="swiglu",
                            out_dtype=BF16, tm=1024, tn=512, name="ffn_gate_up")
        if i + 1 < depth:
            nxt = lambda mods: (p["norm_mix"][i + 1], mods[i + 1], 0, 1)
            xp, hp, wq = _rowmm_call(a_p, p["w_down"], i, xp, mods_p[i], 5, nxt(mods_p), tm=256,
                                     name="ffn_down_residual_norm")
            xs, hs = _rowmm_call(a_s, wq, i, xs, mods_s[i], 5, nxt(mods_s), tm=256,
                                 name="ffn_down_residual_norm")
        else:
            y_p, wq = _rowmm_call(a_p, p["w_down"], i, xp, mods_p[i], 5, (p["norm_out"],), tm=256,
                                  name="ffn_down_residual_final_norm")
            y_s = _rowmm_call(a_s, wq, i, xs, mods_s[i], 5, (p["norm_out"],), tm=256,
                              name="ffn_down_residual_final_norm")
    stack = jnp.stack
    return (y_p.reshape(bp, tp, d), y_s.reshape(bs, ts, d), stack(conv_p), stack(k_p), stack(v_p),
            stack(conv_s), stack(k_s), stack(v_s))


def kernel(x_prompt, x_sample, c_prompt, c_sample, state_conv, cache_win_k, cache_win_v, w_mod, b_mod, norm_mix, norm_ffn, w_pw1, b_pw1, w_dw, b_dw, conv_ln_g, conv_ln_b, w_pw2, b_pw2, w_qkv, b_qkv, w_o, b_o, attn_sinks, rel_bias_table, w_gu, w_down, norm_out):
    p = dict(w_mod=w_mod, b_mod=b_mod, norm_mix=norm_mix, norm_ffn=norm_ffn, w_pw1=w_pw1,
             b_pw1=b_pw1, w_dw=w_dw, b_dw=b_dw, conv_ln_g=conv_ln_g, conv_ln_b=conv_ln_b,
             w_pw2=w_pw2, b_pw2=b_pw2, w_qkv=w_qkv, b_qkv=b_qkv, w_o=w_o, b_o=b_o,
             attn_sinks=attn_sinks, rel_bias_table=rel_bias_table, w_gu=w_gu, w_down=w_down,
             norm_out=norm_out)
    bp, sp, d = x_prompt.shape
    bs, ts, _ = x_sample.shape
    depth = w_mod.shape[0]

    n_c = bp + bs
    c_all = jnp.concatenate(
        [c_prompt, c_sample, jnp.zeros((-n_c % (2 * SUBLANES), d), F32)], axis=0)
    mod_all = _mod_call(c_all, w_mod, b_mod)

    mods_p = [_Mod(mod_all[l, :bp].reshape(bp, 1, 6 * d), sp) for l in range(depth)]
    ms = bs * ts
    mods_s = [_Mod(jnp.repeat(mod_all[l, bp:n_c], ts, axis=0).reshape(1, ms, 6 * d), ms)
              for l in range(depth)]

    return _trunks(x_prompt, x_sample, mods_p, mods_s, state_conv, cache_win_k, cache_win_v, p)
```

```python
import functools
import math

import numpy as np
import jax
import jax.numpy as jnp
from jax import lax
from jax.experimental import pallas as pl
from jax.experimental.pallas import tpu as pltpu

F32 = jnp.float32
BF16 = jnp.bfloat16

CHUNK = 64
WINDOW = 128
HEAD_DIM = 64
MAX_DISTANCE = 128
EPS = 1e-6
LOG2E = math.log2(math.e)

V7X_VMEM_BYTES = 64 * 2**20
V7X_VMEM_CAP = V7X_VMEM_BYTES - 8 * 2**20
SUBLANES = 8
LANES = 128


def _nbytes(shape, dtype):
    return math.prod(shape) * jnp.dtype(dtype).itemsize


def _vmem_limit(blocks, scratch=()):
    est = 2 * sum(_nbytes(s, d) for s, d in blocks) + sum(_nbytes(s, d) for s, d in scratch)
    return int(min(V7X_VMEM_CAP, max(32 * 2**20, 2 * est)))


def _tile(dim, pref, mult=LANES):
    if dim <= pref:
        return dim
    t = (pref // mult) * mult
    while t >= mult:
        if dim % t == 0:
            return t
        t -= mult
    return dim


def _sigmoid(x):
    return 1.0 / (1.0 + jnp.exp(-x))


def _mod_body(c_ref, w_ref, b_ref, o_ref):
    c = c_ref[...]
    a = (c * _sigmoid(c)).astype(BF16)
    o_ref[...] = jnp.dot(a, w_ref[...].astype(BF16), preferred_element_type=F32) + b_ref[...]


def _mod_call(c_all, w_mod, b_mod):
    depth, d, n = w_mod.shape
    mp = c_all.shape[0]
    tn = _tile(n, 1024)
    blocks = [((mp, d), F32), ((d, tn), F32), ((1, tn), F32), ((mp, tn), F32)]
    return pl.pallas_call(
        _mod_body,
        out_shape=jax.ShapeDtypeStruct((depth, mp, n), F32),
        grid=(depth, n // tn),
        in_specs=[
            pl.BlockSpec((mp, d), lambda l, j: (0, 0)),
            pl.BlockSpec((None, d, tn), lambda l, j: (l, 0, j)),
            pl.BlockSpec((None, 1, tn), lambda l, j: (l, 0, j)),
        ],
        out_specs=pl.BlockSpec((None, mp, tn), lambda l, j: (l, 0, j)),
        compiler_params=pltpu.CompilerParams(
            dimension_semantics=("arbitrary", "arbitrary"),
            vmem_limit_bytes=_vmem_limit(blocks, [((d, tn), BF16)])),
        name="adaln_mod",
    )(c_all, w_mod, b_mod.reshape(depth, 1, n))


class _Mod:
    def __init__(self, arr, rows_per_group):
        self.arr = arr
        self.rows_per_group = rows_per_group

    def tile_rows(self, total_rows, pref, mult):
        if self.rows_per_group >= pref:
            return _tile(self.rows_per_group, pref, mult)
        return _tile(total_rows, pref, max(mult, self.rows_per_group))

    def spec(self, seg, d, tm):
        if tm <= self.rows_per_group:
            per = self.rows_per_group // tm
            return pl.BlockSpec((1, 1, d), lambda i: (i // per, 0, seg))
        return pl.BlockSpec((tm // self.rows_per_group, 1, d), lambda i: (i, 0, seg))


def _grouped(v, m):
    return v.reshape(m.shape[0], v.shape[0] // m.shape[0], v.shape[1])


def _norm_mod_body(x_ref, g_ref, sh_ref, sc_ref, o_ref):
    x = x_ref[...]
    y = x * lax.rsqrt(jnp.mean(x * x, axis=-1, keepdims=True) + EPS) * g_ref[...]
    sc = sc_ref[...]
    o_ref[...] = (_grouped(y, sc) * (1.0 + sc) + sh_ref[...]).reshape(x.shape).astype(o_ref.dtype)


def _norm_mod_call(x, g, mod, seg_shift, seg_scale, tm):
    m, d = x.shape
    tm = mod.tile_rows(m, tm, SUBLANES)
    blocks = [((tm, d), F32), ((1, d), F32), ((tm, d), BF16)]
    return pl.pallas_call(
        _norm_mod_body,
        out_shape=jax.ShapeDtypeStruct((m, d), BF16),
        grid=(m // tm,),
        in_specs=[
            pl.BlockSpec((tm, d), lambda i: (i, 0)),
            pl.BlockSpec((1, d), lambda i: (0, 0)),
            mod.spec(seg_shift, d, tm),
            mod.spec(seg_scale, d, tm),
        ],
        out_specs=pl.BlockSpec((tm, d), lambda i: (i, 0)),
        compiler_params=pltpu.CompilerParams(
            dimension_semantics=("parallel",), vmem_limit_bytes=_vmem_limit(blocks)),
        name="rmsnorm_modulate",
    )(x, g.reshape(1, d), mod.arr, mod.arr)


def _mm_body(x_ref, *refs, n_w, has_bias, act, scale, out_mode, has_x2):
    if has_x2:
        x2_ref, refs = refs[0], refs[1:]
    ws = refs[:n_w]
    refs = refs[n_w:]
    bs = refs[:n_w] if has_bias else ()
    refs = refs[len(bs):]
    n_o = 2 if out_mode == "split_t" else 1
    o_refs = refs[:n_o]
    refs = refs[n_o:]
    if has_x2:
        o2_ref, refs = refs[0], refs[1:]
    wbs = refs

    def compute(x):
        ps = []
        for i in range(n_w):
            p = jnp.dot(x, wbs[i][...], preferred_element_type=F32)
            if has_bias:
                p = p + bs[i][...]
            ps.append(p)
        if act == "glu":
            y = ps[0] * _sigmoid(ps[1])
        elif act == "swiglu":
            y = ps[0] * _sigmoid(ps[0]) * ps[1]
        else:
            y = ps[0]
        return y * scale if scale != 1.0 else y

    @pl.when(pl.program_id(1) == 0)
    def _():
        for w, wb in zip(ws, wbs):
            wb[...] = w[...].astype(BF16)
        if has_x2:
            o2_ref[...] = compute(x2_ref[...]).astype(o2_ref.dtype)

    y = compute(x_ref[...])
    if out_mode == "plain":
        o_refs[0][...] = y.astype(o_refs[0].dtype)
    elif out_mode == "t":
        o_refs[0][...] = y.T.astype(o_refs[0].dtype)
    else:
        half = y.shape[1] // 2
        o_refs[0][...] = y[:, :half].astype(o_refs[0].dtype)
        o_refs[1][...] = y[:, half:].T.astype(o_refs[1].dtype)


def _mm_call(x, w, layer, col_starts, n_out, *, x2=None, bias=None, act=None, scale=1.0,
             out_dtype=F32, tm=512, tn=512, name="matmul", out_mode="plain", seq=None):
    m, k = x.shape
    if out_mode == "plain":
        tm = _tile(m, tm, SUBLANES)
    else:
        tm = _tile(seq, tm, LANES)
    tn = _tile(n_out, tn)
    n_w = len(col_starts)
    assert all(c % tn == 0 for c in col_starts) and m % tm == 0 and n_out % tn == 0
    has_bias = bias is not None
    per_b = None if seq is None else seq // tm

    in_specs = [pl.BlockSpec((tm, k), lambda j, i: (i, 0))]
    args = [x]
    blocks = [((tm, k), x.dtype), ((tm, tn), out_dtype)]
    if x2 is not None:
        m2 = x2.shape[0]
        in_specs.append(pl.BlockSpec((m2, k), lambda j, i: (0, 0)))
        args.append(x2)
        blocks += [((m2, k), x2.dtype), ((m2, tn), out_dtype)]
    for c in col_starts:
        off = c // tn
        in_specs.append(pl.BlockSpec((None, k, tn), lambda j, i, off=off: (layer, 0, off + j)))
        args.append(w)
        blocks.append(((k, tn), F32))
    if has_bias:
        b3 = bias.reshape(bias.shape[0], 1, bias.shape[1])
        for c in col_starts:
            off = c // tn
            in_specs.append(pl.BlockSpec((None, 1, tn), lambda j, i, off=off: (layer, 0, off + j)))
            args.append(b3)
    scratch = [((k, tn), BF16)] * n_w
    body = functools.partial(_mm_body, n_w=n_w, has_bias=has_bias, act=act, scale=scale,
                             out_mode=out_mode, has_x2=x2 is not None)
    if out_mode == "plain":
        out_shape = [jax.ShapeDtypeStruct((m, n_out), out_dtype)]
        out_specs = [pl.BlockSpec((tm, tn), lambda j, i: (i, j))]
    elif out_mode == "t":
        out_shape = [jax.ShapeDtypeStruct((m // seq, n_out, seq), out_dtype)]
        out_specs = [pl.BlockSpec((None, tn, tm), lambda j, i: (i // per_b, j, i % per_b))]
    else:
        assert tn == n_out
        half = n_out // 2
        out_shape = [jax.ShapeDtypeStruct((m, half), out_dtype),
                     jax.ShapeDtypeStruct((m // seq, half, seq), out_dtype)]
        out_specs = [pl.BlockSpec((tm, half), lambda j, i: (i, 0)),
                     pl.BlockSpec((None, half, tm), lambda j, i: (i // per_b, 0, i % per_b))]
    if x2 is not None:
        out_shape.append(jax.ShapeDtypeStruct((m2, n_out), out_dtype))
        out_specs.append(pl.BlockSpec((m2, tn), lambda j, i: (0, j)))
    outs = pl.pallas_call(
        body,
        out_shape=out_shape,
        grid=(n_out // tn, m // tm),
        in_specs=in_specs,
        out_specs=out_specs,
        scratch_shapes=[pltpu.VMEM(s, d) for s, d in scratch],
        compiler_params=pltpu.CompilerParams(
            dimension_semantics=("arbitrary", "arbitrary"),
            vmem_limit_bytes=_vmem_limit(blocks, scratch)),
        name=name,
    )(*args)
    n_primary = 2 if out_mode == "split_t" else 1
    primary = outs[0] if n_primary == 1 else outs[:n_primary]
    return primary if x2 is None else (primary, outs[n_primary])


WEIGHT_CHUNK_ROWS = 256
ROWMM_SUB_ROWS = 256


def _rowmm_body(*refs, layer, kc, has_bias, x_t, final, w_is_bf16):
    x_ref, w_hbm = refs[:2]
    refs = refs[2:]
    if has_bias:
        b_ref, refs = refs[0], refs[1:]
    res_ref, gate_ref, ng_ref = refs[:3]
    refs = refs[3:]
    if not final:
        sh_ref, sc_ref = refs[:2]
        refs = refs[2:]
    n_o = 1 if final else 2
    o_refs = refs[:n_o]
    refs = refs[n_o:]
    if w_is_bf16:
        w_res, sem = refs

        @pl.when(pl.program_id(0) == 0)
        def _():
            whole = pltpu.make_async_copy(w_hbm, w_res, sem.at[0])
            whole.start()
            whole.wait()
    else:
        wq_hbm, w_res, stage, sem = refs
        n_chunks = w_res.shape[0] // kc
        write_back = pltpu.make_async_copy(w_res, wq_hbm, sem.at[2])

        def chunk_copy(c):
            return pltpu.make_async_copy(w_hbm.at[layer, pl.ds(c * kc, kc), :], stage.at[c % 2],
                                         sem.at[c % 2])

        @pl.when(pl.program_id(0) == 0)
        def _():
            chunk_copy(0).start()
            for c in range(n_chunks):
                if c + 1 < n_chunks:
                    chunk_copy(c + 1).start()
                chunk_copy(c).wait()
                w_res[c * kc:(c + 1) * kc, :] = stage[c % 2].astype(BF16)
            write_back.start()

        @pl.when(pl.program_id(0) == pl.num_programs(0) - 1)
        def _():
            write_back.wait()

    tm = res_ref.shape[0]
    sub = min(tm, ROWMM_SUB_ROWS)

    def groups_of(ref, r0):
        per = tm // ref.shape[0]
        return ref[...] if per >= tm else ref[r0 // per:(r0 + sub) // per]

    for r0 in range(0, tm, sub):
        x = x_ref[:, r0:r0 + sub].T if x_t else x_ref[r0:r0 + sub, :]
        y = jnp.dot(x, w_res[...], preferred_element_type=F32)
        if has_bias:
            y = y + b_ref[...]
        gate = groups_of(gate_ref, r0)
        xn = (_grouped(res_ref[r0:r0 + sub, :], gate) + gate * _grouped(y, gate)).reshape(y.shape)
        r = xn * lax.rsqrt(jnp.mean(xn * xn, axis=-1, keepdims=True) + EPS) * ng_ref[...]
        if final:
            o_refs[0][r0:r0 + sub, :] = r
        else:
            sc = groups_of(sc_ref, r0)
            o_refs[0][r0:r0 + sub, :] = xn
            o_refs[1][r0:r0 + sub, :] = (
                _grouped(r, sc) * (1.0 + sc) + groups_of(sh_ref, r0)
            ).reshape(y.shape).astype(o_refs[1].dtype)


def _rowmm_call(x, w, layer, res, gate, gate_seg, nxt, *, bias=None, x_t=False, tm=512,
                name="rowmm"):
    w_is_bf16 = w.ndim == 2
    if x_t:
        nb, k, seq = x.shape
        m = nb * seq
    else:
        m, k = x.shape
    d = w.shape[-1]
    final = len(nxt) == 1
    kc = _tile(k, WEIGHT_CHUNK_ROWS, SUBLANES)
    scratch = [((k, d), BF16)] + ([] if w_is_bf16 else [((2, kc, d), F32)])
    n_mods = 1 if final else 3

    def vmem_need(rows):
        blocks = [((rows, k), BF16), ((rows, d), F32), ((rows, d), F32 if final else BF16)]
        blocks += [((rows, d), F32)] * (0 if final else 1)
        blocks += [((-(-rows // gate.rows_per_group) * SUBLANES, d), F32)] * n_mods
        temporaries = 4 * _nbytes((rows, d), F32)
        return (2 * sum(_nbytes(s, t) for s, t in blocks) + sum(_nbytes(s, t) for s, t in scratch)
                + temporaries)

    mult = LANES if x_t else SUBLANES
    tm = gate.tile_rows(m, tm, mult)
    while vmem_need(tm) > V7X_VMEM_CAP and tm % (2 * max(mult, min(tm, gate.rows_per_group))) == 0:
        tm //= 2
    assert final or nxt[1].rows_per_group == gate.rows_per_group
    has_bias = bias is not None
    row = lambda i: (i, 0)
    if x_t:
        per_b = seq // tm
        in_specs = [pl.BlockSpec((None, k, tm), lambda i: (i // per_b, 0, i % per_b))]
    else:
        in_specs = [pl.BlockSpec((tm, k), row)]
    in_specs.append(pl.BlockSpec(memory_space=pltpu.HBM))
    args = [x, w]
    if has_bias:
        in_specs.append(pl.BlockSpec((None, 1, d), lambda i: (layer, 0, 0)))
        args.append(bias.reshape(bias.shape[0], 1, d))
    in_specs += [pl.BlockSpec((tm, d), row), gate.spec(gate_seg, d, tm),
                 pl.BlockSpec((1, d), lambda i: (0, 0))]
    args += [res, gate.arr, nxt[0].reshape(1, d)]
    if final:
        out_shape = jax.ShapeDtypeStruct((m, d), F32)
        out_specs = pl.BlockSpec((tm, d), row)
    else:
        _, nmod, seg_shift, seg_scale = nxt
        in_specs += [nmod.spec(seg_shift, d, tm), nmod.spec(seg_scale, d, tm)]
        args += [nmod.arr, nmod.arr]
        out_shape = [jax.ShapeDtypeStruct((m, d), F32), jax.ShapeDtypeStruct((m, d), BF16)]
        out_specs = [pl.BlockSpec((tm, d), row), pl.BlockSpec((tm, d), row)]
    if final:
        out_shape, out_specs = [out_shape], [out_specs]
    if not w_is_bf16:
        out_shape.append(jax.ShapeDtypeStruct((k, d), BF16))
        out_specs.append(pl.BlockSpec(memory_space=pltpu.HBM))
    outs = pl.pallas_call(
        functools.partial(_rowmm_body, layer=layer, kc=kc, has_bias=has_bias, x_t=x_t,
                          final=final, w_is_bf16=w_is_bf16),
        out_shape=out_shape,
        grid=(m // tm,),
        in_specs=in_specs,
        out_specs=out_specs,
        scratch_shapes=[pltpu.VMEM(s, dt) for s, dt in scratch] + [pltpu.SemaphoreType.DMA((3,))],
        compiler_params=pltpu.CompilerParams(
            dimension_semantics=("arbitrary",),
            vmem_limit_bytes=min(V7X_VMEM_CAP, max(32 * 2**20, vmem_need(tm) + 4 * 2**20))),
        name=name,
    )(*args)
    return outs[0] if len(outs) == 1 else outs


def _conv_ln_swish(up_ref, pad, tt, w_ref, bdw_ref, lg_ref, lb_ref, acc_ref, o_ref):
    kw, d = w_ref.shape
    by_shift = [[(a, SUBLANES * a + s - pad) for a in range((pad + kw - 1) // SUBLANES + 1)
                 if 0 <= SUBLANES * a + s - pad < kw] for s in range(SUBLANES)]

    def strip(c, carry):
        cols = pl.ds(pl.multiple_of(c * LANES, LANES), LANES)
        z = bdw_ref[:, cols]
        for s, taps in enumerate(by_shift):
            n = tt + SUBLANES if s else tt
            q = None
            for a, k in taps:
                term = up_ref[pl.ds(SUBLANES * a, n), cols] * w_ref[pl.ds(k, 1), cols]
                q = term if q is None else q + term
            if q is not None:
                z = z + q[s:s + tt]
        acc_ref[:, cols] = z
        return carry

    lax.fori_loop(0, d // LANES, strip, 0)
    z = acc_ref[...]
    mu = jnp.mean(z, axis=-1, keepdims=True)
    zc = z - mu
    y = zc * lax.rsqrt(jnp.mean(zc * zc, axis=-1, keepdims=True) + EPS)
    y = y * lg_ref[...] + lb_ref[...]
    o_ref[...] = (y * _sigmoid(y)).astype(o_ref.dtype)


def _conv_prompt_body(main_ref, halo_ref, w_ref, bdw_ref, lg_ref, lb_ref, o_ref, up_ref, acc_ref,
                      *, halo, tt):
    kw = w_ref.shape[0]

    @pl.when(pl.program_id(1) == 0)
    def _():
        up_ref[0:halo, :] = jnp.zeros((halo, up_ref.shape[1]), F32)

    @pl.when(pl.program_id(1) > 0)
    def _():
        up_ref[0:halo, :] = halo_ref[...]

    up_ref[halo:halo + tt, :] = main_ref[...]
    _conv_ln_swish(up_ref, halo - (kw - 1), tt, w_ref, bdw_ref, lg_ref, lb_ref, acc_ref, o_ref)


def _conv_prompt_call(u, w_dw, b_dw, ln_g, ln_b, layer, tt):
    b, t, d = u.shape
    kw = w_dw.shape[1]
    halo = -(-(kw - 1) // SUBLANES) * SUBLANES
    tt = _tile(t, tt, halo)
    hb = tt // halo
    vec = lambda: pl.BlockSpec((None, 1, d), lambda bi, ti: (layer, 0, 0))
    blocks = [((tt, d), F32), ((halo, d), F32), ((kw, d), F32), ((tt, d), BF16)]
    scratch = [((halo + tt, d), F32), ((tt, d), F32)]
    return pl.pallas_call(
        functools.partial(_conv_prompt_body, halo=halo, tt=tt),
        out_shape=jax.ShapeDtypeStruct((b, t, d), BF16),
        grid=(b, t // tt),
        in_specs=[
            pl.BlockSpec((None, tt, d), lambda bi, ti: (bi, ti, 0)),
            pl.BlockSpec((None, halo, d), lambda bi, ti: (bi, jnp.maximum(ti * hb - 1, 0), 0)),
            pl.BlockSpec((None, kw, d), lambda bi, ti: (layer, 0, 0)),
            vec(), vec(), vec(),
        ],
        out_specs=pl.BlockSpec((None, tt, d), lambda bi, ti: (bi, ti, 0)),
        scratch_shapes=[pltpu.VMEM(s, dt) for s, dt in scratch],
        compiler_params=pltpu.CompilerParams(
            dimension_semantics=("parallel", "arbitrary"),
            vmem_limit_bytes=_vmem_limit(blocks, scratch)),
        name="dwconv_ln_swish_prompt",
    )(u, u, w_dw, b_dw.reshape(-1, 1, d), ln_g.reshape(-1, 1, d), ln_b.reshape(-1, 1, d))


def _conv_sample_body(up_ref, w_ref, bdw_ref, lg_ref, lb_ref, o_ref, acc_ref, *, pad, tt):
    _conv_ln_swish(up_ref, pad, tt, w_ref, bdw_ref, lg_ref, lb_ref, acc_ref, o_ref)


def _conv_sample_call(up, pad, tt, w_dw, b_dw, ln_g, ln_b, layer):
    b, rows, d = up.shape
    kw = w_dw.shape[1]
    vec = lambda: pl.BlockSpec((None, 1, d), lambda bi: (layer, 0, 0))
    blocks = [((rows, d), F32), ((kw, d), F32), ((tt, d), BF16)]
    scratch = [((tt, d), F32)]
    return pl.pallas_call(
        functools.partial(_conv_sample_body, pad=pad, tt=tt),
        out_shape=jax.ShapeDtypeStruct((b, tt, d), BF16),
        grid=(b,),
        in_specs=[
            pl.BlockSpec((None, rows, d), lambda bi: (bi, 0, 0)),
            pl.BlockSpec((None, kw, d), lambda bi: (layer, 0, 0)),
            vec(), vec(), vec(),
        ],
        out_specs=pl.BlockSpec((None, tt, d), lambda bi: (bi, 0, 0)),
        scratch_shapes=[pltpu.VMEM(s, dt) for s, dt in scratch],
        compiler_params=pltpu.CompilerParams(
            dimension_semantics=("parallel",), vmem_limit_bytes=_vmem_limit(blocks, scratch)),
        name="dwconv_ln_swish_sample",
    )(up, w_dw, b_dw.reshape(-1, 1, d), ln_g.reshape(-1, 1, d), ln_b.reshape(-1, 1, d))


def _bucket_codes(n_q, n_k, k_off, n_buckets):
    rel = (np.arange(n_k) + k_off)[None, :] - np.arange(n_q)[:, None]
    nb = n_buckets // 2
    max_exact = nb // 2
    ret = np.where(rel > 0, nb, 0)
    n = np.abs(rel)
    nf = np.maximum(n, 1).astype(np.float32)
    large = max_exact + (np.log(nf / np.float32(max_exact))
                         / np.float32(math.log(MAX_DISTANCE / max_exact))
                         * np.float32(nb - max_exact)).astype(np.int32)
    large = np.minimum(large, nb - 1)
    return (ret + np.where(n < max_exact, n, large)).astype(np.int32)


def _bias_body(code_ref, table_ref, o_ref, *, n_buckets, group):
    code = code_ref[...]
    n_cols = code.shape[1]

    def head(hh, carry):
        out = jnp.full(code.shape, -jnp.inf, F32)
        for b in range(n_buckets):
            out = jnp.where(code == b, table_ref[hh, b] * LOG2E, out)
        if group is None:
            o_ref[hh] = out
        else:
            cols = pl.ds(pl.multiple_of((hh % group) * n_cols, LANES), n_cols)
            o_ref[hh // group, :, cols] = out
        return carry

    lax.fori_loop(0, table_ref.shape[0], head, 0)


def _bias_call(code, table, group=None):
    n_heads, n_buckets = table.shape
    rows, cols = code.shape
    shape = (n_heads, rows, cols) if group is None else (n_heads // group, rows, group * cols)
    return pl.pallas_call(
        functools.partial(_bias_body, n_buckets=n_buckets, group=group),
        out_shape=jax.ShapeDtypeStruct(shape, F32),
        in_specs=[pl.BlockSpec(memory_space=pltpu.VMEM), pl.BlockSpec(memory_space=pltpu.SMEM)],
        out_specs=pl.BlockSpec(memory_space=pltpu.VMEM),
        name="rel_bias",
    )(jnp.asarray(code), table)


def _sink_attention_unit(qs, kh, vh, bias, sink_col, maskrow):
    s = lax.dot_general(qs, kh, (((1,), (1,)), ((), ())), preferred_element_type=F32) + bias
    if maskrow is not None:
        s = s + maskrow
    sink_col = sink_col * LOG2E
    m = jnp.maximum(jnp.max(s, axis=-1, keepdims=True), sink_col)
    e = jnp.exp2(s - m)
    l = jnp.sum(e, axis=-1, keepdims=True) + jnp.exp2(sink_col - m)
    return jnp.dot(e.astype(BF16), vh, preferred_element_type=F32) * (1.0 / l)


def _stack_heads(q, h, group):
    hd = HEAD_DIM
    return jnp.concatenate(
        [q[:, (h * group + g) * hd:(h * group + g + 1) * hd] for g in range(group)], axis=0)


def _unstack_heads(o, group):
    tq = o.shape[0] // group
    return [o[g * tq:(g + 1) * tq, :] for g in range(group)]


ATTN_Q_COLS = 256
PAIR = 2 * CHUNK


def _pair_codes(n_buckets):
    code = _bucket_codes(PAIR, WINDOW + PAIR, -WINDOW, n_buckets).T
    key_chunk = np.arange(WINDOW + PAIR)[:, None] // CHUNK
    q_chunk = np.arange(PAIR)[None, :] // CHUNK
    visible = (key_chunk >= q_chunk) & (key_chunk <= q_chunk + WINDOW // CHUNK)
    return np.where(visible, code, n_buckets).astype(np.int32)


def _attn_prompt_body(qT_ref, kp_ref, kc_ref, vTp_ref, vTc_ref, bias_ref, sink_ref, oT_ref,
                      k_scr, vT_scr, *, kvh, group):
    i = pl.program_id(1)
    tq = qT_ref.shape[1]
    nk = WINDOW + PAIR
    hd = HEAD_DIM
    for h in range(kvh):
        k_scr[h, 0:WINDOW, :] = kp_ref[:, h * hd:(h + 1) * hd].astype(BF16)
        k_scr[h, WINDOW:WINDOW + tq, :] = kc_ref[:, h * hd:(h + 1) * hd].astype(BF16)
    vT_scr[:, 0:WINDOW] = vTp_ref[...].astype(BF16)
    vT_scr[:, WINDOW:WINDOW + tq] = vTc_ref[...].astype(BF16)
    row = lax.broadcasted_iota(jnp.int32, (nk, group * PAIR), 0)
    start_mask = jnp.where(row < jnp.where(i == 0, WINDOW, 0), -jnp.inf, 0.0).astype(F32)
    units = [(p * PAIR, h) for p in range(tq // PAIR) for h in range(kvh)]

    def scores(c0, h):
        qsT = jnp.concatenate(
            [qT_ref[hh * hd:(hh + 1) * hd, c0:c0 + PAIR] for hh in range(h * group, (h + 1) * group)],
            axis=1)
        s = jnp.dot(k_scr[h, c0:c0 + nk, :], qsT, preferred_element_type=F32) + bias_ref[h]
        return s + start_mask if c0 == 0 else s

    s_next = scores(*units[0])
    for n, (c0, h) in enumerate(units):
        s = s_next
        if n + 1 < len(units):
            s_next = scores(*units[n + 1])
        sink = sink_ref[h] * LOG2E
        m = jnp.maximum(jnp.max(s, axis=0, keepdims=True), sink)
        e = jnp.exp2(s - m)
        l = jnp.sum(e, axis=0, keepdims=True) + jnp.exp2(sink - m)
        oT = jnp.dot(vT_scr[h * hd:(h + 1) * hd, c0:c0 + nk], e.astype(BF16),
                     preferred_element_type=F32) * (1.0 / l)
        for g in range(group):
            hh = h * group + g
            oT_ref[hh * hd:(hh + 1) * hd, c0:c0 + PAIR] = (
                oT[:, g * PAIR:(g + 1) * PAIR].astype(oT_ref.dtype))


def _attn_prompt_call(qT, k, vT, bias, sink_row, kvh, group):
    b, dq, s = qT.shape
    dk = k.shape[2]
    tq = _tile(s, ATTN_Q_COLS, PAIR)
    per = tq // WINDOW
    prev = lambda i: jnp.maximum(i * per - 1, 0)
    blocks = [((dq, tq), BF16)] * 2 + [((WINDOW + tq, dk), F32)] * 2 + [
        (bias.shape, F32), (sink_row.shape[:1] + (SUBLANES, sink_row.shape[2]), F32)]
    scratch = [((kvh, WINDOW + tq, LANES), BF16), ((dk, WINDOW + tq), BF16)]
    return pl.pallas_call(
        functools.partial(_attn_prompt_body, kvh=kvh, group=group),
        out_shape=jax.ShapeDtypeStruct((b, dq, s), BF16),
        grid=(b, s // tq),
        in_specs=[
            pl.BlockSpec((None, dq, tq), lambda bi, i: (bi, 0, i)),
            pl.BlockSpec((None, WINDOW, dk), lambda bi, i: (bi, prev(i), 0)),
            pl.BlockSpec((None, tq, dk), lambda bi, i: (bi, i, 0)),
            pl.BlockSpec((None, dk, WINDOW), lambda bi, i: (bi, 0, prev(i))),
            pl.BlockSpec((None, dk, tq), lambda bi, i: (bi, 0, i)),
            pl.BlockSpec(bias.shape, lambda bi, i: (0, 0, 0)),
            pl.BlockSpec(sink_row.shape, lambda bi, i: (0, 0, 0)),
        ],
        out_specs=pl.BlockSpec((None, dq, tq), lambda bi, i: (bi, 0, i)),
        scratch_shapes=[pltpu.VMEM((kvh, WINDOW + tq, HEAD_DIM), BF16),
                        pltpu.VMEM((dk, WINDOW + tq), BF16)],
        compiler_params=pltpu.CompilerParams(
            dimension_semantics=("parallel", "arbitrary"),
            vmem_limit_bytes=_vmem_limit(blocks, scratch)),
        name="swa_sink_attention_prompt",
    )(qT, k, k, vT, vT, bias, sink_row)


def _attn_sample_body(q_ref, k_ref, v_ref, bias_ref, sink_ref, o_ref, *, kvh, group):
    hd = HEAD_DIM
    q = q_ref[...].astype(F32)
    outs = []
    for h in range(kvh):
        o = _sink_attention_unit(
            _stack_heads(q, h, group).astype(BF16),
            k_ref[:, h * hd:(h + 1) * hd].astype(BF16), v_ref[:, h * hd:(h + 1) * hd].astype(BF16),
            bias_ref[h], sink_ref[h], None)
        outs += _unstack_heads(o, group)
    o_ref[...] = jnp.concatenate(outs, axis=1).astype(o_ref.dtype)


def _attn_sample_call(q, k_all, v_all, bias, sink_col, kvh, group):
    b, t, dq = q.shape
    nk, dk = k_all.shape[1:]
    blocks = [((t, dq), BF16)] * 2 + [((nk, dk), F32)] * 2 + [
        (bias.shape[:2] + (2 * LANES,), F32), (sink_col.shape[:2] + (LANES,), F32)]
    return pl.pallas_call(
        functools.partial(_attn_sample_body, kvh=kvh, group=group),
        out_shape=jax.ShapeDtypeStruct((b, t, dq), BF16),
        grid=(b,),
        in_specs=[
            pl.BlockSpec((None, t, dq), lambda bi: (bi, 0, 0)),
            pl.BlockSpec((None, nk, dk), lambda bi: (bi, 0, 0)),
            pl.BlockSpec((None, nk, dk), lambda bi: (bi, 0, 0)),
            pl.BlockSpec(bias.shape, lambda bi: (0, 0, 0)),
            pl.BlockSpec(sink_col.shape, lambda bi: (0, 0, 0)),
        ],
        out_specs=pl.BlockSpec((None, t, dq), lambda bi: (bi, 0, 0)),
        compiler_params=pltpu.CompilerParams(
            dimension_semantics=("parallel",), vmem_limit_bytes=_vmem_limit(blocks)),
        name="swa_sink_attention_sample",
    )(q, k_all, v_all, bias, sink_col)


def _trunks(xp3, xs3, mods_p, mods_s, state_conv, win_k, win_v, p):
    bp, tp, d = xp3.shape
    bs, ts, _ = xs3.shape
    mp, ms = bp * tp, bs * ts
    depth = p["w_mod"].shape[0]
    d_ff = p["w_gu"].shape[2] // 2
    n_heads = p["attn_sinks"].shape[1]
    dq = n_heads * HEAD_DIM
    kvh = (p["w_qkv"].shape[2] - dq) // (2 * HEAD_DIM)
    group = n_heads // kvh
    dkv = kvh * HEAD_DIM
    kw = p["w_dw"].shape[1]
    n_buckets = p["rel_bias_table"].shape[1]
    conv_p, k_p, v_p, conv_s, k_s, v_s = [], [], [], [], [], []

    xp, xs = xp3.reshape(mp, d), xs3.reshape(ms, d)
    hp = _norm_mod_call(xp, p["norm_mix"][0], mods_p[0], 0, 1, 1024)
    hs = _norm_mod_call(xs, p["norm_mix"][0], mods_s[0], 0, 1, ms)
    for i in range(depth):
        j = i // 2
        ffn_norm = lambda mods: (p["norm_ffn"][i], mods[i], 3, 4)
        if i % 2 == 0:
            u_p, u_s = _mm_call(hp, p["w_pw1"], j, (0, d), d, x2=hs, bias=p["b_pw1"], act="glu",
                                tm=1024, tn=512, name="pw1_glu")
            u_p, u_s = u_p.reshape(bp, tp, d), u_s.reshape(bs, ts, d)
            conv_w = (p["w_dw"], p["b_dw"], p["conv_ln_g"], p["conv_ln_b"], j)
            z_p = _conv_prompt_call(u_p, *conv_w, 128)
            conv_p.append(u_p[:, tp - (kw - 1):])
            pad = (-(kw - 1 + ts)) % SUBLANES
            up = jnp.concatenate([jnp.zeros((bs, pad, d), F32), state_conv[j], u_s], axis=1)
            z_s = _conv_sample_call(up, pad, ts, *conv_w)
            conv_s.append(up[:, pad + ts:])
            xp, hp, wq = _rowmm_call(z_p.reshape(mp, d), p["w_pw2"], j, xp, mods_p[i], 2,
                                     ffn_norm(mods_p), bias=p["b_pw2"], tm=512,
                                     name="pw2_residual_norm")
            xs, hs = _rowmm_call(z_s.reshape(ms, d), wq, j, xs, mods_s[i], 2,
                                 ffn_norm(mods_s), bias=p["b_pw2"], tm=512,
                                 name="pw2_residual_norm")
        else:
            sinks = p["attn_sinks"][j].reshape(kvh, group, 1)
            qT, q_s = _mm_call(hp, p["w_qkv"], j, (0,), dq, x2=hs, bias=p["b_qkv"],
                               scale=LOG2E * HEAD_DIM ** -0.5, out_dtype=BF16, tm=1024, tn=1024,
                               name="q_proj", out_mode="t", seq=tp)
            (k, vT), kv_s = _mm_call(hp, p["w_qkv"], j, (dq,), 2 * dkv, x2=hs, bias=p["b_qkv"],
                                     tm=1024, tn=2 * dkv, name="kv_proj", out_mode="split_t",
                                     seq=tp)
            k = k.reshape(bp, tp, dkv)
            bias = _bias_call(_pair_codes(n_buckets), p["rel_bias_table"], group)
            sink_row = jnp.repeat(sinks, PAIR, axis=2).reshape(kvh, 1, group * PAIR)
            oT = _attn_prompt_call(qT, k, vT, bias, sink_row, kvh, group)
            k_p.append(k[:, tp - WINDOW:].reshape(bp, WINDOW, kvh, HEAD_DIM))
            v_p.append(jnp.swapaxes(vT[:, :, tp - WINDOW:], 1, 2)
                       .reshape(bp, WINDOW, kvh, HEAD_DIM))
            kv3 = kv_s.reshape(bs, ts, 2, kvh, HEAD_DIM)
            k_all = jnp.concatenate([win_k[j], kv3[:, :, 0]], axis=1)
            v_all = jnp.concatenate([win_v[j], kv3[:, :, 1]], axis=1)
            n_keys = WINDOW + ts
            bias = _bias_call(_bucket_codes(ts, n_keys, -WINDOW, n_buckets),
                              p["rel_bias_table"]).reshape(kvh, group * ts, n_keys)
            o_s = _attn_sample_call(q_s.reshape(bs, ts, dq), k_all.reshape(bs, n_keys, dkv),
                                    v_all.reshape(bs, n_keys, dkv), bias,
                                    jnp.repeat(sinks, ts, axis=1), kvh, group)
            k_s.append(k_all[:, ts:])
            v_s.append(v_all[:, ts:])
            xp, hp, wq = _rowmm_call(oT, p["w_o"], j, xp, mods_p[i], 2, ffn_norm(mods_p),
                                     bias=p["b_o"], x_t=True, tm=512, name="wo_residual_norm")
            xs, hs = _rowmm_call(o_s.reshape(ms, dq), wq, j, xs, mods_s[i], 2,
                                 ffn_norm(mods_s), bias=p["b_o"], tm=512,
                                 name="wo_residual_norm")
        a_p, a_s = _mm_call(hp, p["w_gu"], i, (0, d_ff), d_ff, x2=hs, act="swiglu",
                            out_dtype=BF16, tm=1024, tn=512, name="ffn_gate_up")
        if i + 1 < depth:
            nxt = lambda mods: (p["norm_mix"][i + 1], mods[i + 1], 0, 1)
            xp, hp, wq = _rowmm_call(a_p, p["w_down"], i, xp, mods_p[i], 5, nxt(mods_p), tm=256,
                                     name="ffn_down_residual_norm")
            xs, hs = _rowmm_call(a_s, wq, i, xs, mods_s[i], 5, nxt(mods_s), tm=256,
                                 name="ffn_down_residual_norm")
        else:
            y_p, wq = _rowmm_call(a_p, p["w_down"], i, xp, mods_p[i], 5, (p["norm_out"],), tm=256,
                                  name="ffn_down_residual_final_norm")
            y_s = _rowmm_call(a_s, wq, i, xs, mods_s[i], 5, (p["norm_out"],), tm=256,
                              name="ffn_down_residual_final_norm")
    stack = jnp.stack
    return (y_p.reshape(bp, tp, d), y_s.reshape(bs, ts, d), stack(conv_p), stack(k_p), stack(v_p),
            stack(conv_s), stack(k_s), stack(v_s))


def kernel(x_prompt, x_sample, c_prompt, c_sample, state_conv, cache_win_k, cache_win_v, w_mod, b_mod, norm_mix, norm_ffn, w_pw1, b_pw1, w_dw, b_dw, conv_ln_g, conv_ln_b, w_pw2, b_pw2, w_qkv, b_qkv, w_o, b_o, attn_sinks, rel_bias_table, w_gu, w_down, norm_out):
    p = dict(w_mod=w_mod, b_mod=b_mod, norm_mix=norm_mix, norm_ffn=norm_ffn, w_pw1=w_pw1,
             b_pw1=b_pw1, w_dw=w_dw, b_dw=b_dw, conv_ln_g=conv_ln_g, conv_ln_b=conv_ln_b,
             w_pw2=w_pw2, b_pw2=b_pw2, w_qkv=w_qkv, b_qkv=b_qkv, w_o=w_o, b_o=b_o,
             attn_sinks=attn_sinks, rel_bias_table=rel_bias_table, w_gu=w_gu, w_down=w_down,
             norm_out=norm_out)
    bp, sp, d = x_prompt.shape
    bs, ts, _ = x_sample.shape
    depth = w_mod.shape[0]

    n_c = bp + bs
    c_all = jnp.concatenate(
        [c_prompt, c_sample, jnp.zeros((-n_c % (2 * SUBLANES), d), F32)], axis=0)
    mod_all = _mod_call(c_all, w_mod, b_mod)

    mods_p = [_Mod(mod_all[l, :bp].reshape(bp, 1, 6 * d), sp) for l in range(depth)]
    mods_s = [_Mod(mod_all[l, bp:n_c].reshape(bs, 1, 6 * d), ts) for l in range(depth)]

    return _trunks(x_prompt, x_sample, mods_p, mods_s, state_conv, cache_win_k, cache_win_v, p)
```

```python
import functools
import math

import numpy as np
import jax
import jax.numpy as jnp
from jax import lax
from jax.experimental import pallas as pl
from jax.experimental.pallas import tpu as pltpu

F32 = jnp.float32
BF16 = jnp.bfloat16

CHUNK = 64
WINDOW = 128
HEAD_DIM = 64
MAX_DISTANCE = 128
EPS = 1e-6
LOG2E = math.log2(math.e)

V7X_VMEM_BYTES = 64 * 2**20
V7X_VMEM_CAP = V7X_VMEM_BYTES - 8 * 2**20
SUBLANES = 8
LANES = 128


def _nbytes(shape, dtype):
    return math.prod(shape) * jnp.dtype(dtype).itemsize


def _vmem_limit(blocks, scratch=()):
    est = 2 * sum(_nbytes(s, d) for s, d in blocks) + sum(_nbytes(s, d) for s, d in scratch)
    return int(min(V7X_VMEM_CAP, max(32 * 2**20, 2 * est)))


def _tile(dim, pref, mult=LANES):
    if dim <= pref:
        return dim
    t = (pref // mult) * mult
    while t >= mult:
        if dim % t == 0:
            return t
        t -= mult
    return dim


def _sigmoid(x):
    return 1.0 / (1.0 + jnp.exp(-x))


def _mod_body(c_ref, w_ref, b_ref, o_ref):
    c = c_ref[...]
    a = (c * _sigmoid(c)).astype(BF16)
    o_ref[...] = jnp.dot(a, w_ref[...].astype(BF16), preferred_element_type=F32) + b_ref[...]


def _mod_call(c_all, w_mod, b_mod):
    depth, d, n = w_mod.shape
    mp = c_all.shape[0]
    tn = _tile(n, 1024)
    blocks = [((mp, d), F32), ((d, tn), F32), ((1, tn), F32), ((mp, tn), F32)]
    return pl.pallas_call(
        _mod_body,
        out_shape=jax.ShapeDtypeStruct((depth, mp, n), F32),
        grid=(depth, n // tn),
        in_specs=[
            pl.BlockSpec((mp, d), lambda l, j: (0, 0)),
            pl.BlockSpec((None, d, tn), lambda l, j: (l, 0, j)),
            pl.BlockSpec((None, 1, tn), lambda l, j: (l, 0, j)),
        ],
        out_specs=pl.BlockSpec((None, mp, tn), lambda l, j: (l, 0, j)),
        compiler_params=pltpu.CompilerParams(
            dimension_semantics=("arbitrary", "arbitrary"),
            vmem_limit_bytes=_vmem_limit(blocks, [((d, tn), BF16)])),
        name="adaln_mod",
    )(c_all, w_mod, b_mod.reshape(depth, 1, n))


class _Mod:
    def __init__(self, arr, rows_per_group):
        self.arr = arr
        self.rows_per_group = rows_per_group

    def tile_rows(self, total_rows, pref, mult):
        if self.rows_per_group >= pref:
            return _tile(self.rows_per_group, pref, mult)
        return _tile(total_rows, pref, max(mult, self.rows_per_group))

    def spec(self, seg, d, tm):
        if tm <= self.rows_per_group:
            per = self.rows_per_group // tm
            return pl.BlockSpec((1, 1, d), lambda i: (i // per, 0, seg))
        return pl.BlockSpec((tm // self.rows_per_group, 1, d), lambda i: (i, 0, seg))


def _grouped(v, m):
    return v.reshape(m.shape[0], v.shape[0] // m.shape[0], v.shape[1])


def _norm_mod_body(x_ref, g_ref, sh_ref, sc_ref, o_ref):
    x = x_ref[...]
    y = x * lax.rsqrt(jnp.mean(x * x, axis=-1, keepdims=True) + EPS) * g_ref[...]
    sc = sc_ref[...]
    o_ref[...] = (_grouped(y, sc) * (1.0 + sc) + sh_ref[...]).reshape(x.shape).astype(o_ref.dtype)


def _norm_mod_call(x, g, mod, seg_shift, seg_scale, tm):
    m, d = x.shape
    tm = mod.tile_rows(m, tm, SUBLANES)
    blocks = [((tm, d), F32), ((1, d), F32), ((tm, d), BF16)]
    return pl.pallas_call(
        _norm_mod_body,
        out_shape=jax.ShapeDtypeStruct((m, d), BF16),
        grid=(m // tm,),
        in_specs=[
            pl.BlockSpec((tm, d), lambda i: (i, 0)),
            pl.BlockSpec((1, d), lambda i: (0, 0)),
            mod.spec(seg_shift, d, tm),
            mod.spec(seg_scale, d, tm),
        ],
        out_specs=pl.BlockSpec((tm, d), lambda i: (i, 0)),
        compiler_params=pltpu.CompilerParams(
            dimension_semantics=("parallel",), vmem_limit_bytes=_vmem_limit(blocks)),
        name="rmsnorm_modulate",
    )(x, g.reshape(1, d), mod.arr, mod.arr)


def _mm_body(x_ref, *refs, n_w, has_bias, act, scale, out_mode, has_x2):
    if has_x2:
        x2_ref, refs = refs[0], refs[1:]
    ws = refs[:n_w]
    refs = refs[n_w:]
    bs = refs[:n_w] if has_bias else ()
    refs = refs[len(bs):]
    n_o = 2 if out_mode == "split_t" else 1
    o_refs = refs[:n_o]
    refs = refs[n_o:]
    if has_x2:
        o2_ref, refs = refs[0], refs[1:]
    wbs = refs

    def compute(x):
        ps = []
        for i in range(n_w):
            p = jnp.dot(x, wbs[i][...], preferred_element_type=F32)
            if has_bias:
                p = p + bs[i][...]
            ps.append(p)
        if act == "glu":
            y = ps[0] * _sigmoid(ps[1])
        elif act == "swiglu":
            y = ps[0] * _sigmoid(ps[0]) * ps[1]
        else:
            y = ps[0]
        return y * scale if scale != 1.0 else y

    @pl.when(pl.program_id(1) == 0)
    def _():
        for w, wb in zip(ws, wbs):
            wb[...] = w[...].astype(BF16)
        if has_x2:
            o2_ref[...] = compute(x2_ref[...]).astype(o2_ref.dtype)

    y = compute(x_ref[...])
    if out_mode == "plain":
        o_refs[0][...] = y.astype(o_refs[0].dtype)
    elif out_mode == "t":
        o_refs[0][...] = y.T.astype(o_refs[0].dtype)
    else:
        half = y.shape[1] // 2
        o_refs[0][...] = y[:, :half].astype(o_refs[0].dtype)
        o_refs[1][...] = y[:, half:].T.astype(o_refs[1].dtype)


def _mm_call(x, w, layer, col_starts, n_out, *, x2=None, bias=None, act=None, scale=1.0,
             out_dtype=F32, tm=512, tn=512, name="matmul", out_mode="plain", seq=None):
    m, k = x.shape
    if out_mode == "plain":
        tm = _tile(m, tm, SUBLANES)
    else:
        tm = _tile(seq, tm, LANES)
    tn = _tile(n_out, tn)
    n_w = len(col_starts)
    assert all(c % tn == 0 for c in col_starts) and m % tm == 0 and n_out % tn == 0
    has_bias = bias is not None
    per_b = None if seq is None else seq // tm

    in_specs = [pl.BlockSpec((tm, k), lambda j, i: (i, 0))]
    args = [x]
    blocks = [((tm, k), x.dtype), ((tm, tn), out_dtype)]
    if x2 is not None:
        m2 = x2.shape[0]
        in_specs.append(pl.BlockSpec((m2, k), lambda j, i: (0, 0)))
        args.append(x2)
        blocks += [((m2, k), x2.dtype), ((m2, tn), out_dtype)]
    for c in col_starts:
        off = c // tn
        in_specs.append(pl.BlockSpec((None, k, tn), lambda j, i, off=off: (layer, 0, off + j)))
        args.append(w)
        blocks.append(((k, tn), F32))
    if has_bias:
        b3 = bias.reshape(bias.shape[0], 1, bias.shape[1])
        for c in col_starts:
            off = c // tn
            in_specs.append(pl.BlockSpec((None, 1, tn), lambda j, i, off=off: (layer, 0, off + j)))
            args.append(b3)
    scratch = [((k, tn), BF16)] * n_w
    body = functools.partial(_mm_body, n_w=n_w, has_bias=has_bias, act=act, scale=scale,
                             out_mode=out_mode, has_x2=x2 is not None)
    if out_mode == "plain":
        out_shape = [jax.ShapeDtypeStruct((m, n_out), out_dtype)]
        out_specs = [pl.BlockSpec((tm, tn), lambda j, i: (i, j))]
    elif out_mode == "t":
        out_shape = [jax.ShapeDtypeStruct((m // seq, n_out, seq), out_dtype)]
        out_specs = [pl.BlockSpec((None, tn, tm), lambda j, i: (i // per_b, j, i % per_b))]
    else:
        assert tn == n_out
        half = n_out // 2
        out_shape = [jax.ShapeDtypeStruct((m, half), out_dtype),
                     jax.ShapeDtypeStruct((m // seq, half, seq), out_dtype)]
        out_specs = [pl.BlockSpec((tm, half), lambda j, i: (i, 0)),
                     pl.BlockSpec((None, half, tm), lambda j, i: (i // per_b, 0, i % per_b))]
    if x2 is not None:
        out_shape.append(jax.ShapeDtypeStruct((m2, n_out), out_dtype))
        out_specs.append(pl.BlockSpec((m2, tn), lambda j, i: (0, j)))
    outs = pl.pallas_call(
        body,
        out_shape=out_shape,
        grid=(n_out // tn, m // tm),
        in_specs=in_specs,
        out_specs=out_specs,
        scratch_shapes=[pltpu.VMEM(s, d) for s, d in scratch],
        compiler_params=pltpu.CompilerParams(
            dimension_semantics=("arbitrary", "arbitrary"),
            vmem_limit_bytes=_vmem_limit(blocks, scratch)),
        name=name,
    )(*args)
    n_primary = 2 if out_mode == "split_t" else 1
    primary = outs[0] if n_primary == 1 else outs[:n_primary]
    return primary if x2 is None else (primary, outs[n_primary])


WEIGHT_CHUNK_ROWS = 256
ROWMM_SUB_ROWS = 256


def _rowmm_body(*refs, layer, kc, has_bias, x_t, final, w_is_bf16):
    x_ref, w_hbm = refs[:2]
    refs = refs[2:]
    if has_bias:
        b_ref, refs = refs[0], refs[1:]
    res_ref, gate_ref, ng_ref = refs[:3]
    refs = refs[3:]
    if not final:
        sh_ref, sc_ref = refs[:2]
        refs = refs[2:]
    n_o = 1 if final else 2
    o_refs = refs[:n_o]
    refs = refs[n_o:]
    if w_is_bf16:
        w_res, sem = refs

        @pl.when(pl.program_id(0) == 0)
        def _():
            whole = pltpu.make_async_copy(w_hbm, w_res, sem.at[0])
            whole.start()
            whole.wait()
    else:
        wq_hbm, w_res, stage, sem = refs
        n_chunks = w_res.shape[0] // kc
        write_back = pltpu.make_async_copy(w_res, wq_hbm, sem.at[2])

        def chunk_copy(c):
            return pltpu.make_async_copy(w_hbm.at[layer, pl.ds(c * kc, kc), :], stage.at[c % 2],
                                         sem.at[c % 2])

        @pl.when(pl.program_id(0) == 0)
        def _():
            chunk_copy(0).start()
            for c in range(n_chunks):
                if c + 1 < n_chunks:
                    chunk_copy(c + 1).start()
                chunk_copy(c).wait()
                w_res[c * kc:(c + 1) * kc, :] = stage[c % 2].astype(BF16)
            write_back.start()

        @pl.when(pl.program_id(0) == pl.num_programs(0) - 1)
        def _():
            write_back.wait()

    tm = res_ref.shape[0]
    sub = min(tm, ROWMM_SUB_ROWS)

    def groups_of(ref, r0):
        per = tm // ref.shape[0]
        return ref[...] if per >= tm else ref[r0 // per:(r0 + sub) // per]

    for r0 in range(0, tm, sub):
        x = x_ref[:, r0:r0 + sub].T if x_t else x_ref[r0:r0 + sub, :]
        y = jnp.dot(x, w_res[...], preferred_element_type=F32)
        if has_bias:
            y = y + b_ref[...]
        gate = groups_of(gate_ref, r0)
        xn = (_grouped(res_ref[r0:r0 + sub, :], gate) + gate * _grouped(y, gate)).reshape(y.shape)
        r = xn * lax.rsqrt(jnp.mean(xn * xn, axis=-1, keepdims=True) + EPS) * ng_ref[...]
        if final:
            o_refs[0][r0:r0 + sub, :] = r
        else:
            sc = groups_of(sc_ref, r0)
            o_refs[0][r0:r0 + sub, :] = xn
            o_refs[1][r0:r0 + sub, :] = (
                _grouped(r, sc) * (1.0 + sc) + groups_of(sh_ref, r0)
            ).reshape(y.shape).astype(o_refs[1].dtype)


def _rowmm_call(x, w, layer, res, gate, gate_seg, nxt, *, bias=None, x_t=False, tm=512,
                name="rowmm"):
    w_is_bf16 = w.ndim == 2
    if x_t:
        nb, k, seq = x.shape
        m = nb * seq
    else:
        m, k = x.shape
    d = w.shape[-1]
    final = len(nxt) == 1
    kc = _tile(k, WEIGHT_CHUNK_ROWS, SUBLANES)
    scratch = [((k, d), BF16)] + ([] if w_is_bf16 else [((2, kc, d), F32)])
    n_mods = 1 if final else 3

    def vmem_need(rows):
        blocks = [((rows, k), BF16), ((rows, d), F32), ((rows, d), F32 if final else BF16)]
        blocks += [((rows, d), F32)] * (0 if final else 1)
        blocks += [((-(-rows // gate.rows_per_group) * SUBLANES, d), F32)] * n_mods
        temporaries = 4 * _nbytes((rows, d), F32)
        return (2 * sum(_nbytes(s, t) for s, t in blocks) + sum(_nbytes(s, t) for s, t in scratch)
                + temporaries)

    mult = LANES if x_t else SUBLANES
    tm = gate.tile_rows(m, tm, mult)
    while vmem_need(tm) > V7X_VMEM_CAP and tm % (2 * max(mult, min(tm, gate.rows_per_group))) == 0:
        tm //= 2
    assert final or nxt[1].rows_per_group == gate.rows_per_group
    has_bias = bias is not None
    row = lambda i: (i, 0)
    if x_t:
        per_b = seq // tm
        in_specs = [pl.BlockSpec((None, k, tm), lambda i: (i // per_b, 0, i % per_b))]
    else:
        in_specs = [pl.BlockSpec((tm, k), row)]
    in_specs.append(pl.BlockSpec(memory_space=pltpu.HBM))
    args = [x, w]
    if has_bias:
        in_specs.append(pl.BlockSpec((None, 1, d), lambda i: (layer, 0, 0)))
        args.append(bias.reshape(bias.shape[0], 1, d))
    in_specs += [pl.BlockSpec((tm, d), row), gate.spec(gate_seg, d, tm),
                 pl.BlockSpec((1, d), lambda i: (0, 0))]
    args += [res, gate.arr, nxt[0].reshape(1, d)]
    if final:
        out_shape = jax.ShapeDtypeStruct((m, d), F32)
        out_specs = pl.BlockSpec((tm, d), row)
    else:
        _, nmod, seg_shift, seg_scale = nxt
        in_specs += [nmod.spec(seg_shift, d, tm), nmod.spec(seg_scale, d, tm)]
        args += [nmod.arr, nmod.arr]
        out_shape = [jax.ShapeDtypeStruct((m, d), F32), jax.ShapeDtypeStruct((m, d), BF16)]
        out_specs = [pl.BlockSpec((tm, d), row), pl.BlockSpec((tm, d), row)]
    if final:
        out_shape, out_specs = [out_shape], [out_specs]
    if not w_is_bf16:
        out_shape.append(jax.ShapeDtypeStruct((k, d), BF16))
        out_specs.append(pl.BlockSpec(memory_space=pltpu.HBM))
    outs = pl.pallas_call(
        functools.partial(_rowmm_body, layer=layer, kc=kc, has_bias=has_bias, x_t=x_t,
                          final=final, w_is_bf16=w_is_bf16),
        out_shape=out_shape,
        grid=(m // tm,),
        in_specs=in_specs,
        out_specs=out_specs,
        scratch_shapes=[pltpu.VMEM(s, dt) for s, dt in scratch] + [pltpu.SemaphoreType.DMA((3,))],
        compiler_params=pltpu.CompilerParams(
            dimension_semantics=("arbitrary",),
            vmem_limit_bytes=min(V7X_VMEM_CAP, max(32 * 2**20, vmem_need(tm) + 4 * 2**20))),
        name=name,
    )(*args)
    return outs[0] if len(outs) == 1 else outs


def _conv_ln_swish(up_ref, pad, tt, w_ref, bdw_ref, lg_ref, lb_ref, acc_ref, o_ref):
    kw, d = w_ref.shape
    by_shift = [[(a, SUBLANES * a + s - pad) for a in range((pad + kw - 1) // SUBLANES + 1)
                 if 0 <= SUBLANES * a + s - pad < kw] for s in range(SUBLANES)]

    def strip(c, carry):
        cols = pl.ds(pl.multiple_of(c * LANES, LANES), LANES)
        z = bdw_ref[:, cols]
        for s, taps in enumerate(by_shift):
            n = tt + SUBLANES if s else tt
            q = None
            for a, k in taps:
                term = up_ref[pl.ds(SUBLANES * a, n), cols] * w_ref[pl.ds(k, 1), cols]
                q = term if q is None else q + term
            if q is not None:
                z = z + q[s:s + tt]
        acc_ref[:, cols] = z
        return carry

    lax.fori_loop(0, d // LANES, strip, 0)
    z = acc_ref[...]
    mu = jnp.mean(z, axis=-1, keepdims=True)
    zc = z - mu
    y = zc * lax.rsqrt(jnp.mean(zc * zc, axis=-1, keepdims=True) + EPS)
    y = y * lg_ref[...] + lb_ref[...]
    o_ref[...] = (y * _sigmoid(y)).astype(o_ref.dtype)


def _conv_prompt_body(main_ref, halo_ref, w_ref, bdw_ref, lg_ref, lb_ref, o_ref, up_ref, acc_ref,
                      *, halo, tt):
    kw = w_ref.shape[0]

    @pl.when(pl.program_id(1) == 0)
    def _():
        up_ref[0:halo, :] = jnp.zeros((halo, up_ref.shape[1]), F32)

    @pl.when(pl.program_id(1) > 0)
    def _():
        up_ref[0:halo, :] = halo_ref[...]

    up_ref[halo:halo + tt, :] = main_ref[...]
    _conv_ln_swish(up_ref, halo - (kw - 1), tt, w_ref, bdw_ref, lg_ref, lb_ref, acc_ref, o_ref)


def _conv_prompt_call(u, w_dw, b_dw, ln_g, ln_b, layer, tt):
    b, t, d = u.shape
    kw = w_dw.shape[1]
    halo = -(-(kw - 1) // SUBLANES) * SUBLANES
    tt = _tile(t, tt, halo)
    hb = tt // halo
    vec = lambda: pl.BlockSpec((None, 1, d), lambda bi, ti: (layer, 0, 0))
    blocks = [((tt, d), F32), ((halo, d), F32), ((kw, d), F32), ((tt, d), BF16)]
    scratch = [((halo + tt, d), F32), ((tt, d), F32)]
    return pl.pallas_call(
        functools.partial(_conv_prompt_body, halo=halo, tt=tt),
        out_shape=jax.ShapeDtypeStruct((b, t, d), BF16),
        grid=(b, t // tt),
        in_specs=[
            pl.BlockSpec((None, tt, d), lambda bi, ti: (bi, ti, 0)),
            pl.BlockSpec((None, halo, d), lambda bi, ti: (bi, jnp.maximum(ti * hb - 1, 0), 0)),
            pl.BlockSpec((None, kw, d), lambda bi, ti: (layer, 0, 0)),
            vec(), vec(), vec(),
        ],
        out_specs=pl.BlockSpec((None, tt, d), lambda bi, ti: (bi, ti, 0)),
        scratch_shapes=[pltpu.VMEM(s, dt) for s, dt in scratch],
        compiler_params=pltpu.CompilerParams(
            dimension_semantics=("parallel", "arbitrary"),
            vmem_limit_bytes=_vmem_limit(blocks, scratch)),
        name="dwconv_ln_swish_prompt",
    )(u, u, w_dw, b_dw.reshape(-1, 1, d), ln_g.reshape(-1, 1, d), ln_b.reshape(-1, 1, d))


def _conv_sample_body(up_ref, w_ref, bdw_ref, lg_ref, lb_ref, o_ref, acc_ref, *, pad, tt):
    _conv_ln_swish(up_ref, pad, tt, w_ref, bdw_ref, lg_ref, lb_ref, acc_ref, o_ref)


def _conv_sample_call(up, pad, tt, w_dw, b_dw, ln_g, ln_b, layer):
    b, rows, d = up.shape
    kw = w_dw.shape[1]
    vec = lambda: pl.BlockSpec((None, 1, d), lambda bi: (layer, 0, 0))
    blocks = [((rows, d), F32), ((kw, d), F32), ((tt, d), BF16)]
    scratch = [((tt, d), F32)]
    return pl.pallas_call(
        functools.partial(_conv_sample_body, pad=pad, tt=tt),
        out_shape=jax.ShapeDtypeStruct((b, tt, d), BF16),
        grid=(b,),
        in_specs=[
            pl.BlockSpec((None, rows, d), lambda bi: (bi, 0, 0)),
            pl.BlockSpec((None, kw, d), lambda bi: (layer, 0, 0)),
            vec(), vec(), vec(),
        ],
        out_specs=pl.BlockSpec((None, tt, d), lambda bi: (bi, 0, 0)),
        scratch_shapes=[pltpu.VMEM(s, dt) for s, dt in scratch],
        compiler_params=pltpu.CompilerParams(
            dimension_semantics=("parallel",), vmem_limit_bytes=_vmem_limit(blocks, scratch)),
        name="dwconv_ln_swish_sample",
    )(up, w_dw, b_dw.reshape(-1, 1, d), ln_g.reshape(-1, 1, d), ln_b.reshape(-1, 1, d))


def _bucket_codes(n_q, n_k, k_off, n_buckets):
    rel = (np.arange(n_k) + k_off)[None, :] - np.arange(n_q)[:, None]
    nb = n_buckets // 2
    max_exact = nb // 2
    ret = np.where(rel > 0, nb, 0)
    n = np.abs(rel)
    nf = np.maximum(n, 1).astype(np.float32)
    large = max_exact + (np.log(nf / np.float32(max_exact))
                         / np.float32(math.log(MAX_DISTANCE / max_exact))
                         * np.float32(nb - max_exact)).astype(np.int32)
    large = np.minimum(large, nb - 1)
    return (ret + np.where(n < max_exact, n, large)).astype(np.int32)


def _bias_body(code_ref, table_ref, o_ref, *, n_buckets, group):
    code = code_ref[...]
    n_cols = code.shape[1]

    def head(hh, carry):
        out = jnp.full(code.shape, -jnp.inf, F32)
        for b in range(n_buckets):
            out = jnp.where(code == b, table_ref[hh, b] * LOG2E, out)
        if group is None:
            o_ref[hh] = out
        else:
            cols = pl.ds(pl.multiple_of((hh % group) * n_cols, LANES), n_cols)
            o_ref[hh // group, :, cols] = out
        return carry

    lax.fori_loop(0, table_ref.shape[0], head, 0)


def _bias_call(code, table, group=None):
    n_heads, n_buckets = table.shape
    rows, cols = code.shape
    shape = (n_heads, rows, cols) if group is None else (n_heads // group, rows, group * cols)
    return pl.pallas_call(
        functools.partial(_bias_body, n_buckets=n_buckets, group=group),
        out_shape=jax.ShapeDtypeStruct(shape, F32),
        in_specs=[pl.BlockSpec(memory_space=pltpu.VMEM), pl.BlockSpec(memory_space=pltpu.SMEM)],
        out_specs=pl.BlockSpec(memory_space=pltpu.VMEM),
        name="rel_bias",
    )(jnp.asarray(code), table)


def _unit_scores(qs, kh, bias):
    return lax.dot_general(qs, kh, (((1,), (1,)), ((), ())), preferred_element_type=F32) + bias


def _unit_output(s, vh, sink_col):
    sink_col = sink_col * LOG2E
    m = jnp.maximum(jnp.max(s, axis=-1, keepdims=True), sink_col)
    e = jnp.exp2(s - m)
    l = jnp.sum(e, axis=-1, keepdims=True) + jnp.exp2(sink_col - m)
    return jnp.dot(e.astype(BF16), vh, preferred_element_type=F32) * (1.0 / l)


def _stack_heads(q, h, group):
    hd = HEAD_DIM
    return jnp.concatenate(
        [q[:, (h * group + g) * hd:(h * group + g + 1) * hd] for g in range(group)], axis=0)


def _unstack_heads(o, group):
    tq = o.shape[0] // group
    return [o[g * tq:(g + 1) * tq, :] for g in range(group)]


ATTN_Q_COLS = 512
PAIR = 2 * CHUNK


def _pair_codes(n_buckets):
    code = _bucket_codes(PAIR, WINDOW + PAIR, -WINDOW, n_buckets).T
    key_chunk = np.arange(WINDOW + PAIR)[:, None] // CHUNK
    q_chunk = np.arange(PAIR)[None, :] // CHUNK
    visible = (key_chunk >= q_chunk) & (key_chunk <= q_chunk + WINDOW // CHUNK)
    return np.where(visible, code, n_buckets).astype(np.int32)


def _attn_prompt_body(qT_ref, kp_ref, kc_ref, vTp_ref, vTc_ref, bias_ref, sink_ref, oT_ref,
                      k_scr, vT_scr, *, kvh, group):
    i = pl.program_id(1)
    tq = qT_ref.shape[1]
    nk = WINDOW + PAIR
    hd = HEAD_DIM
    for h in range(kvh):
        k_scr[h, 0:WINDOW, :] = kp_ref[:, h * hd:(h + 1) * hd].astype(BF16)
        k_scr[h, WINDOW:WINDOW + tq, :] = kc_ref[:, h * hd:(h + 1) * hd].astype(BF16)
    vT_scr[:, 0:WINDOW] = vTp_ref[...].astype(BF16)
    vT_scr[:, WINDOW:WINDOW + tq] = vTc_ref[...].astype(BF16)
    row = lax.broadcasted_iota(jnp.int32, (nk, group * PAIR), 0)
    start_mask = jnp.where(row < jnp.where(i == 0, WINDOW, 0), -jnp.inf, 0.0).astype(F32)
    units = [(p * PAIR, h) for p in range(tq // PAIR) for h in range(kvh)]

    def scores(c0, h):
        qsT = jnp.concatenate(
            [qT_ref[hh * hd:(hh + 1) * hd, c0:c0 + PAIR] for hh in range(h * group, (h + 1) * group)],
            axis=1)
        s = jnp.dot(k_scr[h, c0:c0 + nk, :], qsT, preferred_element_type=F32) + bias_ref[h]
        return s + start_mask if c0 == 0 else s

    s_next = scores(*units[0])
    for n, (c0, h) in enumerate(units):
        s = s_next
        if n + 1 < len(units):
            s_next = scores(*units[n + 1])
        sink = sink_ref[h] * LOG2E
        m = jnp.maximum(jnp.max(s, axis=0, keepdims=True), sink)
        e = jnp.exp2(s - m)
        l = jnp.sum(e, axis=0, keepdims=True) + jnp.exp2(sink - m)
        oT = jnp.dot(vT_scr[h * hd:(h + 1) * hd, c0:c0 + nk], e.astype(BF16),
                     preferred_element_type=F32) * (1.0 / l)
        for g in range(group):
            hh = h * group + g
            oT_ref[hh * hd:(hh + 1) * hd, c0:c0 + PAIR] = (
                oT[:, g * PAIR:(g + 1) * PAIR].astype(oT_ref.dtype))


def _attn_prompt_call(qT, k, vT, bias, sink_row, kvh, group):
    b, dq, s = qT.shape
    dk = k.shape[2]
    tq = _tile(s, ATTN_Q_COLS, PAIR)
    per = tq // WINDOW
    prev = lambda i: jnp.maximum(i * per - 1, 0)
    blocks = [((dq, tq), BF16)] * 2 + [((WINDOW + tq, dk), F32)] * 2 + [
        (bias.shape, F32), (sink_row.shape[:1] + (SUBLANES, sink_row.shape[2]), F32)]
    scratch = [((kvh, WINDOW + tq, LANES), BF16), ((dk, WINDOW + tq), BF16)]
    return pl.pallas_call(
        functools.partial(_attn_prompt_body, kvh=kvh, group=group),
        out_shape=jax.ShapeDtypeStruct((b, dq, s), BF16),
        grid=(b, s // tq),
        in_specs=[
            pl.BlockSpec((None, dq, tq), lambda bi, i: (bi, 0, i)),
            pl.BlockSpec((None, WINDOW, dk), lambda bi, i: (bi, prev(i), 0)),
            pl.BlockSpec((None, tq, dk), lambda bi, i: (bi, i, 0)),
            pl.BlockSpec((None, dk, WINDOW), lambda bi, i: (bi, 0, prev(i))),
            pl.BlockSpec((None, dk, tq), lambda bi, i: (bi, 0, i)),
            pl.BlockSpec(bias.shape, lambda bi, i: (0, 0, 0)),
            pl.BlockSpec(sink_row.shape, lambda bi, i: (0, 0, 0)),
        ],
        out_specs=pl.BlockSpec((None, dq, tq), lambda bi, i: (bi, 0, i)),
        scratch_shapes=[pltpu.VMEM((kvh, WINDOW + tq, HEAD_DIM), BF16),
                        pltpu.VMEM((dk, WINDOW + tq), BF16)],
        compiler_params=pltpu.CompilerParams(
            dimension_semantics=("parallel", "arbitrary"),
            vmem_limit_bytes=_vmem_limit(blocks, scratch)),
        name="swa_sink_attention_prompt",
    )(qT, k, k, vT, vT, bias, sink_row)


ATTN_SAMPLE_BATCH = 4


def _attn_sample_body(q_ref, k_ref, v_ref, bias_ref, sink_ref, o_ref, *, kvh, group):
    hd = HEAD_DIM
    units = [(b, h) for b in range(q_ref.shape[0]) for h in range(kvh)]

    def scores(b, h):
        qs = _stack_heads(q_ref[b].astype(F32), h, group).astype(BF16)
        return _unit_scores(qs, k_ref[b, :, h * hd:(h + 1) * hd].astype(BF16), bias_ref[h])

    s_next = scores(*units[0])
    outs = []
    for n, (b, h) in enumerate(units):
        s = s_next
        if n + 1 < len(units):
            s_next = scores(*units[n + 1])
        o = _unit_output(s, v_ref[b, :, h * hd:(h + 1) * hd].astype(BF16), sink_ref[h])
        outs += _unstack_heads(o, group)
        if h == kvh - 1:
            o_ref[b] = jnp.concatenate(outs, axis=1).astype(o_ref.dtype)
            outs = []


def _attn_sample_call(q, k_all, v_all, bias, sink_col, kvh, group):
    b, t, dq = q.shape
    nk, dk = k_all.shape[1:]
    bb = math.gcd(b, ATTN_SAMPLE_BATCH)
    blocks = [((bb, t, dq), BF16)] * 2 + [((bb, nk, dk), F32)] * 2 + [
        (bias.shape[:2] + (2 * LANES,), F32), (sink_col.shape[:2] + (LANES,), F32)]
    return pl.pallas_call(
        functools.partial(_attn_sample_body, kvh=kvh, group=group),
        out_shape=jax.ShapeDtypeStruct((b, t, dq), BF16),
        grid=(b // bb,),
        in_specs=[
            pl.BlockSpec((bb, t, dq), lambda bi: (bi, 0, 0)),
            pl.BlockSpec((bb, nk, dk), lambda bi: (bi, 0, 0)),
            pl.BlockSpec((bb, nk, dk), lambda bi: (bi, 0, 0)),
            pl.BlockSpec(bias.shape, lambda bi: (0, 0, 0)),
            pl.BlockSpec(sink_col.shape, lambda bi: (0, 0, 0)),
        ],
        out_specs=pl.BlockSpec((bb, t, dq), lambda bi: (bi, 0, 0)),
        compiler_params=pltpu.CompilerParams(
            dimension_semantics=("parallel",), vmem_limit_bytes=_vmem_limit(blocks)),
        name="swa_sink_attention_sample",
    )(q, k_all, v_all, bias, sink_col)


def _trunks(xp3, xs3, mods_p, mods_s, state_conv, win_k, win_v, p):
    bp, tp, d = xp3.shape
    bs, ts, _ = xs3.shape
    mp, ms = bp * tp, bs * ts
    depth = p["w_mod"].shape[0]
    d_ff = p["w_gu"].shape[2] // 2
    n_heads = p["attn_sinks"].shape[1]
    dq = n_heads * HEAD_DIM
    kvh = (p["w_qkv"].shape[2] - dq) // (2 * HEAD_DIM)
    group = n_heads // kvh
    dkv = kvh * HEAD_DIM
    kw = p["w_dw"].shape[1]
    n_buckets = p["rel_bias_table"].shape[1]
    conv_p, k_p, v_p, conv_s, k_s, v_s = [], [], [], [], [], []

    xp, xs = xp3.reshape(mp, d), xs3.reshape(ms, d)
    hp = _norm_mod_call(xp, p["norm_mix"][0], mods_p[0], 0, 1, 1024)
    hs = _norm_mod_call(xs, p["norm_mix"][0], mods_s[0], 0, 1, ms)
    for i in range(depth):
        j = i // 2
        ffn_norm = lambda mods: (p["norm_ffn"][i], mods[i], 3, 4)
        if i % 2 == 0:
            u_p, u_s = _mm_call(hp, p["w_pw1"], j, (0, d), d, x2=hs, bias=p["b_pw1"], act="glu",
                                tm=1024, tn=512, name="pw1_glu")
            u_p, u_s = u_p.reshape(bp, tp, d), u_s.reshape(bs, ts, d)
            conv_w = (p["w_dw"], p["b_dw"], p["conv_ln_g"], p["conv_ln_b"], j)
            z_p = _conv_prompt_call(u_p, *conv_w, 128)
            conv_p.append(u_p[:, tp - (kw - 1):])
            pad = (-(kw - 1 + ts)) % SUBLANES
            up = jnp.concatenate([jnp.zeros((bs, pad, d), F32), state_conv[j], u_s], axis=1)
            z_s = _conv_sample_call(up, pad, ts, *conv_w)
            conv_s.append(up[:, pad + ts:])
            xp, hp, wq = _rowmm_call(z_p.reshape(mp, d), p["w_pw2"], j, xp, mods_p[i], 2,
                                     ffn_norm(mods_p), bias=p["b_pw2"], tm=512,
                                     name="pw2_residual_norm")
            xs, hs = _rowmm_call(z_s.reshape(ms, d), wq, j, xs, mods_s[i], 2,
                                 ffn_norm(mods_s), bias=p["b_pw2"], tm=512,
                                 name="pw2_residual_norm")
        else:
            sinks = p["attn_sinks"][j].reshape(kvh, group, 1)
            qT, q_s = _mm_call(hp, p["w_qkv"], j, (0,), dq, x2=hs, bias=p["b_qkv"],
                               scale=LOG2E * HEAD_DIM ** -0.5, out_dtype=BF16, tm=1024, tn=1024,
                               name="q_proj", out_mode="t", seq=tp)
            (k, vT), kv_s = _mm_call(hp, p["w_qkv"], j, (dq,), 2 * dkv, x2=hs, bias=p["b_qkv"],
                                     tm=1024, tn=2 * dkv, name="kv_proj", out_mode="split_t",
                                     seq=tp)
            k = k.reshape(bp, tp, dkv)
            bias = _bias_call(_pair_codes(n_buckets), p["rel_bias_table"], group)
            sink_row = jnp.repeat(sinks, PAIR, axis=2).reshape(kvh, 1, group * PAIR)
            oT = _attn_prompt_call(qT, k, vT, bias, sink_row, kvh, group)
            k_p.append(k[:, tp - WINDOW:].reshape(bp, WINDOW, kvh, HEAD_DIM))
            v_p.append(jnp.swapaxes(vT[:, :, tp - WINDOW:], 1, 2)
                       .reshape(bp, WINDOW, kvh, HEAD_DIM))
            kv3 = kv_s.reshape(bs, ts, 2, kvh, HEAD_DIM)
            k_all = jnp.concatenate([win_k[j], kv3[:, :, 0]], axis=1)
            v_all = jnp.concatenate([win_v[j], kv3[:, :, 1]], axis=1)
            n_keys = WINDOW + ts
            bias = _bias_call(_bucket_codes(ts, n_keys, -WINDOW, n_buckets),
                              p["rel_bias_table"]).reshape(kvh, group * ts, n_keys)
            o_s = _attn_sample_call(q_s.reshape(bs, ts, dq), k_all.reshape(bs, n_keys, dkv),
                                    v_all.reshape(bs, n_keys, dkv), bias,
                                    jnp.repeat(sinks, ts, axis=1), kvh, group)
            k_s.append(k_all[:, ts:])
            v_s.append(v_all[:, ts:])
            xp, hp, wq = _rowmm_call(oT, p["w_o"], j, xp, mods_p[i], 2, ffn_norm(mods_p),
                                     bias=p["b_o"], x_t=True, tm=512, name="wo_residual_norm")
            xs, hs = _rowmm_call(o_s.reshape(ms, dq), wq, j, xs, mods_s[i], 2,
                                 ffn_norm(mods_s), bias=p["b_o"], tm=512,
                                 name="wo_residual_norm")
        a_p, a_s = _mm_call(hp, p["w_gu"], i, (0, d_ff), d_ff, x2=hs, act="swiglu",
                            out_dtype=BF16, tm=1024, tn=512, name="ffn_gate_up")
        if i + 1 < depth:
            nxt = lambda mods: (p["norm_mix"][i + 1], mods[i + 1], 0, 1)
            xp, hp, wq = _rowmm_call(a_p, p["w_down"], i, xp, mods_p[i], 5, nxt(mods_p), tm=256,
                                     name="ffn_down_residual_norm")
            xs, hs = _rowmm_call(a_s, wq, i, xs, mods_s[i], 5, nxt(mods_s), tm=256,
                                 name="ffn_down_residual_norm")
        else:
            y_p, wq = _rowmm_call(a_p, p["w_down"], i, xp, mods_p[i], 5, (p["norm_out"],), tm=256,
                                  name="ffn_down_residual_final_norm")
            y_s = _rowmm_call(a_s, wq, i, xs, mods_s[i], 5, (p["norm_out"],), tm=256,
                              name="ffn_down_residual_final_norm")
    stack = jnp.stack
    return (y_p.reshape(bp, tp, d), y_s.reshape(bs, ts, d), stack(conv_p), stack(k_p), stack(v_p),
            stack(conv_s), stack(k_s), stack(v_s))


def kernel(x_prompt, x_sample, c_prompt, c_sample, state_conv, cache_win_k, cache_win_v, w_mod, b_mod, norm_mix, norm_ffn, w_pw1, b_pw1, w_dw, b_dw, conv_ln_g, conv_ln_b, w_pw2, b_pw2, w_qkv, b_qkv, w_o, b_o, attn_sinks, rel_bias_table, w_gu, w_down, norm_out):
    p = dict(w_mod=w_mod, b_mod=b_mod, norm_mix=norm_mix, norm_ffn=norm_ffn, w_pw1=w_pw1,
             b_pw1=b_pw1, w_dw=w_dw, b_dw=b_dw, conv_ln_g=conv_ln_g, conv_ln_b=conv_ln_b,
             w_pw2=w_pw2, b_pw2=b_pw2, w_qkv=w_qkv, b_qkv=b_qkv, w_o=w_o, b_o=b_o,
             attn_sinks=attn_sinks, rel_bias_table=rel_bias_table, w_gu=w_gu, w_down=w_down,
             norm_out=norm_out)
    bp, sp, d = x_prompt.shape
    bs, ts, _ = x_sample.shape
    depth = w_mod.shape[0]

    n_c = bp + bs
    c_all = jnp.concatenate(
        [c_prompt, c_sample, jnp.zeros((-n_c % (2 * SUBLANES), d), F32)], axis=0)
    mod_all = _mod_call(c_all, w_mod, b_mod)

    mods_p = [_Mod(mod_all[l, :bp].reshape(bp, 1, 6 * d), sp) for l in range(depth)]
    mods_s = [_Mod(mod_all[l, bp:n_c].reshape(bs, 1, 6 * d), ts) for l in range(depth)]

    return _trunks(x_prompt, x_sample, mods_p, mods_s, state_conv, cache_win_k, cache_win_v, p)
```

```python
import functools
import math

import numpy as np
import jax
import jax.numpy as jnp
from jax import lax
from jax.experimental import pallas as pl
from jax.experimental.pallas import tpu as pltpu

F32 = jnp.float32
BF16 = jnp.bfloat16

CHUNK = 64
WINDOW = 128
HEAD_DIM = 64
MAX_DISTANCE = 128
EPS = 1e-6
LOG2E = math.log2(math.e)

V7X_VMEM_BYTES = 64 * 2**20
V7X_VMEM_CAP = V7X_VMEM_BYTES - 8 * 2**20
SUBLANES = 8
LANES = 128


def _nbytes(shape, dtype):
    return math.prod(shape) * jnp.dtype(dtype).itemsize


def _vmem_limit(blocks, scratch=()):
    est = 2 * sum(_nbytes(s, d) for s, d in blocks) + sum(_nbytes(s, d) for s, d in scratch)
    return int(min(V7X_VMEM_CAP, max(32 * 2**20, 2 * est)))


def _tile(dim, pref, mult=LANES):
    if dim <= pref:
        return dim
    t = (pref // mult) * mult
    while t >= mult:
        if dim % t == 0:
            return t
        t -= mult
    return dim


def _sigmoid(x):
    return 1.0 / (1.0 + jnp.exp(-x))


def _mod_body(c_ref, w_ref, b_ref, o_ref):
    c = c_ref[...]
    a = (c * _sigmoid(c)).astype(BF16)
    o_ref[...] = jnp.dot(a, w_ref[...].astype(BF16), preferred_element_type=F32) + b_ref[...]


def _mod_call(c_all, w_mod, b_mod):
    depth, d, n = w_mod.shape
    mp = c_all.shape[0]
    tn = _tile(n, 1024)
    blocks = [((mp, d), F32), ((d, tn), F32), ((1, tn), F32), ((mp, tn), F32)]
    return pl.pallas_call(
        _mod_body,
        out_shape=jax.ShapeDtypeStruct((depth, mp, n), F32),
        grid=(depth, n // tn),
        in_specs=[
            pl.BlockSpec((mp, d), lambda l, j: (0, 0)),
            pl.BlockSpec((None, d, tn), lambda l, j: (l, 0, j)),
            pl.BlockSpec((None, 1, tn), lambda l, j: (l, 0, j)),
        ],
        out_specs=pl.BlockSpec((None, mp, tn), lambda l, j: (l, 0, j)),
        compiler_params=pltpu.CompilerParams(
            dimension_semantics=("arbitrary", "arbitrary"),
            vmem_limit_bytes=_vmem_limit(blocks, [((d, tn), BF16)])),
        name="adaln_mod",
    )(c_all, w_mod, b_mod.reshape(depth, 1, n))


class _Mod:
    def __init__(self, arr, rows_per_group):
        self.arr = arr
        self.rows_per_group = rows_per_group

    def tile_rows(self, total_rows, pref, mult):
        if self.rows_per_group >= pref:
            return _tile(self.rows_per_group, pref, mult)
        return _tile(total_rows, pref, max(mult, self.rows_per_group))

    def spec(self, seg, d, tm):
        if tm <= self.rows_per_group:
            per = self.rows_per_group // tm
            return pl.BlockSpec((1, 1, d), lambda i: (i // per, 0, seg))
        return pl.BlockSpec((tm // self.rows_per_group, 1, d), lambda i: (i, 0, seg))


def _grouped(v, m):
    return v.reshape(m.shape[0], v.shape[0] // m.shape[0], v.shape[1])


def _norm_mod_body(x_ref, g_ref, sh_ref, sc_ref, o_ref):
    x = x_ref[...]
    y = x * lax.rsqrt(jnp.mean(x * x, axis=-1, keepdims=True) + EPS) * g_ref[...]
    sc = sc_ref[...]
    o_ref[...] = (_grouped(y, sc) * (1.0 + sc) + sh_ref[...]).reshape(x.shape).astype(o_ref.dtype)


def _norm_mod_call(x, g, mod, seg_shift, seg_scale, tm):
    m, d = x.shape
    tm = mod.tile_rows(m, tm, SUBLANES)
    blocks = [((tm, d), F32), ((1, d), F32), ((tm, d), BF16)]
    return pl.pallas_call(
        _norm_mod_body,
        out_shape=jax.ShapeDtypeStruct((m, d), BF16),
        grid=(m // tm,),
        in_specs=[
            pl.BlockSpec((tm, d), lambda i: (i, 0)),
            pl.BlockSpec((1, d), lambda i: (0, 0)),
            mod.spec(seg_shift, d, tm),
            mod.spec(seg_scale, d, tm),
        ],
        out_specs=pl.BlockSpec((tm, d), lambda i: (i, 0)),
        compiler_params=pltpu.CompilerParams(
            dimension_semantics=("parallel",), vmem_limit_bytes=_vmem_limit(blocks)),
        name="rmsnorm_modulate",
    )(x, g.reshape(1, d), mod.arr, mod.arr)


def _mm_body(x_ref, *refs, n_w, has_bias, act, scale, out_mode, has_x2):
    if has_x2:
        x2_ref, refs = refs[0], refs[1:]
    ws = refs[:n_w]
    refs = refs[n_w:]
    bs = refs[:n_w] if has_bias else ()
    refs = refs[len(bs):]
    n_o = 2 if out_mode == "split_t" else 1
    o_refs = refs[:n_o]
    refs = refs[n_o:]
    if has_x2:
        o2_ref, refs = refs[0], refs[1:]
    wbs = refs

    def compute(x):
        ps = [None] * n_w
        for i in ((1, 0) if act == "glu" else range(n_w)):
            p = jnp.dot(x, wbs[i][...], preferred_element_type=F32)
            if has_bias:
                p = p + bs[i][...]
            ps[i] = p
        if act == "glu":
            y = ps[0] * _sigmoid(ps[1])
        elif act == "swiglu":
            y = ps[0] * _sigmoid(ps[0]) * ps[1]
        else:
            y = ps[0]
        return y * scale if scale != 1.0 else y

    @pl.when(pl.program_id(1) == 0)
    def _():
        for w, wb in zip(ws, wbs):
            wb[...] = w[...].astype(BF16)
        if has_x2:
            o2_ref[...] = compute(x2_ref[...]).astype(o2_ref.dtype)

    y = compute(x_ref[...])
    if out_mode == "plain":
        o_refs[0][...] = y.astype(o_refs[0].dtype)
    elif out_mode == "t":
        o_refs[0][...] = y.T.astype(o_refs[0].dtype)
    else:
        half = y.shape[1] // 2
        o_refs[0][...] = y[:, :half].astype(o_refs[0].dtype)
        o_refs[1][...] = y[:, half:].T.astype(o_refs[1].dtype)


def _mm_call(x, w, layer, col_starts, n_out, *, x2=None, bias=None, act=None, scale=1.0,
             out_dtype=F32, tm=512, tn=512, name="matmul", out_mode="plain", seq=None):
    m, k = x.shape
    if out_mode == "plain":
        tm = _tile(m, tm, SUBLANES)
    else:
        tm = _tile(seq, tm, LANES)
    tn = _tile(n_out, tn)
    n_w = len(col_starts)
    assert all(c % tn == 0 for c in col_starts) and m % tm == 0 and n_out % tn == 0
    has_bias = bias is not None
    per_b = None if seq is None else seq // tm

    in_specs = [pl.BlockSpec((tm, k), lambda j, i: (i, 0))]
    args = [x]
    blocks = [((tm, k), x.dtype), ((tm, tn), out_dtype)]
    if x2 is not None:
        m2 = x2.shape[0]
        in_specs.append(pl.BlockSpec((m2, k), lambda j, i: (0, 0)))
        args.append(x2)
        blocks += [((m2, k), x2.dtype), ((m2, tn), out_dtype)]
    for c in col_starts:
        off = c // tn
        in_specs.append(pl.BlockSpec((None, k, tn), lambda j, i, off=off: (layer, 0, off + j)))
        args.append(w)
        blocks.append(((k, tn), F32))
    if has_bias:
        b3 = bias.reshape(bias.shape[0], 1, bias.shape[1])
        for c in col_starts:
            off = c // tn
            in_specs.append(pl.BlockSpec((None, 1, tn), lambda j, i, off=off: (layer, 0, off + j)))
            args.append(b3)
    scratch = [((k, tn), BF16)] * n_w
    body = functools.partial(_mm_body, n_w=n_w, has_bias=has_bias, act=act, scale=scale,
                             out_mode=out_mode, has_x2=x2 is not None)
    if out_mode == "plain":
        out_shape = [jax.ShapeDtypeStruct((m, n_out), out_dtype)]
        out_specs = [pl.BlockSpec((tm, tn), lambda j, i: (i, j))]
    elif out_mode == "t":
        out_shape = [jax.ShapeDtypeStruct((m // seq, n_out, seq), out_dtype)]
        out_specs = [pl.BlockSpec((None, tn, tm), lambda j, i: (i // per_b, j, i % per_b))]
    else:
        assert tn == n_out
        half = n_out // 2
        out_shape = [jax.ShapeDtypeStruct((m, half), out_dtype),
                     jax.ShapeDtypeStruct((m // seq, half, seq), out_dtype)]
        out_specs = [pl.BlockSpec((tm, half), lambda j, i: (i, 0)),
                     pl.BlockSpec((None, half, tm), lambda j, i: (i // per_b, 0, i % per_b))]
    if x2 is not None:
        out_shape.append(jax.ShapeDtypeStruct((m2, n_out), out_dtype))
        out_specs.append(pl.BlockSpec((m2, tn), lambda j, i: (0, j)))
    outs = pl.pallas_call(
        body,
        out_shape=out_shape,
        grid=(n_out // tn, m // tm),
        in_specs=in_specs,
        out_specs=out_specs,
        scratch_shapes=[pltpu.VMEM(s, d) for s, d in scratch],
        compiler_params=pltpu.CompilerParams(
            dimension_semantics=("arbitrary", "arbitrary"),
            vmem_limit_bytes=_vmem_limit(blocks, scratch)),
        name=name,
    )(*args)
    n_primary = 2 if out_mode == "split_t" else 1
    primary = outs[0] if n_primary == 1 else outs[:n_primary]
    return primary if x2 is None else (primary, outs[n_primary])


WEIGHT_CHUNK_ROWS = 256
ROWMM_SUB_ROWS = 256


def _rowmm_body(*refs, layer, kc, has_bias, x_t, final, w_is_bf16):
    x_ref, w_hbm = refs[:2]
    refs = refs[2:]
    if has_bias:
        b_ref, refs = refs[0], refs[1:]
    res_ref, gate_ref, ng_ref = refs[:3]
    refs = refs[3:]
    if not final:
        sh_ref, sc_ref = refs[:2]
        refs = refs[2:]
    n_o = 1 if final else 2
    o_refs = refs[:n_o]
    refs = refs[n_o:]
    if w_is_bf16:
        w_res, sem = refs

        @pl.when(pl.program_id(0) == 0)
        def _():
            whole = pltpu.make_async_copy(w_hbm, w_res, sem.at[0])
            whole.start()
            whole.wait()
    else:
        wq_hbm, w_res, stage, sem = refs
        n_chunks = w_res.shape[0] // kc
        write_back = pltpu.make_async_copy(w_res, wq_hbm, sem.at[2])

        def chunk_copy(c):
            return pltpu.make_async_copy(w_hbm.at[layer, pl.ds(c * kc, kc), :], stage.at[c % 2],
                                         sem.at[c % 2])

        @pl.when(pl.program_id(0) == 0)
        def _():
            chunk_copy(0).start()
            for c in range(n_chunks):
                if c + 1 < n_chunks:
                    chunk_copy(c + 1).start()
                chunk_copy(c).wait()
                w_res[c * kc:(c + 1) * kc, :] = stage[c % 2].astype(BF16)
            write_back.start()

        @pl.when(pl.program_id(0) == pl.num_programs(0) - 1)
        def _():
            write_back.wait()

    tm = res_ref.shape[0]
    sub = min(tm, ROWMM_SUB_ROWS)

    def groups_of(ref, r0):
        per = tm // ref.shape[0]
        return ref[...] if per >= tm else ref[r0 // per:(r0 + sub) // per]

    for r0 in range(0, tm, sub):
        x = x_ref[:, r0:r0 + sub].T if x_t else x_ref[r0:r0 + sub, :]
        y = jnp.dot(x, w_res[...], preferred_element_type=F32)
        if has_bias:
            y = y + b_ref[...]
        gate = groups_of(gate_ref, r0)
        xn = (_grouped(res_ref[r0:r0 + sub, :], gate) + gate * _grouped(y, gate)).reshape(y.shape)
        r = xn * lax.rsqrt(jnp.mean(xn * xn, axis=-1, keepdims=True) + EPS) * ng_ref[...]
        if final:
            o_refs[0][r0:r0 + sub, :] = r
        else:
            sc = groups_of(sc_ref, r0)
            o_refs[0][r0:r0 + sub, :] = xn
            o_refs[1][r0:r0 + sub, :] = (
                _grouped(r, sc) * (1.0 + sc) + groups_of(sh_ref, r0)
            ).reshape(y.shape).astype(o_refs[1].dtype)


def _rowmm_call(x, w, layer, res, gate, gate_seg, nxt, *, bias=None, x_t=False, tm=512,
                name="rowmm"):
    w_is_bf16 = w.ndim == 2
    if x_t:
        nb, k, seq = x.shape
        m = nb * seq
    else:
        m, k = x.shape
    d = w.shape[-1]
    final = len(nxt) == 1
    kc = _tile(k, WEIGHT_CHUNK_ROWS, SUBLANES)
    scratch = [((k, d), BF16)] + ([] if w_is_bf16 else [((2, kc, d), F32)])
    n_mods = 1 if final else 3

    def vmem_need(rows):
        blocks = [((rows, k), BF16), ((rows, d), F32), ((rows, d), F32 if final else BF16)]
        blocks += [((rows, d), F32)] * (0 if final else 1)
        blocks += [((-(-rows // gate.rows_per_group) * SUBLANES, d), F32)] * n_mods
        temporaries = 4 * _nbytes((rows, d), F32)
        return (2 * sum(_nbytes(s, t) for s, t in blocks) + sum(_nbytes(s, t) for s, t in scratch)
                + temporaries)

    mult = LANES if x_t else SUBLANES
    tm = gate.tile_rows(m, tm, mult)
    while vmem_need(tm) > V7X_VMEM_CAP and tm % (2 * max(mult, min(tm, gate.rows_per_group))) == 0:
        tm //= 2
    assert final or nxt[1].rows_per_group == gate.rows_per_group
    has_bias = bias is not None
    row = lambda i: (i, 0)
    if x_t:
        per_b = seq // tm
        in_specs = [pl.BlockSpec((None, k, tm), lambda i: (i // per_b, 0, i % per_b))]
    else:
        in_specs = [pl.BlockSpec((tm, k), row)]
    in_specs.append(pl.BlockSpec(memory_space=pltpu.HBM))
    args = [x, w]
    if has_bias:
        in_specs.append(pl.BlockSpec((None, 1, d), lambda i: (layer, 0, 0)))
        args.append(bias.reshape(bias.shape[0], 1, d))
    in_specs += [pl.BlockSpec((tm, d), row), gate.spec(gate_seg, d, tm),
                 pl.BlockSpec((1, d), lambda i: (0, 0))]
    args += [res, gate.arr, nxt[0].reshape(1, d)]
    if final:
        out_shape = jax.ShapeDtypeStruct((m, d), F32)
        out_specs = pl.BlockSpec((tm, d), row)
    else:
        _, nmod, seg_shift, seg_scale = nxt
        in_specs += [nmod.spec(seg_shift, d, tm), nmod.spec(seg_scale, d, tm)]
        args += [nmod.arr, nmod.arr]
        out_shape = [jax.ShapeDtypeStruct((m, d), F32), jax.ShapeDtypeStruct((m, d), BF16)]
        out_specs = [pl.BlockSpec((tm, d), row), pl.BlockSpec((tm, d), row)]
    if final:
        out_shape, out_specs = [out_shape], [out_specs]
    if not w_is_bf16:
        out_shape.append(jax.ShapeDtypeStruct((k, d), BF16))
        out_specs.append(pl.BlockSpec(memory_space=pltpu.HBM))
    outs = pl.pallas_call(
        functools.partial(_rowmm_body, layer=layer, kc=kc, has_bias=has_bias, x_t=x_t,
                          final=final, w_is_bf16=w_is_bf16),
        out_shape=out_shape,
        grid=(m // tm,),
        in_specs=in_specs,
        out_specs=out_specs,
        scratch_shapes=[pltpu.VMEM(s, dt) for s, dt in scratch] + [pltpu.SemaphoreType.DMA((3,))],
        compiler_params=pltpu.CompilerParams(
            dimension_semantics=("arbitrary",),
            vmem_limit_bytes=min(V7X_VMEM_CAP, max(32 * 2**20, vmem_need(tm) + 4 * 2**20))),
        name=name,
    )(*args)
    return outs[0] if len(outs) == 1 else outs


def _conv_ln_swish(up_ref, pad, tt, w_ref, bdw_ref, lg_ref, lb_ref, acc_ref, o_ref):
    kw, d = w_ref.shape
    by_shift = [[(a, SUBLANES * a + s - pad) for a in range((pad + kw - 1) // SUBLANES + 1)
                 if 0 <= SUBLANES * a + s - pad < kw] for s in range(SUBLANES)]

    def strip(c, carry):
        cols = pl.ds(pl.multiple_of(c * LANES, LANES), LANES)
        z = bdw_ref[:, cols]
        for s, taps in enumerate(by_shift):
            n = tt + SUBLANES if s else tt
            q = None
            for a, k in taps:
                term = up_ref[pl.ds(SUBLANES * a, n), cols] * w_ref[pl.ds(k, 1), cols]
                q = term if q is None else q + term
            if q is not None:
                z = z + q[s:s + tt]
        acc_ref[:, cols] = z
        return carry

    lax.fori_loop(0, d // LANES, strip, 0)
    z = acc_ref[...]
    mu = jnp.mean(z, axis=-1, keepdims=True)
    zc = z - mu
    y = zc * lax.rsqrt(jnp.mean(zc * zc, axis=-1, keepdims=True) + EPS)
    y = y * lg_ref[...] + lb_ref[...]
    o_ref[...] = (y * _sigmoid(y)).astype(o_ref.dtype)


def _conv_prompt_body(main_ref, halo_ref, w_ref, bdw_ref, lg_ref, lb_ref, o_ref, up_ref, acc_ref,
                      *, halo, tt):
    kw = w_ref.shape[0]

    @pl.when(pl.program_id(1) == 0)
    def _():
        up_ref[0:halo, :] = jnp.zeros((halo, up_ref.shape[1]), F32)

    @pl.when(pl.program_id(1) > 0)
    def _():
        up_ref[0:halo, :] = halo_ref[...]

    up_ref[halo:halo + tt, :] = main_ref[...]
    _conv_ln_swish(up_ref, halo - (kw - 1), tt, w_ref, bdw_ref, lg_ref, lb_ref, acc_ref, o_ref)


def _conv_prompt_call(u, w_dw, b_dw, ln_g, ln_b, layer, tt):
    b, t, d = u.shape
    kw = w_dw.shape[1]
    halo = -(-(kw - 1) // SUBLANES) * SUBLANES
    tt = _tile(t, tt, halo)
    hb = tt // halo
    vec = lambda: pl.BlockSpec((None, 1, d), lambda bi, ti: (layer, 0, 0))
    blocks = [((tt, d), F32), ((halo, d), F32), ((kw, d), F32), ((tt, d), BF16)]
    scratch = [((halo + tt, d), F32), ((tt, d), F32)]
    return pl.pallas_call(
        functools.partial(_conv_prompt_body, halo=halo, tt=tt),
        out_shape=jax.ShapeDtypeStruct((b, t, d), BF16),
        grid=(b, t // tt),
        in_specs=[
            pl.BlockSpec((None, tt, d), lambda bi, ti: (bi, ti, 0)),
            pl.BlockSpec((None, halo, d), lambda bi, ti: (bi, jnp.maximum(ti * hb - 1, 0), 0)),
            pl.BlockSpec((None, kw, d), lambda bi, ti: (layer, 0, 0)),
            vec(), vec(), vec(),
        ],
        out_specs=pl.BlockSpec((None, tt, d), lambda bi, ti: (bi, ti, 0)),
        scratch_shapes=[pltpu.VMEM(s, dt) for s, dt in scratch],
        compiler_params=pltpu.CompilerParams(
            dimension_semantics=("parallel", "arbitrary"),
            vmem_limit_bytes=_vmem_limit(blocks, scratch)),
        name="dwconv_ln_swish_prompt",
    )(u, u, w_dw, b_dw.reshape(-1, 1, d), ln_g.reshape(-1, 1, d), ln_b.reshape(-1, 1, d))


def _conv_sample_body(up_ref, w_ref, bdw_ref, lg_ref, lb_ref, o_ref, acc_ref, *, pad, tt):
    _conv_ln_swish(up_ref, pad, tt, w_ref, bdw_ref, lg_ref, lb_ref, acc_ref, o_ref)


def _conv_sample_call(up, pad, tt, w_dw, b_dw, ln_g, ln_b, layer):
    b, rows, d = up.shape
    kw = w_dw.shape[1]
    vec = lambda: pl.BlockSpec((None, 1, d), lambda bi: (layer, 0, 0))
    blocks = [((rows, d), F32), ((kw, d), F32), ((tt, d), BF16)]
    scratch = [((tt, d), F32)]
    return pl.pallas_call(
        functools.partial(_conv_sample_body, pad=pad, tt=tt),
        out_shape=jax.ShapeDtypeStruct((b, tt, d), BF16),
        grid=(b,),
        in_specs=[
            pl.BlockSpec((None, rows, d), lambda bi: (bi, 0, 0)),
            pl.BlockSpec((None, kw, d), lambda bi: (layer, 0, 0)),
            vec(), vec(), vec(),
        ],
        out_specs=pl.BlockSpec((None, tt, d), lambda bi: (bi, 0, 0)),
        scratch_shapes=[pltpu.VMEM(s, dt) for s, dt in scratch],
        compiler_params=pltpu.CompilerParams(
            dimension_semantics=("parallel",), vmem_limit_bytes=_vmem_limit(blocks, scratch)),
        name="dwconv_ln_swish_sample",
    )(up, w_dw, b_dw.reshape(-1, 1, d), ln_g.reshape(-1, 1, d), ln_b.reshape(-1, 1, d))


def _bucket_codes(n_q, n_k, k_off, n_buckets):
    rel = (np.arange(n_k) + k_off)[None, :] - np.arange(n_q)[:, None]
    nb = n_buckets // 2
    max_exact = nb // 2
    ret = np.where(rel > 0, nb, 0)
    n = np.abs(rel)
    nf = np.maximum(n, 1).astype(np.float32)
    large = max_exact + (np.log(nf / np.float32(max_exact))
                         / np.float32(math.log(MAX_DISTANCE / max_exact))
                         * np.float32(nb - max_exact)).astype(np.int32)
    large = np.minimum(large, nb - 1)
    return (ret + np.where(n < max_exact, n, large)).astype(np.int32)


def _bias_body(code_ref, table_ref, o_ref, *, n_buckets, group):
    code = code_ref[...]
    n_cols = code.shape[1]

    def head(hh, carry):
        out = jnp.full(code.shape, -jnp.inf, F32)
        for b in range(n_buckets):
            out = jnp.where(code == b, table_ref[hh, b] * LOG2E, out)
        if group is None:
            o_ref[hh] = out
        else:
            cols = pl.ds(pl.multiple_of((hh % group) * n_cols, LANES), n_cols)
            o_ref[hh // group, :, cols] = out
        return carry

    lax.fori_loop(0, table_ref.shape[0], head, 0)


def _bias_call(code, table, group=None):
    n_heads, n_buckets = table.shape
    rows, cols = code.shape
    shape = (n_heads, rows, cols) if group is None else (n_heads // group, rows, group * cols)
    return pl.pallas_call(
        functools.partial(_bias_body, n_buckets=n_buckets, group=group),
        out_shape=jax.ShapeDtypeStruct(shape, F32),
        in_specs=[pl.BlockSpec(memory_space=pltpu.VMEM), pl.BlockSpec(memory_space=pltpu.SMEM)],
        out_specs=pl.BlockSpec(memory_space=pltpu.VMEM),
        name="rel_bias",
    )(jnp.asarray(code), table)


def _unit_scores(qs, kh, bias):
    return lax.dot_general(qs, kh, (((1,), (1,)), ((), ())), preferred_element_type=F32) + bias


def _unit_output(s, vh, sink_col):
    sink_col = sink_col * LOG2E
    m = jnp.maximum(jnp.max(s, axis=-1, keepdims=True), sink_col)
    e = jnp.exp2(s - m)
    l = jnp.sum(e, axis=-1, keepdims=True) + jnp.exp2(sink_col - m)
    return jnp.dot(e.astype(BF16), vh, preferred_element_type=F32) * (1.0 / l)


def _stack_heads(q, h, group):
    hd = HEAD_DIM
    return jnp.concatenate(
        [q[:, (h * group + g) * hd:(h * group + g + 1) * hd] for g in range(group)], axis=0)


def _unstack_heads(o, group):
    tq = o.shape[0] // group
    return [o[g * tq:(g + 1) * tq, :] for g in range(group)]


ATTN_Q_COLS = 512
PAIR = 2 * CHUNK


def _pair_codes(n_buckets):
    code = _bucket_codes(PAIR, WINDOW + PAIR, -WINDOW, n_buckets).T
    key_chunk = np.arange(WINDOW + PAIR)[:, None] // CHUNK
    q_chunk = np.arange(PAIR)[None, :] // CHUNK
    visible = (key_chunk >= q_chunk) & (key_chunk <= q_chunk + WINDOW // CHUNK)
    return np.where(visible, code, n_buckets).astype(np.int32)


def _attn_prompt_body(qT_ref, kp_ref, kc_ref, vTp_ref, vTc_ref, bias_ref, sink_ref, oT_ref,
                      k_scr, vT_scr, *, kvh, group):
    i = pl.program_id(1)
    tq = qT_ref.shape[1]
    nk = WINDOW + PAIR
    hd = HEAD_DIM
    for h in range(kvh):
        k_scr[h, 0:WINDOW, :] = kp_ref[:, h * hd:(h + 1) * hd].astype(BF16)
        k_scr[h, WINDOW:WINDOW + tq, :] = kc_ref[:, h * hd:(h + 1) * hd].astype(BF16)
    vT_scr[:, 0:WINDOW] = vTp_ref[...].astype(BF16)
    vT_scr[:, WINDOW:WINDOW + tq] = vTc_ref[...].astype(BF16)
    row = lax.broadcasted_iota(jnp.int32, (nk, group * PAIR), 0)
    start_mask = jnp.where(row < jnp.where(i == 0, WINDOW, 0), -jnp.inf, 0.0).astype(F32)
    units = [(p * PAIR, h) for p in range(tq // PAIR) for h in range(kvh)]

    def scores(c0, h):
        qsT = jnp.concatenate(
            [qT_ref[hh * hd:(hh + 1) * hd, c0:c0 + PAIR] for hh in range(h * group, (h + 1) * group)],
            axis=1)
        s = jnp.dot(k_scr[h, c0:c0 + nk, :], qsT, preferred_element_type=F32) + bias_ref[h]
        return s + start_mask if c0 == 0 else s

    s_next = scores(*units[0])
    for n, (c0, h) in enumerate(units):
        s = s_next
        if n + 1 < len(units):
            s_next = scores(*units[n + 1])
        sink = sink_ref[h] * LOG2E
        m = jnp.maximum(jnp.max(s, axis=0, keepdims=True), sink)
        e = jnp.exp2(s - m)
        l = jnp.sum(e, axis=0, keepdims=True) + jnp.exp2(sink - m)
        oT = jnp.dot(vT_scr[h * hd:(h + 1) * hd, c0:c0 + nk], e.astype(BF16),
                     preferred_element_type=F32) * (1.0 / l)
        for g in range(group):
            hh = h * group + g
            oT_ref[hh * hd:(hh + 1) * hd, c0:c0 + PAIR] = (
                oT[:, g * PAIR:(g + 1) * PAIR].astype(oT_ref.dtype))


def _attn_prompt_call(qT, k, vT, bias, sink_row, kvh, group):
    b, dq, s = qT.shape
    dk = k.shape[2]
    tq = _tile(s, ATTN_Q_COLS, PAIR)
    per = tq // WINDOW
    prev = lambda i: jnp.maximum(i * per - 1, 0)
    blocks = [((dq, tq), BF16)] * 2 + [((WINDOW + tq, dk), F32)] * 2 + [
        (bias.shape, F32), (sink_row.shape[:1] + (SUBLANES, sink_row.shape[2]), F32)]
    scratch = [((kvh, WINDOW + tq, LANES), BF16), ((dk, WINDOW + tq), BF16)]
    return pl.pallas_call(
        functools.partial(_attn_prompt_body, kvh=kvh, group=group),
        out_shape=jax.ShapeDtypeStruct((b, dq, s), BF16),
        grid=(b, s // tq),
        in_specs=[
            pl.BlockSpec((None, dq, tq), lambda bi, i: (bi, 0, i)),
            pl.BlockSpec((None, WINDOW, dk), lambda bi, i: (bi, prev(i), 0)),
            pl.BlockSpec((None, tq, dk), lambda bi, i: (bi, i, 0)),
            pl.BlockSpec((None, dk, WINDOW), lambda bi, i: (bi, 0, prev(i))),
            pl.BlockSpec((None, dk, tq), lambda bi, i: (bi, 0, i)),
            pl.BlockSpec(bias.shape, lambda bi, i: (0, 0, 0)),
            pl.BlockSpec(sink_row.shape, lambda bi, i: (0, 0, 0)),
        ],
        out_specs=pl.BlockSpec((None, dq, tq), lambda bi, i: (bi, 0, i)),
        scratch_shapes=[pltpu.VMEM((kvh, WINDOW + tq, HEAD_DIM), BF16),
                        pltpu.VMEM((dk, WINDOW + tq), BF16)],
        compiler_params=pltpu.CompilerParams(
            dimension_semantics=("parallel", "arbitrary"),
            vmem_limit_bytes=_vmem_limit(blocks, scratch)),
        name="swa_sink_attention_prompt",
    )(qT, k, k, vT, vT, bias, sink_row)


ATTN_SAMPLE_BATCH = 4


def _attn_sample_body(q_ref, k_ref, v_ref, bias_ref, sink_ref, o_ref, *, kvh, group):
    hd = HEAD_DIM
    units = [(b, h) for b in range(q_ref.shape[0]) for h in range(kvh)]

    def scores(b, h):
        qs = _stack_heads(q_ref[b].astype(F32), h, group).astype(BF16)
        return _unit_scores(qs, k_ref[b, :, h * hd:(h + 1) * hd].astype(BF16), bias_ref[h])

    s_next = scores(*units[0])
    outs = []
    for n, (b, h) in enumerate(units):
        s = s_next
        if n + 1 < len(units):
            s_next = scores(*units[n + 1])
        o = _unit_output(s, v_ref[b, :, h * hd:(h + 1) * hd].astype(BF16), sink_ref[h])
        outs += _unstack_heads(o, group)
        if h == kvh - 1:
            o_ref[b] = jnp.concatenate(outs, axis=1).astype(o_ref.dtype)
            outs = []


def _attn_sample_call(q, k_all, v_all, bias, sink_col, kvh, group):
    b, t, dq = q.shape
    nk, dk = k_all.shape[1:]
    bb = math.gcd(b, ATTN_SAMPLE_BATCH)
    blocks = [((bb, t, dq), BF16)] * 2 + [((bb, nk, dk), F32)] * 2 + [
        (bias.shape[:2] + (2 * LANES,), F32), (sink_col.shape[:2] + (LANES,), F32)]
    return pl.pallas_call(
        functools.partial(_attn_sample_body, kvh=kvh, group=group),
        out_shape=jax.ShapeDtypeStruct((b, t, dq), BF16),
        grid=(b // bb,),
        in_specs=[
            pl.BlockSpec((bb, t, dq), lambda bi: (bi, 0, 0)),
            pl.BlockSpec((bb, nk, dk), lambda bi: (bi, 0, 0)),
            pl.BlockSpec((bb, nk, dk), lambda bi: (bi, 0, 0)),
            pl.BlockSpec(bias.shape, lambda bi: (0, 0, 0)),
            pl.BlockSpec(sink_col.shape, lambda bi: (0, 0, 0)),
        ],
        out_specs=pl.BlockSpec((bb, t, dq), lambda bi: (bi, 0, 0)),
        compiler_params=pltpu.CompilerParams(
            dimension_semantics=("parallel",), vmem_limit_bytes=_vmem_limit(blocks)),
        name="swa_sink_attention_sample",
    )(q, k_all, v_all, bias, sink_col)


def _trunks(xp3, xs3, mods_p, mods_s, state_conv, win_k, win_v, p):
    bp, tp, d = xp3.shape
    bs, ts, _ = xs3.shape
    mp, ms = bp * tp, bs * ts
    depth = p["w_mod"].shape[0]
    d_ff = p["w_gu"].shape[2] // 2
    n_heads = p["attn_sinks"].shape[1]
    dq = n_heads * HEAD_DIM
    kvh = (p["w_qkv"].shape[2] - dq) // (2 * HEAD_DIM)
    group = n_heads // kvh
    dkv = kvh * HEAD_DIM
    kw = p["w_dw"].shape[1]
    n_buckets = p["rel_bias_table"].shape[1]
    conv_p, k_p, v_p, conv_s, k_s, v_s = [], [], [], [], [], []

    xp, xs = xp3.reshape(mp, d), xs3.reshape(ms, d)
    hp = _norm_mod_call(xp, p["norm_mix"][0], mods_p[0], 0, 1, 1024)
    hs = _norm_mod_call(xs, p["norm_mix"][0], mods_s[0], 0, 1, ms)
    for i in range(depth):
        j = i // 2
        ffn_norm = lambda mods: (p["norm_ffn"][i], mods[i], 3, 4)
        if i % 2 == 0:
            u_p, u_s = _mm_call(hp, p["w_pw1"], j, (0, d), d, x2=hs, bias=p["b_pw1"], act="glu",
                                tm=1024, tn=512, name="pw1_glu")
            u_p, u_s = u_p.reshape(bp, tp, d), u_s.reshape(bs, ts, d)
            conv_w = (p["w_dw"], p["b_dw"], p["conv_ln_g"], p["conv_ln_b"], j)
            z_p = _conv_prompt_call(u_p, *conv_w, 128)
            conv_p.append(u_p[:, tp - (kw - 1):])
            pad = (-(kw - 1 + ts)) % SUBLANES
            up = jnp.concatenate([jnp.zeros((bs, pad, d), F32), state_conv[j], u_s], axis=1)
            z_s = _conv_sample_call(up, pad, ts, *conv_w)
            conv_s.append(up[:, pad + ts:])
            xp, hp, wq = _rowmm_call(z_p.reshape(mp, d), p["w_pw2"], j, xp, mods_p[i], 2,
                                     ffn_norm(mods_p), bias=p["b_pw2"], tm=512,
                                     name="pw2_residual_norm")
            xs, hs = _rowmm_call(z_s.reshape(ms, d), wq, j, xs, mods_s[i], 2,
                                 ffn_norm(mods_s), bias=p["b_pw2"], tm=512,
                                 name="pw2_residual_norm")
        else:
            sinks = p["attn_sinks"][j].reshape(kvh, group, 1)
            qT, q_s = _mm_call(hp, p["w_qkv"], j, (0,), dq, x2=hs, bias=p["b_qkv"],
                               scale=LOG2E * HEAD_DIM ** -0.5, out_dtype=BF16, tm=1024, tn=1024,
                               name="q_proj", out_mode="t", seq=tp)
            (k, vT), kv_s = _mm_call(hp, p["w_qkv"], j, (dq,), 2 * dkv, x2=hs, bias=p["b_qkv"],
                                     tm=1024, tn=2 * dkv, name="kv_proj", out_mode="split_t",
                                     seq=tp)
            k = k.reshape(bp, tp, dkv)
            bias = _bias_call(_pair_codes(n_buckets), p["rel_bias_table"], group)
            sink_row = jnp.repeat(sinks, PAIR, axis=2).reshape(kvh, 1, group * PAIR)
            oT = _attn_prompt_call(qT, k, vT, bias, sink_row, kvh, group)
            k_p.append(k[:, tp - WINDOW:].reshape(bp, WINDOW, kvh, HEAD_DIM))
            v_p.append(jnp.swapaxes(vT[:, :, tp - WINDOW:], 1, 2)
                       .reshape(bp, WINDOW, kvh, HEAD_DIM))
            kv3 = kv_s.reshape(bs, ts, 2, kvh, HEAD_DIM)
            k_all = jnp.concatenate([win_k[j], kv3[:, :, 0]], axis=1)
            v_all = jnp.concatenate([win_v[j], kv3[:, :, 1]], axis=1)
            n_keys = WINDOW + ts
            bias = _bias_call(_bucket_codes(ts, n_keys, -WINDOW, n_buckets),
                              p["rel_bias_table"]).reshape(kvh, group * ts, n_keys)
            o_s = _attn_sample_call(q_s.reshape(bs, ts, dq), k_all.reshape(bs, n_keys, dkv),
                                    v_all.reshape(bs, n_keys, dkv), bias,
                                    jnp.repeat(sinks, ts, axis=1), kvh, group)
            k_s.append(k_all[:, ts:])
            v_s.append(v_all[:, ts:])
            xp, hp, wq = _rowmm_call(oT, p["w_o"], j, xp, mods_p[i], 2, ffn_norm(mods_p),
                                     bias=p["b_o"], x_t=True, tm=512, name="wo_residual_norm")
            xs, hs = _rowmm_call(o_s.reshape(ms, dq), wq, j, xs, mods_s[i], 2,
                                 ffn_norm(mods_s), bias=p["b_o"], tm=512,
                                 name="wo_residual_norm")
        a_p, a_s = _mm_call(hp, p["w_gu"], i, (0, d_ff), d_ff, x2=hs, act="swiglu",
                            out_dtype=BF16, tm=1024, tn=512, name="ffn_gate_up")
        if i + 1 < depth:
            nxt = lambda mods: (p["norm_mix"][i + 1], mods[i + 1], 0, 1)
            xp, hp, wq = _rowmm_call(a_p, p["w_down"], i, xp, mods_p[i], 5, nxt(mods_p), tm=256,
                                     name="ffn_down_residual_norm")
            xs, hs = _rowmm_call(a_s, wq, i, xs, mods_s[i], 5, nxt(mods_s), tm=256,
                                 name="ffn_down_residual_norm")
        else:
            y_p, wq = _rowmm_call(a_p, p["w_down"], i, xp, mods_p[i], 5, (p["norm_out"],), tm=256,
                                  name="ffn_down_residual_final_norm")
            y_s = _rowmm_call(a_s, wq, i, xs, mods_s[i], 5, (p["norm_out"],), tm=256,
                              name="ffn_down_residual_final_norm")
    stack = jnp.stack
    return (y_p.reshape(bp, tp, d), y_s.reshape(bs, ts, d), stack(conv_p), stack(k_p), stack(v_p),
            stack(conv_s), stack(k_s), stack(v_s))


def kernel(x_prompt, x_sample, c_prompt, c_sample, state_conv, cache_win_k, cache_win_v, w_mod, b_mod, norm_mix, norm_ffn, w_pw1, b_pw1, w_dw, b_dw, conv_ln_g, conv_ln_b, w_pw2, b_pw2, w_qkv, b_qkv, w_o, b_o, attn_sinks, rel_bias_table, w_gu, w_down, norm_out):
    p = dict(w_mod=w_mod, b_mod=b_mod, norm_mix=norm_mix, norm_ffn=norm_ffn, w_pw1=w_pw1,
             b_pw1=b_pw1, w_dw=w_dw, b_dw=b_dw, conv_ln_g=conv_ln_g, conv_ln_b=conv_ln_b,
             w_pw2=w_pw2, b_pw2=b_pw2, w_qkv=w_qkv, b_qkv=b_qkv, w_o=w_o, b_o=b_o,
             attn_sinks=attn_sinks, rel_bias_table=rel_bias_table, w_gu=w_gu, w_down=w_down,
             norm_out=norm_out)
    bp, sp, d = x_prompt.shape
    bs, ts, _ = x_sample.shape
    depth = w_mod.shape[0]

    n_c = bp + bs
    c_all = jnp.concatenate(
        [c_prompt, c_sample, jnp.zeros((-n_c % (2 * SUBLANES), d), F32)], axis=0)
    mod_all = _mod_call(c_all, w_mod, b_mod)

    mods_p = [_Mod(mod_all[l, :bp].reshape(bp, 1, 6 * d), sp) for l in range(depth)]
    mods_s = [_Mod(mod_all[l, bp:n_c].reshape(bs, 1, 6 * d), ts) for l in range(depth)]

    return _trunks(x_prompt, x_sample, mods_p, mods_s, state_conv, cache_win_k, cache_win_v, p)
```

```python
import functools
import math

import numpy as np
import jax
import jax.numpy as jnp
from jax import lax
from jax.experimental import pallas as pl
from jax.experimental.pallas import tpu as pltpu

F32 = jnp.float32
BF16 = jnp.bfloat16

CHUNK = 64
WINDOW = 128
HEAD_DIM = 64
MAX_DISTANCE = 128
EPS = 1e-6
LOG2E = math.log2(math.e)

V7X_VMEM_BYTES = 64 * 2**20
V7X_VMEM_CAP = V7X_VMEM_BYTES - 8 * 2**20
SUBLANES = 8
LANES = 128


def _nbytes(shape, dtype):
    return math.prod(shape) * jnp.dtype(dtype).itemsize


def _vmem_limit(blocks, scratch=()):
    est = 2 * sum(_nbytes(s, d) for s, d in blocks) + sum(_nbytes(s, d) for s, d in scratch)
    return int(min(V7X_VMEM_CAP, max(32 * 2**20, 2 * est)))


def _tile(dim, pref, mult=LANES):
    if dim <= pref:
        return dim
    t = (pref // mult) * mult
    while t >= mult:
        if dim % t == 0:
            return t
        t -= mult
    return dim


def _sigmoid(x):
    return 1.0 / (1.0 + jnp.exp(-x))


def _mod_body(c_ref, w_ref, b_ref, o_ref):
    c = c_ref[...]
    a = (c * _sigmoid(c)).astype(BF16)
    o_ref[...] = jnp.dot(a, w_ref[...].astype(BF16), preferred_element_type=F32) + b_ref[...]


def _mod_call(c_all, w_mod, b_mod):
    depth, d, n = w_mod.shape
    mp = c_all.shape[0]
    tn = _tile(n, 1024)
    blocks = [((mp, d), F32), ((d, tn), F32), ((1, tn), F32), ((mp, tn), F32)]
    return pl.pallas_call(
        _mod_body,
        out_shape=jax.ShapeDtypeStruct((depth, mp, n), F32),
        grid=(depth, n // tn),
        in_specs=[
            pl.BlockSpec((mp, d), lambda l, j: (0, 0)),
            pl.BlockSpec((None, d, tn), lambda l, j: (l, 0, j)),
            pl.BlockSpec((None, 1, tn), lambda l, j: (l, 0, j)),
        ],
        out_specs=pl.BlockSpec((None, mp, tn), lambda l, j: (l, 0, j)),
        compiler_params=pltpu.CompilerParams(
            dimension_semantics=("arbitrary", "arbitrary"),
            vmem_limit_bytes=_vmem_limit(blocks, [((d, tn), BF16)])),
        name="adaln_mod",
    )(c_all, w_mod, b_mod.reshape(depth, 1, n))


class _Mod:
    def __init__(self, arr, rows_per_group):
        self.arr = arr
        self.rows_per_group = rows_per_group

    def tile_rows(self, total_rows, pref, mult):
        if self.rows_per_group >= pref:
            return _tile(self.rows_per_group, pref, mult)
        return _tile(total_rows, pref, max(mult, self.rows_per_group))

    def spec(self, seg, d, tm):
        if tm <= self.rows_per_group:
            per = self.rows_per_group // tm
            return pl.BlockSpec((1, 1, d), lambda i: (i // per, 0, seg))
        return pl.BlockSpec((tm // self.rows_per_group, 1, d), lambda i: (i, 0, seg))


def _grouped(v, m):
    return v.reshape(m.shape[0], v.shape[0] // m.shape[0], v.shape[1])


def _norm_mod_body(x_ref, g_ref, sh_ref, sc_ref, o_ref):
    x = x_ref[...]
    y = x * lax.rsqrt(jnp.mean(x * x, axis=-1, keepdims=True) + EPS) * g_ref[...]
    sc = sc_ref[...]
    o_ref[...] = (_grouped(y, sc) * (1.0 + sc) + sh_ref[...]).reshape(x.shape).astype(o_ref.dtype)


def _norm_mod_call(x, g, mod, seg_shift, seg_scale, tm):
    m, d = x.shape
    tm = mod.tile_rows(m, tm, SUBLANES)
    blocks = [((tm, d), F32), ((1, d), F32), ((tm, d), BF16)]
    return pl.pallas_call(
        _norm_mod_body,
        out_shape=jax.ShapeDtypeStruct((m, d), BF16),
        grid=(m // tm,),
        in_specs=[
            pl.BlockSpec((tm, d), lambda i: (i, 0)),
            pl.BlockSpec((1, d), lambda i: (0, 0)),
            mod.spec(seg_shift, d, tm),
            mod.spec(seg_scale, d, tm),
        ],
        out_specs=pl.BlockSpec((tm, d), lambda i: (i, 0)),
        compiler_params=pltpu.CompilerParams(
            dimension_semantics=("parallel",), vmem_limit_bytes=_vmem_limit(blocks)),
        name="rmsnorm_modulate",
    )(x, g.reshape(1, d), mod.arr, mod.arr)


def _mm_body(x_ref, *refs, n_w, has_bias, act, scale, out_mode, has_x2, has_cast):
    if has_x2:
        x2_ref, refs = refs[0], refs[1:]
    ws = refs[:n_w]
    refs = refs[n_w:]
    bs = refs[:n_w] if has_bias else ()
    refs = refs[len(bs):]
    if has_cast:
        wc_ref, refs = refs[0], refs[1:]
    n_o = 2 if out_mode == "split_t" else 1
    o_refs = refs[:n_o]
    refs = refs[n_o:]
    if has_x2:
        o2_ref, refs = refs[0], refs[1:]
    if has_cast:
        oc_ref, refs = refs[0], refs[1:]
        oc_ref[...] = wc_ref[...].astype(BF16)
    wbs = refs

    def compute(x):
        ps = [None] * n_w
        for i in ((1, 0) if act == "glu" else range(n_w)):
            p = jnp.dot(x, wbs[i][...], preferred_element_type=F32)
            if has_bias:
                p = p + bs[i][...]
            ps[i] = p
        if act == "glu":
            y = ps[0] * _sigmoid(ps[1])
        elif act == "swiglu":
            y = ps[0] * _sigmoid(ps[0]) * ps[1]
        else:
            y = ps[0]
        return y * scale if scale != 1.0 else y

    @pl.when(pl.program_id(1) == 0)
    def _():
        for w, wb in zip(ws, wbs):
            wb[...] = w[...].astype(BF16)
        if has_x2:
            o2_ref[...] = compute(x2_ref[...]).astype(o2_ref.dtype)

    y = compute(x_ref[...])
    if out_mode == "plain":
        o_refs[0][...] = y.astype(o_refs[0].dtype)
    elif out_mode == "t":
        o_refs[0][...] = y.T.astype(o_refs[0].dtype)
    else:
        half = y.shape[1] // 2
        o_refs[0][...] = y[:, :half].astype(o_refs[0].dtype)
        o_refs[1][...] = y[:, half:].T.astype(o_refs[1].dtype)


def _mm_call(x, w, layer, col_starts, n_out, *, x2=None, cast=None, bias=None, act=None,
             scale=1.0, out_dtype=F32, tm=512, tn=512, name="matmul", out_mode="plain", seq=None):
    m, k = x.shape
    if out_mode == "plain":
        tm = _tile(m, tm, SUBLANES)
    else:
        tm = _tile(seq, tm, LANES)
    tn = _tile(n_out, tn)
    n_w = len(col_starts)
    assert all(c % tn == 0 for c in col_starts) and m % tm == 0 and n_out % tn == 0
    has_bias = bias is not None
    per_b = None if seq is None else seq // tm

    in_specs = [pl.BlockSpec((tm, k), lambda j, i: (i, 0))]
    args = [x]
    blocks = [((tm, k), x.dtype), ((tm, tn), out_dtype)]
    if x2 is not None:
        m2 = x2.shape[0]
        in_specs.append(pl.BlockSpec((m2, k), lambda j, i: (0, 0)))
        args.append(x2)
        blocks += [((m2, k), x2.dtype), ((m2, tn), out_dtype)]
    for c in col_starts:
        off = c // tn
        in_specs.append(pl.BlockSpec((None, k, tn), lambda j, i, off=off: (layer, 0, off + j)))
        args.append(w)
        blocks.append(((k, tn), F32))
    if has_bias:
        b3 = bias.reshape(bias.shape[0], 1, bias.shape[1])
        for c in col_starts:
            off = c // tn
            in_specs.append(pl.BlockSpec((None, 1, tn), lambda j, i, off=off: (layer, 0, off + j)))
            args.append(b3)
    n_i = m // tm
    if cast is not None:
        w_other, layer_other = cast
        kc, dc = w_other.shape[1:]
        slab = kc // ((n_out // tn) * n_i)
        assert slab * (n_out // tn) * n_i == kc and slab % (2 * SUBLANES) == 0
        in_specs.append(pl.BlockSpec((None, slab, dc), lambda j, i: (layer_other, j * n_i + i, 0)))
        args.append(w_other)
        blocks += [((slab, dc), F32), ((slab, dc), BF16)]
    scratch = [((k, tn), BF16)] * n_w
    body = functools.partial(_mm_body, n_w=n_w, has_bias=has_bias, act=act, scale=scale,
                             out_mode=out_mode, has_x2=x2 is not None, has_cast=cast is not None)
    if out_mode == "plain":
        out_shape = [jax.ShapeDtypeStruct((m, n_out), out_dtype)]
        out_specs = [pl.BlockSpec((tm, tn), lambda j, i: (i, j))]
    elif out_mode == "t":
        out_shape = [jax.ShapeDtypeStruct((m // seq, n_out, seq), out_dtype)]
        out_specs = [pl.BlockSpec((None, tn, tm), lambda j, i: (i // per_b, j, i % per_b))]
    else:
        assert tn == n_out
        half = n_out // 2
        out_shape = [jax.ShapeDtypeStruct((m, half), out_dtype),
                     jax.ShapeDtypeStruct((m // seq, half, seq), out_dtype)]
        out_specs = [pl.BlockSpec((tm, half), lambda j, i: (i, 0)),
                     pl.BlockSpec((None, half, tm), lambda j, i: (i // per_b, 0, i % per_b))]
    if x2 is not None:
        out_shape.append(jax.ShapeDtypeStruct((m2, n_out), out_dtype))
        out_specs.append(pl.BlockSpec((m2, tn), lambda j, i: (0, j)))
    if cast is not None:
        out_shape.append(jax.ShapeDtypeStruct((kc, dc), BF16))
        out_specs.append(pl.BlockSpec((slab, dc), lambda j, i: (j * n_i + i, 0)))
    outs = pl.pallas_call(
        body,
        out_shape=out_shape,
        grid=(n_out // tn, n_i),
        in_specs=in_specs,
        out_specs=out_specs,
        scratch_shapes=[pltpu.VMEM(s, d) for s, d in scratch],
        compiler_params=pltpu.CompilerParams(
            dimension_semantics=("arbitrary", "arbitrary"),
            vmem_limit_bytes=_vmem_limit(blocks, scratch)),
        name=name,
    )(*args)
    n_primary = 2 if out_mode == "split_t" else 1
    primary = outs[0] if n_primary == 1 else outs[:n_primary]
    extras = list(outs[n_primary:])
    return (primary, *extras) if extras else primary


ROWMM_SUB_ROWS = 256


def _rowmm_body(*refs, has_bias, x_t, final):
    x_ref, w_hbm = refs[:2]
    refs = refs[2:]
    if has_bias:
        b_ref, refs = refs[0], refs[1:]
    res_ref, gate_ref, ng_ref = refs[:3]
    refs = refs[3:]
    if not final:
        sh_ref, sc_ref = refs[:2]
        refs = refs[2:]
    n_o = 1 if final else 2
    o_refs = refs[:n_o]
    refs = refs[n_o:]
    w_res, sem = refs

    @pl.when(pl.program_id(0) == 0)
    def _():
        whole = pltpu.make_async_copy(w_hbm, w_res, sem)
        whole.start()
        whole.wait()

    tm = res_ref.shape[0]
    sub = min(tm, ROWMM_SUB_ROWS)

    def groups_of(ref, r0):
        per = tm // ref.shape[0]
        return ref[...] if per >= tm else ref[r0 // per:(r0 + sub) // per]

    for r0 in range(0, tm, sub):
        x = x_ref[:, r0:r0 + sub].T if x_t else x_ref[r0:r0 + sub, :]
        y = jnp.dot(x, w_res[...], preferred_element_type=F32)
        if has_bias:
            y = y + b_ref[...]
        gate = groups_of(gate_ref, r0)
        xn = (_grouped(res_ref[r0:r0 + sub, :], gate) + gate * _grouped(y, gate)).reshape(y.shape)
        r = xn * lax.rsqrt(jnp.mean(xn * xn, axis=-1, keepdims=True) + EPS) * ng_ref[...]
        if final:
            o_refs[0][r0:r0 + sub, :] = r
        else:
            sc = groups_of(sc_ref, r0)
            o_refs[0][r0:r0 + sub, :] = xn
            o_refs[1][r0:r0 + sub, :] = (
                _grouped(r, sc) * (1.0 + sc) + groups_of(sh_ref, r0)
            ).reshape(y.shape).astype(o_refs[1].dtype)


def _rowmm_call(x, w, res, gate, gate_seg, nxt, *, bias=None, x_t=False, tm=512, name="rowmm"):
    if x_t:
        nb, k, seq = x.shape
        m = nb * seq
    else:
        m, k = x.shape
    d = w.shape[-1]
    final = len(nxt) == 1
    scratch = [((k, d), BF16)]
    n_mods = 1 if final else 3

    def vmem_need(rows):
        blocks = [((rows, k), BF16), ((rows, d), F32), ((rows, d), F32 if final else BF16)]
        blocks += [((rows, d), F32)] * (0 if final else 1)
        blocks += [((-(-rows // gate.rows_per_group) * SUBLANES, d), F32)] * n_mods
        temporaries = 4 * _nbytes((rows, d), F32)
        return (2 * sum(_nbytes(s, t) for s, t in blocks) + sum(_nbytes(s, t) for s, t in scratch)
                + temporaries)

    mult = LANES if x_t else SUBLANES
    tm = gate.tile_rows(m, tm, mult)
    while vmem_need(tm) > V7X_VMEM_CAP and tm % (2 * max(mult, min(tm, gate.rows_per_group))) == 0:
        tm //= 2
    assert final or nxt[1].rows_per_group == gate.rows_per_group
    has_bias = bias is not None
    row = lambda i: (i, 0)
    if x_t:
        per_b = seq // tm
        in_specs = [pl.BlockSpec((None, k, tm), lambda i: (i // per_b, 0, i % per_b))]
    else:
        in_specs = [pl.BlockSpec((tm, k), row)]
    in_specs.append(pl.BlockSpec(memory_space=pltpu.HBM))
    args = [x, w]
    if has_bias:
        in_specs.append(pl.BlockSpec((1, d), lambda i: (0, 0)))
        args.append(bias.reshape(1, d))
    in_specs += [pl.BlockSpec((tm, d), row), gate.spec(gate_seg, d, tm),
                 pl.BlockSpec((1, d), lambda i: (0, 0))]
    args += [res, gate.arr, nxt[0].reshape(1, d)]
    if final:
        out_shape = jax.ShapeDtypeStruct((m, d), F32)
        out_specs = pl.BlockSpec((tm, d), row)
    else:
        _, nmod, seg_shift, seg_scale = nxt
        in_specs += [nmod.spec(seg_shift, d, tm), nmod.spec(seg_scale, d, tm)]
        args += [nmod.arr, nmod.arr]
        out_shape = [jax.ShapeDtypeStruct((m, d), F32), jax.ShapeDtypeStruct((m, d), BF16)]
        out_specs = [pl.BlockSpec((tm, d), row), pl.BlockSpec((tm, d), row)]
    return pl.pallas_call(
        functools.partial(_rowmm_body, has_bias=has_bias, x_t=x_t, final=final),
        out_shape=out_shape,
        grid=(m // tm,),
        in_specs=in_specs,
        out_specs=out_specs,
        scratch_shapes=[pltpu.VMEM(s, dt) for s, dt in scratch] + [pltpu.SemaphoreType.DMA(())],
        compiler_params=pltpu.CompilerParams(
            dimension_semantics=("arbitrary",),
            vmem_limit_bytes=min(V7X_VMEM_CAP, max(32 * 2**20, vmem_need(tm) + 4 * 2**20))),
        name=name,
    )(*args)


def _conv_ln_swish(up_ref, pad, tt, w_ref, bdw_ref, lg_ref, lb_ref, acc_ref, o_ref):
    kw, d = w_ref.shape
    by_shift = [[(a, SUBLANES * a + s - pad) for a in range((pad + kw - 1) // SUBLANES + 1)
                 if 0 <= SUBLANES * a + s - pad < kw] for s in range(SUBLANES)]

    def strip(c, carry):
        cols = pl.ds(pl.multiple_of(c * LANES, LANES), LANES)
        z = bdw_ref[:, cols]
        for s, taps in enumerate(by_shift):
            n = tt + SUBLANES if s else tt
            q = None
            for a, k in taps:
                term = up_ref[pl.ds(SUBLANES * a, n), cols] * w_ref[pl.ds(k, 1), cols]
                q = term if q is None else q + term
            if q is not None:
                z = z + q[s:s + tt]
        acc_ref[:, cols] = z
        return carry

    lax.fori_loop(0, d // LANES, strip, 0)
    z = acc_ref[...]
    mu = jnp.mean(z, axis=-1, keepdims=True)
    zc = z - mu
    y = zc * lax.rsqrt(jnp.mean(zc * zc, axis=-1, keepdims=True) + EPS)
    y = y * lg_ref[...] + lb_ref[...]
    o_ref[...] = (y * _sigmoid(y)).astype(o_ref.dtype)


def _conv_prompt_body(main_ref, halo_ref, w_ref, bdw_ref, lg_ref, lb_ref, o_ref, up_ref, acc_ref,
                      *, halo, tt):
    kw = w_ref.shape[0]

    @pl.when(pl.program_id(1) == 0)
    def _():
        up_ref[0:halo, :] = jnp.zeros((halo, up_ref.shape[1]), F32)

    @pl.when(pl.program_id(1) > 0)
    def _():
        up_ref[0:halo, :] = halo_ref[...]

    up_ref[halo:halo + tt, :] = main_ref[...]
    _conv_ln_swish(up_ref, halo - (kw - 1), tt, w_ref, bdw_ref, lg_ref, lb_ref, acc_ref, o_ref)


def _conv_prompt_call(u, w_dw, b_dw, ln_g, ln_b, layer, tt):
    b, t, d = u.shape
    kw = w_dw.shape[1]
    halo = -(-(kw - 1) // SUBLANES) * SUBLANES
    tt = _tile(t, tt, halo)
    hb = tt // halo
    vec = lambda: pl.BlockSpec((None, 1, d), lambda bi, ti: (layer, 0, 0))
    blocks = [((tt, d), F32), ((halo, d), F32), ((kw, d), F32), ((tt, d), BF16)]
    scratch = [((halo + tt, d), F32), ((tt, d), F32)]
    return pl.pallas_call(
        functools.partial(_conv_prompt_body, halo=halo, tt=tt),
        out_shape=jax.ShapeDtypeStruct((b, t, d), BF16),
        grid=(b, t // tt),
        in_specs=[
            pl.BlockSpec((None, tt, d), lambda bi, ti: (bi, ti, 0)),
            pl.BlockSpec((None, halo, d), lambda bi, ti: (bi, jnp.maximum(ti * hb - 1, 0), 0)),
            pl.BlockSpec((None, kw, d), lambda bi, ti: (layer, 0, 0)),
            vec(), vec(), vec(),
        ],
        out_specs=pl.BlockSpec((None, tt, d), lambda bi, ti: (bi, ti, 0)),
        scratch_shapes=[pltpu.VMEM(s, dt) for s, dt in scratch],
        compiler_params=pltpu.CompilerParams(
            dimension_semantics=("parallel", "arbitrary"),
            vmem_limit_bytes=_vmem_limit(blocks, scratch)),
        name="dwconv_ln_swish_prompt",
    )(u, u, w_dw, b_dw.reshape(-1, 1, d), ln_g.reshape(-1, 1, d), ln_b.reshape(-1, 1, d))


def _conv_sample_body(up_ref, w_ref, bdw_ref, lg_ref, lb_ref, o_ref, acc_ref, *, pad, tt):
    _conv_ln_swish(up_ref, pad, tt, w_ref, bdw_ref, lg_ref, lb_ref, acc_ref, o_ref)


def _conv_sample_call(up, pad, tt, w_dw, b_dw, ln_g, ln_b, layer):
    b, rows, d = up.shape
    kw = w_dw.shape[1]
    vec = lambda: pl.BlockSpec((None, 1, d), lambda bi: (layer, 0, 0))
    blocks = [((rows, d), F32), ((kw, d), F32), ((tt, d), BF16)]
    scratch = [((tt, d), F32)]
    return pl.pallas_call(
        functools.partial(_conv_sample_body, pad=pad, tt=tt),
        out_shape=jax.ShapeDtypeStruct((b, tt, d), BF16),
        grid=(b,),
        in_specs=[
            pl.BlockSpec((None, rows, d), lambda bi: (bi, 0, 0)),
            pl.BlockSpec((None, kw, d), lambda bi: (layer, 0, 0)),
            vec(), vec(), vec(),
        ],
        out_specs=pl.BlockSpec((None, tt, d), lambda bi: (bi, 0, 0)),
        scratch_shapes=[pltpu.VMEM(s, dt) for s, dt in scratch],
        compiler_params=pltpu.CompilerParams(
            dimension_semantics=("parallel",), vmem_limit_bytes=_vmem_limit(blocks, scratch)),
        name="dwconv_ln_swish_sample",
    )(up, w_dw, b_dw.reshape(-1, 1, d), ln_g.reshape(-1, 1, d), ln_b.reshape(-1, 1, d))


def _bucket_codes(n_q, n_k, k_off, n_buckets):
    rel = (np.arange(n_k) + k_off)[None, :] - np.arange(n_q)[:, None]
    nb = n_buckets // 2
    max_exact = nb // 2
    ret = np.where(rel > 0, nb, 0)
    n = np.abs(rel)
    nf = np.maximum(n, 1).astype(np.float32)
    large = max_exact + (np.log(nf / np.float32(max_exact))
                         / np.float32(math.log(MAX_DISTANCE / max_exact))
                         * np.float32(nb - max_exact)).astype(np.int32)
    large = np.minimum(large, nb - 1)
    return (ret + np.where(n < max_exact, n, large)).astype(np.int32)


def _bias_body(code_ref, table_ref, o_ref, *, n_buckets, group):
    code = code_ref[...]
    n_cols = code.shape[1]

    def head(hh, carry):
        out = jnp.full(code.shape, -jnp.inf, F32)
        for b in range(n_buckets):
            out = jnp.where(code == b, table_ref[hh, b] * LOG2E, out)
        if group is None:
            o_ref[hh] = out
        else:
            cols = pl.ds(pl.multiple_of((hh % group) * n_cols, LANES), n_cols)
            o_ref[hh // group, :, cols] = out
        return carry

    lax.fori_loop(0, table_ref.shape[0], head, 0)


def _bias_call(code, table, group=None):
    n_heads, n_buckets = table.shape
    rows, cols = code.shape
    shape = (n_heads, rows, cols) if group is None else (n_heads // group, rows, group * cols)
    return pl.pallas_call(
        functools.partial(_bias_body, n_buckets=n_buckets, group=group),
        out_shape=jax.ShapeDtypeStruct(shape, F32),
        in_specs=[pl.BlockSpec(memory_space=pltpu.VMEM), pl.BlockSpec(memory_space=pltpu.SMEM)],
        out_specs=pl.BlockSpec(memory_space=pltpu.VMEM),
        name="rel_bias",
    )(jnp.asarray(code), table)


def _unit_scores(qs, kh, bias):
    return lax.dot_general(qs, kh, (((1,), (1,)), ((), ())), preferred_element_type=F32) + bias


def _unit_output(s, vh, sink_col):
    sink_col = sink_col * LOG2E
    m = jnp.maximum(jnp.max(s, axis=-1, keepdims=True), sink_col)
    e = jnp.exp2(s - m)
    l = jnp.sum(e, axis=-1, keepdims=True) + jnp.exp2(sink_col - m)
    return jnp.dot(e.astype(BF16), vh, preferred_element_type=F32) * (1.0 / l)


def _stack_heads(q, h, group):
    hd = HEAD_DIM
    return jnp.concatenate(
        [q[:, (h * group + g) * hd:(h * group + g + 1) * hd] for g in range(group)], axis=0)


def _unstack_heads(o, group):
    tq = o.shape[0] // group
    return [o[g * tq:(g + 1) * tq, :] for g in range(group)]


ATTN_Q_COLS = 512
PAIR = 2 * CHUNK


def _pair_codes(n_buckets):
    code = _bucket_codes(PAIR, WINDOW + PAIR, -WINDOW, n_buckets).T
    key_chunk = np.arange(WINDOW + PAIR)[:, None] // CHUNK
    q_chunk = np.arange(PAIR)[None, :] // CHUNK
    visible = (key_chunk >= q_chunk) & (key_chunk <= q_chunk + WINDOW // CHUNK)
    return np.where(visible, code, n_buckets).astype(np.int32)


def _attn_prompt_body(qT_ref, kp_ref, kc_ref, vTp_ref, vTc_ref, bias_ref, sink_ref, oT_ref,
                      k_scr, vT_scr, *, kvh, group):
    i = pl.program_id(1)
    tq = qT_ref.shape[1]
    nk = WINDOW + PAIR
    hd = HEAD_DIM
    for h in range(kvh):
        k_scr[h, 0:WINDOW, :] = kp_ref[:, h * hd:(h + 1) * hd].astype(BF16)
        k_scr[h, WINDOW:WINDOW + tq, :] = kc_ref[:, h * hd:(h + 1) * hd].astype(BF16)
    vT_scr[:, 0:WINDOW] = vTp_ref[...].astype(BF16)
    vT_scr[:, WINDOW:WINDOW + tq] = vTc_ref[...].astype(BF16)
    row = lax.broadcasted_iota(jnp.int32, (nk, group * PAIR), 0)
    start_mask = jnp.where(row < jnp.where(i == 0, WINDOW, 0), -jnp.inf, 0.0).astype(F32)
    units = [(p * PAIR, h) for p in range(tq // PAIR) for h in range(kvh)]

    def scores(c0, h):
        qsT = jnp.concatenate(
            [qT_ref[hh * hd:(hh + 1) * hd, c0:c0 + PAIR] for hh in range(h * group, (h + 1) * group)],
            axis=1)
        s = jnp.dot(k_scr[h, c0:c0 + nk, :], qsT, preferred_element_type=F32) + bias_ref[h]
        return s + start_mask if c0 == 0 else s

    s_next = scores(*units[0])
    for n, (c0, h) in enumerate(units):
        s = s_next
        if n + 1 < len(units):
            s_next = scores(*units[n + 1])
        sink = sink_ref[h] * LOG2E
        m = jnp.maximum(jnp.max(s, axis=0, keepdims=True), sink)
        e = jnp.exp2(s - m)
        l = jnp.sum(e, axis=0, keepdims=True) + jnp.exp2(sink - m)
        oT = jnp.dot(vT_scr[h * hd:(h + 1) * hd, c0:c0 + nk], e.astype(BF16),
                     preferred_element_type=F32) * (1.0 / l)
        for g in range(group):
            hh = h * group + g
            oT_ref[hh * hd:(hh + 1) * hd, c0:c0 + PAIR] = (
                oT[:, g * PAIR:(g + 1) * PAIR].astype(oT_ref.dtype))


def _attn_prompt_call(qT, k, vT, bias, sink_row, kvh, group):
    b, dq, s = qT.shape
    dk = k.shape[2]
    tq = _tile(s, ATTN_Q_COLS, PAIR)
    per = tq // WINDOW
    prev = lambda i: jnp.maximum(i * per - 1, 0)
    blocks = [((dq, tq), BF16)] * 2 + [((WINDOW + tq, dk), F32)] * 2 + [
        (bias.shape, F32), (sink_row.shape[:1] + (SUBLANES, sink_row.shape[2]), F32)]
    scratch = [((kvh, WINDOW + tq, LANES), BF16), ((dk, WINDOW + tq), BF16)]
    return pl.pallas_call(
        functools.partial(_attn_prompt_body, kvh=kvh, group=group),
        out_shape=jax.ShapeDtypeStruct((b, dq, s), BF16),
        grid=(b, s // tq),
        in_specs=[
            pl.BlockSpec((None, dq, tq), lambda bi, i: (bi, 0, i)),
            pl.BlockSpec((None, WINDOW, dk), lambda bi, i: (bi, prev(i), 0)),
            pl.BlockSpec((None, tq, dk), lambda bi, i: (bi, i, 0)),
            pl.BlockSpec((None, dk, WINDOW), lambda bi, i: (bi, 0, prev(i))),
            pl.BlockSpec((None, dk, tq), lambda bi, i: (bi, 0, i)),
            pl.BlockSpec(bias.shape, lambda bi, i: (0, 0, 0)),
            pl.BlockSpec(sink_row.shape, lambda bi, i: (0, 0, 0)),
        ],
        out_specs=pl.BlockSpec((None, dq, tq), lambda bi, i: (bi, 0, i)),
        scratch_shapes=[pltpu.VMEM((kvh, WINDOW + tq, HEAD_DIM), BF16),
                        pltpu.VMEM((dk, WINDOW + tq), BF16)],
        compiler_params=pltpu.CompilerParams(
            dimension_semantics=("parallel", "arbitrary"),
            vmem_limit_bytes=_vmem_limit(blocks, scratch)),
        name="swa_sink_attention_prompt",
    )(qT, k, k, vT, vT, bias, sink_row)


ATTN_SAMPLE_BATCH = 4


def _attn_sample_body(q_ref, k_ref, v_ref, bias_ref, sink_ref, o_ref, *, kvh, group):
    hd = HEAD_DIM
    units = [(b, h) for b in range(q_ref.shape[0]) for h in range(kvh)]

    def scores(b, h):
        qs = _stack_heads(q_ref[b].astype(F32), h, group).astype(BF16)
        return _unit_scores(qs, k_ref[b, :, h * hd:(h + 1) * hd].astype(BF16), bias_ref[h])

    s_next = scores(*units[0])
    outs = []
    for n, (b, h) in enumerate(units):
        s = s_next
        if n + 1 < len(units):
            s_next = scores(*units[n + 1])
        o = _unit_output(s, v_ref[b, :, h * hd:(h + 1) * hd].astype(BF16), sink_ref[h])
        outs += _unstack_heads(o, group)
        if h == kvh - 1:
            o_ref[b] = jnp.concatenate(outs, axis=1).astype(o_ref.dtype)
            outs = []


def _attn_sample_call(q, k_all, v_all, bias, sink_col, kvh, group):
    b, t, dq = q.shape
    nk, dk = k_all.shape[1:]
    bb = math.gcd(b, ATTN_SAMPLE_BATCH)
    blocks = [((bb, t, dq), BF16)] * 2 + [((bb, nk, dk), F32)] * 2 + [
        (bias.shape[:2] + (2 * LANES,), F32), (sink_col.shape[:2] + (LANES,), F32)]
    return pl.pallas_call(
        functools.partial(_attn_sample_body, kvh=kvh, group=group),
        out_shape=jax.ShapeDtypeStruct((b, t, dq), BF16),
        grid=(b // bb,),
        in_specs=[
            pl.BlockSpec((bb, t, dq), lambda bi: (bi, 0, 0)),
            pl.BlockSpec((bb, nk, dk), lambda bi: (bi, 0, 0)),
            pl.BlockSpec((bb, nk, dk), lambda bi: (bi, 0, 0)),
            pl.BlockSpec(bias.shape, lambda bi: (0, 0, 0)),
            pl.BlockSpec(sink_col.shape, lambda bi: (0, 0, 0)),
        ],
        out_specs=pl.BlockSpec((bb, t, dq), lambda bi: (bi, 0, 0)),
        compiler_params=pltpu.CompilerParams(
            dimension_semantics=("parallel",), vmem_limit_bytes=_vmem_limit(blocks)),
        name="swa_sink_attention_sample",
    )(q, k_all, v_all, bias, sink_col)


def _trunks(xp3, xs3, mods_p, mods_s, state_conv, win_k, win_v, p):
    bp, tp, d = xp3.shape
    bs, ts, _ = xs3.shape
    mp, ms = bp * tp, bs * ts
    depth = p["w_mod"].shape[0]
    d_ff = p["w_gu"].shape[2] // 2
    n_heads = p["attn_sinks"].shape[1]
    dq = n_heads * HEAD_DIM
    kvh = (p["w_qkv"].shape[2] - dq) // (2 * HEAD_DIM)
    group = n_heads // kvh
    dkv = kvh * HEAD_DIM
    kw = p["w_dw"].shape[1]
    n_buckets = p["rel_bias_table"].shape[1]
    conv_p, k_p, v_p, conv_s, k_s, v_s = [], [], [], [], [], []

    xp, xs = xp3.reshape(mp, d), xs3.reshape(ms, d)
    hp = _norm_mod_call(xp, p["norm_mix"][0], mods_p[0], 0, 1, 1024)
    hs = _norm_mod_call(xs, p["norm_mix"][0], mods_s[0], 0, 1, ms)
    for i in range(depth):
        j = i // 2
        ffn_norm = lambda mods: (p["norm_ffn"][i], mods[i], 3, 4)
        if i % 2 == 0:
            u_p, u_s, wq = _mm_call(hp, p["w_pw1"], j, (0, d), d, x2=hs, cast=(p["w_pw2"], j),
                                    bias=p["b_pw1"], act="glu", tm=1024, tn=512, name="pw1_glu")
            u_p, u_s = u_p.reshape(bp, tp, d), u_s.reshape(bs, ts, d)
            conv_w = (p["w_dw"], p["b_dw"], p["conv_ln_g"], p["conv_ln_b"], j)
            z_p = _conv_prompt_call(u_p, *conv_w, 128)
            conv_p.append(u_p[:, tp - (kw - 1):])
            pad = (-(kw - 1 + ts)) % SUBLANES
            up = jnp.concatenate([jnp.zeros((bs, pad, d), F32), state_conv[j], u_s], axis=1)
            z_s = _conv_sample_call(up, pad, ts, *conv_w)
            conv_s.append(up[:, pad + ts:])
            xp, hp = _rowmm_call(z_p.reshape(mp, d), wq, xp, mods_p[i], 2, ffn_norm(mods_p),
                                 bias=p["b_pw2"][j], tm=512, name="pw2_residual_norm")
            xs, hs = _rowmm_call(z_s.reshape(ms, d), wq, xs, mods_s[i], 2, ffn_norm(mods_s),
                                 bias=p["b_pw2"][j], tm=512, name="pw2_residual_norm")
        else:
            sinks = p["attn_sinks"][j].reshape(kvh, group, 1)
            qT, q_s, wq = _mm_call(hp, p["w_qkv"], j, (0,), dq, x2=hs, cast=(p["w_o"], j),
                                   bias=p["b_qkv"], scale=LOG2E * HEAD_DIM ** -0.5,
                                   out_dtype=BF16, tm=1024, tn=1024, name="q_proj",
                                   out_mode="t", seq=tp)
            (k, vT), kv_s = _mm_call(hp, p["w_qkv"], j, (dq,), 2 * dkv, x2=hs, bias=p["b_qkv"],
                                     tm=1024, tn=2 * dkv, name="kv_proj", out_mode="split_t",
                                     seq=tp)
            k = k.reshape(bp, tp, dkv)
            bias = _bias_call(_pair_codes(n_buckets), p["rel_bias_table"], group)
            sink_row = jnp.repeat(sinks, PAIR, axis=2).reshape(kvh, 1, group * PAIR)
            oT = _attn_prompt_call(qT, k, vT, bias, sink_row, kvh, group)
            k_p.append(k[:, tp - WINDOW:].reshape(bp, WINDOW, kvh, HEAD_DIM))
            v_p.append(jnp.swapaxes(vT[:, :, tp - WINDOW:], 1, 2)
                       .reshape(bp, WINDOW, kvh, HEAD_DIM))
            kv3 = kv_s.reshape(bs, ts, 2, kvh, HEAD_DIM)
            k_all = jnp.concatenate([win_k[j], kv3[:, :, 0]], axis=1)
            v_all = jnp.concatenate([win_v[j], kv3[:, :, 1]], axis=1)
            n_keys = WINDOW + ts
            bias = _bias_call(_bucket_codes(ts, n_keys, -WINDOW, n_buckets),
                              p["rel_bias_table"]).reshape(kvh, group * ts, n_keys)
            o_s = _attn_sample_call(q_s.reshape(bs, ts, dq), k_all.reshape(bs, n_keys, dkv),
                                    v_all.reshape(bs, n_keys, dkv), bias,
                                    jnp.repeat(sinks, ts, axis=1), kvh, group)
            k_s.append(k_all[:, ts:])
            v_s.append(v_all[:, ts:])
            xp, hp = _rowmm_call(oT, wq, xp, mods_p[i], 2, ffn_norm(mods_p), bias=p["b_o"][j],
                                 x_t=True, tm=512, name="wo_residual_norm")
            xs, hs = _rowmm_call(o_s.reshape(ms, dq), wq, xs, mods_s[i], 2, ffn_norm(mods_s),
                                 bias=p["b_o"][j], tm=512, name="wo_residual_norm")
        a_p, a_s, wq = _mm_call(hp, p["w_gu"], i, (0, d_ff), d_ff, x2=hs, cast=(p["w_down"], i),
                                act="swiglu", out_dtype=BF16, tm=1024, tn=512, name="ffn_gate_up")
        if i + 1 < depth:
            nxt = lambda mods: (p["norm_mix"][i + 1], mods[i + 1], 0, 1)
            xp, hp = _rowmm_call(a_p, wq, xp, mods_p[i], 5, nxt(mods_p), tm=256,
                                 name="ffn_down_residual_norm")
            xs, hs = _rowmm_call(a_s, wq, xs, mods_s[i], 5, nxt(mods_s), tm=256,
                                 name="ffn_down_residual_norm")
        else:
            y_p = _rowmm_call(a_p, wq, xp, mods_p[i], 5, (p["norm_out"],), tm=256,
                              name="ffn_down_residual_final_norm")
            y_s = _rowmm_call(a_s, wq, xs, mods_s[i], 5, (p["norm_out"],), tm=256,
                              name="ffn_down_residual_final_norm")
    stack = jnp.stack
    return (y_p.reshape(bp, tp, d), y_s.reshape(bs, ts, d), stack(conv_p), stack(k_p), stack(v_p),
            stack(conv_s), stack(k_s), stack(v_s))


def kernel(x_prompt, x_sample, c_prompt, c_sample, state_conv, cache_win_k, cache_win_v, w_mod, b_mod, norm_mix, norm_ffn, w_pw1, b_pw1, w_dw, b_dw, conv_ln_g, conv_ln_b, w_pw2, b_pw2, w_qkv, b_qkv, w_o, b_o, attn_sinks, rel_bias_table, w_gu, w_down, norm_out):
    p = dict(w_mod=w_mod, b_mod=b_mod, norm_mix=norm_mix, norm_ffn=norm_ffn, w_pw1=w_pw1,
             b_pw1=b_pw1, w_dw=w_dw, b_dw=b_dw, conv_ln_g=conv_ln_g, conv_ln_b=conv_ln_b,
             w_pw2=w_pw2, b_pw2=b_pw2, w_qkv=w_qkv, b_qkv=b_qkv, w_o=w_o, b_o=b_o,
             attn_sinks=attn_sinks, rel_bias_table=rel_bias_table, w_gu=w_gu, w_down=w_down,
             norm_out=norm_out)
    bp, sp, d = x_prompt.shape
    bs, ts, _ = x_sample.shape
    depth = w_mod.shape[0]

    n_c = bp + bs
    c_all = jnp.concatenate(
        [c_prompt, c_sample, jnp.zeros((-n_c % (2 * SUBLANES), d), F32)], axis=0)
    mod_all = _mod_call(c_all, w_mod, b_mod)

    mods_p = [_Mod(mod_all[l, :bp].reshape(bp, 1, 6 * d), sp) for l in range(depth)]
    mods_s = [_Mod(mod_all[l, bp:n_c].reshape(bs, 1, 6 * d), ts) for l in range(depth)]

    return _trunks(x_prompt, x_sample, mods_p, mods_s, state_conv, cache_win_k, cache_win_v, p)
```

```python
import functools
import math

import numpy as np
import jax
import jax.numpy as jnp
from jax import lax
from jax.experimental import pallas as pl
from jax.experimental.pallas import tpu as pltpu

F32 = jnp.float32
BF16 = jnp.bfloat16

CHUNK = 64
WINDOW = 128
HEAD_DIM = 64
MAX_DISTANCE = 128
EPS = 1e-6
LOG2E = math.log2(math.e)

V7X_VMEM_BYTES = 64 * 2**20
V7X_VMEM_CAP = V7X_VMEM_BYTES - 8 * 2**20
SUBLANES = 8
LANES = 128


def _nbytes(shape, dtype):
    return math.prod(shape) * jnp.dtype(dtype).itemsize


def _vmem_limit(blocks, scratch=()):
    est = 2 * sum(_nbytes(s, d) for s, d in blocks) + sum(_nbytes(s, d) for s, d in scratch)
    return int(min(V7X_VMEM_CAP, max(32 * 2**20, 2 * est)))


def _tile(dim, pref, mult=LANES):
    if dim <= pref:
        return dim
    t = (pref // mult) * mult
    while t >= mult:
        if dim % t == 0:
            return t
        t -= mult
    return dim


def _sigmoid(x):
    return 1.0 / (1.0 + jnp.exp(-x))


def _mod_body(c_ref, w_ref, b_ref, o_ref):
    c = c_ref[...]
    a = (c * _sigmoid(c)).astype(BF16)
    o_ref[...] = jnp.dot(a, w_ref[...].astype(BF16), preferred_element_type=F32) + b_ref[...]


def _mod_call(c_all, w, layer, b_mod):
    d, n = w.shape[-2:]
    mp = c_all.shape[0]
    tn = _tile(n, 1024)
    if w.ndim == 3:
        w_spec = pl.BlockSpec((None, d, tn), lambda j: (layer, 0, j))
    else:
        w_spec = pl.BlockSpec((d, tn), lambda j: (0, j))
    blocks = [((mp, d), F32), ((d, tn), w.dtype), ((1, tn), F32), ((mp, tn), F32)]
    return pl.pallas_call(
        _mod_body,
        out_shape=jax.ShapeDtypeStruct((mp, n), F32),
        grid=(n // tn,),
        in_specs=[
            pl.BlockSpec((mp, d), lambda j: (0, 0)),
            w_spec,
            pl.BlockSpec((None, 1, tn), lambda j: (layer, 0, j)),
        ],
        out_specs=pl.BlockSpec((mp, tn), lambda j: (0, j)),
        compiler_params=pltpu.CompilerParams(
            dimension_semantics=("arbitrary",),
            vmem_limit_bytes=_vmem_limit(blocks, [((d, tn), BF16)])),
        name="adaln_mod",
    )(c_all, w, b_mod.reshape(b_mod.shape[0], 1, n))


class _Mod:
    def __init__(self, arr, rows_per_group):
        self.arr = arr
        self.rows_per_group = rows_per_group

    def tile_rows(self, total_rows, pref, mult):
        if self.rows_per_group >= pref:
            return _tile(self.rows_per_group, pref, mult)
        return _tile(total_rows, pref, max(mult, self.rows_per_group))

    def spec(self, seg, d, tm):
        if tm <= self.rows_per_group:
            per = self.rows_per_group // tm
            return pl.BlockSpec((1, 1, d), lambda i: (i // per, 0, seg))
        return pl.BlockSpec((tm // self.rows_per_group, 1, d), lambda i: (i, 0, seg))


def _grouped(v, m):
    return v.reshape(m.shape[0], v.shape[0] // m.shape[0], v.shape[1])


def _norm_mod_body(x_ref, g_ref, sh_ref, sc_ref, o_ref):
    x = x_ref[...]
    y = x * lax.rsqrt(jnp.mean(x * x, axis=-1, keepdims=True) + EPS) * g_ref[...]
    sc = sc_ref[...]
    o_ref[...] = (_grouped(y, sc) * (1.0 + sc) + sh_ref[...]).reshape(x.shape).astype(o_ref.dtype)


def _norm_mod_call(x, g, mod, seg_shift, seg_scale, tm):
    m, d = x.shape
    tm = mod.tile_rows(m, tm, SUBLANES)
    blocks = [((tm, d), F32), ((1, d), F32), ((tm, d), BF16)]
    return pl.pallas_call(
        _norm_mod_body,
        out_shape=jax.ShapeDtypeStruct((m, d), BF16),
        grid=(m // tm,),
        in_specs=[
            pl.BlockSpec((tm, d), lambda i: (i, 0)),
            pl.BlockSpec((1, d), lambda i: (0, 0)),
            mod.spec(seg_shift, d, tm),
            mod.spec(seg_scale, d, tm),
        ],
        out_specs=pl.BlockSpec((tm, d), lambda i: (i, 0)),
        compiler_params=pltpu.CompilerParams(
            dimension_semantics=("parallel",), vmem_limit_bytes=_vmem_limit(blocks)),
        name="rmsnorm_modulate",
    )(x, g.reshape(1, d), mod.arr, mod.arr)


def _mm_body(x_ref, *refs, n_w, has_bias, act, scale, out_mode, has_x2, n_cast):
    if has_x2:
        x2_ref, refs = refs[0], refs[1:]
    ws = refs[:n_w]
    refs = refs[n_w:]
    bs = refs[:n_w] if has_bias else ()
    refs = refs[len(bs):]
    wc_refs, refs = refs[:n_cast], refs[n_cast:]
    n_o = 2 if out_mode == "split_t" else 1
    o_refs = refs[:n_o]
    refs = refs[n_o:]
    if has_x2:
        o2_ref, refs = refs[0], refs[1:]
    for wc_ref, oc_ref in zip(wc_refs, refs[:n_cast]):
        oc_ref[...] = wc_ref[...].astype(BF16)
    wbs = refs[n_cast:]

    def compute(x):
        ps = [None] * n_w
        for i in ((1, 0) if act == "glu" else range(n_w)):
            p = jnp.dot(x, wbs[i][...], preferred_element_type=F32)
            if has_bias:
                p = p + bs[i][...]
            ps[i] = p
        if act == "glu":
            y = ps[0] * _sigmoid(ps[1])
        elif act == "swiglu":
            y = ps[0] * _sigmoid(ps[0]) * ps[1]
        else:
            y = ps[0]
        return y * scale if scale != 1.0 else y

    @pl.when(pl.program_id(1) == 0)
    def _():
        for w, wb in zip(ws, wbs):
            wb[...] = w[...].astype(BF16)
        if has_x2:
            o2_ref[...] = compute(x2_ref[...]).astype(o2_ref.dtype)

    y = compute(x_ref[...])
    if out_mode == "plain":
        o_refs[0][...] = y.astype(o_refs[0].dtype)
    elif out_mode == "t":
        o_refs[0][...] = y.T.astype(o_refs[0].dtype)
    else:
        half = y.shape[1] // 2
        o_refs[0][...] = y[:, :half].astype(o_refs[0].dtype)
        o_refs[1][...] = y[:, half:].T.astype(o_refs[1].dtype)


def _mm_call(x, w, layer, col_starts, n_out, *, x2=None, cast=(), bias=None, act=None,
             scale=1.0, out_dtype=F32, tm=512, tn=512, name="matmul", out_mode="plain", seq=None):
    m, k = x.shape
    if out_mode == "plain":
        tm = _tile(m, tm, SUBLANES)
    else:
        tm = _tile(seq, tm, LANES)
    tn = _tile(n_out, tn)
    n_w = len(col_starts)
    assert all(c % tn == 0 for c in col_starts) and m % tm == 0 and n_out % tn == 0
    has_bias = bias is not None
    per_b = None if seq is None else seq // tm

    in_specs = [pl.BlockSpec((tm, k), lambda j, i: (i, 0))]
    args = [x]
    blocks = [((tm, k), x.dtype), ((tm, tn), out_dtype)]
    if x2 is not None:
        m2 = x2.shape[0]
        in_specs.append(pl.BlockSpec((m2, k), lambda j, i: (0, 0)))
        args.append(x2)
        blocks += [((m2, k), x2.dtype), ((m2, tn), out_dtype)]
    for c in col_starts:
        off = c // tn
        in_specs.append(pl.BlockSpec((None, k, tn), lambda j, i, off=off: (layer, 0, off + j)))
        args.append(w)
        blocks.append(((k, tn), F32))
    if has_bias:
        b3 = bias.reshape(bias.shape[0], 1, bias.shape[1])
        for c in col_starts:
            off = c // tn
            in_specs.append(pl.BlockSpec((None, 1, tn), lambda j, i, off=off: (layer, 0, off + j)))
            args.append(b3)
    n_i = m // tm
    n_steps = (n_out // tn) * n_i
    cast_out = []
    for w_other, layer_other in cast:
        kc, dc = w_other.shape[1:]
        slab = kc // n_steps
        assert slab * n_steps == kc and slab % (2 * SUBLANES) == 0
        in_specs.append(pl.BlockSpec(
            (None, slab, dc), lambda j, i, layer_other=layer_other: (layer_other, j * n_i + i, 0)))
        args.append(w_other)
        blocks += [((slab, dc), F32), ((slab, dc), BF16)]
        cast_out.append((jax.ShapeDtypeStruct((kc, dc), BF16),
                         pl.BlockSpec((slab, dc), lambda j, i: (j * n_i + i, 0))))
    scratch = [((k, tn), BF16)] * n_w
    body = functools.partial(_mm_body, n_w=n_w, has_bias=has_bias, act=act, scale=scale,
                             out_mode=out_mode, has_x2=x2 is not None, n_cast=len(cast))
    if out_mode == "plain":
        out_shape = [jax.ShapeDtypeStruct((m, n_out), out_dtype)]
        out_specs = [pl.BlockSpec((tm, tn), lambda j, i: (i, j))]
    elif out_mode == "t":
        out_shape = [jax.ShapeDtypeStruct((m // seq, n_out, seq), out_dtype)]
        out_specs = [pl.BlockSpec((None, tn, tm), lambda j, i: (i // per_b, j, i % per_b))]
    else:
        assert tn == n_out
        half = n_out // 2
        out_shape = [jax.ShapeDtypeStruct((m, half), out_dtype),
                     jax.ShapeDtypeStruct((m // seq, half, seq), out_dtype)]
        out_specs = [pl.BlockSpec((tm, half), lambda j, i: (i, 0)),
                     pl.BlockSpec((None, half, tm), lambda j, i: (i // per_b, 0, i % per_b))]
    if x2 is not None:
        out_shape.append(jax.ShapeDtypeStruct((m2, n_out), out_dtype))
        out_specs.append(pl.BlockSpec((m2, tn), lambda j, i: (0, j)))
    for shape, spec in cast_out:
        out_shape.append(shape)
        out_specs.append(spec)
    outs = pl.pallas_call(
        body,
        out_shape=out_shape,
        grid=(n_out // tn, n_i),
        in_specs=in_specs,
        out_specs=out_specs,
        scratch_shapes=[pltpu.VMEM(s, d) for s, d in scratch],
        compiler_params=pltpu.CompilerParams(
            dimension_semantics=("arbitrary", "arbitrary"),
            vmem_limit_bytes=_vmem_limit(blocks, scratch)),
        name=name,
    )(*args)
    n_primary = 2 if out_mode == "split_t" else 1
    primary = outs[0] if n_primary == 1 else outs[:n_primary]
    extras = list(outs[n_primary:])
    return (primary, *extras) if extras else primary


ROWMM_SUB_ROWS = 256


def _rowmm_body(*refs, has_bias, x_t, final):
    x_ref, w_hbm = refs[:2]
    refs = refs[2:]
    if has_bias:
        b_ref, refs = refs[0], refs[1:]
    res_ref, gate_ref, ng_ref = refs[:3]
    refs = refs[3:]
    if not final:
        sh_ref, sc_ref = refs[:2]
        refs = refs[2:]
    n_o = 1 if final else 2
    o_refs = refs[:n_o]
    refs = refs[n_o:]
    w_res, sem = refs

    @pl.when(pl.program_id(0) == 0)
    def _():
        whole = pltpu.make_async_copy(w_hbm, w_res, sem)
        whole.start()
        whole.wait()

    tm = res_ref.shape[0]
    sub = min(tm, ROWMM_SUB_ROWS)

    def groups_of(ref, r0):
        per = tm // ref.shape[0]
        return ref[...] if per >= tm else ref[r0 // per:(r0 + sub) // per]

    for r0 in range(0, tm, sub):
        x = x_ref[:, r0:r0 + sub].T if x_t else x_ref[r0:r0 + sub, :]
        y = jnp.dot(x, w_res[...], preferred_element_type=F32)
        if has_bias:
            y = y + b_ref[...]
        gate = groups_of(gate_ref, r0)
        xn = (_grouped(res_ref[r0:r0 + sub, :], gate) + gate * _grouped(y, gate)).reshape(y.shape)
        r = xn * lax.rsqrt(jnp.mean(xn * xn, axis=-1, keepdims=True) + EPS) * ng_ref[...]
        if final:
            o_refs[0][r0:r0 + sub, :] = r
        else:
            sc = groups_of(sc_ref, r0)
            o_refs[0][r0:r0 + sub, :] = xn
            o_refs[1][r0:r0 + sub, :] = (
                _grouped(r, sc) * (1.0 + sc) + groups_of(sh_ref, r0)
            ).reshape(y.shape).astype(o_refs[1].dtype)


def _rowmm_call(x, w, res, gate, gate_seg, nxt, *, bias=None, x_t=False, tm=512, name="rowmm"):
    if x_t:
        nb, k, seq = x.shape
        m = nb * seq
    else:
        m, k = x.shape
    d = w.shape[-1]
    final = len(nxt) == 1
    scratch = [((k, d), BF16)]
    n_mods = 1 if final else 3

    def vmem_need(rows):
        blocks = [((rows, k), BF16), ((rows, d), F32), ((rows, d), F32 if final else BF16)]
        blocks += [((rows, d), F32)] * (0 if final else 1)
        blocks += [((-(-rows // gate.rows_per_group) * SUBLANES, d), F32)] * n_mods
        temporaries = 4 * _nbytes((rows, d), F32)
        return (2 * sum(_nbytes(s, t) for s, t in blocks) + sum(_nbytes(s, t) for s, t in scratch)
                + temporaries)

    mult = LANES if x_t else SUBLANES
    tm = gate.tile_rows(m, tm, mult)
    while vmem_need(tm) > V7X_VMEM_CAP and tm % (2 * max(mult, min(tm, gate.rows_per_group))) == 0:
        tm //= 2
    assert final or nxt[1].rows_per_group == gate.rows_per_group
    has_bias = bias is not None
    row = lambda i: (i, 0)
    if x_t:
        per_b = seq // tm
        in_specs = [pl.BlockSpec((None, k, tm), lambda i: (i // per_b, 0, i % per_b))]
    else:
        in_specs = [pl.BlockSpec((tm, k), row)]
    in_specs.append(pl.BlockSpec(memory_space=pltpu.HBM))
    args = [x, w]
    if has_bias:
        in_specs.append(pl.BlockSpec((1, d), lambda i: (0, 0)))
        args.append(bias.reshape(1, d))
    in_specs += [pl.BlockSpec((tm, d), row), gate.spec(gate_seg, d, tm),
                 pl.BlockSpec((1, d), lambda i: (0, 0))]
    args += [res, gate.arr, nxt[0].reshape(1, d)]
    if final:
        out_shape = jax.ShapeDtypeStruct((m, d), F32)
        out_specs = pl.BlockSpec((tm, d), row)
    else:
        _, nmod, seg_shift, seg_scale = nxt
        in_specs += [nmod.spec(seg_shift, d, tm), nmod.spec(seg_scale, d, tm)]
        args += [nmod.arr, nmod.arr]
        out_shape = [jax.ShapeDtypeStruct((m, d), F32), jax.ShapeDtypeStruct((m, d), BF16)]
        out_specs = [pl.BlockSpec((tm, d), row), pl.BlockSpec((tm, d), row)]
    return pl.pallas_call(
        functools.partial(_rowmm_body, has_bias=has_bias, x_t=x_t, final=final),
        out_shape=out_shape,
        grid=(m // tm,),
        in_specs=in_specs,
        out_specs=out_specs,
        scratch_shapes=[pltpu.VMEM(s, dt) for s, dt in scratch] + [pltpu.SemaphoreType.DMA(())],
        compiler_params=pltpu.CompilerParams(
            dimension_semantics=("arbitrary",),
            vmem_limit_bytes=min(V7X_VMEM_CAP, max(32 * 2**20, vmem_need(tm) + 4 * 2**20))),
        name=name,
    )(*args)


def _conv_ln_swish(up_ref, pad, tt, w_ref, bdw_ref, lg_ref, lb_ref, acc_ref, o_ref):
    kw, d = w_ref.shape
    by_shift = [[(a, SUBLANES * a + s - pad) for a in range((pad + kw - 1) // SUBLANES + 1)
                 if 0 <= SUBLANES * a + s - pad < kw] for s in range(SUBLANES)]

    def strip(c, carry):
        cols = pl.ds(pl.multiple_of(c * LANES, LANES), LANES)
        z = bdw_ref[:, cols]
        for s, taps in enumerate(by_shift):
            n = tt + SUBLANES if s else tt
            q = None
            for a, k in taps:
                term = up_ref[pl.ds(SUBLANES * a, n), cols] * w_ref[pl.ds(k, 1), cols]
                q = term if q is None else q + term
            if q is not None:
                z = z + q[s:s + tt]
        acc_ref[:, cols] = z
        return carry

    lax.fori_loop(0, d // LANES, strip, 0)
    z = acc_ref[...]
    mu = jnp.mean(z, axis=-1, keepdims=True)
    zc = z - mu
    y = zc * lax.rsqrt(jnp.mean(zc * zc, axis=-1, keepdims=True) + EPS)
    y = y * lg_ref[...] + lb_ref[...]
    o_ref[...] = (y * _sigmoid(y)).astype(o_ref.dtype)


def _conv_prompt_body(main_ref, halo_ref, w_ref, bdw_ref, lg_ref, lb_ref, o_ref, up_ref, acc_ref,
                      *, halo, tt):
    kw = w_ref.shape[0]

    @pl.when(pl.program_id(1) == 0)
    def _():
        up_ref[0:halo, :] = jnp.zeros((halo, up_ref.shape[1]), F32)

    @pl.when(pl.program_id(1) > 0)
    def _():
        up_ref[0:halo, :] = halo_ref[...]

    up_ref[halo:halo + tt, :] = main_ref[...]
    _conv_ln_swish(up_ref, halo - (kw - 1), tt, w_ref, bdw_ref, lg_ref, lb_ref, acc_ref, o_ref)


def _conv_prompt_call(u, w_dw, b_dw, ln_g, ln_b, layer, tt):
    b, t, d = u.shape
    kw = w_dw.shape[1]
    halo = -(-(kw - 1) // SUBLANES) * SUBLANES
    tt = _tile(t, tt, halo)
    hb = tt // halo
    vec = lambda: pl.BlockSpec((None, 1, d), lambda bi, ti: (layer, 0, 0))
    blocks = [((tt, d), F32), ((halo, d), F32), ((kw, d), F32), ((tt, d), BF16)]
    scratch = [((halo + tt, d), F32), ((tt, d), F32)]
    return pl.pallas_call(
        functools.partial(_conv_prompt_body, halo=halo, tt=tt),
        out_shape=jax.ShapeDtypeStruct((b, t, d), BF16),
        grid=(b, t // tt),
        in_specs=[
            pl.BlockSpec((None, tt, d), lambda bi, ti: (bi, ti, 0)),
            pl.BlockSpec((None, halo, d), lambda bi, ti: (bi, jnp.maximum(ti * hb - 1, 0), 0)),
            pl.BlockSpec((None, kw, d), lambda bi, ti: (layer, 0, 0)),
            vec(), vec(), vec(),
        ],
        out_specs=pl.BlockSpec((None, tt, d), lambda bi, ti: (bi, ti, 0)),
        scratch_shapes=[pltpu.VMEM(s, dt) for s, dt in scratch],
        compiler_params=pltpu.CompilerParams(
            dimension_semantics=("parallel", "arbitrary"),
            vmem_limit_bytes=_vmem_limit(blocks, scratch)),
        name="dwconv_ln_swish_prompt",
    )(u, u, w_dw, b_dw.reshape(-1, 1, d), ln_g.reshape(-1, 1, d), ln_b.reshape(-1, 1, d))


def _conv_sample_body(up_ref, w_ref, bdw_ref, lg_ref, lb_ref, o_ref, acc_ref, *, pad, tt):
    _conv_ln_swish(up_ref, pad, tt, w_ref, bdw_ref, lg_ref, lb_ref, acc_ref, o_ref)


def _conv_sample_call(up, pad, tt, w_dw, b_dw, ln_g, ln_b, layer):
    b, rows, d = up.shape
    kw = w_dw.shape[1]
    vec = lambda: pl.BlockSpec((None, 1, d), lambda bi: (layer, 0, 0))
    blocks = [((rows, d), F32), ((kw, d), F32), ((tt, d), BF16)]
    scratch = [((tt, d), F32)]
    return pl.pallas_call(
        functools.partial(_conv_sample_body, pad=pad, tt=tt),
        out_shape=jax.ShapeDtypeStruct((b, tt, d), BF16),
        grid=(b,),
        in_specs=[
            pl.BlockSpec((None, rows, d), lambda bi: (bi, 0, 0)),
            pl.BlockSpec((None, kw, d), lambda bi: (layer, 0, 0)),
            vec(), vec(), vec(),
        ],
        out_specs=pl.BlockSpec((None, tt, d), lambda bi: (bi, 0, 0)),
        scratch_shapes=[pltpu.VMEM(s, dt) for s, dt in scratch],
        compiler_params=pltpu.CompilerParams(
            dimension_semantics=("parallel",), vmem_limit_bytes=_vmem_limit(blocks, scratch)),
        name="dwconv_ln_swish_sample",
    )(up, w_dw, b_dw.reshape(-1, 1, d), ln_g.reshape(-1, 1, d), ln_b.reshape(-1, 1, d))


def _bucket_codes(n_q, n_k, k_off, n_buckets):
    rel = (np.arange(n_k) + k_off)[None, :] - np.arange(n_q)[:, None]
    nb = n_buckets // 2
    max_exact = nb // 2
    ret = np.where(rel > 0, nb, 0)
    n = np.abs(rel)
    nf = np.maximum(n, 1).astype(np.float32)
    large = max_exact + (np.log(nf / np.float32(max_exact))
                         / np.float32(math.log(MAX_DISTANCE / max_exact))
                         * np.float32(nb - max_exact)).astype(np.int32)
    large = np.minimum(large, nb - 1)
    return (ret + np.where(n < max_exact, n, large)).astype(np.int32)


def _bias_body(code_ref, table_ref, o_ref, *, n_buckets, group):
    code = code_ref[...]
    n_cols = code.shape[1]

    def head(hh, carry):
        out = jnp.full(code.shape, -jnp.inf, F32)
        for b in range(n_buckets):
            out = jnp.where(code == b, table_ref[hh, b] * LOG2E, out)
        if group is None:
            o_ref[hh] = out
        else:
            cols = pl.ds(pl.multiple_of((hh % group) * n_cols, LANES), n_cols)
            o_ref[hh // group, :, cols] = out
        return carry

    lax.fori_loop(0, table_ref.shape[0], head, 0)


def _bias_call(code, table, group=None):
    n_heads, n_buckets = table.shape
    rows, cols = code.shape
    shape = (n_heads, rows, cols) if group is None else (n_heads // group, rows, group * cols)
    return pl.pallas_call(
        functools.partial(_bias_body, n_buckets=n_buckets, group=group),
        out_shape=jax.ShapeDtypeStruct(shape, F32),
        in_specs=[pl.BlockSpec(memory_space=pltpu.VMEM), pl.BlockSpec(memory_space=pltpu.SMEM)],
        out_specs=pl.BlockSpec(memory_space=pltpu.VMEM),
        name="rel_bias",
    )(jnp.asarray(code), table)


def _unit_scores(qs, kh, bias):
    return lax.dot_general(qs, kh, (((1,), (1,)), ((), ())), preferred_element_type=F32) + bias


def _unit_output(s, vh, sink_col):
    sink_col = sink_col * LOG2E
    m = jnp.maximum(jnp.max(s, axis=-1, keepdims=True), sink_col)
    e = jnp.exp2(s - m)
    l = jnp.sum(e, axis=-1, keepdims=True) + jnp.exp2(sink_col - m)
    return jnp.dot(e.astype(BF16), vh, preferred_element_type=F32) * (1.0 / l)


def _stack_heads(q, h, group):
    hd = HEAD_DIM
    return jnp.concatenate(
        [q[:, (h * group + g) * hd:(h * group + g + 1) * hd] for g in range(group)], axis=0)


def _unstack_heads(o, group):
    tq = o.shape[0] // group
    return [o[g * tq:(g + 1) * tq, :] for g in range(group)]


ATTN_Q_COLS = 512
PAIR = 2 * CHUNK


def _pair_codes(n_buckets):
    code = _bucket_codes(PAIR, WINDOW + PAIR, -WINDOW, n_buckets).T
    key_chunk = np.arange(WINDOW + PAIR)[:, None] // CHUNK
    q_chunk = np.arange(PAIR)[None, :] // CHUNK
    visible = (key_chunk >= q_chunk) & (key_chunk <= q_chunk + WINDOW // CHUNK)
    return np.where(visible, code, n_buckets).astype(np.int32)


def _attn_prompt_body(qT_ref, kp_ref, kc_ref, vTp_ref, vTc_ref, bias_ref, sink_ref, oT_ref,
                      k_scr, vT_scr, *, kvh, group):
    i = pl.program_id(1)
    tq = qT_ref.shape[1]
    nk = WINDOW + PAIR
    hd = HEAD_DIM
    for h in range(kvh):
        k_scr[h, 0:WINDOW, :] = kp_ref[:, h * hd:(h + 1) * hd].astype(BF16)
        k_scr[h, WINDOW:WINDOW + tq, :] = kc_ref[:, h * hd:(h + 1) * hd].astype(BF16)
    vT_scr[:, 0:WINDOW] = vTp_ref[...].astype(BF16)
    vT_scr[:, WINDOW:WINDOW + tq] = vTc_ref[...].astype(BF16)
    row = lax.broadcasted_iota(jnp.int32, (nk, group * PAIR), 0)
    start_mask = jnp.where(row < jnp.where(i == 0, WINDOW, 0), -jnp.inf, 0.0).astype(F32)
    units = [(p * PAIR, h) for p in range(tq // PAIR) for h in range(kvh)]

    def scores(c0, h):
        qsT = jnp.concatenate(
            [qT_ref[hh * hd:(hh + 1) * hd, c0:c0 + PAIR] for hh in range(h * group, (h + 1) * group)],
            axis=1)
        s = jnp.dot(k_scr[h, c0:c0 + nk, :], qsT, preferred_element_type=F32) + bias_ref[h]
        return s + start_mask if c0 == 0 else s

    s_next = scores(*units[0])
    for n, (c0, h) in enumerate(units):
        s = s_next
        if n + 1 < len(units):
            s_next = scores(*units[n + 1])
        sink = sink_ref[h] * LOG2E
        m = jnp.maximum(jnp.max(s, axis=0, keepdims=True), sink)
        e = jnp.exp2(s - m)
        l = jnp.sum(e, axis=0, keepdims=True) + jnp.exp2(sink - m)
        oT = jnp.dot(vT_scr[h * hd:(h + 1) * hd, c0:c0 + nk], e.astype(BF16),
                     preferred_element_type=F32) * (1.0 / l)
        for g in range(group):
            hh = h * group + g
            oT_ref[hh * hd:(hh + 1) * hd, c0:c0 + PAIR] = (
                oT[:, g * PAIR:(g + 1) * PAIR].astype(oT_ref.dtype))


def _attn_prompt_call(qT, k, vT, bias, sink_row, kvh, group):
    b, dq, s = qT.shape
    dk = k.shape[2]
    tq = _tile(s, ATTN_Q_COLS, PAIR)
    per = tq // WINDOW
    prev = lambda i: jnp.maximum(i * per - 1, 0)
    blocks = [((dq, tq), BF16)] * 2 + [((WINDOW + tq, dk), F32)] * 2 + [
        (bias.shape, F32), (sink_row.shape[:1] + (SUBLANES, sink_row.shape[2]), F32)]
    scratch = [((kvh, WINDOW + tq, LANES), BF16), ((dk, WINDOW + tq), BF16)]
    return pl.pallas_call(
        functools.partial(_attn_prompt_body, kvh=kvh, group=group),
        out_shape=jax.ShapeDtypeStruct((b, dq, s), BF16),
        grid=(b, s // tq),
        in_specs=[
            pl.BlockSpec((None, dq, tq), lambda bi, i: (bi, 0, i)),
            pl.BlockSpec((None, WINDOW, dk), lambda bi, i: (bi, prev(i), 0)),
            pl.BlockSpec((None, tq, dk), lambda bi, i: (bi, i, 0)),
            pl.BlockSpec((None, dk, WINDOW), lambda bi, i: (bi, 0, prev(i))),
            pl.BlockSpec((None, dk, tq), lambda bi, i: (bi, 0, i)),
            pl.BlockSpec(bias.shape, lambda bi, i: (0, 0, 0)),
            pl.BlockSpec(sink_row.shape, lambda bi, i: (0, 0, 0)),
        ],
        out_specs=pl.BlockSpec((None, dq, tq), lambda bi, i: (bi, 0, i)),
        scratch_shapes=[pltpu.VMEM((kvh, WINDOW + tq, HEAD_DIM), BF16),
                        pltpu.VMEM((dk, WINDOW + tq), BF16)],
        compiler_params=pltpu.CompilerParams(
            dimension_semantics=("parallel", "arbitrary"),
            vmem_limit_bytes=_vmem_limit(blocks, scratch)),
        name="swa_sink_attention_prompt",
    )(qT, k, k, vT, vT, bias, sink_row)


ATTN_SAMPLE_BATCH = 4


def _attn_sample_body(q_ref, k_ref, v_ref, bias_ref, sink_ref, o_ref, *, kvh, group):
    hd = HEAD_DIM
    units = [(b, h) for b in range(q_ref.shape[0]) for h in range(kvh)]

    def scores(b, h):
        qs = _stack_heads(q_ref[b].astype(F32), h, group).astype(BF16)
        return _unit_scores(qs, k_ref[b, :, h * hd:(h + 1) * hd].astype(BF16), bias_ref[h])

    s_next = scores(*units[0])
    outs = []
    for n, (b, h) in enumerate(units):
        s = s_next
        if n + 1 < len(units):
            s_next = scores(*units[n + 1])
        o = _unit_output(s, v_ref[b, :, h * hd:(h + 1) * hd].astype(BF16), sink_ref[h])
        outs += _unstack_heads(o, group)
        if h == kvh - 1:
            o_ref[b] = jnp.concatenate(outs, axis=1).astype(o_ref.dtype)
            outs = []


def _attn_sample_call(q, k_all, v_all, bias, sink_col, kvh, group):
    b, t, dq = q.shape
    nk, dk = k_all.shape[1:]
    bb = math.gcd(b, ATTN_SAMPLE_BATCH)
    blocks = [((bb, t, dq), BF16)] * 2 + [((bb, nk, dk), F32)] * 2 + [
        (bias.shape[:2] + (2 * LANES,), F32), (sink_col.shape[:2] + (LANES,), F32)]
    return pl.pallas_call(
        functools.partial(_attn_sample_body, kvh=kvh, group=group),
        out_shape=jax.ShapeDtypeStruct((b, t, dq), BF16),
        grid=(b // bb,),
        in_specs=[
            pl.BlockSpec((bb, t, dq), lambda bi: (bi, 0, 0)),
            pl.BlockSpec((bb, nk, dk), lambda bi: (bi, 0, 0)),
            pl.BlockSpec((bb, nk, dk), lambda bi: (bi, 0, 0)),
            pl.BlockSpec(bias.shape, lambda bi: (0, 0, 0)),
            pl.BlockSpec(sink_col.shape, lambda bi: (0, 0, 0)),
        ],
        out_specs=pl.BlockSpec((bb, t, dq), lambda bi: (bi, 0, 0)),
        compiler_params=pltpu.CompilerParams(
            dimension_semantics=("parallel",), vmem_limit_bytes=_vmem_limit(blocks)),
        name="swa_sink_attention_sample",
    )(q, k_all, v_all, bias, sink_col)


def _trunks(xp3, xs3, c_all, state_conv, win_k, win_v, p):
    bp, tp, d = xp3.shape
    bs, ts, _ = xs3.shape
    mp, ms = bp * tp, bs * ts
    depth = p["w_mod"].shape[0]
    d_ff = p["w_gu"].shape[2] // 2
    n_heads = p["attn_sinks"].shape[1]
    dq = n_heads * HEAD_DIM
    kvh = (p["w_qkv"].shape[2] - dq) // (2 * HEAD_DIM)
    group = n_heads // kvh
    dkv = kvh * HEAD_DIM
    kw = p["w_dw"].shape[1]
    n_buckets = p["rel_bias_table"].shape[1]
    conv_p, k_p, v_p, conv_s, k_s, v_s = [], [], [], [], [], []
    mods_p, mods_s = [], []

    def add_mods(w, layer):
        mod = _mod_call(c_all, w, layer, p["b_mod"])
        mods_p.append(_Mod(mod[:bp].reshape(bp, 1, 6 * d), tp))
        mods_s.append(_Mod(mod[bp:bp + bs].reshape(bs, 1, 6 * d), ts))

    add_mods(p["w_mod"], 0)

    xp, xs = xp3.reshape(mp, d), xs3.reshape(ms, d)
    hp = _norm_mod_call(xp, p["norm_mix"][0], mods_p[0], 0, 1, 1024)
    hs = _norm_mod_call(xs, p["norm_mix"][0], mods_s[0], 0, 1, ms)
    for i in range(depth):
        j = i // 2
        ffn_norm = lambda mods: (p["norm_ffn"][i], mods[i], 3, 4)
        next_mod = [(p["w_mod"], i + 1)] if i + 1 < depth else []
        if i % 2 == 0:
            u_p, u_s, wq, *wq_mod = _mm_call(
                hp, p["w_pw1"], j, (0, d), d, x2=hs, cast=[(p["w_pw2"], j)] + next_mod,
                bias=p["b_pw1"], act="glu", tm=1024, tn=512, name="pw1_glu")
            u_p, u_s = u_p.reshape(bp, tp, d), u_s.reshape(bs, ts, d)
            conv_w = (p["w_dw"], p["b_dw"], p["conv_ln_g"], p["conv_ln_b"], j)
            z_p = _conv_prompt_call(u_p, *conv_w, 128)
            conv_p.append(u_p[:, tp - (kw - 1):])
            pad = (-(kw - 1 + ts)) % SUBLANES
            up = jnp.concatenate([jnp.zeros((bs, pad, d), F32), state_conv[j], u_s], axis=1)
            z_s = _conv_sample_call(up, pad, ts, *conv_w)
            conv_s.append(up[:, pad + ts:])
            xp, hp = _rowmm_call(z_p.reshape(mp, d), wq, xp, mods_p[i], 2, ffn_norm(mods_p),
                                 bias=p["b_pw2"][j], tm=512, name="pw2_residual_norm")
            xs, hs = _rowmm_call(z_s.reshape(ms, d), wq, xs, mods_s[i], 2, ffn_norm(mods_s),
                                 bias=p["b_pw2"][j], tm=512, name="pw2_residual_norm")
        else:
            sinks = p["attn_sinks"][j].reshape(kvh, group, 1)
            qT, q_s, wq, *wq_mod = _mm_call(
                hp, p["w_qkv"], j, (0,), dq, x2=hs, cast=[(p["w_o"], j)] + next_mod,
                bias=p["b_qkv"], scale=LOG2E * HEAD_DIM ** -0.5, out_dtype=BF16, tm=1024,
                tn=1024, name="q_proj", out_mode="t", seq=tp)
            (k, vT), kv_s = _mm_call(hp, p["w_qkv"], j, (dq,), 2 * dkv, x2=hs, bias=p["b_qkv"],
                                     tm=1024, tn=2 * dkv, name="kv_proj", out_mode="split_t",
                                     seq=tp)
            k = k.reshape(bp, tp, dkv)
            bias = _bias_call(_pair_codes(n_buckets), p["rel_bias_table"], group)
            sink_row = jnp.repeat(sinks, PAIR, axis=2).reshape(kvh, 1, group * PAIR)
            oT = _attn_prompt_call(qT, k, vT, bias, sink_row, kvh, group)
            k_p.append(k[:, tp - WINDOW:].reshape(bp, WINDOW, kvh, HEAD_DIM))
            v_p.append(jnp.swapaxes(vT[:, :, tp - WINDOW:], 1, 2)
                       .reshape(bp, WINDOW, kvh, HEAD_DIM))
            kv3 = kv_s.reshape(bs, ts, 2, kvh, HEAD_DIM)
            k_all = jnp.concatenate([win_k[j], kv3[:, :, 0]], axis=1)
            v_all = jnp.concatenate([win_v[j], kv3[:, :, 1]], axis=1)
            n_keys = WINDOW + ts
            bias = _bias_call(_bucket_codes(ts, n_keys, -WINDOW, n_buckets),
                              p["rel_bias_table"]).reshape(kvh, group * ts, n_keys)
            o_s = _attn_sample_call(q_s.reshape(bs, ts, dq), k_all.reshape(bs, n_keys, dkv),
                                    v_all.reshape(bs, n_keys, dkv), bias,
                                    jnp.repeat(sinks, ts, axis=1), kvh, group)
            k_s.append(k_all[:, ts:])
            v_s.append(v_all[:, ts:])
            xp, hp = _rowmm_call(oT, wq, xp, mods_p[i], 2, ffn_norm(mods_p), bias=p["b_o"][j],
                                 x_t=True, tm=512, name="wo_residual_norm")
            xs, hs = _rowmm_call(o_s.reshape(ms, dq), wq, xs, mods_s[i], 2, ffn_norm(mods_s),
                                 bias=p["b_o"][j], tm=512, name="wo_residual_norm")
        if wq_mod:
            add_mods(wq_mod[0], i + 1)
        a_p, a_s, wq = _mm_call(hp, p["w_gu"], i, (0, d_ff), d_ff, x2=hs, cast=[(p["w_down"], i)],
                                act="swiglu", out_dtype=BF16, tm=1024, tn=512, name="ffn_gate_up")
        if i + 1 < depth:
            nxt = lambda mods: (p["norm_mix"][i + 1], mods[i + 1], 0, 1)
            xp, hp = _rowmm_call(a_p, wq, xp, mods_p[i], 5, nxt(mods_p), tm=256,
                                 name="ffn_down_residual_norm")
            xs, hs = _rowmm_call(a_s, wq, xs, mods_s[i], 5, nxt(mods_s), tm=256,
                                 name="ffn_down_residual_norm")
        else:
            y_p = _rowmm_call(a_p, wq, xp, mods_p[i], 5, (p["norm_out"],), tm=256,
                              name="ffn_down_residual_final_norm")
            y_s = _rowmm_call(a_s, wq, xs, mods_s[i], 5, (p["norm_out"],), tm=256,
                              name="ffn_down_residual_final_norm")
    stack = jnp.stack
    return (y_p.reshape(bp, tp, d), y_s.reshape(bs, ts, d), stack(conv_p), stack(k_p), stack(v_p),
            stack(conv_s), stack(k_s), stack(v_s))


def kernel(x_prompt, x_sample, c_prompt, c_sample, state_conv, cache_win_k, cache_win_v, w_mod, b_mod, norm_mix, norm_ffn, w_pw1, b_pw1, w_dw, b_dw, conv_ln_g, conv_ln_b, w_pw2, b_pw2, w_qkv, b_qkv, w_o, b_o, attn_sinks, rel_bias_table, w_gu, w_down, norm_out):
    p = dict(w_mod=w_mod, b_mod=b_mod, norm_mix=norm_mix, norm_ffn=norm_ffn, w_pw1=w_pw1,
             b_pw1=b_pw1, w_dw=w_dw, b_dw=b_dw, conv_ln_g=conv_ln_g, conv_ln_b=conv_ln_b,
             w_pw2=w_pw2, b_pw2=b_pw2, w_qkv=w_qkv, b_qkv=b_qkv, w_o=w_o, b_o=b_o,
             attn_sinks=attn_sinks, rel_bias_table=rel_bias_table, w_gu=w_gu, w_down=w_down,
             norm_out=norm_out)
    bp, sp, d = x_prompt.shape
    bs = x_sample.shape[0]
    n_c = bp + bs
    c_all = jnp.concatenate(
        [c_prompt, c_sample, jnp.zeros((-n_c % (2 * SUBLANES), d), F32)], axis=0)
    return _trunks(x_prompt, x_sample, c_all, state_conv, cache_win_k, cache_win_v, p)
```

```python
import functools
import math

import numpy as np
import jax
import jax.numpy as jnp
from jax import lax
from jax.experimental import pallas as pl
from jax.experimental.pallas import tpu as pltpu

F32 = jnp.float32
BF16 = jnp.bfloat16

CHUNK = 64
WINDOW = 128
HEAD_DIM = 64
MAX_DISTANCE = 128
EPS = 1e-6
LOG2E = math.log2(math.e)

V7X_VMEM_BYTES = 64 * 2**20
V7X_VMEM_CAP = V7X_VMEM_BYTES - 8 * 2**20
SUBLANES = 8
LANES = 128
BF16_SUBLANES = 2 * SUBLANES

PANEL_ROWS = 1024
PANEL_COLS = 512
Q_PANEL_COLS = 1024
ROWMM_ROWS = 512
CONV_ROWS = 128
MOD_COLS = 1024


def _nbytes(shape, dtype):
    return math.prod(shape) * jnp.dtype(dtype).itemsize


def _vmem_limit(blocks, scratch=()):
    est = 2 * sum(_nbytes(s, d) for s, d in blocks) + sum(_nbytes(s, d) for s, d in scratch)
    return int(min(V7X_VMEM_CAP, max(32 * 2**20, 2 * est)))


def _tile(dim, pref, mult=LANES):
    if dim <= pref:
        return dim
    t = (pref // mult) * mult
    while t >= mult:
        if dim % t == 0:
            return t
        t -= mult
    return dim


def _sigmoid(x):
    return 1.0 / (1.0 + jnp.exp(-x))


def _mod_body(c_ref, w_ref, b_ref, o_ref):
    c = c_ref[...]
    a = (c * _sigmoid(c)).astype(BF16)
    o_ref[...] = jnp.dot(a, w_ref[...].astype(BF16), preferred_element_type=F32) + b_ref[...]


def _mod_call(c_all, w_mod, b_mod):
    depth, d, n = w_mod.shape
    mp = c_all.shape[0]
    tn = _tile(n, MOD_COLS)
    blocks = [((mp, d), F32), ((d, tn), F32), ((1, tn), F32), ((mp, tn), F32)]
    return pl.pallas_call(
        _mod_body,
        out_shape=jax.ShapeDtypeStruct((depth, mp, n), F32),
        grid=(depth, n // tn),
        in_specs=[
            pl.BlockSpec((mp, d), lambda l, j: (0, 0)),
            pl.BlockSpec((None, d, tn), lambda l, j: (l, 0, j)),
            pl.BlockSpec((None, 1, tn), lambda l, j: (l, 0, j)),
        ],
        out_specs=pl.BlockSpec((None, mp, tn), lambda l, j: (l, 0, j)),
        compiler_params=pltpu.CompilerParams(
            dimension_semantics=("arbitrary", "arbitrary"),
            vmem_limit_bytes=_vmem_limit(blocks, [((d, tn), BF16)])),
        name="adaln_mod",
    )(c_all, w_mod, b_mod.reshape(depth, 1, n))


class _Mod:
    def __init__(self, arr, rows_per_group):
        self.arr = arr
        self.rows_per_group = rows_per_group

    def tile_rows(self, total_rows, pref, mult):
        if self.rows_per_group >= pref:
            return _tile(self.rows_per_group, pref, mult)
        return _tile(total_rows, pref, max(mult, self.rows_per_group))

    def spec(self, seg, d, tm):
        if tm <= self.rows_per_group:
            per = self.rows_per_group // tm
            return pl.BlockSpec((1, 1, d), lambda i: (i // per, 0, seg))
        return pl.BlockSpec((tm // self.rows_per_group, 1, d), lambda i: (i, 0, seg))


def _grouped(v, m):
    return v.reshape(m.shape[0], v.shape[0] // m.shape[0], v.shape[1])


def _norm_mod_body(x_ref, g_ref, sh_ref, sc_ref, o_ref):
    x = x_ref[...]
    y = x * lax.rsqrt(jnp.mean(x * x, axis=-1, keepdims=True) + EPS) * g_ref[...]
    sc = sc_ref[...]
    o_ref[...] = (_grouped(y, sc) * (1.0 + sc) + sh_ref[...]).reshape(x.shape).astype(o_ref.dtype)


def _norm_mod_call(x, g, mod, seg_shift, seg_scale, tm):
    m, d = x.shape
    tm = mod.tile_rows(m, tm, SUBLANES)
    blocks = [((tm, d), F32), ((1, d), F32), ((tm, d), BF16)]
    return pl.pallas_call(
        _norm_mod_body,
        out_shape=jax.ShapeDtypeStruct((m, d), BF16),
        grid=(m // tm,),
        in_specs=[
            pl.BlockSpec((tm, d), lambda i: (i, 0)),
            pl.BlockSpec((1, d), lambda i: (0, 0)),
            mod.spec(seg_shift, d, tm),
            mod.spec(seg_scale, d, tm),
        ],
        out_specs=pl.BlockSpec((tm, d), lambda i: (i, 0)),
        compiler_params=pltpu.CompilerParams(
            dimension_semantics=("parallel",), vmem_limit_bytes=_vmem_limit(blocks)),
        name="rmsnorm_modulate",
    )(x, g.reshape(1, d), mod.arr, mod.arr)


def _mm_body(x_ref, *refs, n_w, has_bias, act, scale, out_mode, has_x2, has_cast):
    if has_x2:
        x2_ref, refs = refs[0], refs[1:]
    ws = refs[:n_w]
    refs = refs[n_w:]
    bs = refs[:n_w] if has_bias else ()
    refs = refs[len(bs):]
    if has_cast:
        wc_ref, refs = refs[0], refs[1:]
    n_o = 2 if out_mode == "split_t" else 1
    o_refs = refs[:n_o]
    refs = refs[n_o:]
    if has_x2:
        o2_ref, refs = refs[0], refs[1:]
    if has_cast:
        oc_ref, refs = refs[0], refs[1:]
        oc_ref[...] = wc_ref[...].astype(BF16)
    wbs = refs

    def compute(x):
        ps = [None] * n_w
        for i in ((1, 0) if act == "glu" else range(n_w)):
            p = jnp.dot(x, wbs[i][...], preferred_element_type=F32)
            if has_bias:
                p = p + bs[i][...]
            ps[i] = p
        if act == "glu":
            y = ps[0] * _sigmoid(ps[1])
        elif act == "swiglu":
            y = ps[0] * _sigmoid(ps[0]) * ps[1]
        else:
            y = ps[0]
        return y * scale if scale != 1.0 else y

    @pl.when(pl.program_id(1) == 0)
    def _():
        for w, wb in zip(ws, wbs):
            wb[...] = w[...].astype(BF16)
        if has_x2:
            o2_ref[...] = compute(x2_ref[...]).astype(o2_ref.dtype)

    y = compute(x_ref[...])
    if out_mode == "plain":
        o_refs[0][...] = y.astype(o_refs[0].dtype)
    elif out_mode == "t":
        o_refs[0][...] = y.T.astype(o_refs[0].dtype)
    else:
        half = y.shape[1] // 2
        o_refs[0][...] = y[:, :half].astype(o_refs[0].dtype)
        o_refs[1][...] = y[:, half:].T.astype(o_refs[1].dtype)


def _mm_call(x, w, layer, col_starts, n_out, *, x2=None, cast=None, bias=None, act=None,
             scale=1.0, out_dtype=F32, tm=PANEL_ROWS, tn=PANEL_COLS, name="matmul",
             out_mode="plain", seq=None):
    m, k = x.shape
    if out_mode == "plain":
        tm = _tile(m, tm, SUBLANES)
    else:
        tm = _tile(seq, tm, LANES)
    tn = _tile(n_out, tn)
    n_w = len(col_starts)
    assert all(c % tn == 0 for c in col_starts) and m % tm == 0 and n_out % tn == 0
    has_bias = bias is not None
    per_b = None if seq is None else seq // tm

    in_specs = [pl.BlockSpec((tm, k), lambda j, i: (i, 0))]
    args = [x]
    blocks = [((tm, k), x.dtype), ((tm, tn), out_dtype)]
    if x2 is not None:
        m2 = x2.shape[0]
        in_specs.append(pl.BlockSpec((m2, k), lambda j, i: (0, 0)))
        args.append(x2)
        blocks += [((m2, k), x2.dtype), ((m2, tn), out_dtype)]
    for c in col_starts:
        off = c // tn
        in_specs.append(pl.BlockSpec((None, k, tn), lambda j, i, off=off: (layer, 0, off + j)))
        args.append(w)
        blocks.append(((k, tn), F32))
    if has_bias:
        b3 = bias.reshape(bias.shape[0], 1, bias.shape[1])
        for c in col_starts:
            off = c // tn
            in_specs.append(pl.BlockSpec((None, 1, tn), lambda j, i, off=off: (layer, 0, off + j)))
            args.append(b3)
    n_i = m // tm
    if cast is not None:
        w_other, layer_other = cast
        kc, dc = w_other.shape[1:]
        slab = kc // ((n_out // tn) * n_i)
        assert slab * (n_out // tn) * n_i == kc and slab % BF16_SUBLANES == 0
        in_specs.append(pl.BlockSpec((None, slab, dc), lambda j, i: (layer_other, j * n_i + i, 0)))
        args.append(w_other)
        blocks += [((slab, dc), F32), ((slab, dc), BF16)]
    scratch = [((k, tn), BF16)] * n_w
    body = functools.partial(_mm_body, n_w=n_w, has_bias=has_bias, act=act, scale=scale,
                             out_mode=out_mode, has_x2=x2 is not None, has_cast=cast is not None)
    if out_mode == "plain":
        out_shape = [jax.ShapeDtypeStruct((m, n_out), out_dtype)]
        out_specs = [pl.BlockSpec((tm, tn), lambda j, i: (i, j))]
    elif out_mode == "t":
        out_shape = [jax.ShapeDtypeStruct((m // seq, n_out, seq), out_dtype)]
        out_specs = [pl.BlockSpec((None, tn, tm), lambda j, i: (i // per_b, j, i % per_b))]
    else:
        assert tn == n_out
        half = n_out // 2
        out_shape = [jax.ShapeDtypeStruct((m, half), out_dtype),
                     jax.ShapeDtypeStruct((m // seq, half, seq), out_dtype)]
        out_specs = [pl.BlockSpec((tm, half), lambda j, i: (i, 0)),
                     pl.BlockSpec((None, half, tm), lambda j, i: (i // per_b, 0, i % per_b))]
    if x2 is not None:
        out_shape.append(jax.ShapeDtypeStruct((m2, n_out), out_dtype))
        out_specs.append(pl.BlockSpec((m2, tn), lambda j, i: (0, j)))
    if cast is not None:
        out_shape.append(jax.ShapeDtypeStruct((kc, dc), BF16))
        out_specs.append(pl.BlockSpec((slab, dc), lambda j, i: (j * n_i + i, 0)))
    outs = pl.pallas_call(
        body,
        out_shape=out_shape,
        grid=(n_out // tn, n_i),
        in_specs=in_specs,
        out_specs=out_specs,
        scratch_shapes=[pltpu.VMEM(s, d) for s, d in scratch],
        compiler_params=pltpu.CompilerParams(
            dimension_semantics=("arbitrary", "arbitrary"),
            vmem_limit_bytes=_vmem_limit(blocks, scratch)),
        name=name,
    )(*args)
    n_primary = 2 if out_mode == "split_t" else 1
    primary = outs[0] if n_primary == 1 else outs[:n_primary]
    extras = list(outs[n_primary:])
    return (primary, *extras) if extras else primary


ROWMM_SUB_ROWS = 256


def _rowmm_body(*refs, has_bias, x_t, final):
    x_ref, w_hbm = refs[:2]
    refs = refs[2:]
    if has_bias:
        b_ref, refs = refs[0], refs[1:]
    res_ref, gate_ref, ng_ref = refs[:3]
    refs = refs[3:]
    if not final:
        sh_ref, sc_ref = refs[:2]
        refs = refs[2:]
    n_o = 1 if final else 2
    o_refs = refs[:n_o]
    refs = refs[n_o:]
    w_res, sem = refs

    @pl.when(pl.program_id(0) == 0)
    def _():
        whole = pltpu.make_async_copy(w_hbm, w_res, sem)
        whole.start()
        whole.wait()

    tm = res_ref.shape[0]
    sub = min(tm, ROWMM_SUB_ROWS)

    def groups_of(ref, r0):
        per = tm // ref.shape[0]
        return ref[...] if per >= tm else ref[r0 // per:(r0 + sub) // per]

    for r0 in range(0, tm, sub):
        x = x_ref[:, r0:r0 + sub].T if x_t else x_ref[r0:r0 + sub, :]
        y = jnp.dot(x, w_res[...], preferred_element_type=F32)
        if has_bias:
            y = y + b_ref[...]
        gate = groups_of(gate_ref, r0)
        xn = (_grouped(res_ref[r0:r0 + sub, :], gate) + gate * _grouped(y, gate)).reshape(y.shape)
        r = xn * lax.rsqrt(jnp.mean(xn * xn, axis=-1, keepdims=True) + EPS) * ng_ref[...]
        if final:
            o_refs[0][r0:r0 + sub, :] = r
        else:
            sc = groups_of(sc_ref, r0)
            o_refs[0][r0:r0 + sub, :] = xn
            o_refs[1][r0:r0 + sub, :] = (
                _grouped(r, sc) * (1.0 + sc) + groups_of(sh_ref, r0)
            ).reshape(y.shape).astype(o_refs[1].dtype)


def _rowmm_call(x, w, res, gate, gate_seg, nxt, *, bias=None, x_t=False, tm=ROWMM_ROWS,
                name="rowmm"):
    if x_t:
        nb, k, seq = x.shape
        m = nb * seq
    else:
        m, k = x.shape
    d = w.shape[-1]
    final = len(nxt) == 1
    scratch = [((k, d), BF16)]
    n_mods = 1 if final else 3

    def vmem_need(rows):
        blocks = [((rows, k), BF16), ((rows, d), F32), ((rows, d), F32 if final else BF16)]
        blocks += [((rows, d), F32)] * (0 if final else 1)
        blocks += [((-(-rows // gate.rows_per_group) * SUBLANES, d), F32)] * n_mods
        temporaries = 4 * _nbytes((rows, d), F32)
        return (2 * sum(_nbytes(s, t) for s, t in blocks) + sum(_nbytes(s, t) for s, t in scratch)
                + temporaries)

    mult = LANES if x_t else SUBLANES
    tm = gate.tile_rows(m, tm, mult)
    while vmem_need(tm) > V7X_VMEM_CAP and tm % (2 * max(mult, min(tm, gate.rows_per_group))) == 0:
        tm //= 2
    assert final or nxt[1].rows_per_group == gate.rows_per_group
    has_bias = bias is not None
    row = lambda i: (i, 0)
    if x_t:
        per_b = seq // tm
        in_specs = [pl.BlockSpec((None, k, tm), lambda i: (i // per_b, 0, i % per_b))]
    else:
        in_specs = [pl.BlockSpec((tm, k), row)]
    in_specs.append(pl.BlockSpec(memory_space=pltpu.HBM))
    args = [x, w]
    if has_bias:
        in_specs.append(pl.BlockSpec((1, d), lambda i: (0, 0)))
        args.append(bias.reshape(1, d))
    in_specs += [pl.BlockSpec((tm, d), row), gate.spec(gate_seg, d, tm),
                 pl.BlockSpec((1, d), lambda i: (0, 0))]
    args += [res, gate.arr, nxt[0].reshape(1, d)]
    if final:
        out_shape = jax.ShapeDtypeStruct((m, d), F32)
        out_specs = pl.BlockSpec((tm, d), row)
    else:
        _, nmod, seg_shift, seg_scale = nxt
        in_specs += [nmod.spec(seg_shift, d, tm), nmod.spec(seg_scale, d, tm)]
        args += [nmod.arr, nmod.arr]
        out_shape = [jax.ShapeDtypeStruct((m, d), F32), jax.ShapeDtypeStruct((m, d), BF16)]
        out_specs = [pl.BlockSpec((tm, d), row), pl.BlockSpec((tm, d), row)]
    return pl.pallas_call(
        functools.partial(_rowmm_body, has_bias=has_bias, x_t=x_t, final=final),
        out_shape=out_shape,
        grid=(m // tm,),
        in_specs=in_specs,
        out_specs=out_specs,
        scratch_shapes=[pltpu.VMEM(s, dt) for s, dt in scratch] + [pltpu.SemaphoreType.DMA(())],
        compiler_params=pltpu.CompilerParams(
            dimension_semantics=("arbitrary",),
            vmem_limit_bytes=min(V7X_VMEM_CAP, max(32 * 2**20, vmem_need(tm) + 4 * 2**20))),
        name=name,
    )(*args)


def _conv_ln_swish(up_ref, pad, tt, w_ref, bdw_ref, lg_ref, lb_ref, acc_ref, o_ref):
    kw, d = w_ref.shape
    by_shift = [[(a, SUBLANES * a + s - pad) for a in range((pad + kw - 1) // SUBLANES + 1)
                 if 0 <= SUBLANES * a + s - pad < kw] for s in range(SUBLANES)]

    def strip(c, carry):
        cols = pl.ds(pl.multiple_of(c * LANES, LANES), LANES)
        z = bdw_ref[:, cols]
        for s, taps in enumerate(by_shift):
            n = tt + SUBLANES if s else tt
            q = None
            for a, k in taps:
                term = up_ref[pl.ds(SUBLANES * a, n), cols] * w_ref[pl.ds(k, 1), cols]
                q = term if q is None else q + term
            if q is not None:
                z = z + q[s:s + tt]
        acc_ref[:, cols] = z
        return carry

    lax.fori_loop(0, d // LANES, strip, 0)
    z = acc_ref[...]
    mu = jnp.mean(z, axis=-1, keepdims=True)
    zc = z - mu
    y = zc * lax.rsqrt(jnp.mean(zc * zc, axis=-1, keepdims=True) + EPS)
    y = y * lg_ref[...] + lb_ref[...]
    o_ref[...] = (y * _sigmoid(y)).astype(o_ref.dtype)


def _conv_prompt_body(main_ref, halo_ref, w_ref, bdw_ref, lg_ref, lb_ref, o_ref, up_ref, acc_ref,
                      *, halo, tt):
    kw = w_ref.shape[0]

    @pl.when(pl.program_id(1) == 0)
    def _():
        up_ref[0:halo, :] = jnp.zeros((halo, up_ref.shape[1]), F32)

    @pl.when(pl.program_id(1) > 0)
    def _():
        up_ref[0:halo, :] = halo_ref[...]

    up_ref[halo:halo + tt, :] = main_ref[...]
    _conv_ln_swish(up_ref, halo - (kw - 1), tt, w_ref, bdw_ref, lg_ref, lb_ref, acc_ref, o_ref)


def _conv_prompt_call(u, w_dw, b_dw, ln_g, ln_b, layer, tt=CONV_ROWS):
    b, t, d = u.shape
    kw = w_dw.shape[1]
    halo = -(-(kw - 1) // SUBLANES) * SUBLANES
    tt = _tile(t, tt, halo)
    hb = tt // halo
    vec = lambda: pl.BlockSpec((None, 1, d), lambda bi, ti: (layer, 0, 0))
    blocks = [((tt, d), F32), ((halo, d), F32), ((kw, d), F32), ((tt, d), BF16)]
    scratch = [((halo + tt, d), F32), ((tt, d), F32)]
    return pl.pallas_call(
        functools.partial(_conv_prompt_body, halo=halo, tt=tt),
        out_shape=jax.ShapeDtypeStruct((b, t, d), BF16),
        grid=(b, t // tt),
        in_specs=[
            pl.BlockSpec((None, tt, d), lambda bi, ti: (bi, ti, 0)),
            pl.BlockSpec((None, halo, d), lambda bi, ti: (bi, jnp.maximum(ti * hb - 1, 0), 0)),
            pl.BlockSpec((None, kw, d), lambda bi, ti: (layer, 0, 0)),
            vec(), vec(), vec(),
        ],
        out_specs=pl.BlockSpec((None, tt, d), lambda bi, ti: (bi, ti, 0)),
        scratch_shapes=[pltpu.VMEM(s, dt) for s, dt in scratch],
        compiler_params=pltpu.CompilerParams(
            dimension_semantics=("parallel", "arbitrary"),
            vmem_limit_bytes=_vmem_limit(blocks, scratch)),
        name="dwconv_ln_swish_prompt",
    )(u, u, w_dw, b_dw.reshape(-1, 1, d), ln_g.reshape(-1, 1, d), ln_b.reshape(-1, 1, d))


def _conv_sample_body(up_ref, w_ref, bdw_ref, lg_ref, lb_ref, o_ref, acc_ref, *, pad, tt):
    _conv_ln_swish(up_ref, pad, tt, w_ref, bdw_ref, lg_ref, lb_ref, acc_ref, o_ref)


def _conv_sample_call(up, pad, tt, w_dw, b_dw, ln_g, ln_b, layer):
    b, rows, d = up.shape
    kw = w_dw.shape[1]
    vec = lambda: pl.BlockSpec((None, 1, d), lambda bi: (layer, 0, 0))
    blocks = [((rows, d), F32), ((kw, d), F32), ((tt, d), BF16)]
    scratch = [((tt, d), F32)]
    return pl.pallas_call(
        functools.partial(_conv_sample_body, pad=pad, tt=tt),
        out_shape=jax.ShapeDtypeStruct((b, tt, d), BF16),
        grid=(b,),
        in_specs=[
            pl.BlockSpec((None, rows, d), lambda bi: (bi, 0, 0)),
            pl.BlockSpec((None, kw, d), lambda bi: (layer, 0, 0)),
            vec(), vec(), vec(),
        ],
        out_specs=pl.BlockSpec((None, tt, d), lambda bi: (bi, 0, 0)),
        scratch_shapes=[pltpu.VMEM(s, dt) for s, dt in scratch],
        compiler_params=pltpu.CompilerParams(
            dimension_semantics=("parallel",), vmem_limit_bytes=_vmem_limit(blocks, scratch)),
        name="dwconv_ln_swish_sample",
    )(up, w_dw, b_dw.reshape(-1, 1, d), ln_g.reshape(-1, 1, d), ln_b.reshape(-1, 1, d))


def _bucket_codes(n_q, n_k, k_off, n_buckets):
    rel = (np.arange(n_k) + k_off)[None, :] - np.arange(n_q)[:, None]
    nb = n_buckets // 2
    max_exact = nb // 2
    ret = np.where(rel > 0, nb, 0)
    n = np.abs(rel)
    nf = np.maximum(n, 1).astype(np.float32)
    large = max_exact + (np.log(nf / np.float32(max_exact))
                         / np.float32(math.log(MAX_DISTANCE / max_exact))
                         * np.float32(nb - max_exact)).astype(np.int32)
    large = np.minimum(large, nb - 1)
    return (ret + np.where(n < max_exact, n, large)).astype(np.int32)


def _bias_body(code_ref, table_ref, o_ref, *, present, group):
    n_cols = code_ref.shape[1]

    used = sorted({b for _, buckets in present for b in buckets})

    def head(hh, carry):
        value = {b: table_ref[hh, b] * LOG2E for b in used}
        for r0, buckets in present:
            rows = slice(r0, r0 + SUBLANES)
            code = code_ref[rows, :]
            out = jnp.full(code.shape, -jnp.inf, F32)
            for b in buckets:
                out = jnp.where(code == b, value[b], out)
            if group is None:
                o_ref[hh, rows, :] = out
            else:
                cols = pl.ds(pl.multiple_of((hh % group) * n_cols, LANES), n_cols)
                o_ref[hh // group, rows, cols] = out
        return carry

    lax.fori_loop(0, table_ref.shape[0], head, 0)


def _bias_call(code, table, group=None):
    n_heads, n_buckets = table.shape
    rows, cols = code.shape
    shape = (n_heads, rows, cols) if group is None else (n_heads // group, rows, group * cols)
    present = tuple(
        (r0, tuple(int(b) for b in np.unique(code[r0:r0 + SUBLANES]) if b < n_buckets))
        for r0 in range(0, rows, SUBLANES))
    return pl.pallas_call(
        functools.partial(_bias_body, present=present, group=group),
        out_shape=jax.ShapeDtypeStruct(shape, F32),
        in_specs=[pl.BlockSpec(memory_space=pltpu.VMEM), pl.BlockSpec(memory_space=pltpu.SMEM)],
        out_specs=pl.BlockSpec(memory_space=pltpu.VMEM),
        name="rel_bias",
    )(jnp.asarray(code), table)


def _unit_scores(qs, kh, bias):
    return lax.dot_general(qs, kh, (((1,), (1,)), ((), ())), preferred_element_type=F32) + bias


def _unit_output(s, vh, sink_col):
    sink_col = sink_col * LOG2E
    m = jnp.maximum(jnp.max(s, axis=-1, keepdims=True), sink_col)
    e = jnp.exp2(s - m)
    l = jnp.sum(e, axis=-1, keepdims=True) + jnp.exp2(sink_col - m)
    return jnp.dot(e.astype(BF16), vh, preferred_element_type=F32) * (1.0 / l)


def _stack_heads(q, h, group):
    hd = HEAD_DIM
    return jnp.concatenate(
        [q[:, (h * group + g) * hd:(h * group + g + 1) * hd] for g in range(group)], axis=0)


def _unstack_heads(o, group):
    tq = o.shape[0] // group
    return [o[g * tq:(g + 1) * tq, :] for g in range(group)]


ATTN_Q_COLS = 512
PAIR = 2 * CHUNK


def _pair_codes(n_buckets):
    code = _bucket_codes(PAIR, WINDOW + PAIR, -WINDOW, n_buckets).T
    key_chunk = np.arange(WINDOW + PAIR)[:, None] // CHUNK
    q_chunk = np.arange(PAIR)[None, :] // CHUNK
    visible = (key_chunk >= q_chunk) & (key_chunk <= q_chunk + WINDOW // CHUNK)
    return np.where(visible, code, n_buckets).astype(np.int32)


def _attn_prompt_body(qT_ref, kp_ref, kc_ref, vTp_ref, vTc_ref, bias_ref, sink_ref, oT_ref,
                      k_scr, vT_scr, *, kvh, group):
    i = pl.program_id(1)
    tq = qT_ref.shape[1]
    nk = WINDOW + PAIR
    hd = HEAD_DIM
    for h in range(kvh):
        k_scr[h, 0:WINDOW, :] = kp_ref[:, h * hd:(h + 1) * hd].astype(BF16)
        k_scr[h, WINDOW:WINDOW + tq, :] = kc_ref[:, h * hd:(h + 1) * hd].astype(BF16)
    vT_scr[:, 0:WINDOW] = vTp_ref[...].astype(BF16)
    vT_scr[:, WINDOW:WINDOW + tq] = vTc_ref[...].astype(BF16)
    row = lax.broadcasted_iota(jnp.int32, (nk, group * PAIR), 0)
    start_mask = jnp.where(row < jnp.where(i == 0, WINDOW, 0), -jnp.inf, 0.0).astype(F32)
    units = [(p * PAIR, h) for p in range(tq // PAIR) for h in range(kvh)]

    def scores(c0, h):
        qsT = jnp.concatenate(
            [qT_ref[hh * hd:(hh + 1) * hd, c0:c0 + PAIR] for hh in range(h * group, (h + 1) * group)],
            axis=1)
        s = jnp.dot(k_scr[h, c0:c0 + nk, :], qsT, preferred_element_type=F32) + bias_ref[h]
        return s + start_mask if c0 == 0 else s

    s_next = scores(*units[0])
    for n, (c0, h) in enumerate(units):
        s = s_next
        if n + 1 < len(units):
            s_next = scores(*units[n + 1])
        sink = sink_ref[h] * LOG2E
        m = jnp.maximum(jnp.max(s, axis=0, keepdims=True), sink)
        e = jnp.exp2(s - m)
        l = jnp.sum(e, axis=0, keepdims=True) + jnp.exp2(sink - m)
        oT = jnp.dot(vT_scr[h * hd:(h + 1) * hd, c0:c0 + nk], e.astype(BF16),
                     preferred_element_type=F32) * (1.0 / l)
        for g in range(group):
            hh = h * group + g
            oT_ref[hh * hd:(hh + 1) * hd, c0:c0 + PAIR] = (
                oT[:, g * PAIR:(g + 1) * PAIR].astype(oT_ref.dtype))


def _attn_prompt_call(qT, k, vT, bias, sink_row, kvh, group):
    b, dq, s = qT.shape
    dk = k.shape[2]
    tq = _tile(s, ATTN_Q_COLS, PAIR)
    per = tq // WINDOW
    prev = lambda i: jnp.maximum(i * per - 1, 0)
    blocks = [((dq, tq), BF16)] * 2 + [((WINDOW + tq, dk), F32)] * 2 + [
        (bias.shape, F32), (sink_row.shape[:1] + (SUBLANES, sink_row.shape[2]), F32)]
    scratch = [((kvh, WINDOW + tq, LANES), BF16), ((dk, WINDOW + tq), BF16)]
    return pl.pallas_call(
        functools.partial(_attn_prompt_body, kvh=kvh, group=group),
        out_shape=jax.ShapeDtypeStruct((b, dq, s), BF16),
        grid=(b, s // tq),
        in_specs=[
            pl.BlockSpec((None, dq, tq), lambda bi, i: (bi, 0, i)),
            pl.BlockSpec((None, WINDOW, dk), lambda bi, i: (bi, prev(i), 0)),
            pl.BlockSpec((None, tq, dk), lambda bi, i: (bi, i, 0)),
            pl.BlockSpec((None, dk, WINDOW), lambda bi, i: (bi, 0, prev(i))),
            pl.BlockSpec((None, dk, tq), lambda bi, i: (bi, 0, i)),
            pl.BlockSpec(bias.shape, lambda bi, i: (0, 0, 0)),
            pl.BlockSpec(sink_row.shape, lambda bi, i: (0, 0, 0)),
        ],
        out_specs=pl.BlockSpec((None, dq, tq), lambda bi, i: (bi, 0, i)),
        scratch_shapes=[pltpu.VMEM((kvh, WINDOW + tq, HEAD_DIM), BF16),
                        pltpu.VMEM((dk, WINDOW + tq), BF16)],
        compiler_params=pltpu.CompilerParams(
            dimension_semantics=("parallel", "arbitrary"),
            vmem_limit_bytes=_vmem_limit(blocks, scratch)),
        name="swa_sink_attention_prompt",
    )(qT, k, k, vT, vT, bias, sink_row)


ATTN_SAMPLE_BATCH = 4


def _attn_sample_body(q_ref, k_ref, v_ref, bias_ref, sink_ref, o_ref, *, kvh, group):
    hd = HEAD_DIM
    units = [(b, h) for b in range(q_ref.shape[0]) for h in range(kvh)]

    def scores(b, h):
        qs = _stack_heads(q_ref[b].astype(F32), h, group).astype(BF16)
        return _unit_scores(qs, k_ref[b, :, h * hd:(h + 1) * hd].astype(BF16), bias_ref[h])

    s_next = scores(*units[0])
    outs = []
    for n, (b, h) in enumerate(units):
        s = s_next
        if n + 1 < len(units):
            s_next = scores(*units[n + 1])
        o = _unit_output(s, v_ref[b, :, h * hd:(h + 1) * hd].astype(BF16), sink_ref[h])
        outs += _unstack_heads(o, group)
        if h == kvh - 1:
            o_ref[b] = jnp.concatenate(outs, axis=1).astype(o_ref.dtype)
            outs = []


def _attn_sample_call(q, k_all, v_all, bias, sink_col, kvh, group):
    b, t, dq = q.shape
    nk, dk = k_all.shape[1:]
    bb = math.gcd(b, ATTN_SAMPLE_BATCH)
    blocks = [((bb, t, dq), BF16)] * 2 + [((bb, nk, dk), F32)] * 2 + [
        (bias.shape[:2] + (2 * LANES,), F32), (sink_col.shape[:2] + (LANES,), F32)]
    return pl.pallas_call(
        functools.partial(_attn_sample_body, kvh=kvh, group=group),
        out_shape=jax.ShapeDtypeStruct((b, t, dq), BF16),
        grid=(b // bb,),
        in_specs=[
            pl.BlockSpec((bb, t, dq), lambda bi: (bi, 0, 0)),
            pl.BlockSpec((bb, nk, dk), lambda bi: (bi, 0, 0)),
            pl.BlockSpec((bb, nk, dk), lambda bi: (bi, 0, 0)),
            pl.BlockSpec(bias.shape, lambda bi: (0, 0, 0)),
            pl.BlockSpec(sink_col.shape, lambda bi: (0, 0, 0)),
        ],
        out_specs=pl.BlockSpec((bb, t, dq), lambda bi: (bi, 0, 0)),
        compiler_params=pltpu.CompilerParams(
            dimension_semantics=("parallel",), vmem_limit_bytes=_vmem_limit(blocks)),
        name="swa_sink_attention_sample",
    )(q, k_all, v_all, bias, sink_col)


def _trunks(xp3, xs3, mods_p, mods_s, state_conv, win_k, win_v, p):
    bp, tp, d = xp3.shape
    bs, ts, _ = xs3.shape
    mp, ms = bp * tp, bs * ts
    depth = p["w_mod"].shape[0]
    d_ff = p["w_gu"].shape[2] // 2
    n_heads = p["attn_sinks"].shape[1]
    dq = n_heads * HEAD_DIM
    kvh = (p["w_qkv"].shape[2] - dq) // (2 * HEAD_DIM)
    group = n_heads // kvh
    dkv = kvh * HEAD_DIM
    kw = p["w_dw"].shape[1]
    n_buckets = p["rel_bias_table"].shape[1]
    conv_p, k_p, v_p, conv_s, k_s, v_s = [], [], [], [], [], []

    xp, xs = xp3.reshape(mp, d), xs3.reshape(ms, d)
    hp = _norm_mod_call(xp, p["norm_mix"][0], mods_p[0], 0, 1, PANEL_ROWS)
    hs = _norm_mod_call(xs, p["norm_mix"][0], mods_s[0], 0, 1, ms)
    for i in range(depth):
        j = i // 2
        ffn_norm = lambda mods: (p["norm_ffn"][i], mods[i], 3, 4)
        if i % 2 == 0:
            u_p, u_s, wq = _mm_call(hp, p["w_pw1"], j, (0, d), d, x2=hs, cast=(p["w_pw2"], j),
                                    bias=p["b_pw1"], act="glu", name="pw1_glu")
            u_p, u_s = u_p.reshape(bp, tp, d), u_s.reshape(bs, ts, d)
            conv_w = (p["w_dw"], p["b_dw"], p["conv_ln_g"], p["conv_ln_b"], j)
            z_p = _conv_prompt_call(u_p, *conv_w)
            conv_p.append(u_p[:, tp - (kw - 1):])
            pad = (-(kw - 1 + ts)) % SUBLANES
            up = jnp.concatenate([jnp.zeros((bs, pad, d), F32), state_conv[j], u_s], axis=1)
            z_s = _conv_sample_call(up, pad, ts, *conv_w)
            conv_s.append(up[:, pad + ts:])
            xp, hp = _rowmm_call(z_p.reshape(mp, d), wq, xp, mods_p[i], 2, ffn_norm(mods_p),
                                 bias=p["b_pw2"][j], name="pw2_residual_norm")
            xs, hs = _rowmm_call(z_s.reshape(ms, d), wq, xs, mods_s[i], 2, ffn_norm(mods_s),
                                 bias=p["b_pw2"][j], name="pw2_residual_norm")
        else:
            sinks = p["attn_sinks"][j].reshape(kvh, group, 1)
            qT, q_s, wq = _mm_call(hp, p["w_qkv"], j, (0,), dq, x2=hs, cast=(p["w_o"], j),
                                   bias=p["b_qkv"], scale=LOG2E * HEAD_DIM ** -0.5,
                                   out_dtype=BF16, tn=Q_PANEL_COLS, name="q_proj",
                                   out_mode="t", seq=tp)
            (k, vT), kv_s = _mm_call(hp, p["w_qkv"], j, (dq,), 2 * dkv, x2=hs, bias=p["b_qkv"],
                                     tn=2 * dkv, name="kv_proj", out_mode="split_t", seq=tp)
            k = k.reshape(bp, tp, dkv)
            bias = _bias_call(_pair_codes(n_buckets), p["rel_bias_table"], group)
            sink_row = jnp.repeat(sinks, PAIR, axis=2).reshape(kvh, 1, group * PAIR)
            oT = _attn_prompt_call(qT, k, vT, bias, sink_row, kvh, group)
            k_p.append(k[:, tp - WINDOW:].reshape(bp, WINDOW, kvh, HEAD_DIM))
            v_p.append(jnp.swapaxes(vT[:, :, tp - WINDOW:], 1, 2)
                       .reshape(bp, WINDOW, kvh, HEAD_DIM))
            kv3 = kv_s.reshape(bs, ts, 2, kvh, HEAD_DIM)
            k_all = jnp.concatenate([win_k[j], kv3[:, :, 0]], axis=1)
            v_all = jnp.concatenate([win_v[j], kv3[:, :, 1]], axis=1)
            n_keys = WINDOW + ts
            bias = _bias_call(_bucket_codes(ts, n_keys, -WINDOW, n_buckets),
                              p["rel_bias_table"]).reshape(kvh, group * ts, n_keys)
            o_s = _attn_sample_call(q_s.reshape(bs, ts, dq), k_all.reshape(bs, n_keys, dkv),
                                    v_all.reshape(bs, n_keys, dkv), bias,
                                    jnp.repeat(sinks, ts, axis=1), kvh, group)
            k_s.append(k_all[:, ts:])
            v_s.append(v_all[:, ts:])
            xp, hp = _rowmm_call(oT, wq, xp, mods_p[i], 2, ffn_norm(mods_p), bias=p["b_o"][j],
                                 x_t=True, name="wo_residual_norm")
            xs, hs = _rowmm_call(o_s.reshape(ms, dq), wq, xs, mods_s[i], 2, ffn_norm(mods_s),
                                 bias=p["b_o"][j], name="wo_residual_norm")
        a_p, a_s, wq = _mm_call(hp, p["w_gu"], i, (0, d_ff), d_ff, x2=hs, cast=(p["w_down"], i),
                                act="swiglu", out_dtype=BF16, name="ffn_gate_up")
        if i + 1 < depth:
            nxt = lambda mods: (p["norm_mix"][i + 1], mods[i + 1], 0, 1)
            xp, hp = _rowmm_call(a_p, wq, xp, mods_p[i], 5, nxt(mods_p),
                                 name="ffn_down_residual_norm")
            xs, hs = _rowmm_call(a_s, wq, xs, mods_s[i], 5, nxt(mods_s),
                                 name="ffn_down_residual_norm")
        else:
            y_p = _rowmm_call(a_p, wq, xp, mods_p[i], 5, (p["norm_out"],),
                              name="ffn_down_residual_final_norm")
            y_s = _rowmm_call(a_s, wq, xs, mods_s[i], 5, (p["norm_out"],),
                              name="ffn_down_residual_final_norm")
    stack = jnp.stack
    return (y_p.reshape(bp, tp, d), y_s.reshape(bs, ts, d), stack(conv_p), stack(k_p), stack(v_p),
            stack(conv_s), stack(k_s), stack(v_s))


def kernel(x_prompt, x_sample, c_prompt, c_sample, state_conv, cache_win_k, cache_win_v, w_mod, b_mod, norm_mix, norm_ffn, w_pw1, b_pw1, w_dw, b_dw, conv_ln_g, conv_ln_b, w_pw2, b_pw2, w_qkv, b_qkv, w_o, b_o, attn_sinks, rel_bias_table, w_gu, w_down, norm_out):
    p = dict(w_mod=w_mod, b_mod=b_mod, norm_mix=norm_mix, norm_ffn=norm_ffn, w_pw1=w_pw1,
             b_pw1=b_pw1, w_dw=w_dw, b_dw=b_dw, conv_ln_g=conv_ln_g, conv_ln_b=conv_ln_b,
             w_pw2=w_pw2, b_pw2=b_pw2, w_qkv=w_qkv, b_qkv=b_qkv, w_o=w_o, b_o=b_o,
             attn_sinks=attn_sinks, rel_bias_table=rel_bias_table, w_gu=w_gu, w_down=w_down,
             norm_out=norm_out)
    bp, sp, d = x_prompt.shape
    bs, ts, _ = x_sample.shape
    depth = w_mod.shape[0]

    n_c = bp + bs
    c_all = jnp.concatenate(
        [c_prompt, c_sample, jnp.zeros((-n_c % BF16_SUBLANES, d), F32)], axis=0)
    mod_all = _mod_call(c_all, w_mod, b_mod)

    mods_p = [_Mod(mod_all[l, :bp].reshape(bp, 1, 6 * d), sp) for l in range(depth)]
    mods_s = [_Mod(mod_all[l, bp:n_c].reshape(bs, 1, 6 * d), ts) for l in range(depth)]

    return _trunks(x_prompt, x_sample, mods_p, mods_s, state_conv, cache_win_k, cache_win_v, p)
```

```python
import functools
import math

import numpy as np
import jax
import jax.numpy as jnp
from jax import lax
from jax.experimental import pallas as pl
from jax.experimental.pallas import tpu as pltpu

F32 = jnp.float32
BF16 = jnp.bfloat16

CHUNK = 64
WINDOW = 128
HEAD_DIM = 64
MAX_DISTANCE = 128
EPS = 1e-6
LOG2E = math.log2(math.e)

V7X_VMEM_BYTES = 64 * 2**20
V7X_VMEM_CAP = V7X_VMEM_BYTES - 8 * 2**20
SUBLANES = 8
LANES = 128
BF16_SUBLANES = 2 * SUBLANES

PANEL_ROWS = 1024
PANEL_COLS = 512
Q_PANEL_COLS = 1024
ROWMM_ROWS = 512
CONV_ROWS = 128
MOD_COLS = 1024


def _nbytes(shape, dtype):
    return math.prod(shape) * jnp.dtype(dtype).itemsize


def _vmem_limit(blocks, scratch=()):
    est = 2 * sum(_nbytes(s, d) for s, d in blocks) + sum(_nbytes(s, d) for s, d in scratch)
    return int(min(V7X_VMEM_CAP, max(32 * 2**20, 2 * est)))


def _tile(dim, pref, mult=LANES):
    if dim <= pref:
        return dim
    t = (pref // mult) * mult
    while t >= mult:
        if dim % t == 0:
            return t
        t -= mult
    return dim


def _sigmoid(x):
    return 1.0 / (1.0 + jnp.exp(-x))


def _mod_body(c_ref, w_ref, b_ref, o_ref):
    c = c_ref[...]
    a = (c * _sigmoid(c)).astype(BF16)
    o_ref[...] = jnp.dot(a, w_ref[...].astype(BF16), preferred_element_type=F32) + b_ref[...]


def _mod_call(c_all, w_mod, b_mod):
    depth, d, n = w_mod.shape
    mp = c_all.shape[0]
    tn = _tile(n, MOD_COLS)
    blocks = [((mp, d), F32), ((d, tn), F32), ((1, tn), F32), ((mp, tn), F32)]
    return pl.pallas_call(
        _mod_body,
        out_shape=jax.ShapeDtypeStruct((depth, mp, n), F32),
        grid=(depth, n // tn),
        in_specs=[
            pl.BlockSpec((mp, d), lambda l, j: (0, 0)),
            pl.BlockSpec((None, d, tn), lambda l, j: (l, 0, j)),
            pl.BlockSpec((None, 1, tn), lambda l, j: (l, 0, j)),
        ],
        out_specs=pl.BlockSpec((None, mp, tn), lambda l, j: (l, 0, j)),
        compiler_params=pltpu.CompilerParams(
            dimension_semantics=("arbitrary", "arbitrary"),
            vmem_limit_bytes=_vmem_limit(blocks, [((d, tn), BF16)])),
        name="adaln_mod",
    )(c_all, w_mod, b_mod.reshape(depth, 1, n))


class _Mod:
    def __init__(self, arr, rows_per_group):
        self.arr = arr
        self.rows_per_group = rows_per_group

    def tile_rows(self, total_rows, pref, mult):
        if self.rows_per_group >= pref:
            return _tile(self.rows_per_group, pref, mult)
        return _tile(total_rows, pref, max(mult, self.rows_per_group))

    def spec(self, seg, d, tm):
        if tm <= self.rows_per_group:
            per = self.rows_per_group // tm
            return pl.BlockSpec((1, 1, d), lambda i: (i // per, 0, seg))
        return pl.BlockSpec((tm // self.rows_per_group, 1, d), lambda i: (i, 0, seg))


def _grouped(v, m):
    return v.reshape(m.shape[0], v.shape[0] // m.shape[0], v.shape[1])


def _norm_mod_body(x_ref, g_ref, sh_ref, sc_ref, o_ref):
    x = x_ref[...]
    y = x * lax.rsqrt(jnp.mean(x * x, axis=-1, keepdims=True) + EPS) * g_ref[...]
    sc = sc_ref[...]
    o_ref[...] = (_grouped(y, sc) * (1.0 + sc) + sh_ref[...]).reshape(x.shape).astype(o_ref.dtype)


def _norm_mod_call(x, g, mod, seg_shift, seg_scale, tm):
    m, d = x.shape
    tm = mod.tile_rows(m, tm, SUBLANES)
    blocks = [((tm, d), F32), ((1, d), F32), ((tm, d), BF16)]
    return pl.pallas_call(
        _norm_mod_body,
        out_shape=jax.ShapeDtypeStruct((m, d), BF16),
        grid=(m // tm,),
        in_specs=[
            pl.BlockSpec((tm, d), lambda i: (i, 0)),
            pl.BlockSpec((1, d), lambda i: (0, 0)),
            mod.spec(seg_shift, d, tm),
            mod.spec(seg_scale, d, tm),
        ],
        out_specs=pl.BlockSpec((tm, d), lambda i: (i, 0)),
        compiler_params=pltpu.CompilerParams(
            dimension_semantics=("parallel",), vmem_limit_bytes=_vmem_limit(blocks)),
        name="rmsnorm_modulate",
    )(x, g.reshape(1, d), mod.arr, mod.arr)


def _mm_body(x_ref, *refs, n_w, has_bias, act, scale, out_mode, has_x2, has_cast):
    if has_x2:
        x2_ref, refs = refs[0], refs[1:]
    ws = refs[:n_w]
    refs = refs[n_w:]
    bs = refs[:n_w] if has_bias else ()
    refs = refs[len(bs):]
    if has_cast:
        wc_ref, refs = refs[0], refs[1:]
    n_o = 2 if out_mode == "split_t" else 1
    o_refs = refs[:n_o]
    refs = refs[n_o:]
    if has_x2:
        o2_ref, refs = refs[0], refs[1:]
    if has_cast:
        oc_ref, refs = refs[0], refs[1:]
        oc_ref[...] = wc_ref[...].astype(BF16)
    wbs = refs

    def compute(x):
        ps = [None] * n_w
        for i in ((1, 0) if act == "glu" else range(n_w)):
            p = jnp.dot(x, wbs[i][...], preferred_element_type=F32)
            if has_bias:
                p = p + bs[i][...]
            ps[i] = p
        if act == "glu":
            y = ps[0] * _sigmoid(ps[1])
        elif act == "swiglu":
            y = ps[0] * _sigmoid(ps[0]) * ps[1]
        else:
            y = ps[0]
        return y * scale if scale != 1.0 else y

    @pl.when(pl.program_id(1) == 0)
    def _():
        for w, wb in zip(ws, wbs):
            wb[...] = w[...].astype(BF16)
        if has_x2:
            o2_ref[...] = compute(x2_ref[...]).astype(o2_ref.dtype)

    y = compute(x_ref[...])
    if out_mode == "plain":
        o_refs[0][...] = y.astype(o_refs[0].dtype)
    elif out_mode == "t":
        o_refs[0][...] = y.T.astype(o_refs[0].dtype)
    else:
        half = y.shape[1] // 2
        o_refs[0][...] = y[:, :half].astype(o_refs[0].dtype)
        o_refs[1][...] = y[:, half:].T.astype(o_refs[1].dtype)


def _mm_call(x, w, layer, col_starts, n_out, *, x2=None, cast=None, bias=None, act=None,
             scale=1.0, out_dtype=F32, tm=PANEL_ROWS, tn=PANEL_COLS, name="matmul",
             out_mode="plain", seq=None):
    m, k = x.shape
    if out_mode == "plain":
        tm = _tile(m, tm, SUBLANES)
    else:
        tm = _tile(seq, tm, LANES)
    tn = _tile(n_out, tn)
    n_w = len(col_starts)
    assert all(c % tn == 0 for c in col_starts) and m % tm == 0 and n_out % tn == 0
    has_bias = bias is not None
    per_b = None if seq is None else seq // tm

    in_specs = [pl.BlockSpec((tm, k), lambda j, i: (i, 0))]
    args = [x]
    blocks = [((tm, k), x.dtype), ((tm, tn), out_dtype)]
    if x2 is not None:
        m2 = x2.shape[0]
        in_specs.append(pl.BlockSpec((m2, k), lambda j, i: (0, 0)))
        args.append(x2)
        blocks += [((m2, k), x2.dtype), ((m2, tn), out_dtype)]
    for c in col_starts:
        off = c // tn
        in_specs.append(pl.BlockSpec((None, k, tn), lambda j, i, off=off: (layer, 0, off + j)))
        args.append(w)
        blocks.append(((k, tn), F32))
    if has_bias:
        b3 = bias.reshape(bias.shape[0], 1, bias.shape[1])
        for c in col_starts:
            off = c // tn
            in_specs.append(pl.BlockSpec((None, 1, tn), lambda j, i, off=off: (layer, 0, off + j)))
            args.append(b3)
    n_i = m // tm
    if cast is not None:
        w_other, layer_other = cast
        kc, dc = w_other.shape[1:]
        slab = kc // ((n_out // tn) * n_i)
        assert slab * (n_out // tn) * n_i == kc and slab % BF16_SUBLANES == 0
        in_specs.append(pl.BlockSpec((None, slab, dc), lambda j, i: (layer_other, j * n_i + i, 0)))
        args.append(w_other)
        blocks += [((slab, dc), F32), ((slab, dc), BF16)]
    scratch = [((k, tn), BF16)] * n_w
    body = functools.partial(_mm_body, n_w=n_w, has_bias=has_bias, act=act, scale=scale,
                             out_mode=out_mode, has_x2=x2 is not None, has_cast=cast is not None)
    if out_mode == "plain":
        out_shape = [jax.ShapeDtypeStruct((m, n_out), out_dtype)]
        out_specs = [pl.BlockSpec((tm, tn), lambda j, i: (i, j))]
    elif out_mode == "t":
        out_shape = [jax.ShapeDtypeStruct((m // seq, n_out, seq), out_dtype)]
        out_specs = [pl.BlockSpec((None, tn, tm), lambda j, i: (i // per_b, j, i % per_b))]
    else:
        assert tn == n_out
        half = n_out // 2
        out_shape = [jax.ShapeDtypeStruct((m, half), out_dtype),
                     jax.ShapeDtypeStruct((m // seq, half, seq), out_dtype)]
        out_specs = [pl.BlockSpec((tm, half), lambda j, i: (i, 0)),
                     pl.BlockSpec((None, half, tm), lambda j, i: (i // per_b, 0, i % per_b))]
    if x2 is not None:
        out_shape.append(jax.ShapeDtypeStruct((m2, n_out), out_dtype))
        out_specs.append(pl.BlockSpec((m2, tn), lambda j, i: (0, j)))
    if cast is not None:
        out_shape.append(jax.ShapeDtypeStruct((kc, dc), BF16))
        out_specs.append(pl.BlockSpec((slab, dc), lambda j, i: (j * n_i + i, 0)))
    outs = pl.pallas_call(
        body,
        out_shape=out_shape,
        grid=(n_out // tn, n_i),
        in_specs=in_specs,
        out_specs=out_specs,
        scratch_shapes=[pltpu.VMEM(s, d) for s, d in scratch],
        compiler_params=pltpu.CompilerParams(
            dimension_semantics=("arbitrary", "arbitrary"),
            vmem_limit_bytes=_vmem_limit(blocks, scratch)),
        name=name,
    )(*args)
    n_primary = 2 if out_mode == "split_t" else 1
    primary = outs[0] if n_primary == 1 else outs[:n_primary]
    extras = list(outs[n_primary:])
    return (primary, *extras) if extras else primary


ROWMM_SUB_ROWS = 256
ROWMM_K_CHUNKS = 4
MXU_K = 256


def _rowmm_body(*refs, has_bias, x_t, final, single_step):
    x_ref, w_hbm = refs[:2]
    refs = refs[2:]
    if has_bias:
        b_ref, refs = refs[0], refs[1:]
    res_ref, gate_ref, ng_ref = refs[:3]
    refs = refs[3:]
    if not final:
        sh_ref, sc_ref = refs[:2]
        refs = refs[2:]
    n_o = 1 if final else 2
    o_refs = refs[:n_o]
    refs = refs[n_o:]
    w_res, sem = refs
    k = w_res.shape[0]
    tiles = k // MXU_K
    n_chunks = min(ROWMM_K_CHUNKS, tiles)
    bounds = [MXU_K * (tiles * c // n_chunks) for c in range(n_chunks + 1)]

    def chunk_copy(c):
        rows = pl.ds(bounds[c], bounds[c + 1] - bounds[c])
        return pltpu.make_async_copy(w_hbm.at[rows, :], w_res.at[rows, :], sem.at[c])

    tm = res_ref.shape[0]
    sub = min(tm, ROWMM_SUB_ROWS)

    def lhs(r0, k0=0, k1=k):
        return x_ref[k0:k1, r0:r0 + sub].T if x_t else x_ref[r0:r0 + sub, k0:k1]

    def groups_of(ref, r0):
        per = tm // ref.shape[0]
        return ref[...] if per >= tm else ref[r0 // per:(r0 + sub) // per]

    def finish(r0, y):
        if has_bias:
            y = y + b_ref[...]
        gate = groups_of(gate_ref, r0)
        xn = (_grouped(res_ref[r0:r0 + sub, :], gate) + gate * _grouped(y, gate)).reshape(y.shape)
        r = xn * lax.rsqrt(jnp.mean(xn * xn, axis=-1, keepdims=True) + EPS) * ng_ref[...]
        if final:
            o_refs[0][r0:r0 + sub, :] = r
        else:
            sc = groups_of(sc_ref, r0)
            o_refs[0][r0:r0 + sub, :] = xn
            o_refs[1][r0:r0 + sub, :] = (
                _grouped(r, sc) * (1.0 + sc) + groups_of(sh_ref, r0)
            ).reshape(y.shape).astype(o_refs[1].dtype)

    for c in range(n_chunks):
        @pl.when(pl.program_id(0) == 0)
        def _():
            chunk_copy(c).start()

    first = 0
    if single_step:
        y = None
        for c in range(n_chunks):
            chunk_copy(c).wait()
            part = jnp.dot(lhs(0, bounds[c], bounds[c + 1]), w_res[bounds[c]:bounds[c + 1], :],
                           preferred_element_type=F32)
            y = part if y is None else y + part
        finish(0, y)
        first = sub
    else:
        for c in range(n_chunks):
            @pl.when(pl.program_id(0) == 0)
            def _():
                chunk_copy(c).wait()

    for r0 in range(first, tm, sub):
        finish(r0, jnp.dot(lhs(r0), w_res[...], preferred_element_type=F32))


def _rowmm_call(x, w, res, gate, gate_seg, nxt, *, bias=None, x_t=False, tm=ROWMM_ROWS,
                name="rowmm"):
    if x_t:
        nb, k, seq = x.shape
        m = nb * seq
    else:
        m, k = x.shape
    d = w.shape[-1]
    final = len(nxt) == 1
    scratch = [((k, d), BF16)]
    n_mods = 1 if final else 3

    def vmem_need(rows):
        blocks = [((rows, k), BF16), ((rows, d), F32), ((rows, d), F32 if final else BF16)]
        blocks += [((rows, d), F32)] * (0 if final else 1)
        blocks += [((-(-rows // gate.rows_per_group) * SUBLANES, d), F32)] * n_mods
        temporaries = 4 * _nbytes((rows, d), F32)
        return (2 * sum(_nbytes(s, t) for s, t in blocks) + sum(_nbytes(s, t) for s, t in scratch)
                + temporaries)

    mult = LANES if x_t else SUBLANES
    tm = gate.tile_rows(m, tm, mult)
    rpg = gate.rows_per_group
    fits_groups = lambda t: t % mult == 0 and (rpg % t == 0 or t % rpg == 0)
    while vmem_need(tm) > V7X_VMEM_CAP and tm % 2 == 0 and fits_groups(tm // 2):
        tm //= 2
    assert final or nxt[1].rows_per_group == gate.rows_per_group
    has_bias = bias is not None
    row = lambda i: (i, 0)
    if x_t:
        per_b = seq // tm
        in_specs = [pl.BlockSpec((None, k, tm), lambda i: (i // per_b, 0, i % per_b))]
    else:
        in_specs = [pl.BlockSpec((tm, k), row)]
    in_specs.append(pl.BlockSpec(memory_space=pltpu.HBM))
    args = [x, w]
    if has_bias:
        in_specs.append(pl.BlockSpec((1, d), lambda i: (0, 0)))
        args.append(bias.reshape(1, d))
    in_specs += [pl.BlockSpec((tm, d), row), gate.spec(gate_seg, d, tm),
                 pl.BlockSpec((1, d), lambda i: (0, 0))]
    args += [res, gate.arr, nxt[0].reshape(1, d)]
    if final:
        out_shape = jax.ShapeDtypeStruct((m, d), F32)
        out_specs = pl.BlockSpec((tm, d), row)
    else:
        _, nmod, seg_shift, seg_scale = nxt
        in_specs += [nmod.spec(seg_shift, d, tm), nmod.spec(seg_scale, d, tm)]
        args += [nmod.arr, nmod.arr]
        out_shape = [jax.ShapeDtypeStruct((m, d), F32), jax.ShapeDtypeStruct((m, d), BF16)]
        out_specs = [pl.BlockSpec((tm, d), row), pl.BlockSpec((tm, d), row)]
    return pl.pallas_call(
        functools.partial(_rowmm_body, has_bias=has_bias, x_t=x_t, final=final,
                          single_step=m == tm),
        out_shape=out_shape,
        grid=(m // tm,),
        in_specs=in_specs,
        out_specs=out_specs,
        scratch_shapes=[pltpu.VMEM(s, dt) for s, dt in scratch]
        + [pltpu.SemaphoreType.DMA((ROWMM_K_CHUNKS,))],
        compiler_params=pltpu.CompilerParams(
            dimension_semantics=("arbitrary",),
            vmem_limit_bytes=min(V7X_VMEM_CAP, max(32 * 2**20, vmem_need(tm) + 4 * 2**20))),
        name=name,
    )(*args)


def _conv_ln_swish(up_ref, pad, tt, w_ref, bdw_ref, lg_ref, lb_ref, acc_ref, o_ref):
    kw, d = w_ref.shape
    by_shift = [[(a, SUBLANES * a + s - pad) for a in range((pad + kw - 1) // SUBLANES + 1)
                 if 0 <= SUBLANES * a + s - pad < kw] for s in range(SUBLANES)]

    def strip(c, carry):
        cols = pl.ds(pl.multiple_of(c * LANES, LANES), LANES)
        z = bdw_ref[:, cols]
        for s, taps in enumerate(by_shift):
            n = tt + SUBLANES if s else tt
            q = None
            for a, k in taps:
                term = up_ref[pl.ds(SUBLANES * a, n), cols] * w_ref[pl.ds(k, 1), cols]
                q = term if q is None else q + term
            if q is not None:
                z = z + q[s:s + tt]
        acc_ref[:, cols] = z
        return carry

    lax.fori_loop(0, d // LANES, strip, 0)
    z = acc_ref[...]
    mu = jnp.mean(z, axis=-1, keepdims=True)
    zc = z - mu
    y = zc * lax.rsqrt(jnp.mean(zc * zc, axis=-1, keepdims=True) + EPS)
    y = y * lg_ref[...] + lb_ref[...]
    o_ref[...] = (y * _sigmoid(y)).astype(o_ref.dtype)


def _conv_prompt_body(main_ref, halo_ref, w_ref, bdw_ref, lg_ref, lb_ref, o_ref, up_ref, acc_ref,
                      *, halo, tt):
    kw = w_ref.shape[0]

    @pl.when(pl.program_id(1) == 0)
    def _():
        up_ref[0:halo, :] = jnp.zeros((halo, up_ref.shape[1]), F32)

    @pl.when(pl.program_id(1) > 0)
    def _():
        up_ref[0:halo, :] = halo_ref[...]

    up_ref[halo:halo + tt, :] = main_ref[...]
    _conv_ln_swish(up_ref, halo - (kw - 1), tt, w_ref, bdw_ref, lg_ref, lb_ref, acc_ref, o_ref)


def _conv_prompt_call(u, w_dw, b_dw, ln_g, ln_b, layer, tt=CONV_ROWS):
    b, t, d = u.shape
    kw = w_dw.shape[1]
    halo = -(-(kw - 1) // SUBLANES) * SUBLANES
    tt = _tile(t, tt, halo)
    hb = tt // halo
    vec = lambda: pl.BlockSpec((None, 1, d), lambda bi, ti: (layer, 0, 0))
    blocks = [((tt, d), F32), ((halo, d), F32), ((kw, d), F32), ((tt, d), BF16)]
    scratch = [((halo + tt, d), F32), ((tt, d), F32)]
    return pl.pallas_call(
        functools.partial(_conv_prompt_body, halo=halo, tt=tt),
        out_shape=jax.ShapeDtypeStruct((b, t, d), BF16),
        grid=(b, t // tt),
        in_specs=[
            pl.BlockSpec((None, tt, d), lambda bi, ti: (bi, ti, 0)),
            pl.BlockSpec((None, halo, d), lambda bi, ti: (bi, jnp.maximum(ti * hb - 1, 0), 0)),
            pl.BlockSpec((None, kw, d), lambda bi, ti: (layer, 0, 0)),
            vec(), vec(), vec(),
        ],
        out_specs=pl.BlockSpec((None, tt, d), lambda bi, ti: (bi, ti, 0)),
        scratch_shapes=[pltpu.VMEM(s, dt) for s, dt in scratch],
        compiler_params=pltpu.CompilerParams(
            dimension_semantics=("parallel", "arbitrary"),
            vmem_limit_bytes=_vmem_limit(blocks, scratch)),
        name="dwconv_ln_swish_prompt",
    )(u, u, w_dw, b_dw.reshape(-1, 1, d), ln_g.reshape(-1, 1, d), ln_b.reshape(-1, 1, d))


def _conv_sample_body(up_ref, w_ref, bdw_ref, lg_ref, lb_ref, o_ref, acc_ref, *, pad, tt):
    _conv_ln_swish(up_ref, pad, tt, w_ref, bdw_ref, lg_ref, lb_ref, acc_ref, o_ref)


def _conv_sample_call(up, pad, tt, w_dw, b_dw, ln_g, ln_b, layer):
    b, rows, d = up.shape
    kw = w_dw.shape[1]
    vec = lambda: pl.BlockSpec((None, 1, d), lambda bi: (layer, 0, 0))
    blocks = [((rows, d), F32), ((kw, d), F32), ((tt, d), BF16)]
    scratch = [((tt, d), F32)]
    return pl.pallas_call(
        functools.partial(_conv_sample_body, pad=pad, tt=tt),
        out_shape=jax.ShapeDtypeStruct((b, tt, d), BF16),
        grid=(b,),
        in_specs=[
            pl.BlockSpec((None, rows, d), lambda bi: (bi, 0, 0)),
            pl.BlockSpec((None, kw, d), lambda bi: (layer, 0, 0)),
            vec(), vec(), vec(),
        ],
        out_specs=pl.BlockSpec((None, tt, d), lambda bi: (bi, 0, 0)),
        scratch_shapes=[pltpu.VMEM(s, dt) for s, dt in scratch],
        compiler_params=pltpu.CompilerParams(
            dimension_semantics=("parallel",), vmem_limit_bytes=_vmem_limit(blocks, scratch)),
        name="dwconv_ln_swish_sample",
    )(up, w_dw, b_dw.reshape(-1, 1, d), ln_g.reshape(-1, 1, d), ln_b.reshape(-1, 1, d))


def _bucket_codes(n_q, n_k, k_off, n_buckets):
    rel = (np.arange(n_k) + k_off)[None, :] - np.arange(n_q)[:, None]
    nb = n_buckets // 2
    max_exact = nb // 2
    ret = np.where(rel > 0, nb, 0)
    n = np.abs(rel)
    nf = np.maximum(n, 1).astype(np.float32)
    large = max_exact + (np.log(nf / np.float32(max_exact))
                         / np.float32(math.log(MAX_DISTANCE / max_exact))
                         * np.float32(nb - max_exact)).astype(np.int32)
    large = np.minimum(large, nb - 1)
    return (ret + np.where(n < max_exact, n, large)).astype(np.int32)


def _bias_body(code_ref, table_ref, o_ref, *, present, group):
    n_cols = code_ref.shape[1]

    used = sorted({b for _, buckets in present for b in buckets})

    def head(hh, carry):
        value = {b: table_ref[hh, b] * LOG2E for b in used}
        for r0, buckets in present:
            rows = slice(r0, r0 + SUBLANES)
            code = code_ref[rows, :]
            out = jnp.full(code.shape, -jnp.inf, F32)
            for b in buckets:
                out = jnp.where(code == b, value[b], out)
            if group is None:
                o_ref[hh, rows, :] = out
            else:
                cols = pl.ds(pl.multiple_of((hh % group) * n_cols, LANES), n_cols)
                o_ref[hh // group, rows, cols] = out
        return carry

    lax.fori_loop(0, table_ref.shape[0], head, 0)


def _bias_call(code, table, group=None):
    n_heads, n_buckets = table.shape
    rows, cols = code.shape
    shape = (n_heads, rows, cols) if group is None else (n_heads // group, rows, group * cols)
    present = tuple(
        (r0, tuple(int(b) for b in np.unique(code[r0:r0 + SUBLANES]) if b < n_buckets))
        for r0 in range(0, rows, SUBLANES))
    return pl.pallas_call(
        functools.partial(_bias_body, present=present, group=group),
        out_shape=jax.ShapeDtypeStruct(shape, F32),
        in_specs=[pl.BlockSpec(memory_space=pltpu.VMEM), pl.BlockSpec(memory_space=pltpu.SMEM)],
        out_specs=pl.BlockSpec(memory_space=pltpu.VMEM),
        name="rel_bias",
    )(jnp.asarray(code), table)


def _unit_scores(qs, kh, bias):
    return lax.dot_general(qs, kh, (((1,), (1,)), ((), ())), preferred_element_type=F32) + bias


def _unit_output(s, vh, sink_col):
    sink_col = sink_col * LOG2E
    m = jnp.maximum(jnp.max(s, axis=-1, keepdims=True), sink_col)
    e = jnp.exp2(s - m)
    l = jnp.sum(e, axis=-1, keepdims=True) + jnp.exp2(sink_col - m)
    return jnp.dot(e.astype(BF16), vh, preferred_element_type=F32) * (1.0 / l)


def _stack_heads(q, h, group):
    hd = HEAD_DIM
    return jnp.concatenate(
        [q[:, (h * group + g) * hd:(h * group + g + 1) * hd] for g in range(group)], axis=0)


def _unstack_heads(o, group):
    tq = o.shape[0] // group
    return [o[g * tq:(g + 1) * tq, :] for g in range(group)]


ATTN_Q_COLS = 512
PAIR = 2 * CHUNK


def _pair_codes(n_buckets):
    code = _bucket_codes(PAIR, WINDOW + PAIR, -WINDOW, n_buckets).T
    key_chunk = np.arange(WINDOW + PAIR)[:, None] // CHUNK
    q_chunk = np.arange(PAIR)[None, :] // CHUNK
    visible = (key_chunk >= q_chunk) & (key_chunk <= q_chunk + WINDOW // CHUNK)
    return np.where(visible, code, n_buckets).astype(np.int32)


def _attn_prompt_body(qT_ref, kp_ref, kc_ref, vTp_ref, vTc_ref, bias_ref, sink_ref, oT_ref,
                      k_scr, vT_scr, *, kvh, group):
    i = pl.program_id(1)
    tq = qT_ref.shape[1]
    nk = WINDOW + PAIR
    hd = HEAD_DIM
    for h in range(kvh):
        k_scr[h, 0:WINDOW, :] = kp_ref[:, h * hd:(h + 1) * hd].astype(BF16)
        k_scr[h, WINDOW:WINDOW + tq, :] = kc_ref[:, h * hd:(h + 1) * hd].astype(BF16)
    vT_scr[:, 0:WINDOW] = vTp_ref[...].astype(BF16)
    vT_scr[:, WINDOW:WINDOW + tq] = vTc_ref[...].astype(BF16)
    row = lax.broadcasted_iota(jnp.int32, (nk, group * PAIR), 0)
    start_mask = jnp.where(row < jnp.where(i == 0, WINDOW, 0), -jnp.inf, 0.0).astype(F32)
    units = [(p * PAIR, h) for p in range(tq // PAIR) for h in range(kvh)]

    def scores(c0, h):
        qsT = jnp.concatenate(
            [qT_ref[hh * hd:(hh + 1) * hd, c0:c0 + PAIR] for hh in range(h * group, (h + 1) * group)],
            axis=1)
        s = jnp.dot(k_scr[h, c0:c0 + nk, :], qsT, preferred_element_type=F32) + bias_ref[h]
        return s + start_mask if c0 == 0 else s

    s_next = scores(*units[0])
    for n, (c0, h) in enumerate(units):
        s = s_next
        if n + 1 < len(units):
            s_next = scores(*units[n + 1])
        sink = sink_ref[h] * LOG2E
        m = jnp.maximum(jnp.max(s, axis=0, keepdims=True), sink)
        e = jnp.exp2(s - m)
        l = jnp.sum(e, axis=0, keepdims=True) + jnp.exp2(sink - m)
        oT = jnp.dot(vT_scr[h * hd:(h + 1) * hd, c0:c0 + nk], e.astype(BF16),
                     preferred_element_type=F32) * (1.0 / l)
        for g in range(group):
            hh = h * group + g
            oT_ref[hh * hd:(hh + 1) * hd, c0:c0 + PAIR] = (
                oT[:, g * PAIR:(g + 1) * PAIR].astype(oT_ref.dtype))


def _attn_prompt_call(qT, k, vT, bias, sink_row, kvh, group):
    b, dq, s = qT.shape
    dk = k.shape[2]
    tq = _tile(s, ATTN_Q_COLS, PAIR)
    per = tq // WINDOW
    prev = lambda i: jnp.maximum(i * per - 1, 0)
    blocks = [((dq, tq), BF16)] * 2 + [((WINDOW + tq, dk), F32)] * 2 + [
        (bias.shape, F32), (sink_row.shape[:1] + (SUBLANES, sink_row.shape[2]), F32)]
    scratch = [((kvh, WINDOW + tq, LANES), BF16), ((dk, WINDOW + tq), BF16)]
    return pl.pallas_call(
        functools.partial(_attn_prompt_body, kvh=kvh, group=group),
        out_shape=jax.ShapeDtypeStruct((b, dq, s), BF16),
        grid=(b, s // tq),
        in_specs=[
            pl.BlockSpec((None, dq, tq), lambda bi, i: (bi, 0, i)),
            pl.BlockSpec((None, WINDOW, dk), lambda bi, i: (bi, prev(i), 0)),
            pl.BlockSpec((None, tq, dk), lambda bi, i: (bi, i, 0)),
            pl.BlockSpec((None, dk, WINDOW), lambda bi, i: (bi, 0, prev(i))),
            pl.BlockSpec((None, dk, tq), lambda bi, i: (bi, 0, i)),
            pl.BlockSpec(bias.shape, lambda bi, i: (0, 0, 0)),
            pl.BlockSpec(sink_row.shape, lambda bi, i: (0, 0, 0)),
        ],
        out_specs=pl.BlockSpec((None, dq, tq), lambda bi, i: (bi, 0, i)),
        scratch_shapes=[pltpu.VMEM((kvh, WINDOW + tq, HEAD_DIM), BF16),
                        pltpu.VMEM((dk, WINDOW + tq), BF16)],
        compiler_params=pltpu.CompilerParams(
            dimension_semantics=("parallel", "arbitrary"),
            vmem_limit_bytes=_vmem_limit(blocks, scratch)),
        name="swa_sink_attention_prompt",
    )(qT, k, k, vT, vT, bias, sink_row)


ATTN_SAMPLE_BATCH = 4


def _attn_sample_body(q_ref, k_ref, v_ref, bias_ref, sink_ref, o_ref, *, kvh, group):
    hd = HEAD_DIM
    units = [(b, h) for b in range(q_ref.shape[0]) for h in range(kvh)]

    def scores(b, h):
        qs = _stack_heads(q_ref[b].astype(F32), h, group).astype(BF16)
        return _unit_scores(qs, k_ref[b, :, h * hd:(h + 1) * hd].astype(BF16), bias_ref[h])

    s_next = scores(*units[0])
    outs = []
    for n, (b, h) in enumerate(units):
        s = s_next
        if n + 1 < len(units):
            s_next = scores(*units[n + 1])
        o = _unit_output(s, v_ref[b, :, h * hd:(h + 1) * hd].astype(BF16), sink_ref[h])
        outs += _unstack_heads(o, group)
        if h == kvh - 1:
            o_ref[b] = jnp.concatenate(outs, axis=1).astype(o_ref.dtype)
            outs = []


def _attn_sample_call(q, k_all, v_all, bias, sink_col, kvh, group):
    b, t, dq = q.shape
    nk, dk = k_all.shape[1:]
    bb = math.gcd(b, ATTN_SAMPLE_BATCH)
    blocks = [((bb, t, dq), BF16)] * 2 + [((bb, nk, dk), F32)] * 2 + [
        (bias.shape[:2] + (2 * LANES,), F32), (sink_col.shape[:2] + (LANES,), F32)]
    return pl.pallas_call(
        functools.partial(_attn_sample_body, kvh=kvh, group=group),
        out_shape=jax.ShapeDtypeStruct((b, t, dq), BF16),
        grid=(b // bb,),
        in_specs=[
            pl.BlockSpec((bb, t, dq), lambda bi: (bi, 0, 0)),
            pl.BlockSpec((bb, nk, dk), lambda bi: (bi, 0, 0)),
            pl.BlockSpec((bb, nk, dk), lambda bi: (bi, 0, 0)),
            pl.BlockSpec(bias.shape, lambda bi: (0, 0, 0)),
            pl.BlockSpec(sink_col.shape, lambda bi: (0, 0, 0)),
        ],
        out_specs=pl.BlockSpec((bb, t, dq), lambda bi: (bi, 0, 0)),
        compiler_params=pltpu.CompilerParams(
            dimension_semantics=("parallel",), vmem_limit_bytes=_vmem_limit(blocks)),
        name="swa_sink_attention_sample",
    )(q, k_all, v_all, bias, sink_col)


def _trunks(xp3, xs3, mods_p, mods_s, state_conv, win_k, win_v, p):
    bp, tp, d = xp3.shape
    bs, ts, _ = xs3.shape
    mp, ms = bp * tp, bs * ts
    depth = p["w_mod"].shape[0]
    d_ff = p["w_gu"].shape[2] // 2
    n_heads = p["attn_sinks"].shape[1]
    dq = n_heads * HEAD_DIM
    kvh = (p["w_qkv"].shape[2] - dq) // (2 * HEAD_DIM)
    group = n_heads // kvh
    dkv = kvh * HEAD_DIM
    kw = p["w_dw"].shape[1]
    n_buckets = p["rel_bias_table"].shape[1]
    conv_p, k_p, v_p, conv_s, k_s, v_s = [], [], [], [], [], []

    xp, xs = xp3.reshape(mp, d), xs3.reshape(ms, d)
    hp = _norm_mod_call(xp, p["norm_mix"][0], mods_p[0], 0, 1, PANEL_ROWS)
    hs = _norm_mod_call(xs, p["norm_mix"][0], mods_s[0], 0, 1, ms)
    for i in range(depth):
        j = i // 2
        ffn_norm = lambda mods: (p["norm_ffn"][i], mods[i], 3, 4)
        if i % 2 == 0:
            u_p, u_s, wq = _mm_call(hp, p["w_pw1"], j, (0, d), d, x2=hs, cast=(p["w_pw2"], j),
                                    bias=p["b_pw1"], act="glu", name="pw1_glu")
            u_p, u_s = u_p.reshape(bp, tp, d), u_s.reshape(bs, ts, d)
            conv_w = (p["w_dw"], p["b_dw"], p["conv_ln_g"], p["conv_ln_b"], j)
            z_p = _conv_prompt_call(u_p, *conv_w)
            conv_p.append(u_p[:, tp - (kw - 1):])
            pad = (-(kw - 1 + ts)) % SUBLANES
            up = jnp.concatenate([jnp.zeros((bs, pad, d), F32), state_conv[j], u_s], axis=1)
            z_s = _conv_sample_call(up, pad, ts, *conv_w)
            conv_s.append(up[:, pad + ts:])
            xp, hp = _rowmm_call(z_p.reshape(mp, d), wq, xp, mods_p[i], 2, ffn_norm(mods_p),
                                 bias=p["b_pw2"][j], name="pw2_residual_norm")
            xs, hs = _rowmm_call(z_s.reshape(ms, d), wq, xs, mods_s[i], 2, ffn_norm(mods_s),
                                 bias=p["b_pw2"][j], name="pw2_residual_norm")
        else:
            sinks = p["attn_sinks"][j].reshape(kvh, group, 1)
            qT, q_s, wq = _mm_call(hp, p["w_qkv"], j, (0,), dq, x2=hs, cast=(p["w_o"], j),
                                   bias=p["b_qkv"], scale=LOG2E * HEAD_DIM ** -0.5,
                                   out_dtype=BF16, tn=Q_PANEL_COLS, name="q_proj",
                                   out_mode="t", seq=tp)
            (k, vT), kv_s = _mm_call(hp, p["w_qkv"], j, (dq,), 2 * dkv, x2=hs, bias=p["b_qkv"],
                                     tn=2 * dkv, name="kv_proj", out_mode="split_t", seq=tp)
            k = k.reshape(bp, tp, dkv)
            bias = _bias_call(_pair_codes(n_buckets), p["rel_bias_table"], group)
            sink_row = jnp.repeat(sinks, PAIR, axis=2).reshape(kvh, 1, group * PAIR)
            oT = _attn_prompt_call(qT, k, vT, bias, sink_row, kvh, group)
            k_p.append(k[:, tp - WINDOW:].reshape(bp, WINDOW, kvh, HEAD_DIM))
            v_p.append(jnp.swapaxes(vT[:, :, tp - WINDOW:], 1, 2)
                       .reshape(bp, WINDOW, kvh, HEAD_DIM))
            kv3 = kv_s.reshape(bs, ts, 2, kvh, HEAD_DIM)
            k_all = jnp.concatenate([win_k[j], kv3[:, :, 0]], axis=1)
            v_all = jnp.concatenate([win_v[j], kv3[:, :, 1]], axis=1)
            n_keys = WINDOW + ts
            bias = _bias_call(_bucket_codes(ts, n_keys, -WINDOW, n_buckets),
                              p["rel_bias_table"]).reshape(kvh, group * ts, n_keys)
            o_s = _attn_sample_call(q_s.reshape(bs, ts, dq), k_all.reshape(bs, n_keys, dkv),
                                    v_all.reshape(bs, n_keys, dkv), bias,
                                    jnp.repeat(sinks, ts, axis=1), kvh, group)
            k_s.append(k_all[:, ts:])
            v_s.append(v_all[:, ts:])
            xp, hp = _rowmm_call(oT, wq, xp, mods_p[i], 2, ffn_norm(mods_p), bias=p["b_o"][j],
                                 x_t=True, name="wo_residual_norm")
            xs, hs = _rowmm_call(o_s.reshape(ms, dq), wq, xs, mods_s[i], 2, ffn_norm(mods_s),
                                 bias=p["b_o"][j], name="wo_residual_norm")
        a_p, a_s, wq = _mm_call(hp, p["w_gu"], i, (0, d_ff), d_ff, x2=hs, cast=(p["w_down"], i),
                                act="swiglu", out_dtype=BF16, name="ffn_gate_up")
        if i + 1 < depth:
            nxt = lambda mods: (p["norm_mix"][i + 1], mods[i + 1], 0, 1)
            xp, hp = _rowmm_call(a_p, wq, xp, mods_p[i], 5, nxt(mods_p),
                                 name="ffn_down_residual_norm")
            xs, hs = _rowmm_call(a_s, wq, xs, mods_s[i], 5, nxt(mods_s),
                                 name="ffn_down_residual_norm")
        else:
            y_p = _rowmm_call(a_p, wq, xp, mods_p[i], 5, (p["norm_out"],),
                              name="ffn_down_residual_final_norm")
            y_s = _rowmm_call(a_s, wq, xs, mods_s[i], 5, (p["norm_out"],),
                              name="ffn_down_residual_final_norm")
    stack = jnp.stack
    return (y_p.reshape(bp, tp, d), y_s.reshape(bs, ts, d), stack(conv_p), stack(k_p), stack(v_p),
            stack(conv_s), stack(k_s), stack(v_s))


def kernel(x_prompt, x_sample, c_prompt, c_sample, state_conv, cache_win_k, cache_win_v, w_mod, b_mod, norm_mix, norm_ffn, w_pw1, b_pw1, w_dw, b_dw, conv_ln_g, conv_ln_b, w_pw2, b_pw2, w_qkv, b_qkv, w_o, b_o, attn_sinks, rel_bias_table, w_gu, w_down, norm_out):
    p = dict(w_mod=w_mod, b_mod=b_mod, norm_mix=norm_mix, norm_ffn=norm_ffn, w_pw1=w_pw1,
             b_pw1=b_pw1, w_dw=w_dw, b_dw=b_dw, conv_ln_g=conv_ln_g, conv_ln_b=conv_ln_b,
             w_pw2=w_pw2, b_pw2=b_pw2, w_qkv=w_qkv, b_qkv=b_qkv, w_o=w_o, b_o=b_o,
             attn_sinks=attn_sinks, rel_bias_table=rel_bias_table, w_gu=w_gu, w_down=w_down,
             norm_out=norm_out)
    bp, sp, d = x_prompt.shape
    bs, ts, _ = x_sample.shape
    depth = w_mod.shape[0]

    n_c = bp + bs
    c_all = jnp.concatenate(
        [c_prompt, c_sample, jnp.zeros((-n_c % BF16_SUBLANES, d), F32)], axis=0)
    mod_all = _mod_call(c_all, w_mod, b_mod)

    mods_p = [_Mod(mod_all[l, :bp].reshape(bp, 1, 6 * d), sp) for l in range(depth)]
    mods_s = [_Mod(mod_all[l, bp:n_c].reshape(bs, 1, 6 * d), ts) for l in range(depth)]

    return _trunks(x_prompt, x_sample, mods_p, mods_s, state_conv, cache_win_k, cache_win_v, p)
```

```python
import functools
import math

import numpy as np
import jax
import jax.numpy as jnp
from jax import lax
from jax.experimental import pallas as pl
from jax.experimental.pallas import tpu as pltpu

F32 = jnp.float32
BF16 = jnp.bfloat16

CHUNK = 64
WINDOW = 128
HEAD_DIM = 64
MAX_DISTANCE = 128
EPS = 1e-6
LOG2E = math.log2(math.e)

V7X_VMEM_BYTES = 64 * 2**20
V7X_VMEM_CAP = V7X_VMEM_BYTES - 8 * 2**20
SUBLANES = 8
LANES = 128
BF16_SUBLANES = 2 * SUBLANES

PANEL_ROWS = 1024
PANEL_COLS = 512
Q_PANEL_COLS = 1024
ROWMM_ROWS = 512
CONV_ROWS = 128
CONV_SAMPLE_BATCH = 4
MOD_COLS = 1024


def _nbytes(shape, dtype):
    return math.prod(shape) * jnp.dtype(dtype).itemsize


def _vmem_limit(blocks, scratch=()):
    est = 2 * sum(_nbytes(s, d) for s, d in blocks) + sum(_nbytes(s, d) for s, d in scratch)
    return int(min(V7X_VMEM_CAP, max(32 * 2**20, 2 * est)))


def _tile(dim, pref, mult=LANES):
    if dim <= pref:
        return dim
    t = (pref // mult) * mult
    while t >= mult:
        if dim % t == 0:
            return t
        t -= mult
    return dim


def _sigmoid(x):
    return 1.0 / (1.0 + jnp.exp(-x))


def _mod_body(c_ref, w_ref, b_ref, o_ref):
    c = c_ref[...]
    a = (c * _sigmoid(c)).astype(BF16)
    o_ref[...] = jnp.dot(a, w_ref[...].astype(BF16), preferred_element_type=F32) + b_ref[...]


def _mod_call(c_all, w_mod, b_mod):
    depth, d, n = w_mod.shape
    mp = c_all.shape[0]
    tn = _tile(n, MOD_COLS)
    blocks = [((mp, d), F32), ((d, tn), F32), ((1, tn), F32), ((mp, tn), F32)]
    return pl.pallas_call(
        _mod_body,
        out_shape=jax.ShapeDtypeStruct((depth, mp, n), F32),
        grid=(depth, n // tn),
        in_specs=[
            pl.BlockSpec((mp, d), lambda l, j: (0, 0)),
            pl.BlockSpec((None, d, tn), lambda l, j: (l, 0, j)),
            pl.BlockSpec((None, 1, tn), lambda l, j: (l, 0, j)),
        ],
        out_specs=pl.BlockSpec((None, mp, tn), lambda l, j: (l, 0, j)),
        compiler_params=pltpu.CompilerParams(
            dimension_semantics=("arbitrary", "arbitrary"),
            vmem_limit_bytes=_vmem_limit(blocks, [((d, tn), BF16)])),
        name="adaln_mod",
    )(c_all, w_mod, b_mod.reshape(depth, 1, n))


class _Mod:
    def __init__(self, arr, rows_per_group):
        self.arr = arr
        self.rows_per_group = rows_per_group

    def tile_rows(self, total_rows, pref, mult):
        if self.rows_per_group >= pref:
            return _tile(self.rows_per_group, pref, mult)
        return _tile(total_rows, pref, max(mult, self.rows_per_group))

    def spec(self, seg, d, tm):
        if tm <= self.rows_per_group:
            per = self.rows_per_group // tm
            return pl.BlockSpec((1, 1, d), lambda i: (i // per, 0, seg))
        return pl.BlockSpec((tm // self.rows_per_group, 1, d), lambda i: (i, 0, seg))


def _grouped(v, m):
    return v.reshape(m.shape[0], v.shape[0] // m.shape[0], v.shape[1])


def _norm_mod_body(x_ref, g_ref, sh_ref, sc_ref, o_ref):
    x = x_ref[...]
    y = x * lax.rsqrt(jnp.mean(x * x, axis=-1, keepdims=True) + EPS) * g_ref[...]
    sc = sc_ref[...]
    o_ref[...] = (_grouped(y, sc) * (1.0 + sc) + sh_ref[...]).reshape(x.shape).astype(o_ref.dtype)


def _norm_mod_call(x, g, mod, seg_shift, seg_scale, tm):
    m, d = x.shape
    tm = mod.tile_rows(m, tm, SUBLANES)
    blocks = [((tm, d), F32), ((1, d), F32), ((tm, d), BF16)]
    return pl.pallas_call(
        _norm_mod_body,
        out_shape=jax.ShapeDtypeStruct((m, d), BF16),
        grid=(m // tm,),
        in_specs=[
            pl.BlockSpec((tm, d), lambda i: (i, 0)),
            pl.BlockSpec((1, d), lambda i: (0, 0)),
            mod.spec(seg_shift, d, tm),
            mod.spec(seg_scale, d, tm),
        ],
        out_specs=pl.BlockSpec((tm, d), lambda i: (i, 0)),
        compiler_params=pltpu.CompilerParams(
            dimension_semantics=("parallel",), vmem_limit_bytes=_vmem_limit(blocks)),
        name="rmsnorm_modulate",
    )(x, g.reshape(1, d), mod.arr, mod.arr)


def _mm_body(x_ref, *refs, n_w, has_bias, act, scale, out_mode, has_x2, has_cast):
    if has_x2:
        x2_ref, refs = refs[0], refs[1:]
    ws = refs[:n_w]
    refs = refs[n_w:]
    bs = refs[:n_w] if has_bias else ()
    refs = refs[len(bs):]
    if has_cast:
        wc_ref, refs = refs[0], refs[1:]
    n_o = 2 if out_mode == "split_t" else 1
    o_refs = refs[:n_o]
    refs = refs[n_o:]
    if has_x2:
        o2_ref, refs = refs[0], refs[1:]
    if has_cast:
        oc_ref, refs = refs[0], refs[1:]
        oc_ref[...] = wc_ref[...].astype(BF16)
    wbs = refs

    def compute(x):
        ps = [None] * n_w
        for i in ((1, 0) if act == "glu" else range(n_w)):
            p = jnp.dot(x, wbs[i][...], preferred_element_type=F32)
            if has_bias:
                p = p + bs[i][...]
            ps[i] = p
        if act == "glu":
            y = ps[0] * _sigmoid(ps[1])
        elif act == "swiglu":
            y = ps[0] * _sigmoid(ps[0]) * ps[1]
        else:
            y = ps[0]
        return y * scale if scale != 1.0 else y

    @pl.when(pl.program_id(1) == 0)
    def _():
        for w, wb in zip(ws, wbs):
            wb[...] = w[...].astype(BF16)
        if has_x2:
            o2_ref[...] = compute(x2_ref[...]).astype(o2_ref.dtype)

    y = compute(x_ref[...])
    if out_mode == "plain":
        o_refs[0][...] = y.astype(o_refs[0].dtype)
    elif out_mode == "t":
        o_refs[0][...] = y.T.astype(o_refs[0].dtype)
    else:
        half = y.shape[1] // 2
        o_refs[0][...] = y[:, :half].astype(o_refs[0].dtype)
        o_refs[1][...] = y[:, half:].T.astype(o_refs[1].dtype)


def _mm_call(x, w, layer, col_starts, n_out, *, x2=None, cast=None, bias=None, act=None,
             scale=1.0, out_dtype=F32, tm=PANEL_ROWS, tn=PANEL_COLS, name="matmul",
             out_mode="plain", seq=None):
    m, k = x.shape
    if out_mode == "plain":
        tm = _tile(m, tm, SUBLANES)
    else:
        tm = _tile(seq, tm, LANES)
    tn = _tile(n_out, tn)
    n_w = len(col_starts)
    assert all(c % tn == 0 for c in col_starts) and m % tm == 0 and n_out % tn == 0
    has_bias = bias is not None
    per_b = None if seq is None else seq // tm

    in_specs = [pl.BlockSpec((tm, k), lambda j, i: (i, 0))]
    args = [x]
    blocks = [((tm, k), x.dtype), ((tm, tn), out_dtype)]
    if x2 is not None:
        m2 = x2.shape[0]
        in_specs.append(pl.BlockSpec((m2, k), lambda j, i: (0, 0)))
        args.append(x2)
        blocks += [((m2, k), x2.dtype), ((m2, tn), out_dtype)]
    for c in col_starts:
        off = c // tn
        in_specs.append(pl.BlockSpec((None, k, tn), lambda j, i, off=off: (layer, 0, off + j)))
        args.append(w)
        blocks.append(((k, tn), F32))
    if has_bias:
        b3 = bias.reshape(bias.shape[0], 1, bias.shape[1])
        for c in col_starts:
            off = c // tn
            in_specs.append(pl.BlockSpec((None, 1, tn), lambda j, i, off=off: (layer, 0, off + j)))
            args.append(b3)
    n_i = m // tm
    if cast is not None:
        w_other, layer_other = cast
        kc, dc = w_other.shape[1:]
        slab = kc // ((n_out // tn) * n_i)
        assert slab * (n_out // tn) * n_i == kc and slab % BF16_SUBLANES == 0
        in_specs.append(pl.BlockSpec((None, slab, dc), lambda j, i: (layer_other, j * n_i + i, 0)))
        args.append(w_other)
        blocks += [((slab, dc), F32), ((slab, dc), BF16)]
    scratch = [((k, tn), BF16)] * n_w
    body = functools.partial(_mm_body, n_w=n_w, has_bias=has_bias, act=act, scale=scale,
                             out_mode=out_mode, has_x2=x2 is not None, has_cast=cast is not None)
    if out_mode == "plain":
        out_shape = [jax.ShapeDtypeStruct((m, n_out), out_dtype)]
        out_specs = [pl.BlockSpec((tm, tn), lambda j, i: (i, j))]
    elif out_mode == "t":
        out_shape = [jax.ShapeDtypeStruct((m // seq, n_out, seq), out_dtype)]
        out_specs = [pl.BlockSpec((None, tn, tm), lambda j, i: (i // per_b, j, i % per_b))]
    else:
        assert tn == n_out
        half = n_out // 2
        out_shape = [jax.ShapeDtypeStruct((m, half), out_dtype),
                     jax.ShapeDtypeStruct((m // seq, half, seq), out_dtype)]
        out_specs = [pl.BlockSpec((tm, half), lambda j, i: (i, 0)),
                     pl.BlockSpec((None, half, tm), lambda j, i: (i // per_b, 0, i % per_b))]
    if x2 is not None:
        out_shape.append(jax.ShapeDtypeStruct((m2, n_out), out_dtype))
        out_specs.append(pl.BlockSpec((m2, tn), lambda j, i: (0, j)))
    if cast is not None:
        out_shape.append(jax.ShapeDtypeStruct((kc, dc), BF16))
        out_specs.append(pl.BlockSpec((slab, dc), lambda j, i: (j * n_i + i, 0)))
    outs = pl.pallas_call(
        body,
        out_shape=out_shape,
        grid=(n_out // tn, n_i),
        in_specs=in_specs,
        out_specs=out_specs,
        scratch_shapes=[pltpu.VMEM(s, d) for s, d in scratch],
        compiler_params=pltpu.CompilerParams(
            dimension_semantics=("arbitrary", "arbitrary"),
            vmem_limit_bytes=_vmem_limit(blocks, scratch)),
        name=name,
    )(*args)
    n_primary = 2 if out_mode == "split_t" else 1
    primary = outs[0] if n_primary == 1 else outs[:n_primary]
    extras = list(outs[n_primary:])
    return (primary, *extras) if extras else primary


ROWMM_SUB_ROWS = 256
ROWMM_K_CHUNKS = 4
MXU_K = 256


def _rowmm_body(*refs, has_bias, x_t, final, single_step):
    x_ref, w_hbm = refs[:2]
    refs = refs[2:]
    if has_bias:
        b_ref, refs = refs[0], refs[1:]
    res_ref, gate_ref, ng_ref = refs[:3]
    refs = refs[3:]
    if not final:
        sh_ref, sc_ref = refs[:2]
        refs = refs[2:]
    n_o = 1 if final else 2
    o_refs = refs[:n_o]
    refs = refs[n_o:]
    w_res, sem = refs
    k = w_res.shape[0]
    tiles = k // MXU_K
    n_chunks = min(ROWMM_K_CHUNKS, tiles)
    bounds = [MXU_K * (tiles * c // n_chunks) for c in range(n_chunks + 1)]

    def chunk_copy(c):
        rows = pl.ds(bounds[c], bounds[c + 1] - bounds[c])
        return pltpu.make_async_copy(w_hbm.at[rows, :], w_res.at[rows, :], sem.at[c])

    tm = res_ref.shape[0]
    sub = min(tm, ROWMM_SUB_ROWS)

    def lhs(r0, k0=0, k1=k):
        return x_ref[k0:k1, r0:r0 + sub].T if x_t else x_ref[r0:r0 + sub, k0:k1]

    def groups_of(ref, r0):
        per = tm // ref.shape[0]
        return ref[...] if per >= tm else ref[r0 // per:(r0 + sub) // per]

    def finish(r0, y):
        if has_bias:
            y = y + b_ref[...]
        gate = groups_of(gate_ref, r0)
        xn = (_grouped(res_ref[r0:r0 + sub, :], gate) + gate * _grouped(y, gate)).reshape(y.shape)
        r = xn * lax.rsqrt(jnp.mean(xn * xn, axis=-1, keepdims=True) + EPS) * ng_ref[...]
        if final:
            o_refs[0][r0:r0 + sub, :] = r
        else:
            sc = groups_of(sc_ref, r0)
            o_refs[0][r0:r0 + sub, :] = xn
            o_refs[1][r0:r0 + sub, :] = (
                _grouped(r, sc) * (1.0 + sc) + groups_of(sh_ref, r0)
            ).reshape(y.shape).astype(o_refs[1].dtype)

    for c in range(n_chunks):
        @pl.when(pl.program_id(0) == 0)
        def _():
            chunk_copy(c).start()

    first = 0
    if single_step:
        y = None
        for c in range(n_chunks):
            chunk_copy(c).wait()
            part = jnp.dot(lhs(0, bounds[c], bounds[c + 1]), w_res[bounds[c]:bounds[c + 1], :],
                           preferred_element_type=F32)
            y = part if y is None else y + part
        finish(0, y)
        first = sub
    else:
        for c in range(n_chunks):
            @pl.when(pl.program_id(0) == 0)
            def _():
                chunk_copy(c).wait()

    for r0 in range(first, tm, sub):
        finish(r0, jnp.dot(lhs(r0), w_res[...], preferred_element_type=F32))


def _rowmm_call(x, w, res, gate, gate_seg, nxt, *, bias=None, x_t=False, tm=ROWMM_ROWS,
                name="rowmm"):
    if x_t:
        nb, k, seq = x.shape
        m = nb * seq
    else:
        m, k = x.shape
    d = w.shape[-1]
    final = len(nxt) == 1
    scratch = [((k, d), BF16)]
    n_mods = 1 if final else 3

    def vmem_need(rows):
        blocks = [((rows, k), BF16), ((rows, d), F32), ((rows, d), F32 if final else BF16)]
        blocks += [((rows, d), F32)] * (0 if final else 1)
        blocks += [((-(-rows // gate.rows_per_group) * SUBLANES, d), F32)] * n_mods
        temporaries = 4 * _nbytes((rows, d), F32)
        return (2 * sum(_nbytes(s, t) for s, t in blocks) + sum(_nbytes(s, t) for s, t in scratch)
                + temporaries)

    mult = LANES if x_t else SUBLANES
    tm = gate.tile_rows(m, tm, mult)
    rpg = gate.rows_per_group
    fits_groups = lambda t: t % mult == 0 and (rpg % t == 0 or t % rpg == 0)
    while vmem_need(tm) > V7X_VMEM_CAP and tm % 2 == 0 and fits_groups(tm // 2):
        tm //= 2
    assert final or nxt[1].rows_per_group == gate.rows_per_group
    has_bias = bias is not None
    row = lambda i: (i, 0)
    if x_t:
        per_b = seq // tm
        in_specs = [pl.BlockSpec((None, k, tm), lambda i: (i // per_b, 0, i % per_b))]
    else:
        in_specs = [pl.BlockSpec((tm, k), row)]
    in_specs.append(pl.BlockSpec(memory_space=pltpu.HBM))
    args = [x, w]
    if has_bias:
        in_specs.append(pl.BlockSpec((1, d), lambda i: (0, 0)))
        args.append(bias.reshape(1, d))
    in_specs += [pl.BlockSpec((tm, d), row), gate.spec(gate_seg, d, tm),
                 pl.BlockSpec((1, d), lambda i: (0, 0))]
    args += [res, gate.arr, nxt[0].reshape(1, d)]
    if final:
        out_shape = jax.ShapeDtypeStruct((m, d), F32)
        out_specs = pl.BlockSpec((tm, d), row)
    else:
        _, nmod, seg_shift, seg_scale = nxt
        in_specs += [nmod.spec(seg_shift, d, tm), nmod.spec(seg_scale, d, tm)]
        args += [nmod.arr, nmod.arr]
        out_shape = [jax.ShapeDtypeStruct((m, d), F32), jax.ShapeDtypeStruct((m, d), BF16)]
        out_specs = [pl.BlockSpec((tm, d), row), pl.BlockSpec((tm, d), row)]
    return pl.pallas_call(
        functools.partial(_rowmm_body, has_bias=has_bias, x_t=x_t, final=final,
                          single_step=m == tm),
        out_shape=out_shape,
        grid=(m // tm,),
        in_specs=in_specs,
        out_specs=out_specs,
        scratch_shapes=[pltpu.VMEM(s, dt) for s, dt in scratch]
        + [pltpu.SemaphoreType.DMA((ROWMM_K_CHUNKS,))],
        compiler_params=pltpu.CompilerParams(
            dimension_semantics=("arbitrary",),
            vmem_limit_bytes=min(V7X_VMEM_CAP, max(32 * 2**20, vmem_need(tm) + 4 * 2**20))),
        name=name,
    )(*args)


def _conv_ln_swish(up_ref, pad, tt, w_ref, bdw_ref, lg_ref, lb_ref, acc_ref, o_ref):
    kw, d = w_ref.shape
    by_shift = [[(a, SUBLANES * a + s - pad) for a in range((pad + kw - 1) // SUBLANES + 1)
                 if 0 <= SUBLANES * a + s - pad < kw] for s in range(SUBLANES)]

    def strip(c, carry):
        cols = pl.ds(pl.multiple_of(c * LANES, LANES), LANES)
        z = bdw_ref[:, cols]
        for s, taps in enumerate(by_shift):
            n = tt + SUBLANES if s else tt
            q = None
            for a, k in taps:
                term = up_ref[pl.ds(SUBLANES * a, n), cols] * w_ref[pl.ds(k, 1), cols]
                q = term if q is None else q + term
            if q is not None:
                z = z + q[s:s + tt]
        acc_ref[:, cols] = z
        return carry

    lax.fori_loop(0, d // LANES, strip, 0)
    z = acc_ref[...]
    mu = jnp.mean(z, axis=-1, keepdims=True)
    zc = z - mu
    y = zc * lax.rsqrt(jnp.mean(zc * zc, axis=-1, keepdims=True) + EPS)
    y = y * lg_ref[...] + lb_ref[...]
    o_ref[...] = (y * _sigmoid(y)).astype(o_ref.dtype)


def _conv_prompt_body(main_ref, halo_ref, w_ref, bdw_ref, lg_ref, lb_ref, o_ref, up_ref, acc_ref,
                      *, halo, tt):
    kw = w_ref.shape[0]

    @pl.when(pl.program_id(1) == 0)
    def _():
        up_ref[0:halo, :] = jnp.zeros((halo, up_ref.shape[1]), F32)

    @pl.when(pl.program_id(1) > 0)
    def _():
        up_ref[0:halo, :] = halo_ref[...]

    up_ref[halo:halo + tt, :] = main_ref[...]
    _conv_ln_swish(up_ref, halo - (kw - 1), tt, w_ref, bdw_ref, lg_ref, lb_ref, acc_ref, o_ref)


def _conv_prompt_call(u, w_dw, b_dw, ln_g, ln_b, layer, tt=CONV_ROWS):
    b, t, d = u.shape
    kw = w_dw.shape[1]
    halo = -(-(kw - 1) // SUBLANES) * SUBLANES
    tt = _tile(t, tt, halo)
    hb = tt // halo
    vec = lambda: pl.BlockSpec((None, 1, d), lambda bi, ti: (layer, 0, 0))
    blocks = [((tt, d), F32), ((halo, d), F32), ((kw, d), F32), ((tt, d), BF16)]
    scratch = [((halo + tt, d), F32), ((tt, d), F32)]
    return pl.pallas_call(
        functools.partial(_conv_prompt_body, halo=halo, tt=tt),
        out_shape=jax.ShapeDtypeStruct((b, t, d), BF16),
        grid=(b, t // tt),
        in_specs=[
            pl.BlockSpec((None, tt, d), lambda bi, ti: (bi, ti, 0)),
            pl.BlockSpec((None, halo, d), lambda bi, ti: (bi, jnp.maximum(ti * hb - 1, 0), 0)),
            pl.BlockSpec((None, kw, d), lambda bi, ti: (layer, 0, 0)),
            vec(), vec(), vec(),
        ],
        out_specs=pl.BlockSpec((None, tt, d), lambda bi, ti: (bi, ti, 0)),
        scratch_shapes=[pltpu.VMEM(s, dt) for s, dt in scratch],
        compiler_params=pltpu.CompilerParams(
            dimension_semantics=("parallel", "arbitrary"),
            vmem_limit_bytes=_vmem_limit(blocks, scratch)),
        name="dwconv_ln_swish_prompt",
    )(u, u, w_dw, b_dw.reshape(-1, 1, d), ln_g.reshape(-1, 1, d), ln_b.reshape(-1, 1, d))


def _conv_sample_body(up_ref, w_ref, bdw_ref, lg_ref, lb_ref, o_ref, acc_ref, *, pad, tt):
    for b in range(up_ref.shape[0]):
        _conv_ln_swish(up_ref.at[b], pad, tt, w_ref, bdw_ref, lg_ref, lb_ref, acc_ref, o_ref.at[b])


def _conv_sample_call(up, pad, tt, w_dw, b_dw, ln_g, ln_b, layer):
    b, rows, d = up.shape
    kw = w_dw.shape[1]
    bb = math.gcd(b, CONV_SAMPLE_BATCH)
    vec = lambda: pl.BlockSpec((None, 1, d), lambda bi: (layer, 0, 0))
    blocks = [((bb, rows, d), F32), ((kw, d), F32), ((bb, tt, d), BF16)]
    scratch = [((tt, d), F32)]
    return pl.pallas_call(
        functools.partial(_conv_sample_body, pad=pad, tt=tt),
        out_shape=jax.ShapeDtypeStruct((b, tt, d), BF16),
        grid=(b // bb,),
        in_specs=[
            pl.BlockSpec((bb, rows, d), lambda bi: (bi, 0, 0)),
            pl.BlockSpec((None, kw, d), lambda bi: (layer, 0, 0)),
            vec(), vec(), vec(),
        ],
        out_specs=pl.BlockSpec((bb, tt, d), lambda bi: (bi, 0, 0)),
        scratch_shapes=[pltpu.VMEM(s, dt) for s, dt in scratch],
        compiler_params=pltpu.CompilerParams(
            dimension_semantics=("parallel",), vmem_limit_bytes=_vmem_limit(blocks, scratch)),
        name="dwconv_ln_swish_sample",
    )(up, w_dw, b_dw.reshape(-1, 1, d), ln_g.reshape(-1, 1, d), ln_b.reshape(-1, 1, d))


def _bucket_codes(n_q, n_k, k_off, n_buckets):
    rel = (np.arange(n_k) + k_off)[None, :] - np.arange(n_q)[:, None]
    nb = n_buckets // 2
    max_exact = nb // 2
    ret = np.where(rel > 0, nb, 0)
    n = np.abs(rel)
    nf = np.maximum(n, 1).astype(np.float32)
    large = max_exact + (np.log(nf / np.float32(max_exact))
                         / np.float32(math.log(MAX_DISTANCE / max_exact))
                         * np.float32(nb - max_exact)).astype(np.int32)
    large = np.minimum(large, nb - 1)
    return (ret + np.where(n < max_exact, n, large)).astype(np.int32)


def _bias_body(code_ref, table_ref, o_ref, *, present, group):
    n_cols = code_ref.shape[1]

    used = sorted({b for _, buckets in present for b in buckets})

    def head(hh, carry):
        value = {b: table_ref[hh, b] * LOG2E for b in used}
        for r0, buckets in present:
            rows = slice(r0, r0 + SUBLANES)
            code = code_ref[rows, :]
            out = jnp.full(code.shape, -jnp.inf, F32)
            for b in buckets:
                out = jnp.where(code == b, value[b], out)
            if group is None:
                o_ref[hh, rows, :] = out
            else:
                cols = pl.ds(pl.multiple_of((hh % group) * n_cols, LANES), n_cols)
                o_ref[hh // group, rows, cols] = out
        return carry

    lax.fori_loop(0, table_ref.shape[0], head, 0)


def _bias_call(code, table, group=None):
    n_heads, n_buckets = table.shape
    rows, cols = code.shape
    shape = (n_heads, rows, cols) if group is None else (n_heads // group, rows, group * cols)
    present = tuple(
        (r0, tuple(int(b) for b in np.unique(code[r0:r0 + SUBLANES]) if b < n_buckets))
        for r0 in range(0, rows, SUBLANES))
    return pl.pallas_call(
        functools.partial(_bias_body, present=present, group=group),
        out_shape=jax.ShapeDtypeStruct(shape, F32),
        in_specs=[pl.BlockSpec(memory_space=pltpu.VMEM), pl.BlockSpec(memory_space=pltpu.SMEM)],
        out_specs=pl.BlockSpec(memory_space=pltpu.VMEM),
        name="rel_bias",
    )(jnp.asarray(code), table)


def _unit_scores(qs, kh, bias):
    return lax.dot_general(qs, kh, (((1,), (1,)), ((), ())), preferred_element_type=F32) + bias


def _unit_output(s, vh, sink_col):
    sink_col = sink_col * LOG2E
    m = jnp.maximum(jnp.max(s, axis=-1, keepdims=True), sink_col)
    e = jnp.exp2(s - m)
    l = jnp.sum(e, axis=-1, keepdims=True) + jnp.exp2(sink_col - m)
    return jnp.dot(e.astype(BF16), vh, preferred_element_type=F32) * (1.0 / l)


def _stack_heads(q, h, group):
    hd = HEAD_DIM
    return jnp.concatenate(
        [q[:, (h * group + g) * hd:(h * group + g + 1) * hd] for g in range(group)], axis=0)


def _unstack_heads(o, group):
    tq = o.shape[0] // group
    return [o[g * tq:(g + 1) * tq, :] for g in range(group)]


ATTN_Q_COLS = 1024
PAIR = 2 * CHUNK


def _pair_codes(n_buckets):
    code = _bucket_codes(PAIR, WINDOW + PAIR, -WINDOW, n_buckets).T
    key_chunk = np.arange(WINDOW + PAIR)[:, None] // CHUNK
    q_chunk = np.arange(PAIR)[None, :] // CHUNK
    visible = (key_chunk >= q_chunk) & (key_chunk <= q_chunk + WINDOW // CHUNK)
    return np.where(visible, code, n_buckets).astype(np.int32)


def _attn_prompt_body(qT_ref, kp_ref, kc_ref, vTp_ref, vTc_ref, bias_ref, sink_ref, oT_ref,
                      k_scr, vT_scr, *, kvh, group):
    i = pl.program_id(1)
    tq = qT_ref.shape[1]
    nk = WINDOW + PAIR
    hd = HEAD_DIM
    for h in range(kvh):
        k_scr[h, 0:WINDOW, :] = kp_ref[:, h * hd:(h + 1) * hd].astype(BF16)
        k_scr[h, WINDOW:WINDOW + tq, :] = kc_ref[:, h * hd:(h + 1) * hd].astype(BF16)
    vT_scr[:, 0:WINDOW] = vTp_ref[...].astype(BF16)
    vT_scr[:, WINDOW:WINDOW + tq] = vTc_ref[...].astype(BF16)
    row = lax.broadcasted_iota(jnp.int32, (nk, group * PAIR), 0)
    start_mask = jnp.where(row < jnp.where(i == 0, WINDOW, 0), -jnp.inf, 0.0).astype(F32)
    units = [(p * PAIR, h) for p in range(tq // PAIR) for h in range(kvh)]

    def scores(c0, h):
        qsT = jnp.concatenate(
            [qT_ref[hh * hd:(hh + 1) * hd, c0:c0 + PAIR] for hh in range(h * group, (h + 1) * group)],
            axis=1)
        s = jnp.dot(k_scr[h, c0:c0 + nk, :], qsT, preferred_element_type=F32) + bias_ref[h]
        return s + start_mask if c0 == 0 else s

    s_next = scores(*units[0])
    for n, (c0, h) in enumerate(units):
        s = s_next
        if n + 1 < len(units):
            s_next = scores(*units[n + 1])
        sink = sink_ref[h] * LOG2E
        m = jnp.maximum(jnp.max(s, axis=0, keepdims=True), sink)
        e = jnp.exp2(s - m)
        l = jnp.sum(e, axis=0, keepdims=True) + jnp.exp2(sink - m)
        oT = jnp.dot(vT_scr[h * hd:(h + 1) * hd, c0:c0 + nk], e.astype(BF16),
                     preferred_element_type=F32) * (1.0 / l)
        for g in range(group):
            hh = h * group + g
            oT_ref[hh * hd:(hh + 1) * hd, c0:c0 + PAIR] = (
                oT[:, g * PAIR:(g + 1) * PAIR].astype(oT_ref.dtype))


def _attn_prompt_call(qT, k, vT, bias, sink_row, kvh, group):
    b, dq, s = qT.shape
    dk = k.shape[2]
    tq = _tile(s, ATTN_Q_COLS, PAIR)
    per = tq // WINDOW
    prev = lambda i: jnp.maximum(i * per - 1, 0)
    blocks = [((dq, tq), BF16)] * 2 + [((WINDOW + tq, dk), F32)] * 2 + [
        (bias.shape, F32), (sink_row.shape[:1] + (SUBLANES, sink_row.shape[2]), F32)]
    scratch = [((kvh, WINDOW + tq, LANES), BF16), ((dk, WINDOW + tq), BF16)]
    return pl.pallas_call(
        functools.partial(_attn_prompt_body, kvh=kvh, group=group),
        out_shape=jax.ShapeDtypeStruct((b, dq, s), BF16),
        grid=(b, s // tq),
        in_specs=[
            pl.BlockSpec((None, dq, tq), lambda bi, i: (bi, 0, i)),
            pl.BlockSpec((None, WINDOW, dk), lambda bi, i: (bi, prev(i), 0)),
            pl.BlockSpec((None, tq, dk), lambda bi, i: (bi, i, 0)),
            pl.BlockSpec((None, dk, WINDOW), lambda bi, i: (bi, 0, prev(i))),
            pl.BlockSpec((None, dk, tq), lambda bi, i: (bi, 0, i)),
            pl.BlockSpec(bias.shape, lambda bi, i: (0, 0, 0)),
            pl.BlockSpec(sink_row.shape, lambda bi, i: (0, 0, 0)),
        ],
        out_specs=pl.BlockSpec((None, dq, tq), lambda bi, i: (bi, 0, i)),
        scratch_shapes=[pltpu.VMEM((kvh, WINDOW + tq, HEAD_DIM), BF16),
                        pltpu.VMEM((dk, WINDOW + tq), BF16)],
        compiler_params=pltpu.CompilerParams(
            dimension_semantics=("parallel", "arbitrary"),
            vmem_limit_bytes=_vmem_limit(blocks, scratch)),
        name="swa_sink_attention_prompt",
    )(qT, k, k, vT, vT, bias, sink_row)


ATTN_SAMPLE_BATCH = 4


def _attn_sample_body(q_ref, k_ref, v_ref, bias_ref, sink_ref, o_ref, *, kvh, group):
    hd = HEAD_DIM
    units = [(b, h) for b in range(q_ref.shape[0]) for h in range(kvh)]

    def scores(b, h):
        qs = _stack_heads(q_ref[b].astype(F32), h, group).astype(BF16)
        return _unit_scores(qs, k_ref[b, :, h * hd:(h + 1) * hd].astype(BF16), bias_ref[h])

    s_next = scores(*units[0])
    outs = []
    for n, (b, h) in enumerate(units):
        s = s_next
        if n + 1 < len(units):
            s_next = scores(*units[n + 1])
        o = _unit_output(s, v_ref[b, :, h * hd:(h + 1) * hd].astype(BF16), sink_ref[h])
        outs += _unstack_heads(o, group)
        if h == kvh - 1:
            o_ref[b] = jnp.concatenate(outs, axis=1).astype(o_ref.dtype)
            outs = []


def _attn_sample_call(q, k_all, v_all, bias, sink_col, kvh, group):
    b, t, dq = q.shape
    nk, dk = k_all.shape[1:]
    bb = math.gcd(b, ATTN_SAMPLE_BATCH)
    blocks = [((bb, t, dq), BF16)] * 2 + [((bb, nk, dk), F32)] * 2 + [
        (bias.shape[:2] + (2 * LANES,), F32), (sink_col.shape[:2] + (LANES,), F32)]
    return pl.pallas_call(
        functools.partial(_attn_sample_body, kvh=kvh, group=group),
        out_shape=jax.ShapeDtypeStruct((b, t, dq), BF16),
        grid=(b // bb,),
        in_specs=[
            pl.BlockSpec((bb, t, dq), lambda bi: (bi, 0, 0)),
            pl.BlockSpec((bb, nk, dk), lambda bi: (bi, 0, 0)),
            pl.BlockSpec((bb, nk, dk), lambda bi: (bi, 0, 0)),
            pl.BlockSpec(bias.shape, lambda bi: (0, 0, 0)),
            pl.BlockSpec(sink_col.shape, lambda bi: (0, 0, 0)),
        ],
        out_specs=pl.BlockSpec((bb, t, dq), lambda bi: (bi, 0, 0)),
        compiler_params=pltpu.CompilerParams(
            dimension_semantics=("parallel",), vmem_limit_bytes=_vmem_limit(blocks)),
        name="swa_sink_attention_sample",
    )(q, k_all, v_all, bias, sink_col)


def _trunks(xp3, xs3, mods_p, mods_s, state_conv, win_k, win_v, p):
    bp, tp, d = xp3.shape
    bs, ts, _ = xs3.shape
    mp, ms = bp * tp, bs * ts
    depth = p["w_mod"].shape[0]
    d_ff = p["w_gu"].shape[2] // 2
    n_heads = p["attn_sinks"].shape[1]
    dq = n_heads * HEAD_DIM
    kvh = (p["w_qkv"].shape[2] - dq) // (2 * HEAD_DIM)
    group = n_heads // kvh
    dkv = kvh * HEAD_DIM
    kw = p["w_dw"].shape[1]
    n_buckets = p["rel_bias_table"].shape[1]
    conv_p, k_p, v_p, conv_s, k_s, v_s = [], [], [], [], [], []

    xp, xs = xp3.reshape(mp, d), xs3.reshape(ms, d)
    hp = _norm_mod_call(xp, p["norm_mix"][0], mods_p[0], 0, 1, PANEL_ROWS)
    hs = _norm_mod_call(xs, p["norm_mix"][0], mods_s[0], 0, 1, ms)
    for i in range(depth):
        j = i // 2
        ffn_norm = lambda mods: (p["norm_ffn"][i], mods[i], 3, 4)
        if i % 2 == 0:
            u_p, u_s, wq = _mm_call(hp, p["w_pw1"], j, (0, d), d, x2=hs, cast=(p["w_pw2"], j),
                                    bias=p["b_pw1"], act="glu", name="pw1_glu")
            u_p, u_s = u_p.reshape(bp, tp, d), u_s.reshape(bs, ts, d)
            conv_w = (p["w_dw"], p["b_dw"], p["conv_ln_g"], p["conv_ln_b"], j)
            z_p = _conv_prompt_call(u_p, *conv_w)
            conv_p.append(u_p[:, tp - (kw - 1):])
            pad = (-(kw - 1 + ts)) % SUBLANES
            up = jnp.concatenate([jnp.zeros((bs, pad, d), F32), state_conv[j], u_s], axis=1)
            z_s = _conv_sample_call(up, pad, ts, *conv_w)
            conv_s.append(up[:, pad + ts:])
            xp, hp = _rowmm_call(z_p.reshape(mp, d), wq, xp, mods_p[i], 2, ffn_norm(mods_p),
                                 bias=p["b_pw2"][j], name="pw2_residual_norm")
            xs, hs = _rowmm_call(z_s.reshape(ms, d), wq, xs, mods_s[i], 2, ffn_norm(mods_s),
                                 bias=p["b_pw2"][j], name="pw2_residual_norm")
        else:
            sinks = p["attn_sinks"][j].reshape(kvh, group, 1)
            qT, q_s, wq = _mm_call(hp, p["w_qkv"], j, (0,), dq, x2=hs, cast=(p["w_o"], j),
                                   bias=p["b_qkv"], scale=LOG2E * HEAD_DIM ** -0.5,
                                   out_dtype=BF16, tn=Q_PANEL_COLS, name="q_proj",
                                   out_mode="t", seq=tp)
            (k, vT), kv_s = _mm_call(hp, p["w_qkv"], j, (dq,), 2 * dkv, x2=hs, bias=p["b_qkv"],
                                     tn=2 * dkv, name="kv_proj", out_mode="split_t", seq=tp)
            k = k.reshape(bp, tp, dkv)
            bias = _bias_call(_pair_codes(n_buckets), p["rel_bias_table"], group)
            sink_row = jnp.repeat(sinks, PAIR, axis=2).reshape(kvh, 1, group * PAIR)
            oT = _attn_prompt_call(qT, k, vT, bias, sink_row, kvh, group)
            k_p.append(k[:, tp - WINDOW:].reshape(bp, WINDOW, kvh, HEAD_DIM))
            v_p.append(jnp.swapaxes(vT[:, :, tp - WINDOW:], 1, 2)
                       .reshape(bp, WINDOW, kvh, HEAD_DIM))
            kv3 = kv_s.reshape(bs, ts, 2, kvh, HEAD_DIM)
            k_all = jnp.concatenate([win_k[j], kv3[:, :, 0]], axis=1)
            v_all = jnp.concatenate([win_v[j], kv3[:, :, 1]], axis=1)
            n_keys = WINDOW + ts
            bias = _bias_call(_bucket_codes(ts, n_keys, -WINDOW, n_buckets),
                              p["rel_bias_table"]).reshape(kvh, group * ts, n_keys)
            o_s = _attn_sample_call(q_s.reshape(bs, ts, dq), k_all.reshape(bs, n_keys, dkv),
                                    v_all.reshape(bs, n_keys, dkv), bias,
                                    jnp.repeat(sinks, ts, axis=1), kvh, group)
            k_s.append(k_all[:, ts:])
            v_s.append(v_all[:, ts:])
            xp, hp = _rowmm_call(oT, wq, xp, mods_p[i], 2, ffn_norm(mods_p), bias=p["b_o"][j],
                                 x_t=True, name="wo_residual_norm")
            xs, hs = _rowmm_call(o_s.reshape(ms, dq), wq, xs, mods_s[i], 2, ffn_norm(mods_s),
                                 bias=p["b_o"][j], name="wo_residual_norm")
        a_p, a_s, wq = _mm_call(hp, p["w_gu"], i, (0, d_ff), d_ff, x2=hs, cast=(p["w_down"], i),
                                act="swiglu", out_dtype=BF16, name="ffn_gate_up")
        if i + 1 < depth:
            nxt = lambda mods: (p["norm_mix"][i + 1], mods[i + 1], 0, 1)
            xp, hp = _rowmm_call(a_p, wq, xp, mods_p[i], 5, nxt(mods_p),
                                 name="ffn_down_residual_norm")
            xs, hs = _rowmm_call(a_s, wq, xs, mods_s[i], 5, nxt(mods_s),
                                 name="ffn_down_residual_norm")
        else:
            y_p = _rowmm_call(a_p, wq, xp, mods_p[i], 5, (p["norm_out"],),
                              name="ffn_down_residual_final_norm")
            y_s = _rowmm_call(a_s, wq, xs, mods_s[i], 5, (p["norm_out"],),
                              name="ffn_down_residual_final_norm")
    stack = jnp.stack
    return (y_p.reshape(bp, tp, d), y_s.reshape(bs, ts, d), stack(conv_p), stack(k_p), stack(v_p),
            stack(conv_s), stack(k_s), stack(v_s))


def kernel(x_prompt, x_sample, c_prompt, c_sample, state_conv, cache_win_k, cache_win_v, w_mod, b_mod, norm_mix, norm_ffn, w_pw1, b_pw1, w_dw, b_dw, conv_ln_g, conv_ln_b, w_pw2, b_pw2, w_qkv, b_qkv, w_o, b_o, attn_sinks, rel_bias_table, w_gu, w_down, norm_out):
    p = dict(w_mod=w_mod, b_mod=b_mod, norm_mix=norm_mix, norm_ffn=norm_ffn, w_pw1=w_pw1,
             b_pw1=b_pw1, w_dw=w_dw, b_dw=b_dw, conv_ln_g=conv_ln_g, conv_ln_b=conv_ln_b,
             w_pw2=w_pw2, b_pw2=b_pw2, w_qkv=w_qkv, b_qkv=b_qkv, w_o=w_o, b_o=b_o,
             attn_sinks=attn_sinks, rel_bias_table=rel_bias_table, w_gu=w_gu, w_down=w_down,
             norm_out=norm_out)
    bp, sp, d = x_prompt.shape
    bs, ts, _ = x_sample.shape
    depth = w_mod.shape[0]

    n_c = bp + bs
    c_all = jnp.concatenate(
        [c_prompt, c_sample, jnp.zeros((-n_c % BF16_SUBLANES, d), F32)], axis=0)
    mod_all = _mod_call(c_all, w_mod, b_mod)

    mods_p = [_Mod(mod_all[l, :bp].reshape(bp, 1, 6 * d), sp) for l in range(depth)]
    mods_s = [_Mod(mod_all[l, bp:n_c].reshape(bs, 1, 6 * d), ts) for l in range(depth)]

    return _trunks(x_prompt, x_sample, mods_p, mods_s, state_conv, cache_win_k, cache_win_v, p)
```

```python
import functools
import math

import numpy as np
import jax
import jax.numpy as jnp
from jax import lax
from jax.experimental import pallas as pl
from jax.experimental.pallas import tpu as pltpu

F32 = jnp.float32
BF16 = jnp.bfloat16

CHUNK = 64
WINDOW = 128
HEAD_DIM = 64
MAX_DISTANCE = 128
EPS = 1e-6
LOG2E = math.log2(math.e)

V7X_VMEM_BYTES = 64 * 2**20
V7X_VMEM_CAP = V7X_VMEM_BYTES - 8 * 2**20
SUBLANES = 8
LANES = 128
BF16_SUBLANES = 2 * SUBLANES

PANEL_ROWS = 1024
PANEL_COLS = 512
Q_PANEL_COLS = 1024
ROWMM_ROWS = 512
CONV_ROWS = 128
MOD_COLS = 1024


def _nbytes(shape, dtype):
    return math.prod(shape) * jnp.dtype(dtype).itemsize


def _vmem_limit(blocks, scratch=()):
    est = 2 * sum(_nbytes(s, d) for s, d in blocks) + sum(_nbytes(s, d) for s, d in scratch)
    return int(min(V7X_VMEM_CAP, max(32 * 2**20, 2 * est)))


def _tile(dim, pref, mult=LANES):
    if dim <= pref:
        return dim
    t = (pref // mult) * mult
    while t >= mult:
        if dim % t == 0:
            return t
        t -= mult
    return dim


def _sigmoid(x):
    return 1.0 / (1.0 + jnp.exp(-x))


def _mod_body(c_ref, w_ref, b_ref, o_ref):
    c = c_ref[...]
    a = (c * _sigmoid(c)).astype(BF16)
    o_ref[...] = jnp.dot(a, w_ref[...].astype(BF16), preferred_element_type=F32) + b_ref[...]


def _mod_call(c_all, w_mod, b_mod):
    depth, d, n = w_mod.shape
    mp = c_all.shape[0]
    tn = _tile(n, MOD_COLS)
    blocks = [((mp, d), F32), ((d, tn), F32), ((1, tn), F32), ((mp, tn), F32)]
    return pl.pallas_call(
        _mod_body,
        out_shape=jax.ShapeDtypeStruct((depth, mp, n), F32),
        grid=(depth, n // tn),
        in_specs=[
            pl.BlockSpec((mp, d), lambda l, j: (0, 0)),
            pl.BlockSpec((None, d, tn), lambda l, j: (l, 0, j)),
            pl.BlockSpec((None, 1, tn), lambda l, j: (l, 0, j)),
        ],
        out_specs=pl.BlockSpec((None, mp, tn), lambda l, j: (l, 0, j)),
        compiler_params=pltpu.CompilerParams(
            dimension_semantics=("arbitrary", "arbitrary"),
            vmem_limit_bytes=_vmem_limit(blocks, [((d, tn), BF16)])),
        name="adaln_mod",
    )(c_all, w_mod, b_mod.reshape(depth, 1, n))


class _Mod:
    def __init__(self, arr, rows_per_group):
        self.arr = arr
        self.rows_per_group = rows_per_group

    def tile_rows(self, total_rows, pref, mult):
        if self.rows_per_group >= pref:
            return _tile(self.rows_per_group, pref, mult)
        return _tile(total_rows, pref, max(mult, self.rows_per_group))

    def spec(self, seg, d, tm):
        if tm <= self.rows_per_group:
            per = self.rows_per_group // tm
            return pl.BlockSpec((1, 1, d), lambda i: (i // per, 0, seg))
        return pl.BlockSpec((tm // self.rows_per_group, 1, d), lambda i: (i, 0, seg))


def _grouped(v, m):
    return v.reshape(m.shape[0], v.shape[0] // m.shape[0], v.shape[1])


def _norm_mod_body(x_ref, g_ref, sh_ref, sc_ref, o_ref):
    x = x_ref[...]
    y = x * lax.rsqrt(jnp.mean(x * x, axis=-1, keepdims=True) + EPS) * g_ref[...]
    sc = sc_ref[...]
    o_ref[...] = (_grouped(y, sc) * (1.0 + sc) + sh_ref[...]).reshape(x.shape).astype(o_ref.dtype)


def _norm_mod_call(x, g, mod, seg_shift, seg_scale, tm):
    m, d = x.shape
    tm = mod.tile_rows(m, tm, SUBLANES)
    blocks = [((tm, d), F32), ((1, d), F32), ((tm, d), BF16)]
    return pl.pallas_call(
        _norm_mod_body,
        out_shape=jax.ShapeDtypeStruct((m, d), BF16),
        grid=(m // tm,),
        in_specs=[
            pl.BlockSpec((tm, d), lambda i: (i, 0)),
            pl.BlockSpec((1, d), lambda i: (0, 0)),
            mod.spec(seg_shift, d, tm),
            mod.spec(seg_scale, d, tm),
        ],
        out_specs=pl.BlockSpec((tm, d), lambda i: (i, 0)),
        compiler_params=pltpu.CompilerParams(
            dimension_semantics=("parallel",), vmem_limit_bytes=_vmem_limit(blocks)),
        name="rmsnorm_modulate",
    )(x, g.reshape(1, d), mod.arr, mod.arr)


def _mm_body(x_ref, *refs, n_w, has_bias, act, scale, out_mode, has_x2, has_cast):
    if has_x2:
        x2_ref, refs = refs[0], refs[1:]
    ws = refs[:n_w]
    refs = refs[n_w:]
    bs = refs[:n_w] if has_bias else ()
    refs = refs[len(bs):]
    if has_cast:
        wc_ref, refs = refs[0], refs[1:]
    n_o = 2 if out_mode == "split_t" else 1
    o_refs = refs[:n_o]
    refs = refs[n_o:]
    if has_x2:
        o2_ref, refs = refs[0], refs[1:]
    if has_cast:
        oc_ref, refs = refs[0], refs[1:]
        oc_ref[...] = wc_ref[...].astype(BF16)
    wbs = refs

    def compute(x):
        ps = [None] * n_w
        for i in ((1, 0) if act == "glu" else range(n_w)):
            p = jnp.dot(x, wbs[i][...], preferred_element_type=F32)
            if has_bias:
                p = p + bs[i][...]
            ps[i] = p
        if act == "glu":
            y = ps[0] * _sigmoid(ps[1])
        elif act == "swiglu":
            y = ps[0] * _sigmoid(ps[0]) * ps[1]
        else:
            y = ps[0]
        return y * scale if scale != 1.0 else y

    def store(y):
        if out_mode == "plain":
            o_refs[0][...] = y.astype(o_refs[0].dtype)
        elif out_mode == "t":
            o_refs[0][...] = y.T.astype(o_refs[0].dtype)
        else:
            half = y.shape[1] // 2
            o_refs[0][...] = y[:, :half].astype(o_refs[0].dtype)
            o_refs[1][...] = y[:, half:].T.astype(o_refs[1].dtype)

    @pl.when(pl.program_id(1) == 0)
    def _():
        for w, wb in zip(ws, wbs):
            wb[...] = w[...].astype(BF16)
        if has_x2:
            m2 = x2_ref.shape[0]
            y = compute(jnp.concatenate([x2_ref[...], x_ref[...]], axis=0))
            o2_ref[...] = y[:m2].astype(o2_ref.dtype)
            store(y[m2:])
        else:
            store(compute(x_ref[...]))

    @pl.when(pl.program_id(1) > 0)
    def _():
        store(compute(x_ref[...]))


def _mm_call(x, w, layer, col_starts, n_out, *, x2=None, cast=None, bias=None, act=None,
             scale=1.0, out_dtype=F32, tm=PANEL_ROWS, tn=PANEL_COLS, name="matmul",
             out_mode="plain", seq=None):
    m, k = x.shape
    if out_mode == "plain":
        tm = _tile(m, tm, SUBLANES)
    else:
        tm = _tile(seq, tm, LANES)
    tn = _tile(n_out, tn)
    n_w = len(col_starts)
    assert all(c % tn == 0 for c in col_starts) and m % tm == 0 and n_out % tn == 0
    has_bias = bias is not None
    per_b = None if seq is None else seq // tm

    in_specs = [pl.BlockSpec((tm, k), lambda j, i: (i, 0))]
    args = [x]
    blocks = [((tm, k), x.dtype), ((tm, tn), out_dtype)]
    if x2 is not None:
        m2 = x2.shape[0]
        in_specs.append(pl.BlockSpec((m2, k), lambda j, i: (0, 0)))
        args.append(x2)
        blocks += [((m2, k), x2.dtype), ((m2, tn), out_dtype)]
    for c in col_starts:
        off = c // tn
        in_specs.append(pl.BlockSpec((None, k, tn), lambda j, i, off=off: (layer, 0, off + j)))
        args.append(w)
        blocks.append(((k, tn), F32))
    if has_bias:
        b3 = bias.reshape(bias.shape[0], 1, bias.shape[1])
        for c in col_starts:
            off = c // tn
            in_specs.append(pl.BlockSpec((None, 1, tn), lambda j, i, off=off: (layer, 0, off + j)))
            args.append(b3)
    n_i = m // tm
    if cast is not None:
        w_other, layer_other = cast
        kc, dc = w_other.shape[1:]
        slab = kc // ((n_out // tn) * n_i)
        assert slab * (n_out // tn) * n_i == kc and slab % BF16_SUBLANES == 0
        in_specs.append(pl.BlockSpec((None, slab, dc), lambda j, i: (layer_other, j * n_i + i, 0)))
        args.append(w_other)
        blocks += [((slab, dc), F32), ((slab, dc), BF16)]
    scratch = [((k, tn), BF16)] * n_w
    body = functools.partial(_mm_body, n_w=n_w, has_bias=has_bias, act=act, scale=scale,
                             out_mode=out_mode, has_x2=x2 is not None, has_cast=cast is not None)
    if out_mode == "plain":
        out_shape = [jax.ShapeDtypeStruct((m, n_out), out_dtype)]
        out_specs = [pl.BlockSpec((tm, tn), lambda j, i: (i, j))]
    elif out_mode == "t":
        out_shape = [jax.ShapeDtypeStruct((m // seq, n_out, seq), out_dtype)]
        out_specs = [pl.BlockSpec((None, tn, tm), lambda j, i: (i // per_b, j, i % per_b))]
    else:
        assert tn == n_out
        half = n_out // 2
        out_shape = [jax.ShapeDtypeStruct((m, half), out_dtype),
                     jax.ShapeDtypeStruct((m // seq, half, seq), out_dtype)]
        out_specs = [pl.BlockSpec((tm, half), lambda j, i: (i, 0)),
                     pl.BlockSpec((None, half, tm), lambda j, i: (i // per_b, 0, i % per_b))]
    if x2 is not None:
        out_shape.append(jax.ShapeDtypeStruct((m2, n_out), out_dtype))
        out_specs.append(pl.BlockSpec((m2, tn), lambda j, i: (0, j)))
    if cast is not None:
        out_shape.append(jax.ShapeDtypeStruct((kc, dc), BF16))
        out_specs.append(pl.BlockSpec((slab, dc), lambda j, i: (j * n_i + i, 0)))
    outs = pl.pallas_call(
        body,
        out_shape=out_shape,
        grid=(n_out // tn, n_i),
        in_specs=in_specs,
        out_specs=out_specs,
        scratch_shapes=[pltpu.VMEM(s, d) for s, d in scratch],
        compiler_params=pltpu.CompilerParams(
            dimension_semantics=("arbitrary", "arbitrary"),
            vmem_limit_bytes=_vmem_limit(blocks, scratch)),
        name=name,
    )(*args)
    n_primary = 2 if out_mode == "split_t" else 1
    primary = outs[0] if n_primary == 1 else outs[:n_primary]
    extras = list(outs[n_primary:])
    return (primary, *extras) if extras else primary


ROWMM_SUB_ROWS = 256
ROWMM_K_CHUNKS = 4
MXU_K = 256


def _rowmm_body(*refs, has_bias, x_t, final, single_step):
    x_ref, w_hbm = refs[:2]
    refs = refs[2:]
    if has_bias:
        b_ref, refs = refs[0], refs[1:]
    res_ref, gate_ref, ng_ref = refs[:3]
    refs = refs[3:]
    if not final:
        sh_ref, sc_ref = refs[:2]
        refs = refs[2:]
    n_o = 1 if final else 2
    o_refs = refs[:n_o]
    refs = refs[n_o:]
    w_res, sem = refs
    k = w_res.shape[0]
    tiles = k // MXU_K
    n_chunks = min(ROWMM_K_CHUNKS, tiles)
    bounds = [MXU_K * (tiles * c // n_chunks) for c in range(n_chunks + 1)]

    def chunk_copy(c):
        rows = pl.ds(bounds[c], bounds[c + 1] - bounds[c])
        return pltpu.make_async_copy(w_hbm.at[rows, :], w_res.at[rows, :], sem.at[c])

    tm = res_ref.shape[0]
    sub = min(tm, ROWMM_SUB_ROWS)

    def lhs(r0, k0=0, k1=k):
        return x_ref[k0:k1, r0:r0 + sub].T if x_t else x_ref[r0:r0 + sub, k0:k1]

    def groups_of(ref, r0):
        per = tm // ref.shape[0]
        return ref[...] if per >= tm else ref[r0 // per:(r0 + sub) // per]

    def finish(r0, y):
        if has_bias:
            y = y + b_ref[...]
        gate = groups_of(gate_ref, r0)
        xn = (_grouped(res_ref[r0:r0 + sub, :], gate) + gate * _grouped(y, gate)).reshape(y.shape)
        r = xn * lax.rsqrt(jnp.mean(xn * xn, axis=-1, keepdims=True) + EPS) * ng_ref[...]
        if final:
            o_refs[0][r0:r0 + sub, :] = r
        else:
            sc = groups_of(sc_ref, r0)
            o_refs[0][r0:r0 + sub, :] = xn
            o_refs[1][r0:r0 + sub, :] = (
                _grouped(r, sc) * (1.0 + sc) + groups_of(sh_ref, r0)
            ).reshape(y.shape).astype(o_refs[1].dtype)

    for c in range(n_chunks):
        @pl.when(pl.program_id(0) == 0)
        def _():
            chunk_copy(c).start()

    first = 0
    if single_step:
        y = None
        for c in range(n_chunks):
            chunk_copy(c).wait()
            part = jnp.dot(lhs(0, bounds[c], bounds[c + 1]), w_res[bounds[c]:bounds[c + 1], :],
                           preferred_element_type=F32)
            y = part if y is None else y + part
        finish(0, y)
        first = sub
    else:
        for c in range(n_chunks):
            @pl.when(pl.program_id(0) == 0)
            def _():
                chunk_copy(c).wait()

    for r0 in range(first, tm, sub):
        finish(r0, jnp.dot(lhs(r0), w_res[...], preferred_element_type=F32))


def _rowmm_call(x, w, res, gate, gate_seg, nxt, *, bias=None, x_t=False, tm=ROWMM_ROWS,
                name="rowmm"):
    if x_t:
        nb, k, seq = x.shape
        m = nb * seq
    else:
        m, k = x.shape
    d = w.shape[-1]
    final = len(nxt) == 1
    scratch = [((k, d), BF16)]
    n_mods = 1 if final else 3

    def vmem_need(rows):
        blocks = [((rows, k), BF16), ((rows, d), F32), ((rows, d), F32 if final else BF16)]
        blocks += [((rows, d), F32)] * (0 if final else 1)
        blocks += [((-(-rows // gate.rows_per_group) * SUBLANES, d), F32)] * n_mods
        temporaries = 4 * _nbytes((rows, d), F32)
        return (2 * sum(_nbytes(s, t) for s, t in blocks) + sum(_nbytes(s, t) for s, t in scratch)
                + temporaries)

    mult = LANES if x_t else SUBLANES
    tm = gate.tile_rows(m, tm, mult)
    rpg = gate.rows_per_group
    fits_groups = lambda t: t % mult == 0 and (rpg % t == 0 or t % rpg == 0)
    while vmem_need(tm) > V7X_VMEM_CAP and tm % 2 == 0 and fits_groups(tm // 2):
        tm //= 2
    assert final or nxt[1].rows_per_group == gate.rows_per_group
    has_bias = bias is not None
    row = lambda i: (i, 0)
    if x_t:
        per_b = seq // tm
        in_specs = [pl.BlockSpec((None, k, tm), lambda i: (i // per_b, 0, i % per_b))]
    else:
        in_specs = [pl.BlockSpec((tm, k), row)]
    in_specs.append(pl.BlockSpec(memory_space=pltpu.HBM))
    args = [x, w]
    if has_bias:
        in_specs.append(pl.BlockSpec((1, d), lambda i: (0, 0)))
        args.append(bias.reshape(1, d))
    in_specs += [pl.BlockSpec((tm, d), row), gate.spec(gate_seg, d, tm),
                 pl.BlockSpec((1, d), lambda i: (0, 0))]
    args += [res, gate.arr, nxt[0].reshape(1, d)]
    if final:
        out_shape = jax.ShapeDtypeStruct((m, d), F32)
        out_specs = pl.BlockSpec((tm, d), row)
    else:
        _, nmod, seg_shift, seg_scale = nxt
        in_specs += [nmod.spec(seg_shift, d, tm), nmod.spec(seg_scale, d, tm)]
        args += [nmod.arr, nmod.arr]
        out_shape = [jax.ShapeDtypeStruct((m, d), F32), jax.ShapeDtypeStruct((m, d), BF16)]
        out_specs = [pl.BlockSpec((tm, d), row), pl.BlockSpec((tm, d), row)]
    return pl.pallas_call(
        functools.partial(_rowmm_body, has_bias=has_bias, x_t=x_t, final=final,
                          single_step=m == tm),
        out_shape=out_shape,
        grid=(m // tm,),
        in_specs=in_specs,
        out_specs=out_specs,
        scratch_shapes=[pltpu.VMEM(s, dt) for s, dt in scratch]
        + [pltpu.SemaphoreType.DMA((ROWMM_K_CHUNKS,))],
        compiler_params=pltpu.CompilerParams(
            dimension_semantics=("arbitrary",),
            vmem_limit_bytes=min(V7X_VMEM_CAP, max(32 * 2**20, vmem_need(tm) + 4 * 2**20))),
        name=name,
    )(*args)


def _conv_ln_swish(up_ref, pad, tt, w_ref, bdw_ref, lg_ref, lb_ref, acc_ref, o_ref):
    kw, d = w_ref.shape
    by_shift = [[(a, SUBLANES * a + s - pad) for a in range((pad + kw - 1) // SUBLANES + 1)
                 if 0 <= SUBLANES * a + s - pad < kw] for s in range(SUBLANES)]

    def strip(c, carry):
        cols = pl.ds(pl.multiple_of(c * LANES, LANES), LANES)
        z = bdw_ref[:, cols]
        for s, taps in enumerate(by_shift):
            n = tt + SUBLANES if s else tt
            q = None
            for a, k in taps:
                term = up_ref[pl.ds(SUBLANES * a, n), cols] * w_ref[pl.ds(k, 1), cols]
                q = term if q is None else q + term
            if q is not None:
                z = z + q[s:s + tt]
        acc_ref[:, cols] = z
        return carry

    lax.fori_loop(0, d // LANES, strip, 0)
    z = acc_ref[...]
    mu = jnp.mean(z, axis=-1, keepdims=True)
    zc = z - mu
    y = zc * lax.rsqrt(jnp.mean(zc * zc, axis=-1, keepdims=True) + EPS)
    y = y * lg_ref[...] + lb_ref[...]
    o_ref[...] = (y * _sigmoid(y)).astype(o_ref.dtype)


def _conv_prompt_body(main_ref, halo_ref, w_ref, bdw_ref, lg_ref, lb_ref, o_ref, up_ref, acc_ref,
                      *, halo, tt):
    kw = w_ref.shape[0]

    @pl.when(pl.program_id(1) == 0)
    def _():
        up_ref[0:halo, :] = jnp.zeros((halo, up_ref.shape[1]), F32)

    @pl.when(pl.program_id(1) > 0)
    def _():
        up_ref[0:halo, :] = halo_ref[...]

    up_ref[halo:halo + tt, :] = main_ref[...]
    _conv_ln_swish(up_ref, halo - (kw - 1), tt, w_ref, bdw_ref, lg_ref, lb_ref, acc_ref, o_ref)


def _conv_prompt_call(u, w_dw, b_dw, ln_g, ln_b, layer, tt=CONV_ROWS):
    b, t, d = u.shape
    kw = w_dw.shape[1]
    halo = -(-(kw - 1) // SUBLANES) * SUBLANES
    tt = _tile(t, tt, halo)
    hb = tt // halo
    vec = lambda: pl.BlockSpec((None, 1, d), lambda bi, ti: (layer, 0, 0))
    blocks = [((tt, d), F32), ((halo, d), F32), ((kw, d), F32), ((tt, d), BF16)]
    scratch = [((halo + tt, d), F32), ((tt, d), F32)]
    return pl.pallas_call(
        functools.partial(_conv_prompt_body, halo=halo, tt=tt),
        out_shape=jax.ShapeDtypeStruct((b, t, d), BF16),
        grid=(b, t // tt),
        in_specs=[
            pl.BlockSpec((None, tt, d), lambda bi, ti: (bi, ti, 0)),
            pl.BlockSpec((None, halo, d), lambda bi, ti: (bi, jnp.maximum(ti * hb - 1, 0), 0)),
            pl.BlockSpec((None, kw, d), lambda bi, ti: (layer, 0, 0)),
            vec(), vec(), vec(),
        ],
        out_specs=pl.BlockSpec((None, tt, d), lambda bi, ti: (bi, ti, 0)),
        scratch_shapes=[pltpu.VMEM(s, dt) for s, dt in scratch],
        compiler_params=pltpu.CompilerParams(
            dimension_semantics=("parallel", "arbitrary"),
            vmem_limit_bytes=_vmem_limit(blocks, scratch)),
        name="dwconv_ln_swish_prompt",
    )(u, u, w_dw, b_dw.reshape(-1, 1, d), ln_g.reshape(-1, 1, d), ln_b.reshape(-1, 1, d))


def _conv_sample_body(up_ref, w_ref, bdw_ref, lg_ref, lb_ref, o_ref, acc_ref, *, pad, tt):
    _conv_ln_swish(up_ref, pad, tt, w_ref, bdw_ref, lg_ref, lb_ref, acc_ref, o_ref)


def _conv_sample_call(up, pad, tt, w_dw, b_dw, ln_g, ln_b, layer):
    b, rows, d = up.shape
    kw = w_dw.shape[1]
    vec = lambda: pl.BlockSpec((None, 1, d), lambda bi: (layer, 0, 0))
    blocks = [((rows, d), F32), ((kw, d), F32), ((tt, d), BF16)]
    scratch = [((tt, d), F32)]
    return pl.pallas_call(
        functools.partial(_conv_sample_body, pad=pad, tt=tt),
        out_shape=jax.ShapeDtypeStruct((b, tt, d), BF16),
        grid=(b,),
        in_specs=[
            pl.BlockSpec((None, rows, d), lambda bi: (bi, 0, 0)),
            pl.BlockSpec((None, kw, d), lambda bi: (layer, 0, 0)),
            vec(), vec(), vec(),
        ],
        out_specs=pl.BlockSpec((None, tt, d), lambda bi: (bi, 0, 0)),
        scratch_shapes=[pltpu.VMEM(s, dt) for s, dt in scratch],
        compiler_params=pltpu.CompilerParams(
            dimension_semantics=("parallel",), vmem_limit_bytes=_vmem_limit(blocks, scratch)),
        name="dwconv_ln_swish_sample",
    )(up, w_dw, b_dw.reshape(-1, 1, d), ln_g.reshape(-1, 1, d), ln_b.reshape(-1, 1, d))


def _bucket_codes(n_q, n_k, k_off, n_buckets):
    rel = (np.arange(n_k) + k_off)[None, :] - np.arange(n_q)[:, None]
    nb = n_buckets // 2
    max_exact = nb // 2
    ret = np.where(rel > 0, nb, 0)
    n = np.abs(rel)
    nf = np.maximum(n, 1).astype(np.float32)
    large = max_exact + (np.log(nf / np.float32(max_exact))
                         / np.float32(math.log(MAX_DISTANCE / max_exact))
                         * np.float32(nb - max_exact)).astype(np.int32)
    large = np.minimum(large, nb - 1)
    return (ret + np.where(n < max_exact, n, large)).astype(np.int32)


def _bias_body(code_ref, table_ref, o_ref, *, present, group):
    n_cols = code_ref.shape[1]

    used = sorted({b for _, buckets in present for b in buckets})

    def head(hh, carry):
        value = {b: table_ref[hh, b] * LOG2E for b in used}
        for r0, buckets in present:
            rows = slice(r0, r0 + SUBLANES)
            code = code_ref[rows, :]
            out = jnp.full(code.shape, -jnp.inf, F32)
            for b in buckets:
                out = jnp.where(code == b, value[b], out)
            if group is None:
                o_ref[hh, rows, :] = out
            else:
                cols = pl.ds(pl.multiple_of((hh % group) * n_cols, LANES), n_cols)
                o_ref[hh // group, rows, cols] = out
        return carry

    lax.fori_loop(0, table_ref.shape[0], head, 0)


def _bias_call(code, table, group=None):
    n_heads, n_buckets = table.shape
    rows, cols = code.shape
    shape = (n_heads, rows, cols) if group is None else (n_heads // group, rows, group * cols)
    present = tuple(
        (r0, tuple(int(b) for b in np.unique(code[r0:r0 + SUBLANES]) if b < n_buckets))
        for r0 in range(0, rows, SUBLANES))
    return pl.pallas_call(
        functools.partial(_bias_body, present=present, group=group),
        out_shape=jax.ShapeDtypeStruct(shape, F32),
        in_specs=[pl.BlockSpec(memory_space=pltpu.VMEM), pl.BlockSpec(memory_space=pltpu.SMEM)],
        out_specs=pl.BlockSpec(memory_space=pltpu.VMEM),
        name="rel_bias",
    )(jnp.asarray(code), table)


def _unit_scores(qs, kh, bias):
    return lax.dot_general(qs, kh, (((1,), (1,)), ((), ())), preferred_element_type=F32) + bias


def _unit_output(s, vh, sink_col):
    sink_col = sink_col * LOG2E
    m = jnp.maximum(jnp.max(s, axis=-1, keepdims=True), sink_col)
    e = jnp.exp2(s - m)
    l = jnp.sum(e, axis=-1, keepdims=True) + jnp.exp2(sink_col - m)
    return jnp.dot(e.astype(BF16), vh, preferred_element_type=F32) * (1.0 / l)


def _stack_heads(q, h, group):
    hd = HEAD_DIM
    return jnp.concatenate(
        [q[:, (h * group + g) * hd:(h * group + g + 1) * hd] for g in range(group)], axis=0)


def _unstack_heads(o, group):
    tq = o.shape[0] // group
    return [o[g * tq:(g + 1) * tq, :] for g in range(group)]


ATTN_Q_COLS = 512
PAIR = 2 * CHUNK


def _pair_codes(n_buckets):
    code = _bucket_codes(PAIR, WINDOW + PAIR, -WINDOW, n_buckets).T
    key_chunk = np.arange(WINDOW + PAIR)[:, None] // CHUNK
    q_chunk = np.arange(PAIR)[None, :] // CHUNK
    visible = (key_chunk >= q_chunk) & (key_chunk <= q_chunk + WINDOW // CHUNK)
    return np.where(visible, code, n_buckets).astype(np.int32)


def _attn_prompt_body(qT_ref, kp_ref, kc_ref, vTp_ref, vTc_ref, bias_ref, sink_ref, oT_ref,
                      k_scr, vT_scr, *, kvh, group):
    i = pl.program_id(1)
    tq = qT_ref.shape[1]
    nk = WINDOW + PAIR
    hd = HEAD_DIM
    for h in range(kvh):
        k_scr[h, 0:WINDOW, :] = kp_ref[:, h * hd:(h + 1) * hd].astype(BF16)
        k_scr[h, WINDOW:WINDOW + tq, :] = kc_ref[:, h * hd:(h + 1) * hd].astype(BF16)
    vT_scr[:, 0:WINDOW] = vTp_ref[...].astype(BF16)
    vT_scr[:, WINDOW:WINDOW + tq] = vTc_ref[...].astype(BF16)
    row = lax.broadcasted_iota(jnp.int32, (nk, group * PAIR), 0)
    start_mask = jnp.where(row < jnp.where(i == 0, WINDOW, 0), -jnp.inf, 0.0).astype(F32)
    units = [(p * PAIR, h) for p in range(tq // PAIR) for h in range(kvh)]

    def scores(c0, h):
        qsT = jnp.concatenate(
            [qT_ref[hh * hd:(hh + 1) * hd, c0:c0 + PAIR] for hh in range(h * group, (h + 1) * group)],
            axis=1)
        s = jnp.dot(k_scr[h, c0:c0 + nk, :], qsT, preferred_element_type=F32) + bias_ref[h]
        return s + start_mask if c0 == 0 else s

    s_next = scores(*units[0])
    for n, (c0, h) in enumerate(units):
        s = s_next
        if n + 1 < len(units):
            s_next = scores(*units[n + 1])
        sink = sink_ref[h] * LOG2E
        m = jnp.maximum(jnp.max(s, axis=0, keepdims=True), sink)
        e = jnp.exp2(s - m)
        l = jnp.sum(e, axis=0, keepdims=True) + jnp.exp2(sink - m)
        oT = jnp.dot(vT_scr[h * hd:(h + 1) * hd, c0:c0 + nk], e.astype(BF16),
                     preferred_element_type=F32) * (1.0 / l)
        for g in range(group):
            hh = h * group + g
            oT_ref[hh * hd:(hh + 1) * hd, c0:c0 + PAIR] = (
                oT[:, g * PAIR:(g + 1) * PAIR].astype(oT_ref.dtype))


def _attn_prompt_call(qT, k, vT, bias, sink_row, kvh, group):
    b, dq, s = qT.shape
    dk = k.shape[2]
    tq = _tile(s, ATTN_Q_COLS, PAIR)
    per = tq // WINDOW
    prev = lambda i: jnp.maximum(i * per - 1, 0)
    blocks = [((dq, tq), BF16)] * 2 + [((WINDOW + tq, dk), F32)] * 2 + [
        (bias.shape, F32), (sink_row.shape[:1] + (SUBLANES, sink_row.shape[2]), F32)]
    scratch = [((kvh, WINDOW + tq, LANES), BF16), ((dk, WINDOW + tq), BF16)]
    return pl.pallas_call(
        functools.partial(_attn_prompt_body, kvh=kvh, group=group),
        out_shape=jax.ShapeDtypeStruct((b, dq, s), BF16),
        grid=(b, s // tq),
        in_specs=[
            pl.BlockSpec((None, dq, tq), lambda bi, i: (bi, 0, i)),
            pl.BlockSpec((None, WINDOW, dk), lambda bi, i: (bi, prev(i), 0)),
            pl.BlockSpec((None, tq, dk), lambda bi, i: (bi, i, 0)),
            pl.BlockSpec((None, dk, WINDOW), lambda bi, i: (bi, 0, prev(i))),
            pl.BlockSpec((None, dk, tq), lambda bi, i: (bi, 0, i)),
            pl.BlockSpec(bias.shape, lambda bi, i: (0, 0, 0)),
            pl.BlockSpec(sink_row.shape, lambda bi, i: (0, 0, 0)),
        ],
        out_specs=pl.BlockSpec((None, dq, tq), lambda bi, i: (bi, 0, i)),
        scratch_shapes=[pltpu.VMEM((kvh, WINDOW + tq, HEAD_DIM), BF16),
                        pltpu.VMEM((dk, WINDOW + tq), BF16)],
        compiler_params=pltpu.CompilerParams(
            dimension_semantics=("parallel", "arbitrary"),
            vmem_limit_bytes=_vmem_limit(blocks, scratch)),
        name="swa_sink_attention_prompt",
    )(qT, k, k, vT, vT, bias, sink_row)


ATTN_SAMPLE_BATCH = 4


def _attn_sample_body(q_ref, k_ref, v_ref, bias_ref, sink_ref, o_ref, *, kvh, group):
    hd = HEAD_DIM
    units = [(b, h) for b in range(q_ref.shape[0]) for h in range(kvh)]

    def scores(b, h):
        qs = _stack_heads(q_ref[b].astype(F32), h, group).astype(BF16)
        return _unit_scores(qs, k_ref[b, :, h * hd:(h + 1) * hd].astype(BF16), bias_ref[h])

    s_next = scores(*units[0])
    outs = []
    for n, (b, h) in enumerate(units):
        s = s_next
        if n + 1 < len(units):
            s_next = scores(*units[n + 1])
        o = _unit_output(s, v_ref[b, :, h * hd:(h + 1) * hd].astype(BF16), sink_ref[h])
        outs += _unstack_heads(o, group)
        if h == kvh - 1:
            o_ref[b] = jnp.concatenate(outs, axis=1).astype(o_ref.dtype)
            outs = []


def _attn_sample_call(q, k_all, v_all, bias, sink_col, kvh, group):
    b, t, dq = q.shape
    nk, dk = k_all.shape[1:]
    bb = math.gcd(b, ATTN_SAMPLE_BATCH)
    blocks = [((bb, t, dq), BF16)] * 2 + [((bb, nk, dk), F32)] * 2 + [
        (bias.shape[:2] + (2 * LANES,), F32), (sink_col.shape[:2] + (LANES,), F32)]
    return pl.pallas_call(
        functools.partial(_attn_sample_body, kvh=kvh, group=group),
        out_shape=jax.ShapeDtypeStruct((b, t, dq), BF16),
        grid=(b // bb,),
        in_specs=[
            pl.BlockSpec((bb, t, dq), lambda bi: (bi, 0, 0)),
            pl.BlockSpec((bb, nk, dk), lambda bi: (bi, 0, 0)),
            pl.BlockSpec((bb, nk, dk), lambda bi: (bi, 0, 0)),
            pl.BlockSpec(bias.shape, lambda bi: (0, 0, 0)),
            pl.BlockSpec(sink_col.shape, lambda bi: (0, 0, 0)),
        ],
        out_specs=pl.BlockSpec((bb, t, dq), lambda bi: (bi, 0, 0)),
        compiler_params=pltpu.CompilerParams(
            dimension_semantics=("parallel",), vmem_limit_bytes=_vmem_limit(blocks)),
        name="swa_sink_attention_sample",
    )(q, k_all, v_all, bias, sink_col)


def _trunks(xp3, xs3, mods_p, mods_s, state_conv, win_k, win_v, p):
    bp, tp, d = xp3.shape
    bs, ts, _ = xs3.shape
    mp, ms = bp * tp, bs * ts
    depth = p["w_mod"].shape[0]
    d_ff = p["w_gu"].shape[2] // 2
    n_heads = p["attn_sinks"].shape[1]
    dq = n_heads * HEAD_DIM
    kvh = (p["w_qkv"].shape[2] - dq) // (2 * HEAD_DIM)
    group = n_heads // kvh
    dkv = kvh * HEAD_DIM
    kw = p["w_dw"].shape[1]
    n_buckets = p["rel_bias_table"].shape[1]
    conv_p, k_p, v_p, conv_s, k_s, v_s = [], [], [], [], [], []

    xp, xs = xp3.reshape(mp, d), xs3.reshape(ms, d)
    hp = _norm_mod_call(xp, p["norm_mix"][0], mods_p[0], 0, 1, PANEL_ROWS)
    hs = _norm_mod_call(xs, p["norm_mix"][0], mods_s[0], 0, 1, ms)
    for i in range(depth):
        j = i // 2
        ffn_norm = lambda mods: (p["norm_ffn"][i], mods[i], 3, 4)
        if i % 2 == 0:
            u_p, u_s, wq = _mm_call(hp, p["w_pw1"], j, (0, d), d, x2=hs, cast=(p["w_pw2"], j),
                                    bias=p["b_pw1"], act="glu", name="pw1_glu")
            u_p, u_s = u_p.reshape(bp, tp, d), u_s.reshape(bs, ts, d)
            conv_w = (p["w_dw"], p["b_dw"], p["conv_ln_g"], p["conv_ln_b"], j)
            z_p = _conv_prompt_call(u_p, *conv_w)
            conv_p.append(u_p[:, tp - (kw - 1):])
            pad = (-(kw - 1 + ts)) % SUBLANES
            up = jnp.concatenate([jnp.zeros((bs, pad, d), F32), state_conv[j], u_s], axis=1)
            z_s = _conv_sample_call(up, pad, ts, *conv_w)
            conv_s.append(up[:, pad + ts:])
            xp, hp = _rowmm_call(z_p.reshape(mp, d), wq, xp, mods_p[i], 2, ffn_norm(mods_p),
                                 bias=p["b_pw2"][j], name="pw2_residual_norm")
            xs, hs = _rowmm_call(z_s.reshape(ms, d), wq, xs, mods_s[i], 2, ffn_norm(mods_s),
                                 bias=p["b_pw2"][j], name="pw2_residual_norm")
        else:
            sinks = p["attn_sinks"][j].reshape(kvh, group, 1)
            qT, q_s, wq = _mm_call(hp, p["w_qkv"], j, (0,), dq, x2=hs, cast=(p["w_o"], j),
                                   bias=p["b_qkv"], scale=LOG2E * HEAD_DIM ** -0.5,
                                   out_dtype=BF16, tn=Q_PANEL_COLS, name="q_proj",
                                   out_mode="t", seq=tp)
            (k, vT), kv_s = _mm_call(hp, p["w_qkv"], j, (dq,), 2 * dkv, x2=hs, bias=p["b_qkv"],
                                     tn=2 * dkv, name="kv_proj", out_mode="split_t", seq=tp)
            k = k.reshape(bp, tp, dkv)
            bias = _bias_call(_pair_codes(n_buckets), p["rel_bias_table"], group)
            sink_row = jnp.repeat(sinks, PAIR, axis=2).reshape(kvh, 1, group * PAIR)
            oT = _attn_prompt_call(qT, k, vT, bias, sink_row, kvh, group)
            k_p.append(k[:, tp - WINDOW:].reshape(bp, WINDOW, kvh, HEAD_DIM))
            v_p.append(jnp.swapaxes(vT[:, :, tp - WINDOW:], 1, 2)
                       .reshape(bp, WINDOW, kvh, HEAD_DIM))
            kv3 = kv_s.reshape(bs, ts, 2, kvh, HEAD_DIM)
            k_all = jnp.concatenate([win_k[j], kv3[:, :, 0]], axis=1)
            v_all = jnp.concatenate([win_v[j], kv3[:, :, 1]], axis=1)
            n_keys = WINDOW + ts
            bias = _bias_call(_bucket_codes(ts, n_keys, -WINDOW, n_buckets),
                              p["rel_bias_table"]).reshape(kvh, group * ts, n_keys)
            o_s = _attn_sample_call(q_s.reshape(bs, ts, dq), k_all.reshape(bs, n_keys, dkv),
                                    v_all.reshape(bs, n_keys, dkv), bias,
                                    jnp.repeat(sinks, ts, axis=1), kvh, group)
            k_s.append(k_all[:, ts:])
            v_s.append(v_all[:, ts:])
            xp, hp = _rowmm_call(oT, wq, xp, mods_p[i], 2, ffn_norm(mods_p), bias=p["b_o"][j],
                                 x_t=True, name="wo_residual_norm")
            xs, hs = _rowmm_call(o_s.reshape(ms, dq), wq, xs, mods_s[i], 2, ffn_norm(mods_s),
                                 bias=p["b_o"][j], name="wo_residual_norm")
        a_p, a_s, wq = _mm_call(hp, p["w_gu"], i, (0, d_ff), d_ff, x2=hs, cast=(p["w_down"], i),
                                act="swiglu", out_dtype=BF16, name="ffn_gate_up")
        if i + 1 < depth:
            nxt = lambda mods: (p["norm_mix"][i + 1], mods[i + 1], 0, 1)
            xp, hp = _rowmm_call(a_p, wq, xp, mods_p[i], 5, nxt(mods_p),
                                 name="ffn_down_residual_norm")
            xs, hs = _rowmm_call(a_s, wq, xs, mods_s[i], 5, nxt(mods_s),
                                 name="ffn_down_residual_norm")
        else:
            y_p = _rowmm_call(a_p, wq, xp, mods_p[i], 5, (p["norm_out"],),
                              name="ffn_down_residual_final_norm")
            y_s = _rowmm_call(a_s, wq, xs, mods_s[i], 5, (p["norm_out"],),
                              name="ffn_down_residual_final_norm")
    stack = jnp.stack
    return (y_p.reshape(bp, tp, d), y_s.reshape(bs, ts, d), stack(conv_p), stack(k_p), stack(v_p),
            stack(conv_s), stack(k_s), stack(v_s))


def kernel(x_prompt, x_sample, c_prompt, c_sample, state_conv, cache_win_k, cache_win_v, w_mod, b_mod, norm_mix, norm_ffn, w_pw1, b_pw1, w_dw, b_dw, conv_ln_g, conv_ln_b, w_pw2, b_pw2, w_qkv, b_qkv, w_o, b_o, attn_sinks, rel_bias_table, w_gu, w_down, norm_out):
    p = dict(w_mod=w_mod, b_mod=b_mod, norm_mix=norm_mix, norm_ffn=norm_ffn, w_pw1=w_pw1,
             b_pw1=b_pw1, w_dw=w_dw, b_dw=b_dw, conv_ln_g=conv_ln_g, conv_ln_b=conv_ln_b,
             w_pw2=w_pw2, b_pw2=b_pw2, w_qkv=w_qkv, b_qkv=b_qkv, w_o=w_o, b_o=b_o,
             attn_sinks=attn_sinks, rel_bias_table=rel_bias_table, w_gu=w_gu, w_down=w_down,
             norm_out=norm_out)
    bp, sp, d = x_prompt.shape
    bs, ts, _ = x_sample.shape
    depth = w_mod.shape[0]

    n_c = bp + bs
    c_all = jnp.concatenate(
        [c_prompt, c_sample, jnp.zeros((-n_c % BF16_SUBLANES, d), F32)], axis=0)
    mod_all = _mod_call(c_all, w_mod, b_mod)

    mods_p = [_Mod(mod_all[l, :bp].reshape(bp, 1, 6 * d), sp) for l in range(depth)]
    mods_s = [_Mod(mod_all[l, bp:n_c].reshape(bs, 1, 6 * d), ts) for l in range(depth)]

    return _trunks(x_prompt, x_sample, mods_p, mods_s, state_conv, cache_win_k, cache_win_v, p)
```

```python
import functools
import math

import numpy as np
import jax
import jax.numpy as jnp
from jax import lax
from jax.experimental import pallas as pl
from jax.experimental.pallas import tpu as pltpu

F32 = jnp.float32
BF16 = jnp.bfloat16

CHUNK = 64
WINDOW = 128
HEAD_DIM = 64
MAX_DISTANCE = 128
EPS = 1e-6
LOG2E = math.log2(math.e)

V7X_VMEM_BYTES = 64 * 2**20
V7X_VMEM_CAP = V7X_VMEM_BYTES - 8 * 2**20
SUBLANES = 8
LANES = 128
BF16_SUBLANES = 2 * SUBLANES

PANEL_ROWS = 1024
PANEL_COLS = 512
Q_PANEL_COLS = 1024
ROWMM_ROWS = 512
CONV_ROWS = 128
MOD_COLS = 1024


def _nbytes(shape, dtype):
    return math.prod(shape) * jnp.dtype(dtype).itemsize


def _vmem_limit(blocks, scratch=()):
    est = 2 * sum(_nbytes(s, d) for s, d in blocks) + sum(_nbytes(s, d) for s, d in scratch)
    return int(min(V7X_VMEM_CAP, max(32 * 2**20, 2 * est)))


def _tile(dim, pref, mult=LANES):
    if dim <= pref:
        return dim
    t = (pref // mult) * mult
    while t >= mult:
        if dim % t == 0:
            return t
        t -= mult
    return dim


def _sigmoid(x):
    return 1.0 / (1.0 + jnp.exp(-x))


def _mod_body(c_ref, w_ref, b_ref, o_ref):
    c = c_ref[...]
    a = (c * _sigmoid(c)).astype(BF16)
    o_ref[...] = jnp.dot(a, w_ref[...].astype(BF16), preferred_element_type=F32) + b_ref[...]


def _mod_call(c_all, w_mod, b_mod):
    depth, d, n = w_mod.shape
    mp = c_all.shape[0]
    tn = _tile(n, MOD_COLS)
    blocks = [((mp, d), F32), ((d, tn), F32), ((1, tn), F32), ((mp, tn), F32)]
    return pl.pallas_call(
        _mod_body,
        out_shape=jax.ShapeDtypeStruct((depth, mp, n), F32),
        grid=(depth, n // tn),
        in_specs=[
            pl.BlockSpec((mp, d), lambda l, j: (0, 0)),
            pl.BlockSpec((None, d, tn), lambda l, j: (l, 0, j)),
            pl.BlockSpec((None, 1, tn), lambda l, j: (l, 0, j)),
        ],
        out_specs=pl.BlockSpec((None, mp, tn), lambda l, j: (l, 0, j)),
        compiler_params=pltpu.CompilerParams(
            dimension_semantics=("arbitrary", "arbitrary"),
            vmem_limit_bytes=_vmem_limit(blocks, [((d, tn), BF16)])),
        name="adaln_mod",
    )(c_all, w_mod, b_mod.reshape(depth, 1, n))


class _Mod:
    def __init__(self, arr, rows_per_group):
        self.arr = arr
        self.rows_per_group = rows_per_group

    def tile_rows(self, total_rows, pref, mult):
        if self.rows_per_group >= pref:
            return _tile(self.rows_per_group, pref, mult)
        return _tile(total_rows, pref, max(mult, self.rows_per_group))

    def spec(self, seg, d, tm):
        if tm <= self.rows_per_group:
            per = self.rows_per_group // tm
            return pl.BlockSpec((1, 1, d), lambda i: (i // per, 0, seg))
        return pl.BlockSpec((tm // self.rows_per_group, 1, d), lambda i: (i, 0, seg))


def _grouped(v, m):
    return v.reshape(m.shape[0], v.shape[0] // m.shape[0], v.shape[1])


def _norm_mod_body(x_ref, g_ref, sh_ref, sc_ref, o_ref):
    x = x_ref[...]
    y = x * lax.rsqrt(jnp.mean(x * x, axis=-1, keepdims=True) + EPS) * g_ref[...]
    sc = sc_ref[...]
    o_ref[...] = (_grouped(y, sc) * (1.0 + sc) + sh_ref[...]).reshape(x.shape).astype(o_ref.dtype)


def _norm_mod_call(x, g, mod, seg_shift, seg_scale, tm):
    m, d = x.shape
    tm = mod.tile_rows(m, tm, SUBLANES)
    blocks = [((tm, d), F32), ((1, d), F32), ((tm, d), BF16)]
    return pl.pallas_call(
        _norm_mod_body,
        out_shape=jax.ShapeDtypeStruct((m, d), BF16),
        grid=(m // tm,),
        in_specs=[
            pl.BlockSpec((tm, d), lambda i: (i, 0)),
            pl.BlockSpec((1, d), lambda i: (0, 0)),
            mod.spec(seg_shift, d, tm),
            mod.spec(seg_scale, d, tm),
        ],
        out_specs=pl.BlockSpec((tm, d), lambda i: (i, 0)),
        compiler_params=pltpu.CompilerParams(
            dimension_semantics=("parallel",), vmem_limit_bytes=_vmem_limit(blocks)),
        name="rmsnorm_modulate",
    )(x, g.reshape(1, d), mod.arr, mod.arr)


def _mm_body(x_ref, *refs, layer, offs, has_bias, act, scale, out_mode, has_x2, has_cast):
    n_w = len(offs)
    if has_x2:
        x2_ref, refs = refs[0], refs[1:]
    w_hbm, refs = refs[0], refs[1:]
    bs = refs[:n_w] if has_bias else ()
    refs = refs[len(bs):]
    if has_cast:
        wc_ref, refs = refs[0], refs[1:]
    n_o = 2 if out_mode == "split_t" else 1
    o_refs = refs[:n_o]
    refs = refs[n_o:]
    if has_x2:
        o2_ref, refs = refs[0], refs[1:]
    if has_cast:
        oc_ref, refs = refs[0], refs[1:]
        oc_ref[...] = wc_ref[...].astype(BF16)
    wbs, (stage, sem) = refs[:n_w], refs[n_w:]
    tn = wbs[0].shape[1]

    def panel_copy(i, j, slot):
        cols = pl.ds(pl.multiple_of((offs[i] + j) * tn, tn), tn)
        return pltpu.make_async_copy(w_hbm.at[layer, :, cols], stage.at[i, slot], sem.at[i, slot])

    def compute(x):
        ps = [None] * n_w
        for i in ((1, 0) if act == "glu" else range(n_w)):
            p = jnp.dot(x, wbs[i][...], preferred_element_type=F32)
            if has_bias:
                p = p + bs[i][...]
            ps[i] = p
        if act == "glu":
            y = ps[0] * _sigmoid(ps[1])
        elif act == "swiglu":
            y = ps[0] * _sigmoid(ps[0]) * ps[1]
        else:
            y = ps[0]
        return y * scale if scale != 1.0 else y

    @pl.when(pl.program_id(1) == 0)
    def _():
        j = pl.program_id(0)
        slot = j % 2

        @pl.when(j == 0)
        def _():
            for i in range(n_w):
                panel_copy(i, 0, 0).start()

        for i in range(n_w):
            panel_copy(i, j, slot).wait()

        @pl.when(j + 1 < pl.num_programs(0))
        def _():
            for i in range(n_w):
                panel_copy(i, j + 1, 1 - slot).start()

        for i in range(n_w):
            wbs[i][...] = stage[i, slot].astype(BF16)
        if has_x2:
            o2_ref[...] = compute(x2_ref[...]).astype(o2_ref.dtype)

    y = compute(x_ref[...])
    if out_mode == "plain":
        o_refs[0][...] = y.astype(o_refs[0].dtype)
    elif out_mode == "t":
        o_refs[0][...] = y.T.astype(o_refs[0].dtype)
    else:
        half = y.shape[1] // 2
        o_refs[0][...] = y[:, :half].astype(o_refs[0].dtype)
        o_refs[1][...] = y[:, half:].T.astype(o_refs[1].dtype)


def _mm_call(x, w, layer, col_starts, n_out, *, x2=None, cast=None, bias=None, act=None,
             scale=1.0, out_dtype=F32, tm=PANEL_ROWS, tn=PANEL_COLS, name="matmul",
             out_mode="plain", seq=None):
    m, k = x.shape
    if out_mode == "plain":
        tm = _tile(m, tm, SUBLANES)
    else:
        tm = _tile(seq, tm, LANES)
    tn = _tile(n_out, tn)
    n_w = len(col_starts)
    assert all(c % tn == 0 for c in col_starts) and m % tm == 0 and n_out % tn == 0
    has_bias = bias is not None
    per_b = None if seq is None else seq // tm

    in_specs = [pl.BlockSpec((tm, k), lambda j, i: (i, 0))]
    args = [x]
    blocks = [((tm, k), x.dtype), ((tm, tn), out_dtype)]
    if x2 is not None:
        m2 = x2.shape[0]
        in_specs.append(pl.BlockSpec((m2, k), lambda j, i: (0, 0)))
        args.append(x2)
        blocks += [((m2, k), x2.dtype), ((m2, tn), out_dtype)]
    in_specs.append(pl.BlockSpec(memory_space=pltpu.HBM))
    args.append(w)
    if has_bias:
        b3 = bias.reshape(bias.shape[0], 1, bias.shape[1])
        for c in col_starts:
            off = c // tn
            in_specs.append(pl.BlockSpec((None, 1, tn), lambda j, i, off=off: (layer, 0, off + j)))
            args.append(b3)
    n_i = m // tm
    if cast is not None:
        w_other, layer_other = cast
        kc, dc = w_other.shape[1:]
        slab = kc // ((n_out // tn) * n_i)
        assert slab * (n_out // tn) * n_i == kc and slab % BF16_SUBLANES == 0
        in_specs.append(pl.BlockSpec((None, slab, dc), lambda j, i: (layer_other, j * n_i + i, 0)))
        args.append(w_other)
        blocks += [((slab, dc), F32), ((slab, dc), BF16)]
    scratch = [((k, tn), BF16)] * n_w + [((n_w, 2, k, tn), F32)]
    body = functools.partial(_mm_body, layer=layer, offs=tuple(c // tn for c in col_starts),
                             has_bias=has_bias, act=act, scale=scale, out_mode=out_mode,
                             has_x2=x2 is not None, has_cast=cast is not None)
    if out_mode == "plain":
        out_shape = [jax.ShapeDtypeStruct((m, n_out), out_dtype)]
        out_specs = [pl.BlockSpec((tm, tn), lambda j, i: (i, j))]
    elif out_mode == "t":
        out_shape = [jax.ShapeDtypeStruct((m // seq, n_out, seq), out_dtype)]
        out_specs = [pl.BlockSpec((None, tn, tm), lambda j, i: (i // per_b, j, i % per_b))]
    else:
        assert tn == n_out
        half = n_out // 2
        out_shape = [jax.ShapeDtypeStruct((m, half), out_dtype),
                     jax.ShapeDtypeStruct((m // seq, half, seq), out_dtype)]
        out_specs = [pl.BlockSpec((tm, half), lambda j, i: (i, 0)),
                     pl.BlockSpec((None, half, tm), lambda j, i: (i // per_b, 0, i % per_b))]
    if x2 is not None:
        out_shape.append(jax.ShapeDtypeStruct((m2, n_out), out_dtype))
        out_specs.append(pl.BlockSpec((m2, tn), lambda j, i: (0, j)))
    if cast is not None:
        out_shape.append(jax.ShapeDtypeStruct((kc, dc), BF16))
        out_specs.append(pl.BlockSpec((slab, dc), lambda j, i: (j * n_i + i, 0)))
    outs = pl.pallas_call(
        body,
        out_shape=out_shape,
        grid=(n_out // tn, n_i),
        in_specs=in_specs,
        out_specs=out_specs,
        scratch_shapes=[pltpu.VMEM(s, d) for s, d in scratch]
        + [pltpu.SemaphoreType.DMA((n_w, 2))],
        compiler_params=pltpu.CompilerParams(
            dimension_semantics=("arbitrary", "arbitrary"),
            vmem_limit_bytes=_vmem_limit(blocks, scratch)),
        name=name,
    )(*args)
    n_primary = 2 if out_mode == "split_t" else 1
    primary = outs[0] if n_primary == 1 else outs[:n_primary]
    extras = list(outs[n_primary:])
    return (primary, *extras) if extras else primary


ROWMM_SUB_ROWS = 256
ROWMM_K_CHUNKS = 4
MXU_K = 256


def _rowmm_body(*refs, has_bias, x_t, final, single_step):
    x_ref, w_hbm = refs[:2]
    refs = refs[2:]
    if has_bias:
        b_ref, refs = refs[0], refs[1:]
    res_ref, gate_ref, ng_ref = refs[:3]
    refs = refs[3:]
    if not final:
        sh_ref, sc_ref = refs[:2]
        refs = refs[2:]
    n_o = 1 if final else 2
    o_refs = refs[:n_o]
    refs = refs[n_o:]
    w_res, sem = refs
    k = w_res.shape[0]
    tiles = k // MXU_K
    n_chunks = min(ROWMM_K_CHUNKS, tiles)
    bounds = [MXU_K * (tiles * c // n_chunks) for c in range(n_chunks + 1)]

    def chunk_copy(c):
        rows = pl.ds(bounds[c], bounds[c + 1] - bounds[c])
        return pltpu.make_async_copy(w_hbm.at[rows, :], w_res.at[rows, :], sem.at[c])

    tm = res_ref.shape[0]
    sub = min(tm, ROWMM_SUB_ROWS)

    def lhs(r0, k0=0, k1=k):
        return x_ref[k0:k1, r0:r0 + sub].T if x_t else x_ref[r0:r0 + sub, k0:k1]

    def groups_of(ref, r0):
        per = tm // ref.shape[0]
        return ref[...] if per >= tm else ref[r0 // per:(r0 + sub) // per]

    def finish(r0, y):
        if has_bias:
            y = y + b_ref[...]
        gate = groups_of(gate_ref, r0)
        xn = (_grouped(res_ref[r0:r0 + sub, :], gate) + gate * _grouped(y, gate)).reshape(y.shape)
        r = xn * lax.rsqrt(jnp.mean(xn * xn, axis=-1, keepdims=True) + EPS) * ng_ref[...]
        if final:
            o_refs[0][r0:r0 + sub, :] = r
        else:
            sc = groups_of(sc_ref, r0)
            o_refs[0][r0:r0 + sub, :] = xn
            o_refs[1][r0:r0 + sub, :] = (
                _grouped(r, sc) * (1.0 + sc) + groups_of(sh_ref, r0)
            ).reshape(y.shape).astype(o_refs[1].dtype)

    for c in range(n_chunks):
        @pl.when(pl.program_id(0) == 0)
        def _():
            chunk_copy(c).start()

    first = 0
    if single_step:
        y = None
        for c in range(n_chunks):
            chunk_copy(c).wait()
            part = jnp.dot(lhs(0, bounds[c], bounds[c + 1]), w_res[bounds[c]:bounds[c + 1], :],
                           preferred_element_type=F32)
            y = part if y is None else y + part
        finish(0, y)
        first = sub
    else:
        for c in range(n_chunks):
            @pl.when(pl.program_id(0) == 0)
            def _():
                chunk_copy(c).wait()

    for r0 in range(first, tm, sub):
        finish(r0, jnp.dot(lhs(r0), w_res[...], preferred_element_type=F32))


def _rowmm_call(x, w, res, gate, gate_seg, nxt, *, bias=None, x_t=False, tm=ROWMM_ROWS,
                name="rowmm"):
    if x_t:
        nb, k, seq = x.shape
        m = nb * seq
    else:
        m, k = x.shape
    d = w.shape[-1]
    final = len(nxt) == 1
    scratch = [((k, d), BF16)]
    n_mods = 1 if final else 3

    def vmem_need(rows):
        blocks = [((rows, k), BF16), ((rows, d), F32), ((rows, d), F32 if final else BF16)]
        blocks += [((rows, d), F32)] * (0 if final else 1)
        blocks += [((-(-rows // gate.rows_per_group) * SUBLANES, d), F32)] * n_mods
        temporaries = 4 * _nbytes((rows, d), F32)
        return (2 * sum(_nbytes(s, t) for s, t in blocks) + sum(_nbytes(s, t) for s, t in scratch)
                + temporaries)

    mult = LANES if x_t else SUBLANES
    tm = gate.tile_rows(m, tm, mult)
    rpg = gate.rows_per_group
    fits_groups = lambda t: t % mult == 0 and (rpg % t == 0 or t % rpg == 0)
    while vmem_need(tm) > V7X_VMEM_CAP and tm % 2 == 0 and fits_groups(tm // 2):
        tm //= 2
    assert final or nxt[1].rows_per_group == gate.rows_per_group
    has_bias = bias is not None
    row = lambda i: (i, 0)
    if x_t:
        per_b = seq // tm
        in_specs = [pl.BlockSpec((None, k, tm), lambda i: (i // per_b, 0, i % per_b))]
    else:
        in_specs = [pl.BlockSpec((tm, k), row)]
    in_specs.append(pl.BlockSpec(memory_space=pltpu.HBM))
    args = [x, w]
    if has_bias:
        in_specs.append(pl.BlockSpec((1, d), lambda i: (0, 0)))
        args.append(bias.reshape(1, d))
    in_specs += [pl.BlockSpec((tm, d), row), gate.spec(gate_seg, d, tm),
                 pl.BlockSpec((1, d), lambda i: (0, 0))]
    args += [res, gate.arr, nxt[0].reshape(1, d)]
    if final:
        out_shape = jax.ShapeDtypeStruct((m, d), F32)
        out_specs = pl.BlockSpec((tm, d), row)
    else:
        _, nmod, seg_shift, seg_scale = nxt
        in_specs += [nmod.spec(seg_shift, d, tm), nmod.spec(seg_scale, d, tm)]
        args += [nmod.arr, nmod.arr]
        out_shape = [jax.ShapeDtypeStruct((m, d), F32), jax.ShapeDtypeStruct((m, d), BF16)]
        out_specs = [pl.BlockSpec((tm, d), row), pl.BlockSpec((tm, d), row)]
    return pl.pallas_call(
        functools.partial(_rowmm_body, has_bias=has_bias, x_t=x_t, final=final,
                          single_step=m == tm),
        out_shape=out_shape,
        grid=(m // tm,),
        in_specs=in_specs,
        out_specs=out_specs,
        scratch_shapes=[pltpu.VMEM(s, dt) for s, dt in scratch]
        + [pltpu.SemaphoreType.DMA((ROWMM_K_CHUNKS,))],
        compiler_params=pltpu.CompilerParams(
            dimension_semantics=("arbitrary",),
            vmem_limit_bytes=min(V7X_VMEM_CAP, max(32 * 2**20, vmem_need(tm) + 4 * 2**20))),
        name=name,
    )(*args)


def _conv_ln_swish(up_ref, pad, tt, w_ref, bdw_ref, lg_ref, lb_ref, acc_ref, o_ref):
    kw, d = w_ref.shape
    by_shift = [[(a, SUBLANES * a + s - pad) for a in range((pad + kw - 1) // SUBLANES + 1)
                 if 0 <= SUBLANES * a + s - pad < kw] for s in range(SUBLANES)]

    def strip(c, carry):
        cols = pl.ds(pl.multiple_of(c * LANES, LANES), LANES)
        z = bdw_ref[:, cols]
        for s, taps in enumerate(by_shift):
            n = tt + SUBLANES if s else tt
            q = None
            for a, k in taps:
                term = up_ref[pl.ds(SUBLANES * a, n), cols] * w_ref[pl.ds(k, 1), cols]
                q = term if q is None else q + term
            if q is not None:
                z = z + q[s:s + tt]
        acc_ref[:, cols] = z
        return carry

    lax.fori_loop(0, d // LANES, strip, 0)
    z = acc_ref[...]
    mu = jnp.mean(z, axis=-1, keepdims=True)
    zc = z - mu
    y = zc * lax.rsqrt(jnp.mean(zc * zc, axis=-1, keepdims=True) + EPS)
    y = y * lg_ref[...] + lb_ref[...]
    o_ref[...] = (y * _sigmoid(y)).astype(o_ref.dtype)


def _conv_prompt_body(main_ref, halo_ref, w_ref, bdw_ref, lg_ref, lb_ref, o_ref, up_ref, acc_ref,
                      *, halo, tt):
    kw = w_ref.shape[0]

    @pl.when(pl.program_id(1) == 0)
    def _():
        up_ref[0:halo, :] = jnp.zeros((halo, up_ref.shape[1]), F32)

    @pl.when(pl.program_id(1) > 0)
    def _():
        up_ref[0:halo, :] = halo_ref[...]

    up_ref[halo:halo + tt, :] = main_ref[...]
    _conv_ln_swish(up_ref, halo - (kw - 1), tt, w_ref, bdw_ref, lg_ref, lb_ref, acc_ref, o_ref)


def _conv_prompt_call(u, w_dw, b_dw, ln_g, ln_b, layer, tt=CONV_ROWS):
    b, t, d = u.shape
    kw = w_dw.shape[1]
    halo = -(-(kw - 1) // SUBLANES) * SUBLANES
    tt = _tile(t, tt, halo)
    hb = tt // halo
    vec = lambda: pl.BlockSpec((None, 1, d), lambda bi, ti: (layer, 0, 0))
    blocks = [((tt, d), F32), ((halo, d), F32), ((kw, d), F32), ((tt, d), BF16)]
    scratch = [((halo + tt, d), F32), ((tt, d), F32)]
    return pl.pallas_call(
        functools.partial(_conv_prompt_body, halo=halo, tt=tt),
        out_shape=jax.ShapeDtypeStruct((b, t, d), BF16),
        grid=(b, t // tt),
        in_specs=[
            pl.BlockSpec((None, tt, d), lambda bi, ti: (bi, ti, 0)),
            pl.BlockSpec((None, halo, d), lambda bi, ti: (bi, jnp.maximum(ti * hb - 1, 0), 0)),
            pl.BlockSpec((None, kw, d), lambda bi, ti: (layer, 0, 0)),
            vec(), vec(), vec(),
        ],
        out_specs=pl.BlockSpec((None, tt, d), lambda bi, ti: (bi, ti, 0)),
        scratch_shapes=[pltpu.VMEM(s, dt) for s, dt in scratch],
        compiler_params=pltpu.CompilerParams(
            dimension_semantics=("parallel", "arbitrary"),
            vmem_limit_bytes=_vmem_limit(blocks, scratch)),
        name="dwconv_ln_swish_prompt",
    )(u, u, w_dw, b_dw.reshape(-1, 1, d), ln_g.reshape(-1, 1, d), ln_b.reshape(-1, 1, d))


def _conv_sample_body(up_ref, w_ref, bdw_ref, lg_ref, lb_ref, o_ref, acc_ref, *, pad, tt):
    _conv_ln_swish(up_ref, pad, tt, w_ref, bdw_ref, lg_ref, lb_ref, acc_ref, o_ref)


def _conv_sample_call(up, pad, tt, w_dw, b_dw, ln_g, ln_b, layer):
    b, rows, d = up.shape
    kw = w_dw.shape[1]
    vec = lambda: pl.BlockSpec((None, 1, d), lambda bi: (layer, 0, 0))
    blocks = [((rows, d), F32), ((kw, d), F32), ((tt, d), BF16)]
    scratch = [((tt, d), F32)]
    return pl.pallas_call(
        functools.partial(_conv_sample_body, pad=pad, tt=tt),
        out_shape=jax.ShapeDtypeStruct((b, tt, d), BF16),
        grid=(b,),
        in_specs=[
            pl.BlockSpec((None, rows, d), lambda bi: (bi, 0, 0)),
            pl.BlockSpec((None, kw, d), lambda bi: (layer, 0, 0)),
            vec(), vec(), vec(),
        ],
        out_specs=pl.BlockSpec((None, tt, d), lambda bi: (bi, 0, 0)),
        scratch_shapes=[pltpu.VMEM(s, dt) for s, dt in scratch],
        compiler_params=pltpu.CompilerParams(
            dimension_semantics=("parallel",), vmem_limit_bytes=_vmem_limit(blocks, scratch)),
        name="dwconv_ln_swish_sample",
    )(up, w_dw, b_dw.reshape(-1, 1, d), ln_g.reshape(-1, 1, d), ln_b.reshape(-1, 1, d))


def _bucket_codes(n_q, n_k, k_off, n_buckets):
    rel = (np.arange(n_k) + k_off)[None, :] - np.arange(n_q)[:, None]
    nb = n_buckets // 2
    max_exact = nb // 2
    ret = np.where(rel > 0, nb, 0)
    n = np.abs(rel)
    nf = np.maximum(n, 1).astype(np.float32)
    large = max_exact + (np.log(nf / np.float32(max_exact))
                         / np.float32(math.log(MAX_DISTANCE / max_exact))
                         * np.float32(nb - max_exact)).astype(np.int32)
    large = np.minimum(large, nb - 1)
    return (ret + np.where(n < max_exact, n, large)).astype(np.int32)


def _bias_body(code_ref, table_ref, o_ref, *, present, group):
    n_cols = code_ref.shape[1]

    used = sorted({b for _, buckets in present for b in buckets})

    def head(hh, carry):
        value = {b: table_ref[hh, b] * LOG2E for b in used}
        for r0, buckets in present:
            rows = slice(r0, r0 + SUBLANES)
            code = code_ref[rows, :]
            out = jnp.full(code.shape, -jnp.inf, F32)
            for b in buckets:
                out = jnp.where(code == b, value[b], out)
            if group is None:
                o_ref[hh, rows, :] = out
            else:
                cols = pl.ds(pl.multiple_of((hh % group) * n_cols, LANES), n_cols)
                o_ref[hh // group, rows, cols] = out
        return carry

    lax.fori_loop(0, table_ref.shape[0], head, 0)


def _bias_call(code, table, group=None):
    n_heads, n_buckets = table.shape
    rows, cols = code.shape
    shape = (n_heads, rows, cols) if group is None else (n_heads // group, rows, group * cols)
    present = tuple(
        (r0, tuple(int(b) for b in np.unique(code[r0:r0 + SUBLANES]) if b < n_buckets))
        for r0 in range(0, rows, SUBLANES))
    return pl.pallas_call(
        functools.partial(_bias_body, present=present, group=group),
        out_shape=jax.ShapeDtypeStruct(shape, F32),
        in_specs=[pl.BlockSpec(memory_space=pltpu.VMEM), pl.BlockSpec(memory_space=pltpu.SMEM)],
        out_specs=pl.BlockSpec(memory_space=pltpu.VMEM),
        name="rel_bias",
    )(jnp.asarray(code), table)


def _unit_scores(qs, kh, bias):
    return lax.dot_general(qs, kh, (((1,), (1,)), ((), ())), preferred_element_type=F32) + bias


def _unit_output(s, vh, sink_col):
    sink_col = sink_col * LOG2E
    m = jnp.maximum(jnp.max(s, axis=-1, keepdims=True), sink_col)
    e = jnp.exp2(s - m)
    l = jnp.sum(e, axis=-1, keepdims=True) + jnp.exp2(sink_col - m)
    return jnp.dot(e.astype(BF16), vh, preferred_element_type=F32) * (1.0 / l)


def _stack_heads(q, h, group):
    hd = HEAD_DIM
    return jnp.concatenate(
        [q[:, (h * group + g) * hd:(h * group + g + 1) * hd] for g in range(group)], axis=0)


def _unstack_heads(o, group):
    tq = o.shape[0] // group
    return [o[g * tq:(g + 1) * tq, :] for g in range(group)]


ATTN_Q_COLS = 512
PAIR = 2 * CHUNK


def _pair_codes(n_buckets):
    code = _bucket_codes(PAIR, WINDOW + PAIR, -WINDOW, n_buckets).T
    key_chunk = np.arange(WINDOW + PAIR)[:, None] // CHUNK
    q_chunk = np.arange(PAIR)[None, :] // CHUNK
    visible = (key_chunk >= q_chunk) & (key_chunk <= q_chunk + WINDOW // CHUNK)
    return np.where(visible, code, n_buckets).astype(np.int32)


def _attn_prompt_body(qT_ref, kp_ref, kc_ref, vTp_ref, vTc_ref, bias_ref, sink_ref, oT_ref,
                      k_scr, vT_scr, *, kvh, group):
    i = pl.program_id(1)
    tq = qT_ref.shape[1]
    nk = WINDOW + PAIR
    hd = HEAD_DIM
    for h in range(kvh):
        k_scr[h, 0:WINDOW, :] = kp_ref[:, h * hd:(h + 1) * hd].astype(BF16)
        k_scr[h, WINDOW:WINDOW + tq, :] = kc_ref[:, h * hd:(h + 1) * hd].astype(BF16)
    vT_scr[:, 0:WINDOW] = vTp_ref[...].astype(BF16)
    vT_scr[:, WINDOW:WINDOW + tq] = vTc_ref[...].astype(BF16)
    row = lax.broadcasted_iota(jnp.int32, (nk, group * PAIR), 0)
    start_mask = jnp.where(row < jnp.where(i == 0, WINDOW, 0), -jnp.inf, 0.0).astype(F32)
    units = [(p * PAIR, h) for p in range(tq // PAIR) for h in range(kvh)]

    def scores(c0, h):
        qsT = jnp.concatenate(
            [qT_ref[hh * hd:(hh + 1) * hd, c0:c0 + PAIR] for hh in range(h * group, (h + 1) * group)],
            axis=1)
        s = jnp.dot(k_scr[h, c0:c0 + nk, :], qsT, preferred_element_type=F32) + bias_ref[h]
        return s + start_mask if c0 == 0 else s

    s_next = scores(*units[0])
    for n, (c0, h) in enumerate(units):
        s = s_next
        if n + 1 < len(units):
            s_next = scores(*units[n + 1])
        sink = sink_ref[h] * LOG2E
        m = jnp.maximum(jnp.max(s, axis=0, keepdims=True), sink)
        e = jnp.exp2(s - m)
        l = jnp.sum(e, axis=0, keepdims=True) + jnp.exp2(sink - m)
        oT = jnp.dot(vT_scr[h * hd:(h + 1) * hd, c0:c0 + nk], e.astype(BF16),
                     preferred_element_type=F32) * (1.0 / l)
        for g in range(group):
            hh = h * group + g
            oT_ref[hh * hd:(hh + 1) * hd, c0:c0 + PAIR] = (
                oT[:, g * PAIR:(g + 1) * PAIR].astype(oT_ref.dtype))


def _attn_prompt_call(qT, k, vT, bias, sink_row, kvh, group):
    b, dq, s = qT.shape
    dk = k.shape[2]
    tq = _tile(s, ATTN_Q_COLS, PAIR)
    per = tq // WINDOW
    prev = lambda i: jnp.maximum(i * per - 1, 0)
    blocks = [((dq, tq), BF16)] * 2 + [((WINDOW + tq, dk), F32)] * 2 + [
        (bias.shape, F32), (sink_row.shape[:1] + (SUBLANES, sink_row.shape[2]), F32)]
    scratch = [((kvh, WINDOW + tq, LANES), BF16), ((dk, WINDOW + tq), BF16)]
    return pl.pallas_call(
        functools.partial(_attn_prompt_body, kvh=kvh, group=group),
        out_shape=jax.ShapeDtypeStruct((b, dq, s), BF16),
        grid=(b, s // tq),
        in_specs=[
            pl.BlockSpec((None, dq, tq), lambda bi, i: (bi, 0, i)),
            pl.BlockSpec((None, WINDOW, dk), lambda bi, i: (bi, prev(i), 0)),
            pl.BlockSpec((None, tq, dk), lambda bi, i: (bi, i, 0)),
            pl.BlockSpec((None, dk, WINDOW), lambda bi, i: (bi, 0, prev(i))),
            pl.BlockSpec((None, dk, tq), lambda bi, i: (bi, 0, i)),
            pl.BlockSpec(bias.shape, lambda bi, i: (0, 0, 0)),
            pl.BlockSpec(sink_row.shape, lambda bi, i: (0, 0, 0)),
        ],
        out_specs=pl.BlockSpec((None, dq, tq), lambda bi, i: (bi, 0, i)),
        scratch_shapes=[pltpu.VMEM((kvh, WINDOW + tq, HEAD_DIM), BF16),
                        pltpu.VMEM((dk, WINDOW + tq), BF16)],
        compiler_params=pltpu.CompilerParams(
            dimension_semantics=("parallel", "arbitrary"),
            vmem_limit_bytes=_vmem_limit(blocks, scratch)),
        name="swa_sink_attention_prompt",
    )(qT, k, k, vT, vT, bias, sink_row)


ATTN_SAMPLE_BATCH = 4


def _attn_sample_body(q_ref, k_ref, v_ref, bias_ref, sink_ref, o_ref, *, kvh, group):
    hd = HEAD_DIM
    units = [(b, h) for b in range(q_ref.shape[0]) for h in range(kvh)]

    def scores(b, h):
        qs = _stack_heads(q_ref[b].astype(F32), h, group).astype(BF16)
        return _unit_scores(qs, k_ref[b, :, h * hd:(h + 1) * hd].astype(BF16), bias_ref[h])

    s_next = scores(*units[0])
    outs = []
    for n, (b, h) in enumerate(units):
        s = s_next
        if n + 1 < len(units):
            s_next = scores(*units[n + 1])
        o = _unit_output(s, v_ref[b, :, h * hd:(h + 1) * hd].astype(BF16), sink_ref[h])
        outs += _unstack_heads(o, group)
        if h == kvh - 1:
            o_ref[b] = jnp.concatenate(outs, axis=1).astype(o_ref.dtype)
            outs = []


def _attn_sample_call(q, k_all, v_all, bias, sink_col, kvh, group):
    b, t, dq = q.shape
    nk, dk = k_all.shape[1:]
    bb = math.gcd(b, ATTN_SAMPLE_BATCH)
    blocks = [((bb, t, dq), BF16)] * 2 + [((bb, nk, dk), F32)] * 2 + [
        (bias.shape[:2] + (2 * LANES,), F32), (sink_col.shape[:2] + (LANES,), F32)]
    return pl.pallas_call(
        functools.partial(_attn_sample_body, kvh=kvh, group=group),
        out_shape=jax.ShapeDtypeStruct((b, t, dq), BF16),
        grid=(b // bb,),
        in_specs=[
            pl.BlockSpec((bb, t, dq), lambda bi: (bi, 0, 0)),
            pl.BlockSpec((bb, nk, dk), lambda bi: (bi, 0, 0)),
            pl.BlockSpec((bb, nk, dk), lambda bi: (bi, 0, 0)),
            pl.BlockSpec(bias.shape, lambda bi: (0, 0, 0)),
            pl.BlockSpec(sink_col.shape, lambda bi: (0, 0, 0)),
        ],
        out_specs=pl.BlockSpec((bb, t, dq), lambda bi: (bi, 0, 0)),
        compiler_params=pltpu.CompilerParams(
            dimension_semantics=("parallel",), vmem_limit_bytes=_vmem_limit(blocks)),
        name="swa_sink_attention_sample",
    )(q, k_all, v_all, bias, sink_col)


def _trunks(xp3, xs3, mods_p, mods_s, state_conv, win_k, win_v, p):
    bp, tp, d = xp3.shape
    bs, ts, _ = xs3.shape
    mp, ms = bp * tp, bs * ts
    depth = p["w_mod"].shape[0]
    d_ff = p["w_gu"].shape[2] // 2
    n_heads = p["attn_sinks"].shape[1]
    dq = n_heads * HEAD_DIM
    kvh = (p["w_qkv"].shape[2] - dq) // (2 * HEAD_DIM)
    group = n_heads // kvh
    dkv = kvh * HEAD_DIM
    kw = p["w_dw"].shape[1]
    n_buckets = p["rel_bias_table"].shape[1]
    conv_p, k_p, v_p, conv_s, k_s, v_s = [], [], [], [], [], []

    xp, xs = xp3.reshape(mp, d), xs3.reshape(ms, d)
    hp = _norm_mod_call(xp, p["norm_mix"][0], mods_p[0], 0, 1, PANEL_ROWS)
    hs = _norm_mod_call(xs, p["norm_mix"][0], mods_s[0], 0, 1, ms)
    for i in range(depth):
        j = i // 2
        ffn_norm = lambda mods: (p["norm_ffn"][i], mods[i], 3, 4)
        if i % 2 == 0:
            u_p, u_s, wq = _mm_call(hp, p["w_pw1"], j, (0, d), d, x2=hs, cast=(p["w_pw2"], j),
                                    bias=p["b_pw1"], act="glu", name="pw1_glu")
            u_p, u_s = u_p.reshape(bp, tp, d), u_s.reshape(bs, ts, d)
            conv_w = (p["w_dw"], p["b_dw"], p["conv_ln_g"], p["conv_ln_b"], j)
            z_p = _conv_prompt_call(u_p, *conv_w)
            conv_p.append(u_p[:, tp - (kw - 1):])
            pad = (-(kw - 1 + ts)) % SUBLANES
            up = jnp.concatenate([jnp.zeros((bs, pad, d), F32), state_conv[j], u_s], axis=1)
            z_s = _conv_sample_call(up, pad, ts, *conv_w)
            conv_s.append(up[:, pad + ts:])
            xp, hp = _rowmm_call(z_p.reshape(mp, d), wq, xp, mods_p[i], 2, ffn_norm(mods_p),
                                 bias=p["b_pw2"][j], name="pw2_residual_norm")
            xs, hs = _rowmm_call(z_s.reshape(ms, d), wq, xs, mods_s[i], 2, ffn_norm(mods_s),
                                 bias=p["b_pw2"][j], name="pw2_residual_norm")
        else:
            sinks = p["attn_sinks"][j].reshape(kvh, group, 1)
            qT, q_s, wq = _mm_call(hp, p["w_qkv"], j, (0,), dq, x2=hs, cast=(p["w_o"], j),
                                   bias=p["b_qkv"], scale=LOG2E * HEAD_DIM ** -0.5,
                                   out_dtype=BF16, tn=Q_PANEL_COLS, name="q_proj",
                                   out_mode="t", seq=tp)
            (k, vT), kv_s = _mm_call(hp, p["w_qkv"], j, (dq,), 2 * dkv, x2=hs, bias=p["b_qkv"],
                                     tn=2 * dkv, name="kv_proj", out_mode="split_t", seq=tp)
            k = k.reshape(bp, tp, dkv)
            bias = _bias_call(_pair_codes(n_buckets), p["rel_bias_table"], group)
            sink_row = jnp.repeat(sinks, PAIR, axis=2).reshape(kvh, 1, group * PAIR)
            oT = _attn_prompt_call(qT, k, vT, bias, sink_row, kvh, group)
            k_p.append(k[:, tp - WINDOW:].reshape(bp, WINDOW, kvh, HEAD_DIM))
            v_p.append(jnp.swapaxes(vT[:, :, tp - WINDOW:], 1, 2)
                       .reshape(bp, WINDOW, kvh, HEAD_DIM))
            kv3 = kv_s.reshape(bs, ts, 2, kvh, HEAD_DIM)
            k_all = jnp.concatenate([win_k[j], kv3[:, :, 0]], axis=1)
            v_all = jnp.concatenate([win_v[j], kv3[:, :, 1]], axis=1)
            n_keys = WINDOW + ts
            bias = _bias_call(_bucket_codes(ts, n_keys, -WINDOW, n_buckets),
                              p["rel_bias_table"]).reshape(kvh, group * ts, n_keys)
            o_s = _attn_sample_call(q_s.reshape(bs, ts, dq), k_all.reshape(bs, n_keys, dkv),
                                    v_all.reshape(bs, n_keys, dkv), bias,
                                    jnp.repeat(sinks, ts, axis=1), kvh, group)
            k_s.append(k_all[:, ts:])
            v_s.append(v_all[:, ts:])
            xp, hp = _rowmm_call(oT, wq, xp, mods_p[i], 2, ffn_norm(mods_p), bias=p["b_o"][j],
                                 x_t=True, name="wo_residual_norm")
            xs, hs = _rowmm_call(o_s.reshape(ms, dq), wq, xs, mods_s[i], 2, ffn_norm(mods_s),
                                 bias=p["b_o"][j], name="wo_residual_norm")
        a_p, a_s, wq = _mm_call(hp, p["w_gu"], i, (0, d_ff), d_ff, x2=hs, cast=(p["w_down"], i),
                                act="swiglu", out_dtype=BF16, name="ffn_gate_up")
        if i + 1 < depth:
            nxt = lambda mods: (p["norm_mix"][i + 1], mods[i + 1], 0, 1)
            xp, hp = _rowmm_call(a_p, wq, xp, mods_p[i], 5, nxt(mods_p),
                                 name="ffn_down_residual_norm")
            xs, hs = _rowmm_call(a_s, wq, xs, mods_s[i], 5, nxt(mods_s),
                                 name="ffn_down_residual_norm")
        else:
            y_p = _rowmm_call(a_p, wq, xp, mods_p[i], 5, (p["norm_out"],),
                              name="ffn_down_residual_final_norm")
            y_s = _rowmm_call(a_s, wq, xs, mods_s[i], 5, (p["norm_out"],),
                              name="ffn_down_residual_final_norm")
    stack = jnp.stack
    return (y_p.reshape(bp, tp, d), y_s.reshape(bs, ts, d), stack(conv_p), stack(k_p), stack(v_p),
            stack(conv_s), stack(k_s), stack(v_s))


def kernel(x_prompt, x_sample, c_prompt, c_sample, state_conv, cache_win_k, cache_win_v, w_mod, b_mod, norm_mix, norm_ffn, w_pw1, b_pw1, w_dw, b_dw, conv_ln_g, conv_ln_b, w_pw2, b_pw2, w_qkv, b_qkv, w_o, b_o, attn_sinks, rel_bias_table, w_gu, w_down, norm_out):
    p = dict(w_mod=w_mod, b_mod=b_mod, norm_mix=norm_mix, norm_ffn=norm_ffn, w_pw1=w_pw1,
             b_pw1=b_pw1, w_dw=w_dw, b_dw=b_dw, conv_ln_g=conv_ln_g, conv_ln_b=conv_ln_b,
             w_pw2=w_pw2, b_pw2=b_pw2, w_qkv=w_qkv, b_qkv=b_qkv, w_o=w_o, b_o=b_o,
             attn_sinks=attn_sinks, rel_bias_table=rel_bias_table, w_gu=w_gu, w_down=w_down,
             norm_out=norm_out)
    bp, sp, d = x_prompt.shape
    bs, ts, _ = x_sample.shape
    depth = w_mod.shape[0]

    n_c = bp + bs
    c_all = jnp.concatenate(
        [c_prompt, c_sample, jnp.zeros((-n_c % BF16_SUBLANES, d), F32)], axis=0)
    mod_all = _mod_call(c_all, w_mod, b_mod)

    mods_p = [_Mod(mod_all[l, :bp].reshape(bp, 1, 6 * d), sp) for l in range(depth)]
    mods_s = [_Mod(mod_all[l, bp:n_c].reshape(bs, 1, 6 * d), ts) for l in range(depth)]

    return _trunks(x_prompt, x_sample, mods_p, mods_s, state_conv, cache_win_k, cache_win_v, p)
```

```python
import functools
import math

import numpy as np
import jax
import jax.numpy as jnp
from jax import lax
from jax.experimental import pallas as pl
from jax.experimental.pallas import tpu as pltpu

F32 = jnp.float32
BF16 = jnp.bfloat16

CHUNK = 64
WINDOW = 128
HEAD_DIM = 64
MAX_DISTANCE = 128
EPS = 1e-6
LOG2E = math.log2(math.e)

V7X_VMEM_BYTES = 64 * 2**20
V7X_VMEM_CAP = V7X_VMEM_BYTES - 8 * 2**20
SUBLANES = 8
LANES = 128
BF16_SUBLANES = 2 * SUBLANES

PANEL_ROWS = 1024
PANEL_COLS = 512
Q_PANEL_COLS = 1024
ROWMM_ROWS = 512
CONV_ROWS = 128
MOD_COLS = 1024


def _nbytes(shape, dtype):
    return math.prod(shape) * jnp.dtype(dtype).itemsize


def _vmem_limit(blocks, scratch=()):
    est = 2 * sum(_nbytes(s, d) for s, d in blocks) + sum(_nbytes(s, d) for s, d in scratch)
    return int(min(V7X_VMEM_CAP, max(32 * 2**20, 2 * est)))


def _tile(dim, pref, mult=LANES):
    if dim <= pref:
        return dim
    t = (pref // mult) * mult
    while t >= mult:
        if dim % t == 0:
            return t
        t -= mult
    return dim


def _sigmoid(x):
    return 1.0 / (1.0 + jnp.exp(-x))


def _mod_body(c_ref, w_ref, b_ref, o_ref):
    c = c_ref[...]
    a = (c * _sigmoid(c)).astype(BF16)
    o_ref[...] = jnp.dot(a, w_ref[...].astype(BF16), preferred_element_type=F32) + b_ref[...]


def _mod_call(c_all, w_mod, b_mod):
    depth, d, n = w_mod.shape
    mp = c_all.shape[0]
    tn = _tile(n, MOD_COLS)
    blocks = [((mp, d), F32), ((d, tn), F32), ((1, tn), F32), ((mp, tn), F32)]
    return pl.pallas_call(
        _mod_body,
        out_shape=jax.ShapeDtypeStruct((depth, mp, n), F32),
        grid=(depth, n // tn),
        in_specs=[
            pl.BlockSpec((mp, d), lambda l, j: (0, 0)),
            pl.BlockSpec((None, d, tn), lambda l, j: (l, 0, j)),
            pl.BlockSpec((None, 1, tn), lambda l, j: (l, 0, j)),
        ],
        out_specs=pl.BlockSpec((None, mp, tn), lambda l, j: (l, 0, j)),
        compiler_params=pltpu.CompilerParams(
            dimension_semantics=("arbitrary", "arbitrary"),
            vmem_limit_bytes=_vmem_limit(blocks, [((d, tn), BF16)])),
        name="adaln_mod",
    )(c_all, w_mod, b_mod.reshape(depth, 1, n))


class _Mod:
    def __init__(self, arr, rows_per_group):
        self.arr = arr
        self.rows_per_group = rows_per_group

    def tile_rows(self, total_rows, pref, mult):
        if self.rows_per_group >= pref:
            return _tile(self.rows_per_group, pref, mult)
        return _tile(total_rows, pref, max(mult, self.rows_per_group))

    def spec(self, seg, d, tm):
        if tm <= self.rows_per_group:
            per = self.rows_per_group // tm
            return pl.BlockSpec((1, 1, d), lambda i: (i // per, 0, seg))
        return pl.BlockSpec((tm // self.rows_per_group, 1, d), lambda i: (i, 0, seg))


def _grouped(v, m):
    return v.reshape(m.shape[0], v.shape[0] // m.shape[0], v.shape[1])


def _norm_mod_body(x_ref, g_ref, sh_ref, sc_ref, o_ref):
    x = x_ref[...]
    y = x * lax.rsqrt(jnp.mean(x * x, axis=-1, keepdims=True) + EPS) * g_ref[...]
    sc = sc_ref[...]
    o_ref[...] = (_grouped(y, sc) * (1.0 + sc) + sh_ref[...]).reshape(x.shape).astype(o_ref.dtype)


def _norm_mod_call(x, g, mod, seg_shift, seg_scale, tm):
    m, d = x.shape
    tm = mod.tile_rows(m, tm, SUBLANES)
    blocks = [((tm, d), F32), ((1, d), F32), ((tm, d), BF16)]
    return pl.pallas_call(
        _norm_mod_body,
        out_shape=jax.ShapeDtypeStruct((m, d), BF16),
        grid=(m // tm,),
        in_specs=[
            pl.BlockSpec((tm, d), lambda i: (i, 0)),
            pl.BlockSpec((1, d), lambda i: (0, 0)),
            mod.spec(seg_shift, d, tm),
            mod.spec(seg_scale, d, tm),
        ],
        out_specs=pl.BlockSpec((tm, d), lambda i: (i, 0)),
        compiler_params=pltpu.CompilerParams(
            dimension_semantics=("parallel",), vmem_limit_bytes=_vmem_limit(blocks)),
        name="rmsnorm_modulate",
    )(x, g.reshape(1, d), mod.arr, mod.arr)


def _mm_body(x_hbm, *refs, layer, offs, has_bias, act, scale, out_mode, has_x2, has_cast):
    n_w = len(offs)
    if has_x2:
        x2_ref, refs = refs[0], refs[1:]
    w_hbm, refs = refs[0], refs[1:]
    bs = refs[:n_w] if has_bias else ()
    refs = refs[len(bs):]
    if has_cast:
        wc_ref, refs = refs[0], refs[1:]
    n_o = 2 if out_mode == "split_t" else 1
    o_refs = refs[:n_o]
    refs = refs[n_o:]
    if has_x2:
        o2_ref, refs = refs[0], refs[1:]
    if has_cast:
        oc_ref, refs = refs[0], refs[1:]
        oc_ref[...] = wc_ref[...].astype(BF16)
    wbs, (stage, xs, sem, xsem) = refs[:n_w], refs[n_w:]
    tn = wbs[0].shape[1]
    tm = xs.shape[1]
    n_i = pl.num_programs(1)
    step = pl.program_id(0) * n_i + pl.program_id(1)
    n_steps = pl.num_programs(0) * n_i

    def x_copy(s, slot):
        rows = pl.ds(pl.multiple_of((s % n_i) * tm, tm), tm)
        return pltpu.make_async_copy(x_hbm.at[rows, :], xs.at[slot], xsem.at[slot])

    @pl.when(step == 0)
    def _():
        x_copy(0, 0).start()

        @pl.when(n_steps > 1)
        def _():
            x_copy(1, 1).start()

    x_copy(step, step % 3).wait()

    @pl.when(step + 2 < n_steps)
    def _():
        x_copy(step + 2, (step + 2) % 3).start()

    def panel_copy(i, j, slot):
        cols = pl.ds(pl.multiple_of((offs[i] + j) * tn, tn), tn)
        return pltpu.make_async_copy(w_hbm.at[layer, :, cols], stage.at[i, slot], sem.at[i, slot])

    def compute(x):
        ps = [None] * n_w
        for i in ((1, 0) if act == "glu" else range(n_w)):
            p = jnp.dot(x, wbs[i][...], preferred_element_type=F32)
            if has_bias:
                p = p + bs[i][...]
            ps[i] = p
        if act == "glu":
            y = ps[0] * _sigmoid(ps[1])
        elif act == "swiglu":
            y = ps[0] * _sigmoid(ps[0]) * ps[1]
        else:
            y = ps[0]
        return y * scale if scale != 1.0 else y

    @pl.when(pl.program_id(1) == 0)
    def _():
        j = pl.program_id(0)
        slot = j % 2

        @pl.when(j == 0)
        def _():
            for i in range(n_w):
                panel_copy(i, 0, 0).start()

        for i in range(n_w):
            panel_copy(i, j, slot).wait()

        @pl.when(j + 1 < pl.num_programs(0))
        def _():
            for i in range(n_w):
                panel_copy(i, j + 1, 1 - slot).start()

        for i in range(n_w):
            wbs[i][...] = stage[i, slot].astype(BF16)
        if has_x2:
            o2_ref[...] = compute(x2_ref[...]).astype(o2_ref.dtype)

    y = compute(xs[step % 3])
    if out_mode == "plain":
        o_refs[0][...] = y.astype(o_refs[0].dtype)
    elif out_mode == "t":
        o_refs[0][...] = y.T.astype(o_refs[0].dtype)
    else:
        half = y.shape[1] // 2
        o_refs[0][...] = y[:, :half].astype(o_refs[0].dtype)
        o_refs[1][...] = y[:, half:].T.astype(o_refs[1].dtype)


def _mm_call(x, w, layer, col_starts, n_out, *, x2=None, cast=None, bias=None, act=None,
             scale=1.0, out_dtype=F32, tm=PANEL_ROWS, tn=PANEL_COLS, name="matmul",
             out_mode="plain", seq=None):
    m, k = x.shape
    if out_mode == "plain":
        tm = _tile(m, tm, SUBLANES)
    else:
        tm = _tile(seq, tm, LANES)
    tn = _tile(n_out, tn)
    n_w = len(col_starts)
    assert all(c % tn == 0 for c in col_starts) and m % tm == 0 and n_out % tn == 0
    has_bias = bias is not None
    per_b = None if seq is None else seq // tm

    in_specs = [pl.BlockSpec(memory_space=pltpu.HBM)]
    args = [x]
    blocks = [((tm, tn), out_dtype)]
    if x2 is not None:
        m2 = x2.shape[0]
        in_specs.append(pl.BlockSpec((m2, k), lambda j, i: (0, 0)))
        args.append(x2)
        blocks += [((m2, k), x2.dtype), ((m2, tn), out_dtype)]
    in_specs.append(pl.BlockSpec(memory_space=pltpu.HBM))
    args.append(w)
    if has_bias:
        b3 = bias.reshape(bias.shape[0], 1, bias.shape[1])
        for c in col_starts:
            off = c // tn
            in_specs.append(pl.BlockSpec((None, 1, tn), lambda j, i, off=off: (layer, 0, off + j)))
            args.append(b3)
    n_i = m // tm
    if cast is not None:
        w_other, layer_other = cast
        kc, dc = w_other.shape[1:]
        slab = kc // ((n_out // tn) * n_i)
        assert slab * (n_out // tn) * n_i == kc and slab % BF16_SUBLANES == 0
        in_specs.append(pl.BlockSpec((None, slab, dc), lambda j, i: (layer_other, j * n_i + i, 0)))
        args.append(w_other)
        blocks += [((slab, dc), F32), ((slab, dc), BF16)]
    scratch = [((k, tn), BF16)] * n_w + [((n_w, 2, k, tn), F32), ((3, tm, k), x.dtype)]
    body = functools.partial(_mm_body, layer=layer, offs=tuple(c // tn for c in col_starts),
                             has_bias=has_bias, act=act, scale=scale, out_mode=out_mode,
                             has_x2=x2 is not None, has_cast=cast is not None)
    if out_mode == "plain":
        out_shape = [jax.ShapeDtypeStruct((m, n_out), out_dtype)]
        out_specs = [pl.BlockSpec((tm, tn), lambda j, i: (i, j))]
    elif out_mode == "t":
        out_shape = [jax.ShapeDtypeStruct((m // seq, n_out, seq), out_dtype)]
        out_specs = [pl.BlockSpec((None, tn, tm), lambda j, i: (i // per_b, j, i % per_b))]
    else:
        assert tn == n_out
        half = n_out // 2
        out_shape = [jax.ShapeDtypeStruct((m, half), out_dtype),
                     jax.ShapeDtypeStruct((m // seq, half, seq), out_dtype)]
        out_specs = [pl.BlockSpec((tm, half), lambda j, i: (i, 0)),
                     pl.BlockSpec((None, half, tm), lambda j, i: (i // per_b, 0, i % per_b))]
    if x2 is not None:
        out_shape.append(jax.ShapeDtypeStruct((m2, n_out), out_dtype))
        out_specs.append(pl.BlockSpec((m2, tn), lambda j, i: (0, j)))
    if cast is not None:
        out_shape.append(jax.ShapeDtypeStruct((kc, dc), BF16))
        out_specs.append(pl.BlockSpec((slab, dc), lambda j, i: (j * n_i + i, 0)))
    outs = pl.pallas_call(
        body,
        out_shape=out_shape,
        grid=(n_out // tn, n_i),
        in_specs=in_specs,
        out_specs=out_specs,
        scratch_shapes=[pltpu.VMEM(s, d) for s, d in scratch]
        + [pltpu.SemaphoreType.DMA((n_w, 2)), pltpu.SemaphoreType.DMA((3,))],
        compiler_params=pltpu.CompilerParams(
            dimension_semantics=("arbitrary", "arbitrary"),
            vmem_limit_bytes=_vmem_limit(blocks, scratch)),
        name=name,
    )(*args)
    n_primary = 2 if out_mode == "split_t" else 1
    primary = outs[0] if n_primary == 1 else outs[:n_primary]
    extras = list(outs[n_primary:])
    return (primary, *extras) if extras else primary


ROWMM_SUB_ROWS = 256
ROWMM_K_CHUNKS = 4
MXU_K = 256


def _rowmm_body(*refs, has_bias, x_t, final, single_step):
    x_ref, w_hbm = refs[:2]
    refs = refs[2:]
    if has_bias:
        b_ref, refs = refs[0], refs[1:]
    res_ref, gate_ref, ng_ref = refs[:3]
    refs = refs[3:]
    if not final:
        sh_ref, sc_ref = refs[:2]
        refs = refs[2:]
    n_o = 1 if final else 2
    o_refs = refs[:n_o]
    refs = refs[n_o:]
    w_res, sem = refs
    k = w_res.shape[0]
    tiles = k // MXU_K
    n_chunks = min(ROWMM_K_CHUNKS, tiles)
    bounds = [MXU_K * (tiles * c // n_chunks) for c in range(n_chunks + 1)]

    def chunk_copy(c):
        rows = pl.ds(bounds[c], bounds[c + 1] - bounds[c])
        return pltpu.make_async_copy(w_hbm.at[rows, :], w_res.at[rows, :], sem.at[c])

    tm = res_ref.shape[0]
    sub = min(tm, ROWMM_SUB_ROWS)

    def lhs(r0, k0=0, k1=k):
        return x_ref[k0:k1, r0:r0 + sub].T if x_t else x_ref[r0:r0 + sub, k0:k1]

    def groups_of(ref, r0):
        per = tm // ref.shape[0]
        return ref[...] if per >= tm else ref[r0 // per:(r0 + sub) // per]

    def finish(r0, y):
        if has_bias:
            y = y + b_ref[...]
        gate = groups_of(gate_ref, r0)
        xn = (_grouped(res_ref[r0:r0 + sub, :], gate) + gate * _grouped(y, gate)).reshape(y.shape)
        r = xn * lax.rsqrt(jnp.mean(xn * xn, axis=-1, keepdims=True) + EPS) * ng_ref[...]
        if final:
            o_refs[0][r0:r0 + sub, :] = r
        else:
            sc = groups_of(sc_ref, r0)
            o_refs[0][r0:r0 + sub, :] = xn
            o_refs[1][r0:r0 + sub, :] = (
                _grouped(r, sc) * (1.0 + sc) + groups_of(sh_ref, r0)
            ).reshape(y.shape).astype(o_refs[1].dtype)

    for c in range(n_chunks):
        @pl.when(pl.program_id(0) == 0)
        def _():
            chunk_copy(c).start()

    first = 0
    if single_step:
        y = None
        for c in range(n_chunks):
            chunk_copy(c).wait()
            part = jnp.dot(lhs(0, bounds[c], bounds[c + 1]), w_res[bounds[c]:bounds[c + 1], :],
                           preferred_element_type=F32)
            y = part if y is None else y + part
        finish(0, y)
        first = sub
    else:
        for c in range(n_chunks):
            @pl.when(pl.program_id(0) == 0)
            def _():
                chunk_copy(c).wait()

    for r0 in range(first, tm, sub):
        finish(r0, jnp.dot(lhs(r0), w_res[...], preferred_element_type=F32))


def _rowmm_call(x, w, res, gate, gate_seg, nxt, *, bias=None, x_t=False, tm=ROWMM_ROWS,
                name="rowmm"):
    if x_t:
        nb, k, seq = x.shape
        m = nb * seq
    else:
        m, k = x.shape
    d = w.shape[-1]
    final = len(nxt) == 1
    scratch = [((k, d), BF16)]
    n_mods = 1 if final else 3

    def vmem_need(rows):
        blocks = [((rows, k), BF16), ((rows, d), F32), ((rows, d), F32 if final else BF16)]
        blocks += [((rows, d), F32)] * (0 if final else 1)
        blocks += [((-(-rows // gate.rows_per_group) * SUBLANES, d), F32)] * n_mods
        temporaries = 4 * _nbytes((rows, d), F32)
        return (2 * sum(_nbytes(s, t) for s, t in blocks) + sum(_nbytes(s, t) for s, t in scratch)
                + temporaries)

    mult = LANES if x_t else SUBLANES
    tm = gate.tile_rows(m, tm, mult)
    rpg = gate.rows_per_group
    fits_groups = lambda t: t % mult == 0 and (rpg % t == 0 or t % rpg == 0)
    while vmem_need(tm) > V7X_VMEM_CAP and tm % 2 == 0 and fits_groups(tm // 2):
        tm //= 2
    assert final or nxt[1].rows_per_group == gate.rows_per_group
    has_bias = bias is not None
    row = lambda i: (i, 0)
    if x_t:
        per_b = seq // tm
        in_specs = [pl.BlockSpec((None, k, tm), lambda i: (i // per_b, 0, i % per_b))]
    else:
        in_specs = [pl.BlockSpec((tm, k), row)]
    in_specs.append(pl.BlockSpec(memory_space=pltpu.HBM))
    args = [x, w]
    if has_bias:
        in_specs.append(pl.BlockSpec((1, d), lambda i: (0, 0)))
        args.append(bias.reshape(1, d))
    in_specs += [pl.BlockSpec((tm, d), row), gate.spec(gate_seg, d, tm),
                 pl.BlockSpec((1, d), lambda i: (0, 0))]
    args += [res, gate.arr, nxt[0].reshape(1, d)]
    if final:
        out_shape = jax.ShapeDtypeStruct((m, d), F32)
        out_specs = pl.BlockSpec((tm, d), row)
    else:
        _, nmod, seg_shift, seg_scale = nxt
        in_specs += [nmod.spec(seg_shift, d, tm), nmod.spec(seg_scale, d, tm)]
        args += [nmod.arr, nmod.arr]
        out_shape = [jax.ShapeDtypeStruct((m, d), F32), jax.ShapeDtypeStruct((m, d), BF16)]
        out_specs = [pl.BlockSpec((tm, d), row), pl.BlockSpec((tm, d), row)]
    return pl.pallas_call(
        functools.partial(_rowmm_body, has_bias=has_bias, x_t=x_t, final=final,
                          single_step=m == tm),
        out_shape=out_shape,
        grid=(m // tm,),
        in_specs=in_specs,
        out_specs=out_specs,
        scratch_shapes=[pltpu.VMEM(s, dt) for s, dt in scratch]
        + [pltpu.SemaphoreType.DMA((ROWMM_K_CHUNKS,))],
        compiler_params=pltpu.CompilerParams(
            dimension_semantics=("arbitrary",),
            vmem_limit_bytes=min(V7X_VMEM_CAP, max(32 * 2**20, vmem_need(tm) + 4 * 2**20))),
        name=name,
    )(*args)


def _conv_ln_swish(up_ref, pad, tt, w_ref, bdw_ref, lg_ref, lb_ref, acc_ref, o_ref):
    kw, d = w_ref.shape
    by_shift = [[(a, SUBLANES * a + s - pad) for a in range((pad + kw - 1) // SUBLANES + 1)
                 if 0 <= SUBLANES * a + s - pad < kw] for s in range(SUBLANES)]

    def strip(c, carry):
        cols = pl.ds(pl.multiple_of(c * LANES, LANES), LANES)
        z = bdw_ref[:, cols]
        for s, taps in enumerate(by_shift):
            n = tt + SUBLANES if s else tt
            q = None
            for a, k in taps:
                term = up_ref[pl.ds(SUBLANES * a, n), cols] * w_ref[pl.ds(k, 1), cols]
                q = term if q is None else q + term
            if q is not None:
                z = z + q[s:s + tt]
        acc_ref[:, cols] = z
        return carry

    lax.fori_loop(0, d // LANES, strip, 0)
    z = acc_ref[...]
    mu = jnp.mean(z, axis=-1, keepdims=True)
    zc = z - mu
    y = zc * lax.rsqrt(jnp.mean(zc * zc, axis=-1, keepdims=True) + EPS)
    y = y * lg_ref[...] + lb_ref[...]
    o_ref[...] = (y * _sigmoid(y)).astype(o_ref.dtype)


def _conv_prompt_body(main_ref, halo_ref, w_ref, bdw_ref, lg_ref, lb_ref, o_ref, up_ref, acc_ref,
                      *, halo, tt):
    kw = w_ref.shape[0]

    @pl.when(pl.program_id(1) == 0)
    def _():
        up_ref[0:halo, :] = jnp.zeros((halo, up_ref.shape[1]), F32)

    @pl.when(pl.program_id(1) > 0)
    def _():
        up_ref[0:halo, :] = halo_ref[...]

    up_ref[halo:halo + tt, :] = main_ref[...]
    _conv_ln_swish(up_ref, halo - (kw - 1), tt, w_ref, bdw_ref, lg_ref, lb_ref, acc_ref, o_ref)


def _conv_prompt_call(u, w_dw, b_dw, ln_g, ln_b, layer, tt=CONV_ROWS):
    b, t, d = u.shape
    kw = w_dw.shape[1]
    halo = -(-(kw - 1) // SUBLANES) * SUBLANES
    tt = _tile(t, tt, halo)
    hb = tt // halo
    vec = lambda: pl.BlockSpec((None, 1, d), lambda bi, ti: (layer, 0, 0))
    blocks = [((tt, d), F32), ((halo, d), F32), ((kw, d), F32), ((tt, d), BF16)]
    scratch = [((halo + tt, d), F32), ((tt, d), F32)]
    return pl.pallas_call(
        functools.partial(_conv_prompt_body, halo=halo, tt=tt),
        out_shape=jax.ShapeDtypeStruct((b, t, d), BF16),
        grid=(b, t // tt),
        in_specs=[
            pl.BlockSpec((None, tt, d), lambda bi, ti: (bi, ti, 0)),
            pl.BlockSpec((None, halo, d), lambda bi, ti: (bi, jnp.maximum(ti * hb - 1, 0), 0)),
            pl.BlockSpec((None, kw, d), lambda bi, ti: (layer, 0, 0)),
            vec(), vec(), vec(),
        ],
        out_specs=pl.BlockSpec((None, tt, d), lambda bi, ti: (bi, ti, 0)),
        scratch_shapes=[pltpu.VMEM(s, dt) for s, dt in scratch],
        compiler_params=pltpu.CompilerParams(
            dimension_semantics=("parallel", "arbitrary"),
            vmem_limit_bytes=_vmem_limit(blocks, scratch)),
        name="dwconv_ln_swish_prompt",
    )(u, u, w_dw, b_dw.reshape(-1, 1, d), ln_g.reshape(-1, 1, d), ln_b.reshape(-1, 1, d))


def _conv_sample_body(up_ref, w_ref, bdw_ref, lg_ref, lb_ref, o_ref, acc_ref, *, pad, tt):
    _conv_ln_swish(up_ref, pad, tt, w_ref, bdw_ref, lg_ref, lb_ref, acc_ref, o_ref)


def _conv_sample_call(up, pad, tt, w_dw, b_dw, ln_g, ln_b, layer):
    b, rows, d = up.shape
    kw = w_dw.shape[1]
    vec = lambda: pl.BlockSpec((None, 1, d), lambda bi: (layer, 0, 0))
    blocks = [((rows, d), F32), ((kw, d), F32), ((tt, d), BF16)]
    scratch = [((tt, d), F32)]
    return pl.pallas_call(
        functools.partial(_conv_sample_body, pad=pad, tt=tt),
        out_shape=jax.ShapeDtypeStruct((b, tt, d), BF16),
        grid=(b,),
        in_specs=[
            pl.BlockSpec((None, rows, d), lambda bi: (bi, 0, 0)),
            pl.BlockSpec((None, kw, d), lambda bi: (layer, 0, 0)),
            vec(), vec(), vec(),
        ],
        out_specs=pl.BlockSpec((None, tt, d), lambda bi: (bi, 0, 0)),
        scratch_shapes=[pltpu.VMEM(s, dt) for s, dt in scratch],
        compiler_params=pltpu.CompilerParams(
            dimension_semantics=("parallel",), vmem_limit_bytes=_vmem_limit(blocks, scratch)),
        name="dwconv_ln_swish_sample",
    )(up, w_dw, b_dw.reshape(-1, 1, d), ln_g.reshape(-1, 1, d), ln_b.reshape(-1, 1, d))


def _bucket_codes(n_q, n_k, k_off, n_buckets):
    rel = (np.arange(n_k) + k_off)[None, :] - np.arange(n_q)[:, None]
    nb = n_buckets // 2
    max_exact = nb // 2
    ret = np.where(rel > 0, nb, 0)
    n = np.abs(rel)
    nf = np.maximum(n, 1).astype(np.float32)
    large = max_exact + (np.log(nf / np.float32(max_exact))
                         / np.float32(math.log(MAX_DISTANCE / max_exact))
                         * np.float32(nb - max_exact)).astype(np.int32)
    large = np.minimum(large, nb - 1)
    return (ret + np.where(n < max_exact, n, large)).astype(np.int32)


def _bias_body(code_ref, table_ref, o_ref, *, present, group):
    n_cols = code_ref.shape[1]

    used = sorted({b for _, buckets in present for b in buckets})

    def head(hh, carry):
        value = {b: table_ref[hh, b] * LOG2E for b in used}
        for r0, buckets in present:
            rows = slice(r0, r0 + SUBLANES)
            code = code_ref[rows, :]
            out = jnp.full(code.shape, -jnp.inf, F32)
            for b in buckets:
                out = jnp.where(code == b, value[b], out)
            if group is None:
                o_ref[hh, rows, :] = out
            else:
                cols = pl.ds(pl.multiple_of((hh % group) * n_cols, LANES), n_cols)
                o_ref[hh // group, rows, cols] = out
        return carry

    lax.fori_loop(0, table_ref.shape[0], head, 0)


def _bias_call(code, table, group=None):
    n_heads, n_buckets = table.shape
    rows, cols = code.shape
    shape = (n_heads, rows, cols) if group is None else (n_heads // group, rows, group * cols)
    present = tuple(
        (r0, tuple(int(b) for b in np.unique(code[r0:r0 + SUBLANES]) if b < n_buckets))
        for r0 in range(0, rows, SUBLANES))
    return pl.pallas_call(
        functools.partial(_bias_body, present=present, group=group),
        out_shape=jax.ShapeDtypeStruct(shape, F32),
        in_specs=[pl.BlockSpec(memory_space=pltpu.VMEM), pl.BlockSpec(memory_space=pltpu.SMEM)],
        out_specs=pl.BlockSpec(memory_space=pltpu.VMEM),
        name="rel_bias",
    )(jnp.asarray(code), table)


def _unit_scores(qs, kh, bias):
    return lax.dot_general(qs, kh, (((1,), (1,)), ((), ())), preferred_element_type=F32) + bias


def _unit_output(s, vh, sink_col):
    sink_col = sink_col * LOG2E
    m = jnp.maximum(jnp.max(s, axis=-1, keepdims=True), sink_col)
    e = jnp.exp2(s - m)
    l = jnp.sum(e, axis=-1, keepdims=True) + jnp.exp2(sink_col - m)
    return jnp.dot(e.astype(BF16), vh, preferred_element_type=F32) * (1.0 / l)


def _stack_heads(q, h, group):
    hd = HEAD_DIM
    return jnp.concatenate(
        [q[:, (h * group + g) * hd:(h * group + g + 1) * hd] for g in range(group)], axis=0)


def _unstack_heads(o, group):
    tq = o.shape[0] // group
    return [o[g * tq:(g + 1) * tq, :] for g in range(group)]


ATTN_Q_COLS = 512
PAIR = 2 * CHUNK


def _pair_codes(n_buckets):
    code = _bucket_codes(PAIR, WINDOW + PAIR, -WINDOW, n_buckets).T
    key_chunk = np.arange(WINDOW + PAIR)[:, None] // CHUNK
    q_chunk = np.arange(PAIR)[None, :] // CHUNK
    visible = (key_chunk >= q_chunk) & (key_chunk <= q_chunk + WINDOW // CHUNK)
    return np.where(visible, code, n_buckets).astype(np.int32)


def _attn_prompt_body(qT_ref, kp_ref, kc_ref, vTp_ref, vTc_ref, bias_ref, sink_ref, oT_ref,
                      k_scr, vT_scr, *, kvh, group):
    i = pl.program_id(1)
    tq = qT_ref.shape[1]
    nk = WINDOW + PAIR
    hd = HEAD_DIM
    for h in range(kvh):
        k_scr[h, 0:WINDOW, :] = kp_ref[:, h * hd:(h + 1) * hd].astype(BF16)
        k_scr[h, WINDOW:WINDOW + tq, :] = kc_ref[:, h * hd:(h + 1) * hd].astype(BF16)
    vT_scr[:, 0:WINDOW] = vTp_ref[...].astype(BF16)
    vT_scr[:, WINDOW:WINDOW + tq] = vTc_ref[...].astype(BF16)
    row = lax.broadcasted_iota(jnp.int32, (nk, group * PAIR), 0)
    start_mask = jnp.where(row < jnp.where(i == 0, WINDOW, 0), -jnp.inf, 0.0).astype(F32)
    units = [(p * PAIR, h) for p in range(tq // PAIR) for h in range(kvh)]

    def scores(c0, h):
        qsT = jnp.concatenate(
            [qT_ref[hh * hd:(hh + 1) * hd, c0:c0 + PAIR] for hh in range(h * group, (h + 1) * group)],
            axis=1)
        s = jnp.dot(k_scr[h, c0:c0 + nk, :], qsT, preferred_element_type=F32) + bias_ref[h]
        return s + start_mask if c0 == 0 else s

    s_next = scores(*units[0])
    for n, (c0, h) in enumerate(units):
        s = s_next
        if n + 1 < len(units):
            s_next = scores(*units[n + 1])
        sink = sink_ref[h] * LOG2E
        m = jnp.maximum(jnp.max(s, axis=0, keepdims=True), sink)
        e = jnp.exp2(s - m)
        l = jnp.sum(e, axis=0, keepdims=True) + jnp.exp2(sink - m)
        oT = jnp.dot(vT_scr[h * hd:(h + 1) * hd, c0:c0 + nk], e.astype(BF16),
                     preferred_element_type=F32) * (1.0 / l)
        for g in range(group):
            hh = h * group + g
            oT_ref[hh * hd:(hh + 1) * hd, c0:c0 + PAIR] = (
                oT[:, g * PAIR:(g + 1) * PAIR].astype(oT_ref.dtype))


def _attn_prompt_call(qT, k, vT, bias, sink_row, kvh, group):
    b, dq, s = qT.shape
    dk = k.shape[2]
    tq = _tile(s, ATTN_Q_COLS, PAIR)
    per = tq // WINDOW
    prev = lambda i: jnp.maximum(i * per - 1, 0)
    blocks = [((dq, tq), BF16)] * 2 + [((WINDOW + tq, dk), F32)] * 2 + [
        (bias.shape, F32), (sink_row.shape[:1] + (SUBLANES, sink_row.shape[2]), F32)]
    scratch = [((kvh, WINDOW + tq, LANES), BF16), ((dk, WINDOW + tq), BF16)]
    return pl.pallas_call(
        functools.partial(_attn_prompt_body, kvh=kvh, group=group),
        out_shape=jax.ShapeDtypeStruct((b, dq, s), BF16),
        grid=(b, s // tq),
        in_specs=[
            pl.BlockSpec((None, dq, tq), lambda bi, i: (bi, 0, i)),
            pl.BlockSpec((None, WINDOW, dk), lambda bi, i: (bi, prev(i), 0)),
            pl.BlockSpec((None, tq, dk), lambda bi, i: (bi, i, 0)),
            pl.BlockSpec((None, dk, WINDOW), lambda bi, i: (bi, 0, prev(i))),
            pl.BlockSpec((None, dk, tq), lambda bi, i: (bi, 0, i)),
            pl.BlockSpec(bias.shape, lambda bi, i: (0, 0, 0)),
            pl.BlockSpec(sink_row.shape, lambda bi, i: (0, 0, 0)),
        ],
        out_specs=pl.BlockSpec((None, dq, tq), lambda bi, i: (bi, 0, i)),
        scratch_shapes=[pltpu.VMEM((kvh, WINDOW + tq, HEAD_DIM), BF16),
                        pltpu.VMEM((dk, WINDOW + tq), BF16)],
        compiler_params=pltpu.CompilerParams(
            dimension_semantics=("parallel", "arbitrary"),
            vmem_limit_bytes=_vmem_limit(blocks, scratch)),
        name="swa_sink_attention_prompt",
    )(qT, k, k, vT, vT, bias, sink_row)


ATTN_SAMPLE_BATCH = 4


def _attn_sample_body(q_ref, k_ref, v_ref, bias_ref, sink_ref, o_ref, *, kvh, group):
    hd = HEAD_DIM
    units = [(b, h) for b in range(q_ref.shape[0]) for h in range(kvh)]

    def scores(b, h):
        qs = _stack_heads(q_ref[b].astype(F32), h, group).astype(BF16)
        return _unit_scores(qs, k_ref[b, :, h * hd:(h + 1) * hd].astype(BF16), bias_ref[h])

    s_next = scores(*units[0])
    outs = []
    for n, (b, h) in enumerate(units):
        s = s_next
        if n + 1 < len(units):
            s_next = scores(*units[n + 1])
        o = _unit_output(s, v_ref[b, :, h * hd:(h + 1) * hd].astype(BF16), sink_ref[h])
        outs += _unstack_heads(o, group)
        if h == kvh - 1:
            o_ref[b] = jnp.concatenate(outs, axis=1).astype(o_ref.dtype)
            outs = []


def _attn_sample_call(q, k_all, v_all, bias, sink_col, kvh, group):
    b, t, dq = q.shape
    nk, dk = k_all.shape[1:]
    bb = math.gcd(b, ATTN_SAMPLE_BATCH)
    blocks = [((bb, t, dq), BF16)] * 2 + [((bb, nk, dk), F32)] * 2 + [
        (bias.shape[:2] + (2 * LANES,), F32), (sink_col.shape[:2] + (LANES,), F32)]
    return pl.pallas_call(
        functools.partial(_attn_sample_body, kvh=kvh, group=group),
        out_shape=jax.ShapeDtypeStruct((b, t, dq), BF16),
        grid=(b // bb,),
        in_specs=[
            pl.BlockSpec((bb, t, dq), lambda bi: (bi, 0, 0)),
            pl.BlockSpec((bb, nk, dk), lambda bi: (bi, 0, 0)),
            pl.BlockSpec((bb, nk, dk), lambda bi: (bi, 0, 0)),
            pl.BlockSpec(bias.shape, lambda bi: (0, 0, 0)),
            pl.BlockSpec(sink_col.shape, lambda bi: (0, 0, 0)),
        ],
        out_specs=pl.BlockSpec((bb, t, dq), lambda bi: (bi, 0, 0)),
        compiler_params=pltpu.CompilerParams(
            dimension_semantics=("parallel",), vmem_limit_bytes=_vmem_limit(blocks)),
        name="swa_sink_attention_sample",
    )(q, k_all, v_all, bias, sink_col)


def _trunks(xp3, xs3, mods_p, mods_s, state_conv, win_k, win_v, p):
    bp, tp, d = xp3.shape
    bs, ts, _ = xs3.shape
    mp, ms = bp * tp, bs * ts
    depth = p["w_mod"].shape[0]
    d_ff = p["w_gu"].shape[2] // 2
    n_heads = p["attn_sinks"].shape[1]
    dq = n_heads * HEAD_DIM
    kvh = (p["w_qkv"].shape[2] - dq) // (2 * HEAD_DIM)
    group = n_heads // kvh
    dkv = kvh * HEAD_DIM
    kw = p["w_dw"].shape[1]
    n_buckets = p["rel_bias_table"].shape[1]
    conv_p, k_p, v_p, conv_s, k_s, v_s = [], [], [], [], [], []

    xp, xs = xp3.reshape(mp, d), xs3.reshape(ms, d)
    hp = _norm_mod_call(xp, p["norm_mix"][0], mods_p[0], 0, 1, PANEL_ROWS)
    hs = _norm_mod_call(xs, p["norm_mix"][0], mods_s[0], 0, 1, ms)
    for i in range(depth):
        j = i // 2
        ffn_norm = lambda mods: (p["norm_ffn"][i], mods[i], 3, 4)
        if i % 2 == 0:
            u_p, u_s, wq = _mm_call(hp, p["w_pw1"], j, (0, d), d, x2=hs, cast=(p["w_pw2"], j),
                                    bias=p["b_pw1"], act="glu", name="pw1_glu")
            u_p, u_s = u_p.reshape(bp, tp, d), u_s.reshape(bs, ts, d)
            conv_w = (p["w_dw"], p["b_dw"], p["conv_ln_g"], p["conv_ln_b"], j)
            z_p = _conv_prompt_call(u_p, *conv_w)
            conv_p.append(u_p[:, tp - (kw - 1):])
            pad = (-(kw - 1 + ts)) % SUBLANES
            up = jnp.concatenate([jnp.zeros((bs, pad, d), F32), state_conv[j], u_s], axis=1)
            z_s = _conv_sample_call(up, pad, ts, *conv_w)
            conv_s.append(up[:, pad + ts:])
            xp, hp = _rowmm_call(z_p.reshape(mp, d), wq, xp, mods_p[i], 2, ffn_norm(mods_p),
                                 bias=p["b_pw2"][j], name="pw2_residual_norm")
            xs, hs = _rowmm_call(z_s.reshape(ms, d), wq, xs, mods_s[i], 2, ffn_norm(mods_s),
                                 bias=p["b_pw2"][j], name="pw2_residual_norm")
        else:
            sinks = p["attn_sinks"][j].reshape(kvh, group, 1)
            qT, q_s, wq = _mm_call(hp, p["w_qkv"], j, (0,), dq, x2=hs, cast=(p["w_o"], j),
                                   bias=p["b_qkv"], scale=LOG2E * HEAD_DIM ** -0.5,
                                   out_dtype=BF16, tn=Q_PANEL_COLS, name="q_proj",
                                   out_mode="t", seq=tp)
            (k, vT), kv_s = _mm_call(hp, p["w_qkv"], j, (dq,), 2 * dkv, x2=hs, bias=p["b_qkv"],
                                     tn=2 * dkv, name="kv_proj", out_mode="split_t", seq=tp)
            k = k.reshape(bp, tp, dkv)
            bias = _bias_call(_pair_codes(n_buckets), p["rel_bias_table"], group)
            sink_row = jnp.repeat(sinks, PAIR, axis=2).reshape(kvh, 1, group * PAIR)
            oT = _attn_prompt_call(qT, k, vT, bias, sink_row, kvh, group)
            k_p.append(k[:, tp - WINDOW:].reshape(bp, WINDOW, kvh, HEAD_DIM))
            v_p.append(jnp.swapaxes(vT[:, :, tp - WINDOW:], 1, 2)
                       .reshape(bp, WINDOW, kvh, HEAD_DIM))
            kv3 = kv_s.reshape(bs, ts, 2, kvh, HEAD_DIM)
            k_all = jnp.concatenate([win_k[j], kv3[:, :, 0]], axis=1)
            v_all = jnp.concatenate([win_v[j], kv3[:, :, 1]], axis=1)
            n_keys = WINDOW + ts
            bias = _bias_call(_bucket_codes(ts, n_keys, -WINDOW, n_buckets),
                              p["rel_bias_table"]).reshape(kvh, group * ts, n_keys)
            o_s = _attn_sample_call(q_s.reshape(bs, ts, dq), k_all.reshape(bs, n_keys, dkv),
                                    v_all.reshape(bs, n_keys, dkv), bias,
                                    jnp.repeat(sinks, ts, axis=1), kvh, group)
            k_s.append(k_all[:, ts:])
            v_s.append(v_all[:, ts:])
            xp, hp = _rowmm_call(oT, wq, xp, mods_p[i], 2, ffn_norm(mods_p), bias=p["b_o"][j],
                                 x_t=True, name="wo_residual_norm")
            xs, hs = _rowmm_call(o_s.reshape(ms, dq), wq, xs, mods_s[i], 2, ffn_norm(mods_s),
                                 bias=p["b_o"][j], name="wo_residual_norm")
        a_p, a_s, wq = _mm_call(hp, p["w_gu"], i, (0, d_ff), d_ff, x2=hs, cast=(p["w_down"], i),
                                act="swiglu", out_dtype=BF16, name="ffn_gate_up")
        if i + 1 < depth:
            nxt = lambda mods: (p["norm_mix"][i + 1], mods[i + 1], 0, 1)
            xp, hp = _rowmm_call(a_p, wq, xp, mods_p[i], 5, nxt(mods_p),
                                 name="ffn_down_residual_norm")
            xs, hs = _rowmm_call(a_s, wq, xs, mods_s[i], 5, nxt(mods_s),
                                 name="ffn_down_residual_norm")
        else:
            y_p = _rowmm_call(a_p, wq, xp, mods_p[i], 5, (p["norm_out"],),
                              name="ffn_down_residual_final_norm")
            y_s = _rowmm_call(a_s, wq, xs, mods_s[i], 5, (p["norm_out"],),
                              name="ffn_down_residual_final_norm")
    stack = jnp.stack
    return (y_p.reshape(bp, tp, d), y_s.reshape(bs, ts, d), stack(conv_p), stack(k_p), stack(v_p),
            stack(conv_s), stack(k_s), stack(v_s))


def kernel(x_prompt, x_sample, c_prompt, c_sample, state_conv, cache_win_k, cache_win_v, w_mod, b_mod, norm_mix, norm_ffn, w_pw1, b_pw1, w_dw, b_dw, conv_ln_g, conv_ln_b, w_pw2, b_pw2, w_qkv, b_qkv, w_o, b_o, attn_sinks, rel_bias_table, w_gu, w_down, norm_out):
    p = dict(w_mod=w_mod, b_mod=b_mod, norm_mix=norm_mix, norm_ffn=norm_ffn, w_pw1=w_pw1,
             b_pw1=b_pw1, w_dw=w_dw, b_dw=b_dw, conv_ln_g=conv_ln_g, conv_ln_b=conv_ln_b,
             w_pw2=w_pw2, b_pw2=b_pw2, w_qkv=w_qkv, b_qkv=b_qkv, w_o=w_o, b_o=b_o,
             attn_sinks=attn_sinks, rel_bias_table=rel_bias_table, w_gu=w_gu, w_down=w_down,
             norm_out=norm_out)
    bp, sp, d = x_prompt.shape
    bs, ts, _ = x_sample.shape
    depth = w_mod.shape[0]

    n_c = bp + bs
    c_all = jnp.concatenate(
        [c_prompt, c_sample, jnp.zeros((-n_c % BF16_SUBLANES, d), F32)], axis=0)
    mod_all = _mod_call(c_all, w_mod, b_mod)

    mods_p = [_Mod(mod_all[l, :bp].reshape(bp, 1, 6 * d), sp) for l in range(depth)]
    mods_s = [_Mod(mod_all[l, bp:n_c].reshape(bs, 1, 6 * d), ts) for l in range(depth)]

    return _trunks(x_prompt, x_sample, mods_p, mods_s, state_conv, cache_win_k, cache_win_v, p)
```

```python
import functools
import math

import numpy as np
import jax
import jax.numpy as jnp
from jax import lax
from jax.experimental import pallas as pl
from jax.experimental.pallas import tpu as pltpu

F32 = jnp.float32
BF16 = jnp.bfloat16

CHUNK = 64
WINDOW = 128
HEAD_DIM = 64
MAX_DISTANCE = 128
EPS = 1e-6
LOG2E = math.log2(math.e)

V7X_VMEM_BYTES = 64 * 2**20
V7X_VMEM_CAP = V7X_VMEM_BYTES - 8 * 2**20
SUBLANES = 8
LANES = 128
BF16_SUBLANES = 2 * SUBLANES

PANEL_ROWS = 1024
PANEL_COLS = 512
Q_PANEL_COLS = 1024
ROWMM_ROWS = 512
CONV_ROWS = 128
MOD_COLS = 1024


def _nbytes(shape, dtype):
    return math.prod(shape) * jnp.dtype(dtype).itemsize


def _vmem_limit(blocks, scratch=()):
    est = 2 * sum(_nbytes(s, d) for s, d in blocks) + sum(_nbytes(s, d) for s, d in scratch)
    return int(min(V7X_VMEM_CAP, max(32 * 2**20, 2 * est)))


def _tile(dim, pref, mult=LANES):
    if dim <= pref:
        return dim
    t = (pref // mult) * mult
    while t >= mult:
        if dim % t == 0:
            return t
        t -= mult
    return dim


def _sigmoid(x):
    return 1.0 / (1.0 + jnp.exp(-x))


def _mod_body(c_ref, w_ref, b_ref, o_ref):
    c = c_ref[...]
    a = (c * _sigmoid(c)).astype(BF16)
    o_ref[...] = jnp.dot(a, w_ref[...].astype(BF16), preferred_element_type=F32) + b_ref[...]


def _mod_call(c_all, w_mod, b_mod):
    depth, d, n = w_mod.shape
    mp = c_all.shape[0]
    tn = _tile(n, MOD_COLS)
    blocks = [((mp, d), F32), ((d, tn), F32), ((1, tn), F32), ((mp, tn), F32)]
    return pl.pallas_call(
        _mod_body,
        out_shape=jax.ShapeDtypeStruct((depth, mp, n), F32),
        grid=(depth, n // tn),
        in_specs=[
            pl.BlockSpec((mp, d), lambda l, j: (0, 0)),
            pl.BlockSpec((None, d, tn), lambda l, j: (l, 0, j)),
            pl.BlockSpec((None, 1, tn), lambda l, j: (l, 0, j)),
        ],
        out_specs=pl.BlockSpec((None, mp, tn), lambda l, j: (l, 0, j)),
        compiler_params=pltpu.CompilerParams(
            dimension_semantics=("arbitrary", "arbitrary"),
            vmem_limit_bytes=_vmem_limit(blocks, [((d, tn), BF16)])),
        name="adaln_mod",
    )(c_all, w_mod, b_mod.reshape(depth, 1, n))


class _Mod:
    def __init__(self, arr, rows_per_group):
        self.arr = arr
        self.rows_per_group = rows_per_group

    def tile_rows(self, total_rows, pref, mult):
        if self.rows_per_group >= pref:
            return _tile(self.rows_per_group, pref, mult)
        return _tile(total_rows, pref, max(mult, self.rows_per_group))

    def spec(self, seg, d, tm):
        if tm <= self.rows_per_group:
            per = self.rows_per_group // tm
            return pl.BlockSpec((1, 1, d), lambda i: (i // per, 0, seg))
        return pl.BlockSpec((tm // self.rows_per_group, 1, d), lambda i: (i, 0, seg))


def _grouped(v, m):
    return v.reshape(m.shape[0], v.shape[0] // m.shape[0], v.shape[1])


def _norm_mod_body(x_ref, g_ref, sh_ref, sc_ref, o_ref):
    x = x_ref[...]
    y = x * lax.rsqrt(jnp.mean(x * x, axis=-1, keepdims=True) + EPS) * g_ref[...]
    sc = sc_ref[...]
    o_ref[...] = (_grouped(y, sc) * (1.0 + sc) + sh_ref[...]).reshape(x.shape).astype(o_ref.dtype)


def _norm_mod_call(x, g, mod, seg_shift, seg_scale, tm):
    m, d = x.shape
    tm = mod.tile_rows(m, tm, SUBLANES)
    blocks = [((tm, d), F32), ((1, d), F32), ((tm, d), BF16)]
    return pl.pallas_call(
        _norm_mod_body,
        out_shape=jax.ShapeDtypeStruct((m, d), BF16),
        grid=(m // tm,),
        in_specs=[
            pl.BlockSpec((tm, d), lambda i: (i, 0)),
            pl.BlockSpec((1, d), lambda i: (0, 0)),
            mod.spec(seg_shift, d, tm),
            mod.spec(seg_scale, d, tm),
        ],
        out_specs=pl.BlockSpec((tm, d), lambda i: (i, 0)),
        compiler_params=pltpu.CompilerParams(
            dimension_semantics=("parallel",), vmem_limit_bytes=_vmem_limit(blocks)),
        name="rmsnorm_modulate",
    )(x, g.reshape(1, d), mod.arr, mod.arr)


def _mm_body(x_ref, *refs, layer, offs, has_bias, act, scale, out_mode, has_x2, has_cast):
    n_w = len(offs)
    if has_x2:
        x2_ref, refs = refs[0], refs[1:]
    w_hbm, refs = refs[0], refs[1:]
    bs = refs[:n_w] if has_bias else ()
    refs = refs[len(bs):]
    if has_cast:
        wc_ref, refs = refs[0], refs[1:]
    n_o = 2 if out_mode == "split_t" else 1
    o_refs = refs[:n_o]
    refs = refs[n_o:]
    if has_x2:
        o2_ref, refs = refs[0], refs[1:]
    if has_cast:
        oc_ref, refs = refs[0], refs[1:]
        oc_ref[...] = wc_ref[...].astype(BF16)
    wbs, (stage, sem) = refs[:n_w], refs[n_w:]
    tn = wbs[0].shape[1]

    def panel_copy(i, j, slot):
        cols = pl.ds(pl.multiple_of((offs[i] + j) * tn, tn), tn)
        return pltpu.make_async_copy(w_hbm.at[layer, :, cols], stage.at[i, slot], sem.at[i, slot])

    def compute(x):
        ps = [None] * n_w
        for i in ((1, 0) if act == "glu" else range(n_w)):
            p = jnp.dot(x, wbs[i][...], preferred_element_type=F32)
            if has_bias:
                p = p + bs[i][...]
            ps[i] = p
        if act == "glu":
            y = ps[0] * _sigmoid(ps[1])
        elif act == "swiglu":
            y = ps[0] * _sigmoid(ps[0]) * ps[1]
        else:
            y = ps[0]
        return y * scale if scale != 1.0 else y

    @pl.when(pl.program_id(1) == 0)
    def _():
        j = pl.program_id(0)
        slot = j % 2

        @pl.when(j == 0)
        def _():
            for i in range(n_w):
                panel_copy(i, 0, 0).start()

        for i in range(n_w):
            panel_copy(i, j, slot).wait()

        @pl.when(j + 1 < pl.num_programs(0))
        def _():
            for i in range(n_w):
                panel_copy(i, j + 1, 1 - slot).start(priority=1)

        for i in range(n_w):
            wbs[i][...] = stage[i, slot].astype(BF16)
        if has_x2:
            o2_ref[...] = compute(x2_ref[...]).astype(o2_ref.dtype)

    y = compute(x_ref[...])
    if out_mode == "plain":
        o_refs[0][...] = y.astype(o_refs[0].dtype)
    elif out_mode == "t":
        o_refs[0][...] = y.T.astype(o_refs[0].dtype)
    else:
        half = y.shape[1] // 2
        o_refs[0][...] = y[:, :half].astype(o_refs[0].dtype)
        o_refs[1][...] = y[:, half:].T.astype(o_refs[1].dtype)


def _mm_call(x, w, layer, col_starts, n_out, *, x2=None, cast=None, bias=None, act=None,
             scale=1.0, out_dtype=F32, tm=PANEL_ROWS, tn=PANEL_COLS, name="matmul",
             out_mode="plain", seq=None):
    m, k = x.shape
    if out_mode == "plain":
        tm = _tile(m, tm, SUBLANES)
    else:
        tm = _tile(seq, tm, LANES)
    tn = _tile(n_out, tn)
    n_w = len(col_starts)
    assert all(c % tn == 0 for c in col_starts) and m % tm == 0 and n_out % tn == 0
    has_bias = bias is not None
    per_b = None if seq is None else seq // tm

    in_specs = [pl.BlockSpec((tm, k), lambda j, i: (i, 0))]
    args = [x]
    blocks = [((tm, k), x.dtype), ((tm, tn), out_dtype)]
    if x2 is not None:
        m2 = x2.shape[0]
        in_specs.append(pl.BlockSpec((m2, k), lambda j, i: (0, 0)))
        args.append(x2)
        blocks += [((m2, k), x2.dtype), ((m2, tn), out_dtype)]
    in_specs.append(pl.BlockSpec(memory_space=pltpu.HBM))
    args.append(w)
    if has_bias:
        b3 = bias.reshape(bias.shape[0], 1, bias.shape[1])
        for c in col_starts:
            off = c // tn
            in_specs.append(pl.BlockSpec((None, 1, tn), lambda j, i, off=off: (layer, 0, off + j)))
            args.append(b3)
    n_i = m // tm
    if cast is not None:
        w_other, layer_other = cast
        kc, dc = w_other.shape[1:]
        slab = kc // ((n_out // tn) * n_i)
        assert slab * (n_out // tn) * n_i == kc and slab % BF16_SUBLANES == 0
        in_specs.append(pl.BlockSpec((None, slab, dc), lambda j, i: (layer_other, j * n_i + i, 0)))
        args.append(w_other)
        blocks += [((slab, dc), F32), ((slab, dc), BF16)]
    scratch = [((k, tn), BF16)] * n_w + [((n_w, 2, k, tn), F32)]
    body = functools.partial(_mm_body, layer=layer, offs=tuple(c // tn for c in col_starts),
                             has_bias=has_bias, act=act, scale=scale, out_mode=out_mode,
                             has_x2=x2 is not None, has_cast=cast is not None)
    if out_mode == "plain":
        out_shape = [jax.ShapeDtypeStruct((m, n_out), out_dtype)]
        out_specs = [pl.BlockSpec((tm, tn), lambda j, i: (i, j))]
    elif out_mode == "t":
        out_shape = [jax.ShapeDtypeStruct((m // seq, n_out, seq), out_dtype)]
        out_specs = [pl.BlockSpec((None, tn, tm), lambda j, i: (i // per_b, j, i % per_b))]
    else:
        assert tn == n_out
        half = n_out // 2
        out_shape = [jax.ShapeDtypeStruct((m, half), out_dtype),
                     jax.ShapeDtypeStruct((m // seq, half, seq), out_dtype)]
        out_specs = [pl.BlockSpec((tm, half), lambda j, i: (i, 0)),
                     pl.BlockSpec((None, half, tm), lambda j, i: (i // per_b, 0, i % per_b))]
    if x2 is not None:
        out_shape.append(jax.ShapeDtypeStruct((m2, n_out), out_dtype))
        out_specs.append(pl.BlockSpec((m2, tn), lambda j, i: (0, j)))
    if cast is not None:
        out_shape.append(jax.ShapeDtypeStruct((kc, dc), BF16))
        out_specs.append(pl.BlockSpec((slab, dc), lambda j, i: (j * n_i + i, 0)))
    outs = pl.pallas_call(
        body,
        out_shape=out_shape,
        grid=(n_out // tn, n_i),
        in_specs=in_specs,
        out_specs=out_specs,
        scratch_shapes=[pltpu.VMEM(s, d) for s, d in scratch]
        + [pltpu.SemaphoreType.DMA((n_w, 2))],
        compiler_params=pltpu.CompilerParams(
            dimension_semantics=("arbitrary", "arbitrary"),
            vmem_limit_bytes=_vmem_limit(blocks, scratch)),
        name=name,
    )(*args)
    n_primary = 2 if out_mode == "split_t" else 1
    primary = outs[0] if n_primary == 1 else outs[:n_primary]
    extras = list(outs[n_primary:])
    return (primary, *extras) if extras else primary


ROWMM_SUB_ROWS = 256
ROWMM_K_CHUNKS = 4
MXU_K = 256


def _rowmm_body(*refs, has_bias, x_t, final, single_step):
    x_ref, w_hbm = refs[:2]
    refs = refs[2:]
    if has_bias:
        b_ref, refs = refs[0], refs[1:]
    res_ref, gate_ref, ng_ref = refs[:3]
    refs = refs[3:]
    if not final:
        sh_ref, sc_ref = refs[:2]
        refs = refs[2:]
    n_o = 1 if final else 2
    o_refs = refs[:n_o]
    refs = refs[n_o:]
    w_res, sem = refs
    k = w_res.shape[0]
    tiles = k // MXU_K
    n_chunks = min(ROWMM_K_CHUNKS, tiles)
    bounds = [MXU_K * (tiles * c // n_chunks) for c in range(n_chunks + 1)]

    def chunk_copy(c):
        rows = pl.ds(bounds[c], bounds[c + 1] - bounds[c])
        return pltpu.make_async_copy(w_hbm.at[rows, :], w_res.at[rows, :], sem.at[c])

    tm = res_ref.shape[0]
    sub = min(tm, ROWMM_SUB_ROWS)

    def lhs(r0, k0=0, k1=k):
        return x_ref[k0:k1, r0:r0 + sub].T if x_t else x_ref[r0:r0 + sub, k0:k1]

    def groups_of(ref, r0):
        per = tm // ref.shape[0]
        return ref[...] if per >= tm else ref[r0 // per:(r0 + sub) // per]

    def finish(r0, y):
        if has_bias:
            y = y + b_ref[...]
        gate = groups_of(gate_ref, r0)
        xn = (_grouped(res_ref[r0:r0 + sub, :], gate) + gate * _grouped(y, gate)).reshape(y.shape)
        r = xn * lax.rsqrt(jnp.mean(xn * xn, axis=-1, keepdims=True) + EPS) * ng_ref[...]
        if final:
            o_refs[0][r0:r0 + sub, :] = r
        else:
            sc = groups_of(sc_ref, r0)
            o_refs[0][r0:r0 + sub, :] = xn
            o_refs[1][r0:r0 + sub, :] = (
                _grouped(r, sc) * (1.0 + sc) + groups_of(sh_ref, r0)
            ).reshape(y.shape).astype(o_refs[1].dtype)

    for c in range(n_chunks):
        @pl.when(pl.program_id(0) == 0)
        def _():
            chunk_copy(c).start()

    first = 0
    if single_step:
        y = None
        for c in range(n_chunks):
            chunk_copy(c).wait()
            part = jnp.dot(lhs(0, bounds[c], bounds[c + 1]), w_res[bounds[c]:bounds[c + 1], :],
                           preferred_element_type=F32)
            y = part if y is None else y + part
        finish(0, y)
        first = sub
    else:
        for c in range(n_chunks):
            @pl.when(pl.program_id(0) == 0)
            def _():
                chunk_copy(c).wait()

    for r0 in range(first, tm, sub):
        finish(r0, jnp.dot(lhs(r0), w_res[...], preferred_element_type=F32))


def _rowmm_call(x, w, res, gate, gate_seg, nxt, *, bias=None, x_t=False, tm=ROWMM_ROWS,
                name="rowmm"):
    if x_t:
        nb, k, seq = x.shape
        m = nb * seq
    else:
        m, k = x.shape
    d = w.shape[-1]
    final = len(nxt) == 1
    scratch = [((k, d), BF16)]
    n_mods = 1 if final else 3

    def vmem_need(rows):
        blocks = [((rows, k), BF16), ((rows, d), F32), ((rows, d), F32 if final else BF16)]
        blocks += [((rows, d), F32)] * (0 if final else 1)
        blocks += [((-(-rows // gate.rows_per_group) * SUBLANES, d), F32)] * n_mods
        temporaries = 4 * _nbytes((rows, d), F32)
        return (2 * sum(_nbytes(s, t) for s, t in blocks) + sum(_nbytes(s, t) for s, t in scratch)
                + temporaries)

    mult = LANES if x_t else SUBLANES
    tm = gate.tile_rows(m, tm, mult)
    rpg = gate.rows_per_group
    fits_groups = lambda t: t % mult == 0 and (rpg % t == 0 or t % rpg == 0)
    while vmem_need(tm) > V7X_VMEM_CAP and tm % 2 == 0 and fits_groups(tm // 2):
        tm //= 2
    assert final or nxt[1].rows_per_group == gate.rows_per_group
    has_bias = bias is not None
    row = lambda i: (i, 0)
    if x_t:
        per_b = seq // tm
        in_specs = [pl.BlockSpec((None, k, tm), lambda i: (i // per_b, 0, i % per_b))]
    else:
        in_specs = [pl.BlockSpec((tm, k), row)]
    in_specs.append(pl.BlockSpec(memory_space=pltpu.HBM))
    args = [x, w]
    if has_bias:
        in_specs.append(pl.BlockSpec((1, d), lambda i: (0, 0)))
        args.append(bias.reshape(1, d))
    in_specs += [pl.BlockSpec((tm, d), row), gate.spec(gate_seg, d, tm),
                 pl.BlockSpec((1, d), lambda i: (0, 0))]
    args += [res, gate.arr, nxt[0].reshape(1, d)]
    if final:
        out_shape = jax.ShapeDtypeStruct((m, d), F32)
        out_specs = pl.BlockSpec((tm, d), row)
    else:
        _, nmod, seg_shift, seg_scale = nxt
        in_specs += [nmod.spec(seg_shift, d, tm), nmod.spec(seg_scale, d, tm)]
        args += [nmod.arr, nmod.arr]
        out_shape = [jax.ShapeDtypeStruct((m, d), F32), jax.ShapeDtypeStruct((m, d), BF16)]
        out_specs = [pl.BlockSpec((tm, d), row), pl.BlockSpec((tm, d), row)]
    return pl.pallas_call(
        functools.partial(_rowmm_body, has_bias=has_bias, x_t=x_t, final=final,
                          single_step=m == tm),
        out_shape=out_shape,
        grid=(m // tm,),
        in_specs=in_specs,
        out_specs=out_specs,
        scratch_shapes=[pltpu.VMEM(s, dt) for s, dt in scratch]
        + [pltpu.SemaphoreType.DMA((ROWMM_K_CHUNKS,))],
        compiler_params=pltpu.CompilerParams(
            dimension_semantics=("arbitrary",),
            vmem_limit_bytes=min(V7X_VMEM_CAP, max(32 * 2**20, vmem_need(tm) + 4 * 2**20))),
        name=name,
    )(*args)


def _conv_ln_swish(up_ref, pad, tt, w_ref, bdw_ref, lg_ref, lb_ref, acc_ref, o_ref):
    kw, d = w_ref.shape
    by_shift = [[(a, SUBLANES * a + s - pad) for a in range((pad + kw - 1) // SUBLANES + 1)
                 if 0 <= SUBLANES * a + s - pad < kw] for s in range(SUBLANES)]

    def strip(c, carry):
        cols = pl.ds(pl.multiple_of(c * LANES, LANES), LANES)
        z = bdw_ref[:, cols]
        for s, taps in enumerate(by_shift):
            n = tt + SUBLANES if s else tt
            q = None
            for a, k in taps:
                term = up_ref[pl.ds(SUBLANES * a, n), cols] * w_ref[pl.ds(k, 1), cols]
                q = term if q is None else q + term
            if q is not None:
                z = z + q[s:s + tt]
        acc_ref[:, cols] = z
        return carry

    lax.fori_loop(0, d // LANES, strip, 0)
    z = acc_ref[...]
    mu = jnp.mean(z, axis=-1, keepdims=True)
    zc = z - mu
    y = zc * lax.rsqrt(jnp.mean(zc * zc, axis=-1, keepdims=True) + EPS)
    y = y * lg_ref[...] + lb_ref[...]
    o_ref[...] = (y * _sigmoid(y)).astype(o_ref.dtype)


def _conv_prompt_body(main_ref, halo_ref, w_ref, bdw_ref, lg_ref, lb_ref, o_ref, up_ref, acc_ref,
                      *, halo, tt):
    kw = w_ref.shape[0]

    @pl.when(pl.program_id(1) == 0)
    def _():
        up_ref[0:halo, :] = jnp.zeros((halo, up_ref.shape[1]), F32)

    @pl.when(pl.program_id(1) > 0)
    def _():
        up_ref[0:halo, :] = halo_ref[...]

    up_ref[halo:halo + tt, :] = main_ref[...]
    _conv_ln_swish(up_ref, halo - (kw - 1), tt, w_ref, bdw_ref, lg_ref, lb_ref, acc_ref, o_ref)


def _conv_prompt_call(u, w_dw, b_dw, ln_g, ln_b, layer, tt=CONV_ROWS):
    b, t, d = u.shape
    kw = w_dw.shape[1]
    halo = -(-(kw - 1) // SUBLANES) * SUBLANES
    tt = _tile(t, tt, halo)
    hb = tt // halo
    vec = lambda: pl.BlockSpec((None, 1, d), lambda bi, ti: (layer, 0, 0))
    blocks = [((tt, d), F32), ((halo, d), F32), ((kw, d), F32), ((tt, d), BF16)]
    scratch = [((halo + tt, d), F32), ((tt, d), F32)]
    return pl.pallas_call(
        functools.partial(_conv_prompt_body, halo=halo, tt=tt),
        out_shape=jax.ShapeDtypeStruct((b, t, d), BF16),
        grid=(b, t // tt),
        in_specs=[
            pl.BlockSpec((None, tt, d), lambda bi, ti: (bi, ti, 0)),
            pl.BlockSpec((None, halo, d), lambda bi, ti: (bi, jnp.maximum(ti * hb - 1, 0), 0)),
            pl.BlockSpec((None, kw, d), lambda bi, ti: (layer, 0, 0)),
            vec(), vec(), vec(),
        ],
        out_specs=pl.BlockSpec((None, tt, d), lambda bi, ti: (bi, ti, 0)),
        scratch_shapes=[pltpu.VMEM(s, dt) for s, dt in scratch],
        compiler_params=pltpu.CompilerParams(
            dimension_semantics=("parallel", "arbitrary"),
            vmem_limit_bytes=_vmem_limit(blocks, scratch)),
        name="dwconv_ln_swish_prompt",
    )(u, u, w_dw, b_dw.reshape(-1, 1, d), ln_g.reshape(-1, 1, d), ln_b.reshape(-1, 1, d))


def _conv_sample_body(up_ref, w_ref, bdw_ref, lg_ref, lb_ref, o_ref, acc_ref, *, pad, tt):
    _conv_ln_swish(up_ref, pad, tt, w_ref, bdw_ref, lg_ref, lb_ref, acc_ref, o_ref)


def _conv_sample_call(up, pad, tt, w_dw, b_dw, ln_g, ln_b, layer):
    b, rows, d = up.shape
    kw = w_dw.shape[1]
    vec = lambda: pl.BlockSpec((None, 1, d), lambda bi: (layer, 0, 0))
    blocks = [((rows, d), F32), ((kw, d), F32), ((tt, d), BF16)]
    scratch = [((tt, d), F32)]
    return pl.pallas_call(
        functools.partial(_conv_sample_body, pad=pad, tt=tt),
        out_shape=jax.ShapeDtypeStruct((b, tt, d), BF16),
        grid=(b,),
        in_specs=[
            pl.BlockSpec((None, rows, d), lambda bi: (bi, 0, 0)),
            pl.BlockSpec((None, kw, d), lambda bi: (layer, 0, 0)),
            vec(), vec(), vec(),
        ],
        out_specs=pl.BlockSpec((None, tt, d), lambda bi: (bi, 0, 0)),
        scratch_shapes=[pltpu.VMEM(s, dt) for s, dt in scratch],
        compiler_params=pltpu.CompilerParams(
            dimension_semantics=("parallel",), vmem_limit_bytes=_vmem_limit(blocks, scratch)),
        name="dwconv_ln_swish_sample",
    )(up, w_dw, b_dw.reshape(-1, 1, d), ln_g.reshape(-1, 1, d), ln_b.reshape(-1, 1, d))


def _bucket_codes(n_q, n_k, k_off, n_buckets):
    rel = (np.arange(n_k) + k_off)[None, :] - np.arange(n_q)[:, None]
    nb = n_buckets // 2
    max_exact = nb // 2
    ret = np.where(rel > 0, nb, 0)
    n = np.abs(rel)
    nf = np.maximum(n, 1).astype(np.float32)
    large = max_exact + (np.log(nf / np.float32(max_exact))
                         / np.float32(math.log(MAX_DISTANCE / max_exact))
                         * np.float32(nb - max_exact)).astype(np.int32)
    large = np.minimum(large, nb - 1)
    return (ret + np.where(n < max_exact, n, large)).astype(np.int32)


def _bias_body(code_ref, table_ref, o_ref, *, present, group):
    n_cols = code_ref.shape[1]

    used = sorted({b for _, buckets in present for b in buckets})

    def head(hh, carry):
        value = {b: table_ref[hh, b] * LOG2E for b in used}
        for r0, buckets in present:
            rows = slice(r0, r0 + SUBLANES)
            code = code_ref[rows, :]
            out = jnp.full(code.shape, -jnp.inf, F32)
            for b in buckets:
                out = jnp.where(code == b, value[b], out)
            if group is None:
                o_ref[hh, rows, :] = out
            else:
                cols = pl.ds(pl.multiple_of((hh % group) * n_cols, LANES), n_cols)
                o_ref[hh // group, rows, cols] = out
        return carry

    lax.fori_loop(0, table_ref.shape[0], head, 0)


def _bias_call(code, table, group=None):
    n_heads, n_buckets = table.shape
    rows, cols = code.shape
    shape = (n_heads, rows, cols) if group is None else (n_heads // group, rows, group * cols)
    present = tuple(
        (r0, tuple(int(b) for b in np.unique(code[r0:r0 + SUBLANES]) if b < n_buckets))
        for r0 in range(0, rows, SUBLANES))
    return pl.pallas_call(
        functools.partial(_bias_body, present=present, group=group),
        out_shape=jax.ShapeDtypeStruct(shape, F32),
        in_specs=[pl.BlockSpec(memory_space=pltpu.VMEM), pl.BlockSpec(memory_space=pltpu.SMEM)],
        out_specs=pl.BlockSpec(memory_space=pltpu.VMEM),
        name="rel_bias",
    )(jnp.asarray(code), table)


def _unit_scores(qs, kh, bias):
    return lax.dot_general(qs, kh, (((1,), (1,)), ((), ())), preferred_element_type=F32) + bias


def _unit_output(s, vh, sink_col):
    sink_col = sink_col * LOG2E
    m = jnp.maximum(jnp.max(s, axis=-1, keepdims=True), sink_col)
    e = jnp.exp2(s - m)
    l = jnp.sum(e, axis=-1, keepdims=True) + jnp.exp2(sink_col - m)
    return jnp.dot(e.astype(BF16), vh, preferred_element_type=F32) * (1.0 / l)


def _stack_heads(q, h, group):
    hd = HEAD_DIM
    return jnp.concatenate(
        [q[:, (h * group + g) * hd:(h * group + g + 1) * hd] for g in range(group)], axis=0)


def _unstack_heads(o, group):
    tq = o.shape[0] // group
    return [o[g * tq:(g + 1) * tq, :] for g in range(group)]


ATTN_Q_COLS = 512
PAIR = 2 * CHUNK


def _pair_codes(n_buckets):
    code = _bucket_codes(PAIR, WINDOW + PAIR, -WINDOW, n_buckets).T
    key_chunk = np.arange(WINDOW + PAIR)[:, None] // CHUNK
    q_chunk = np.arange(PAIR)[None, :] // CHUNK
    visible = (key_chunk >= q_chunk) & (key_chunk <= q_chunk + WINDOW // CHUNK)
    return np.where(visible, code, n_buckets).astype(np.int32)


def _attn_prompt_body(qT_ref, kp_ref, kc_ref, vTp_ref, vTc_ref, bias_ref, sink_ref, oT_ref,
                      k_scr, vT_scr, *, kvh, group):
    i = pl.program_id(1)
    tq = qT_ref.shape[1]
    nk = WINDOW + PAIR
    hd = HEAD_DIM
    for h in range(kvh):
        k_scr[h, 0:WINDOW, :] = kp_ref[:, h * hd:(h + 1) * hd].astype(BF16)
        k_scr[h, WINDOW:WINDOW + tq, :] = kc_ref[:, h * hd:(h + 1) * hd].astype(BF16)
    vT_scr[:, 0:WINDOW] = vTp_ref[...].astype(BF16)
    vT_scr[:, WINDOW:WINDOW + tq] = vTc_ref[...].astype(BF16)
    row = lax.broadcasted_iota(jnp.int32, (nk, group * PAIR), 0)
    start_mask = jnp.where(row < jnp.where(i == 0, WINDOW, 0), -jnp.inf, 0.0).astype(F32)
    units = [(p * PAIR, h) for p in range(tq // PAIR) for h in range(kvh)]

    def scores(c0, h):
        qsT = jnp.concatenate(
            [qT_ref[hh * hd:(hh + 1) * hd, c0:c0 + PAIR] for hh in range(h * group, (h + 1) * group)],
            axis=1)
        s = jnp.dot(k_scr[h, c0:c0 + nk, :], qsT, preferred_element_type=F32) + bias_ref[h]
        return s + start_mask if c0 == 0 else s

    s_next = scores(*units[0])
    for n, (c0, h) in enumerate(units):
        s = s_next
        if n + 1 < len(units):
            s_next = scores(*units[n + 1])
        sink = sink_ref[h] * LOG2E
        m = jnp.maximum(jnp.max(s, axis=0, keepdims=True), sink)
        e = jnp.exp2(s - m)
        l = jnp.sum(e, axis=0, keepdims=True) + jnp.exp2(sink - m)
        oT = jnp.dot(vT_scr[h * hd:(h + 1) * hd, c0:c0 + nk], e.astype(BF16),
                     preferred_element_type=F32) * (1.0 / l)
        for g in range(group):
            hh = h * group + g
            oT_ref[hh * hd:(hh + 1) * hd, c0:c0 + PAIR] = (
                oT[:, g * PAIR:(g + 1) * PAIR].astype(oT_ref.dtype))


def _attn_prompt_call(qT, k, vT, bias, sink_row, kvh, group):
    b, dq, s = qT.shape
    dk = k.shape[2]
    tq = _tile(s, ATTN_Q_COLS, PAIR)
    per = tq // WINDOW
    prev = lambda i: jnp.maximum(i * per - 1, 0)
    blocks = [((dq, tq), BF16)] * 2 + [((WINDOW + tq, dk), F32)] * 2 + [
        (bias.shape, F32), (sink_row.shape[:1] + (SUBLANES, sink_row.shape[2]), F32)]
    scratch = [((kvh, WINDOW + tq, LANES), BF16), ((dk, WINDOW + tq), BF16)]
    return pl.pallas_call(
        functools.partial(_attn_prompt_body, kvh=kvh, group=group),
        out_shape=jax.ShapeDtypeStruct((b, dq, s), BF16),
        grid=(b, s // tq),
        in_specs=[
            pl.BlockSpec((None, dq, tq), lambda bi, i: (bi, 0, i)),
            pl.BlockSpec((None, WINDOW, dk), lambda bi, i: (bi, prev(i), 0)),
            pl.BlockSpec((None, tq, dk), lambda bi, i: (bi, i, 0)),
            pl.BlockSpec((None, dk, WINDOW), lambda bi, i: (bi, 0, prev(i))),
            pl.BlockSpec((None, dk, tq), lambda bi, i: (bi, 0, i)),
            pl.BlockSpec(bias.shape, lambda bi, i: (0, 0, 0)),
            pl.BlockSpec(sink_row.shape, lambda bi, i: (0, 0, 0)),
        ],
        out_specs=pl.BlockSpec((None, dq, tq), lambda bi, i: (bi, 0, i)),
        scratch_shapes=[pltpu.VMEM((kvh, WINDOW + tq, HEAD_DIM), BF16),
                        pltpu.VMEM((dk, WINDOW + tq), BF16)],
        compiler_params=pltpu.CompilerParams(
            dimension_semantics=("parallel", "arbitrary"),
            vmem_limit_bytes=_vmem_limit(blocks, scratch)),
        name="swa_sink_attention_prompt",
    )(qT, k, k, vT, vT, bias, sink_row)


ATTN_SAMPLE_BATCH = 4


def _attn_sample_body(q_ref, k_ref, v_ref, bias_ref, sink_ref, o_ref, *, kvh, group):
    hd = HEAD_DIM
    units = [(b, h) for b in range(q_ref.shape[0]) for h in range(kvh)]

    def scores(b, h):
        qs = _stack_heads(q_ref[b].astype(F32), h, group).astype(BF16)
        return _unit_scores(qs, k_ref[b, :, h * hd:(h + 1) * hd].astype(BF16), bias_ref[h])

    s_next = scores(*units[0])
    outs = []
    for n, (b, h) in enumerate(units):
        s = s_next
        if n + 1 < len(units):
            s_next = scores(*units[n + 1])
        o = _unit_output(s, v_ref[b, :, h * hd:(h + 1) * hd].astype(BF16), sink_ref[h])
        outs += _unstack_heads(o, group)
        if h == kvh - 1:
            o_ref[b] = jnp.concatenate(outs, axis=1).astype(o_ref.dtype)
            outs = []


def _attn_sample_call(q, k_all, v_all, bias, sink_col, kvh, group):
    b, t, dq = q.shape
    nk, dk = k_all.shape[1:]
    bb = math.gcd(b, ATTN_SAMPLE_BATCH)
    blocks = [((bb, t, dq), BF16)] * 2 + [((bb, nk, dk), F32)] * 2 + [
        (bias.shape[:2] + (2 * LANES,), F32), (sink_col.shape[:2] + (LANES,), F32)]
    return pl.pallas_call(
        functools.partial(_attn_sample_body, kvh=kvh, group=group),
        out_shape=jax.ShapeDtypeStruct((b, t, dq), BF16),
        grid=(b // bb,),
        in_specs=[
            pl.BlockSpec((bb, t, dq), lambda bi: (bi, 0, 0)),
            pl.BlockSpec((bb, nk, dk), lambda bi: (bi, 0, 0)),
            pl.BlockSpec((bb, nk, dk), lambda bi: (bi, 0, 0)),
            pl.BlockSpec(bias.shape, lambda bi: (0, 0, 0)),
            pl.BlockSpec(sink_col.shape, lambda bi: (0, 0, 0)),
        ],
        out_specs=pl.BlockSpec((bb, t, dq), lambda bi: (bi, 0, 0)),
        compiler_params=pltpu.CompilerParams(
            dimension_semantics=("parallel",), vmem_limit_bytes=_vmem_limit(blocks)),
        name="swa_sink_attention_sample",
    )(q, k_all, v_all, bias, sink_col)


def _trunks(xp3, xs3, mods_p, mods_s, state_conv, win_k, win_v, p):
    bp, tp, d = xp3.shape
    bs, ts, _ = xs3.shape
    mp, ms = bp * tp, bs * ts
    depth = p["w_mod"].shape[0]
    d_ff = p["w_gu"].shape[2] // 2
    n_heads = p["attn_sinks"].shape[1]
    dq = n_heads * HEAD_DIM
    kvh = (p["w_qkv"].shape[2] - dq) // (2 * HEAD_DIM)
    group = n_heads // kvh
    dkv = kvh * HEAD_DIM
    kw = p["w_dw"].shape[1]
    n_buckets = p["rel_bias_table"].shape[1]
    conv_p, k_p, v_p, conv_s, k_s, v_s = [], [], [], [], [], []

    xp, xs = xp3.reshape(mp, d), xs3.reshape(ms, d)
    hp = _norm_mod_call(xp, p["norm_mix"][0], mods_p[0], 0, 1, PANEL_ROWS)
    hs = _norm_mod_call(xs, p["norm_mix"][0], mods_s[0], 0, 1, ms)
    for i in range(depth):
        j = i // 2
        ffn_norm = lambda mods: (p["norm_ffn"][i], mods[i], 3, 4)
        if i % 2 == 0:
            u_p, u_s, wq = _mm_call(hp, p["w_pw1"], j, (0, d), d, x2=hs, cast=(p["w_pw2"], j),
                                    bias=p["b_pw1"], act="glu", name="pw1_glu")
            u_p, u_s = u_p.reshape(bp, tp, d), u_s.reshape(bs, ts, d)
            conv_w = (p["w_dw"], p["b_dw"], p["conv_ln_g"], p["conv_ln_b"], j)
            z_p = _conv_prompt_call(u_p, *conv_w)
            conv_p.append(u_p[:, tp - (kw - 1):])
            pad = (-(kw - 1 + ts)) % SUBLANES
            up = jnp.concatenate([jnp.zeros((bs, pad, d), F32), state_conv[j], u_s], axis=1)
            z_s = _conv_sample_call(up, pad, ts, *conv_w)
            conv_s.append(up[:, pad + ts:])
            xp, hp = _rowmm_call(z_p.reshape(mp, d), wq, xp, mods_p[i], 2, ffn_norm(mods_p),
                                 bias=p["b_pw2"][j], name="pw2_residual_norm")
            xs, hs = _rowmm_call(z_s.reshape(ms, d), wq, xs, mods_s[i], 2, ffn_norm(mods_s),
                                 bias=p["b_pw2"][j], name="pw2_residual_norm")
        else:
            sinks = p["attn_sinks"][j].reshape(kvh, group, 1)
            qT, q_s, wq = _mm_call(hp, p["w_qkv"], j, (0,), dq, x2=hs, cast=(p["w_o"], j),
                                   bias=p["b_qkv"], scale=LOG2E * HEAD_DIM ** -0.5,
                                   out_dtype=BF16, tn=Q_PANEL_COLS, name="q_proj",
                                   out_mode="t", seq=tp)
            (k, vT), kv_s = _mm_call(hp, p["w_qkv"], j, (dq,), 2 * dkv, x2=hs, bias=p["b_qkv"],
                                     tn=2 * dkv, name="kv_proj", out_mode="split_t", seq=tp)
            k = k.reshape(bp, tp, dkv)
            bias = _bias_call(_pair_codes(n_buckets), p["rel_bias_table"], group)
            sink_row = jnp.repeat(sinks, PAIR, axis=2).reshape(kvh, 1, group * PAIR)
            oT = _attn_prompt_call(qT, k, vT, bias, sink_row, kvh, group)
            k_p.append(k[:, tp - WINDOW:].reshape(bp, WINDOW, kvh, HEAD_DIM))
            v_p.append(jnp.swapaxes(vT[:, :, tp - WINDOW:], 1, 2)
                       .reshape(bp, WINDOW, kvh, HEAD_DIM))
            kv3 = kv_s.reshape(bs, ts, 2, kvh, HEAD_DIM)
            k_all = jnp.concatenate([win_k[j], kv3[:, :, 0]], axis=1)
            v_all = jnp.concatenate([win_v[j], kv3[:, :, 1]], axis=1)
            n_keys = WINDOW + ts
            bias = _bias_call(_bucket_codes(ts, n_keys, -WINDOW, n_buckets),
                              p["rel_bias_table"]).reshape(kvh, group * ts, n_keys)
            o_s = _attn_sample_call(q_s.reshape(bs, ts, dq), k_all.reshape(bs, n_keys, dkv),
                                    v_all.reshape(bs, n_keys, dkv), bias,
                                    jnp.repeat(sinks, ts, axis=1), kvh, group)
            k_s.append(k_all[:, ts:])
            v_s.append(v_all[:, ts:])
            xp, hp = _rowmm_call(oT, wq, xp, mods_p[i], 2, ffn_norm(mods_p), bias=p["b_o"][j],
                                 x_t=True, name="wo_residual_norm")
            xs, hs = _rowmm_call(o_s.reshape(ms, dq), wq, xs, mods_s[i], 2, ffn_norm(mods_s),
                                 bias=p["b_o"][j], name="wo_residual_norm")
        a_p, a_s, wq = _mm_call(hp, p["w_gu"], i, (0, d_ff), d_ff, x2=hs, cast=(p["w_down"], i),
                                act="swiglu", out_dtype=BF16, name="ffn_gate_up")
        if i + 1 < depth:
            nxt = lambda mods: (p["norm_mix"][i + 1], mods[i + 1], 0, 1)
            xp, hp = _rowmm_call(a_p, wq, xp, mods_p[i], 5, nxt(mods_p),
                                 name="ffn_down_residual_norm")
            xs, hs = _rowmm_call(a_s, wq, xs, mods_s[i], 5, nxt(mods_s),
                                 name="ffn_down_residual_norm")
        else:
            y_p = _rowmm_call(a_p, wq, xp, mods_p[i], 5, (p["norm_out"],),
                              name="ffn_down_residual_final_norm")
            y_s = _rowmm_call(a_s, wq, xs, mods_s[i], 5, (p["norm_out"],),
                              name="ffn_down_residual_final_norm")
    stack = jnp.stack
    return (y_p.reshape(bp, tp, d), y_s.reshape(bs, ts, d), stack(conv_p), stack(k_p), stack(v_p),
            stack(conv_s), stack(k_s), stack(v_s))


def kernel(x_prompt, x_sample, c_prompt, c_sample, state_conv, cache_win_k, cache_win_v, w_mod, b_mod, norm_mix, norm_ffn, w_pw1, b_pw1, w_dw, b_dw, conv_ln_g, conv_ln_b, w_pw2, b_pw2, w_qkv, b_qkv, w_o, b_o, attn_sinks, rel_bias_table, w_gu, w_down, norm_out):
    p = dict(w_mod=w_mod, b_mod=b_mod, norm_mix=norm_mix, norm_ffn=norm_ffn, w_pw1=w_pw1,
             b_pw1=b_pw1, w_dw=w_dw, b_dw=b_dw, conv_ln_g=conv_ln_g, conv_ln_b=conv_ln_b,
             w_pw2=w_pw2, b_pw2=b_pw2, w_qkv=w_qkv, b_qkv=b_qkv, w_o=w_o, b_o=b_o,
             attn_sinks=attn_sinks, rel_bias_table=rel_bias_table, w_gu=w_gu, w_down=w_down,
             norm_out=norm_out)
    bp, sp, d = x_prompt.shape
    bs, ts, _ = x_sample.shape
    depth = w_mod.shape[0]

    n_c = bp + bs
    c_all = jnp.concatenate(
        [c_prompt, c_sample, jnp.zeros((-n_c % BF16_SUBLANES, d), F32)], axis=0)
    mod_all = _mod_call(c_all, w_mod, b_mod)

    mods_p = [_Mod(mod_all[l, :bp].reshape(bp, 1, 6 * d), sp) for l in range(depth)]
    mods_s = [_Mod(mod_all[l, bp:n_c].reshape(bs, 1, 6 * d), ts) for l in range(depth)]

    return _trunks(x_prompt, x_sample, mods_p, mods_s, state_conv, cache_win_k, cache_win_v, p)
```

```python
import functools
import math

import numpy as np
import jax
import jax.numpy as jnp
from jax import lax
from jax.experimental import pallas as pl
from jax.experimental.pallas import tpu as pltpu

F32 = jnp.float32
BF16 = jnp.bfloat16

CHUNK = 64
WINDOW = 128
HEAD_DIM = 64
MAX_DISTANCE = 128
EPS = 1e-6
LOG2E = math.log2(math.e)

V7X_VMEM_BYTES = 64 * 2**20
V7X_VMEM_CAP = V7X_VMEM_BYTES - 8 * 2**20
SUBLANES = 8
LANES = 128
BF16_SUBLANES = 2 * SUBLANES

PANEL_ROWS = 1024
PANEL_COLS = 512
Q_PANEL_COLS = 1024
ROWMM_ROWS = 512
CONV_ROWS = 128
MOD_COLS = 1024


def _nbytes(shape, dtype):
    return math.prod(shape) * jnp.dtype(dtype).itemsize


def _vmem_limit(blocks, scratch=()):
    est = 2 * sum(_nbytes(s, d) for s, d in blocks) + sum(_nbytes(s, d) for s, d in scratch)
    return int(min(V7X_VMEM_CAP, max(32 * 2**20, 2 * est)))


def _tile(dim, pref, mult=LANES):
    if dim <= pref:
        return dim
    t = (pref // mult) * mult
    while t >= mult:
        if dim % t == 0:
            return t
        t -= mult
    return dim


def _sigmoid(x):
    return 1.0 / (1.0 + jnp.exp(-x))


def _mod_body(c_ref, w_ref, b_ref, o_ref):
    c = c_ref[...]
    a = (c * _sigmoid(c)).astype(BF16)
    o_ref[...] = jnp.dot(a, w_ref[...].astype(BF16), preferred_element_type=F32) + b_ref[...]


def _mod_call(c_all, w_mod, b_mod):
    depth, d, n = w_mod.shape
    mp = c_all.shape[0]
    tn = _tile(n, MOD_COLS)
    blocks = [((mp, d), F32), ((d, tn), F32), ((1, tn), F32), ((mp, tn), F32)]
    return pl.pallas_call(
        _mod_body,
        out_shape=jax.ShapeDtypeStruct((depth, mp, n), F32),
        grid=(depth, n // tn),
        in_specs=[
            pl.BlockSpec((mp, d), lambda l, j: (0, 0)),
            pl.BlockSpec((None, d, tn), lambda l, j: (l, 0, j)),
            pl.BlockSpec((None, 1, tn), lambda l, j: (l, 0, j)),
        ],
        out_specs=pl.BlockSpec((None, mp, tn), lambda l, j: (l, 0, j)),
        compiler_params=pltpu.CompilerParams(
            dimension_semantics=("arbitrary", "arbitrary"),
            vmem_limit_bytes=_vmem_limit(blocks, [((d, tn), BF16)])),
        name="adaln_mod",
    )(c_all, w_mod, b_mod.reshape(depth, 1, n))


class _Mod:
    def __init__(self, arr, rows_per_group):
        self.arr = arr
        self.rows_per_group = rows_per_group

    def tile_rows(self, total_rows, pref, mult):
        if self.rows_per_group >= pref:
            return _tile(self.rows_per_group, pref, mult)
        return _tile(total_rows, pref, max(mult, self.rows_per_group))

    def spec(self, seg, d, tm):
        if tm <= self.rows_per_group:
            per = self.rows_per_group // tm
            return pl.BlockSpec((1, 1, d), lambda i: (i // per, 0, seg))
        return pl.BlockSpec((tm // self.rows_per_group, 1, d), lambda i: (i, 0, seg))


def _grouped(v, m):
    return v.reshape(m.shape[0], v.shape[0] // m.shape[0], v.shape[1])


def _norm_mod_body(x_ref, g_ref, sh_ref, sc_ref, o_ref):
    x = x_ref[...]
    y = x * lax.rsqrt(jnp.mean(x * x, axis=-1, keepdims=True) + EPS) * g_ref[...]
    sc = sc_ref[...]
    o_ref[...] = (_grouped(y, sc) * (1.0 + sc) + sh_ref[...]).reshape(x.shape).astype(o_ref.dtype)


def _norm_mod_call(x, g, mod, seg_shift, seg_scale, tm):
    m, d = x.shape
    tm = mod.tile_rows(m, tm, SUBLANES)
    blocks = [((tm, d), F32), ((1, d), F32), ((tm, d), BF16)]
    return pl.pallas_call(
        _norm_mod_body,
        out_shape=jax.ShapeDtypeStruct((m, d), BF16),
        grid=(m // tm,),
        in_specs=[
            pl.BlockSpec((tm, d), lambda i: (i, 0)),
            pl.BlockSpec((1, d), lambda i: (0, 0)),
            mod.spec(seg_shift, d, tm),
            mod.spec(seg_scale, d, tm),
        ],
        out_specs=pl.BlockSpec((tm, d), lambda i: (i, 0)),
        compiler_params=pltpu.CompilerParams(
            dimension_semantics=("parallel",), vmem_limit_bytes=_vmem_limit(blocks)),
        name="rmsnorm_modulate",
    )(x, g.reshape(1, d), mod.arr, mod.arr)


def _mm_body(x_ref, *refs, layer, offs, has_bias, act, scale, out_mode, has_x2, has_cast):
    n_w = len(offs)
    if has_x2:
        x2_ref, refs = refs[0], refs[1:]
    w_hbm, refs = refs[0], refs[1:]
    bs = refs[:n_w] if has_bias else ()
    refs = refs[len(bs):]
    if has_cast:
        wc_ref, refs = refs[0], refs[1:]
    n_o = 2 if out_mode == "split_t" else 1
    o_refs = refs[:n_o]
    refs = refs[n_o:]
    if has_x2:
        o2_ref, refs = refs[0], refs[1:]
    if has_cast:
        oc_ref, refs = refs[0], refs[1:]
        oc_ref[...] = wc_ref[...].astype(BF16)
    wbs, (stage, sem) = refs[:n_w], refs[n_w:]
    tn = wbs[0].shape[1]

    def panel_copy(i, j, slot):
        cols = pl.ds(pl.multiple_of((offs[i] + j) * tn, tn), tn)
        return pltpu.make_async_copy(w_hbm.at[layer, :, cols], stage.at[i, slot], sem.at[i, slot])

    def compute(x):
        ps = [None] * n_w
        for i in ((1, 0) if act == "glu" else range(n_w)):
            p = jnp.dot(x, wbs[i][...], preferred_element_type=F32)
            if has_bias:
                p = p + bs[i][...]
            ps[i] = p
        if act == "glu":
            y = ps[0] * _sigmoid(ps[1])
        elif act == "swiglu":
            y = ps[0] * _sigmoid(ps[0]) * ps[1]
        else:
            y = ps[0]
        return y * scale if scale != 1.0 else y

    @pl.when(pl.program_id(1) == 0)
    def _():
        j = pl.program_id(0)
        slot = j % 2

        @pl.when(j == 0)
        def _():
            for i in range(n_w):
                panel_copy(i, 0, 0).start()

        for i in range(n_w):
            panel_copy(i, j, slot).wait()

        @pl.when(j + 1 < pl.num_programs(0))
        def _():
            for i in range(n_w):
                panel_copy(i, j + 1, 1 - slot).start(priority=1)

        for i in range(n_w):
            wbs[i][...] = stage[i, slot].astype(BF16)
        if has_x2:
            o2_ref[...] = compute(x2_ref[...]).astype(o2_ref.dtype)

    y = compute(x_ref[...])
    if out_mode == "plain":
        o_refs[0][...] = y.astype(o_refs[0].dtype)
    elif out_mode == "t":
        o_refs[0][...] = y.T.astype(o_refs[0].dtype)
    else:
        half = y.shape[1] // 2
        o_refs[0][...] = y[:, :half].astype(o_refs[0].dtype)
        o_refs[1][...] = y[:, half:].T.astype(o_refs[1].dtype)


def _mm_call(x, w, layer, col_starts, n_out, *, x2=None, cast=None, bias=None, act=None,
             scale=1.0, out_dtype=F32, tm=PANEL_ROWS, tn=PANEL_COLS, name="matmul",
             out_mode="plain", seq=None):
    m, k = x.shape
    if out_mode == "plain":
        tm = _tile(m, tm, SUBLANES)
    else:
        tm = _tile(seq, tm, LANES)
    tn = _tile(n_out, tn)
    n_w = len(col_starts)
    assert all(c % tn == 0 for c in col_starts) and m % tm == 0 and n_out % tn == 0
    has_bias = bias is not None
    per_b = None if seq is None else seq // tm

    in_specs = [pl.BlockSpec((tm, k), lambda j, i: (i, 0))]
    args = [x]
    blocks = [((tm, k), x.dtype), ((tm, tn), out_dtype)]
    if x2 is not None:
        m2 = x2.shape[0]
        in_specs.append(pl.BlockSpec((m2, k), lambda j, i: (0, 0)))
        args.append(x2)
        blocks += [((m2, k), x2.dtype), ((m2, tn), out_dtype)]
    in_specs.append(pl.BlockSpec(memory_space=pltpu.HBM))
    args.append(w)
    if has_bias:
        b3 = bias.reshape(bias.shape[0], 1, bias.shape[1])
        for c in col_starts:
            off = c // tn
            in_specs.append(pl.BlockSpec((None, 1, tn), lambda j, i, off=off: (layer, 0, off + j)))
            args.append(b3)
    n_i = m // tm
    if cast is not None:
        w_other, layer_other = cast
        kc, dc = w_other.shape[1:]
        slab = kc // ((n_out // tn) * n_i)
        assert slab * (n_out // tn) * n_i == kc and slab % BF16_SUBLANES == 0
        in_specs.append(pl.BlockSpec((None, slab, dc), lambda j, i: (layer_other, j * n_i + i, 0)))
        args.append(w_other)
        blocks += [((slab, dc), F32), ((slab, dc), BF16)]
    scratch = [((k, tn), BF16)] * n_w + [((n_w, 2, k, tn), F32)]
    body = functools.partial(_mm_body, layer=layer, offs=tuple(c // tn for c in col_starts),
                             has_bias=has_bias, act=act, scale=scale, out_mode=out_mode,
                             has_x2=x2 is not None, has_cast=cast is not None)
    if out_mode == "plain":
        out_shape = [jax.ShapeDtypeStruct((m, n_out), out_dtype)]
        out_specs = [pl.BlockSpec((tm, tn), lambda j, i: (i, j))]
    elif out_mode == "t":
        out_shape = [jax.ShapeDtypeStruct((m // seq, n_out, seq), out_dtype)]
        out_specs = [pl.BlockSpec((None, tn, tm), lambda j, i: (i // per_b, j, i % per_b))]
    else:
        assert tn == n_out
        half = n_out // 2
        out_shape = [jax.ShapeDtypeStruct((m, half), out_dtype),
                     jax.ShapeDtypeStruct((m // seq, half, seq), out_dtype)]
        out_specs = [pl.BlockSpec((tm, half), lambda j, i: (i, 0)),
                     pl.BlockSpec((None, half, tm), lambda j, i: (i // per_b, 0, i % per_b))]
    if x2 is not None:
        out_shape.append(jax.ShapeDtypeStruct((m2, n_out), out_dtype))
        out_specs.append(pl.BlockSpec((m2, tn), lambda j, i: (0, j)))
    if cast is not None:
        out_shape.append(jax.ShapeDtypeStruct((kc, dc), BF16))
        out_specs.append(pl.BlockSpec((slab, dc), lambda j, i: (j * n_i + i, 0)))
    outs = pl.pallas_call(
        body,
        out_shape=out_shape,
        grid=(n_out // tn, n_i),
        in_specs=in_specs,
        out_specs=out_specs,
        scratch_shapes=[pltpu.VMEM(s, d) for s, d in scratch]
        + [pltpu.SemaphoreType.DMA((n_w, 2))],
        compiler_params=pltpu.CompilerParams(
            dimension_semantics=("arbitrary", "arbitrary"),
            vmem_limit_bytes=_vmem_limit(blocks, scratch)),
        name=name,
    )(*args)
    n_primary = 2 if out_mode == "split_t" else 1
    primary = outs[0] if n_primary == 1 else outs[:n_primary]
    extras = list(outs[n_primary:])
    return (primary, *extras) if extras else primary


ROWMM_SUB_ROWS = 256
ROWMM_K_CHUNKS = 4
MXU_K = 256


def _rowmm_body(*refs, has_bias, x_t, final, single_step):
    x_ref, w_hbm = refs[:2]
    refs = refs[2:]
    if has_bias:
        b_ref, refs = refs[0], refs[1:]
    res_ref, gate_ref, ng_ref = refs[:3]
    refs = refs[3:]
    if not final:
        sh_ref, sc_ref = refs[:2]
        refs = refs[2:]
    n_o = 1 if final else 2
    o_refs = refs[:n_o]
    refs = refs[n_o:]
    w_res, sem = refs
    k = w_res.shape[0]
    tiles = k // MXU_K
    n_chunks = min(ROWMM_K_CHUNKS, tiles)
    bounds = [MXU_K * (tiles * c // n_chunks) for c in range(n_chunks + 1)]

    def chunk_copy(c):
        rows = pl.ds(bounds[c], bounds[c + 1] - bounds[c])
        return pltpu.make_async_copy(w_hbm.at[rows, :], w_res.at[rows, :], sem.at[c])

    tm = res_ref.shape[0]
    sub = min(tm, ROWMM_SUB_ROWS)

    def lhs(r0, k0=0, k1=k):
        return x_ref[k0:k1, r0:r0 + sub].T if x_t else x_ref[r0:r0 + sub, k0:k1]

    def groups_of(ref, r0):
        per = tm // ref.shape[0]
        return ref[...] if per >= tm else ref[r0 // per:(r0 + sub) // per]

    def finish(r0, y):
        if has_bias:
            y = y + b_ref[...]
        gate = groups_of(gate_ref, r0)
        xn = (_grouped(res_ref[r0:r0 + sub, :], gate) + gate * _grouped(y, gate)).reshape(y.shape)
        r = xn * lax.rsqrt(jnp.mean(xn * xn, axis=-1, keepdims=True) + EPS) * ng_ref[...]
        if final:
            o_refs[0][r0:r0 + sub, :] = r
        else:
            sc = groups_of(sc_ref, r0)
            o_refs[0][r0:r0 + sub, :] = xn
            o_refs[1][r0:r0 + sub, :] = (
                _grouped(r, sc) * (1.0 + sc) + groups_of(sh_ref, r0)
            ).reshape(y.shape).astype(o_refs[1].dtype)

    for c in range(n_chunks):
        @pl.when(pl.program_id(0) == 0)
        def _():
            chunk_copy(c).start(priority=0 if single_step else 1)

    first = 0
    if single_step:
        y = None
        for c in range(n_chunks):
            chunk_copy(c).wait()
            part = jnp.dot(lhs(0, bounds[c], bounds[c + 1]), w_res[bounds[c]:bounds[c + 1], :],
                           preferred_element_type=F32)
            y = part if y is None else y + part
        finish(0, y)
        first = sub
    else:
        for c in range(n_chunks):
            @pl.when(pl.program_id(0) == 0)
            def _():
                chunk_copy(c).wait()

    for r0 in range(first, tm, sub):
        finish(r0, jnp.dot(lhs(r0), w_res[...], preferred_element_type=F32))


def _rowmm_call(x, w, res, gate, gate_seg, nxt, *, bias=None, x_t=False, tm=ROWMM_ROWS,
                name="rowmm"):
    if x_t:
        nb, k, seq = x.shape
        m = nb * seq
    else:
        m, k = x.shape
    d = w.shape[-1]
    final = len(nxt) == 1
    scratch = [((k, d), BF16)]
    n_mods = 1 if final else 3

    def vmem_need(rows):
        blocks = [((rows, k), BF16), ((rows, d), F32), ((rows, d), F32 if final else BF16)]
        blocks += [((rows, d), F32)] * (0 if final else 1)
        blocks += [((-(-rows // gate.rows_per_group) * SUBLANES, d), F32)] * n_mods
        temporaries = 4 * _nbytes((rows, d), F32)
        return (2 * sum(_nbytes(s, t) for s, t in blocks) + sum(_nbytes(s, t) for s, t in scratch)
                + temporaries)

    mult = LANES if x_t else SUBLANES
    tm = gate.tile_rows(m, tm, mult)
    rpg = gate.rows_per_group
    fits_groups = lambda t: t % mult == 0 and (rpg % t == 0 or t % rpg == 0)
    while vmem_need(tm) > V7X_VMEM_CAP and tm % 2 == 0 and fits_groups(tm // 2):
        tm //= 2
    assert final or nxt[1].rows_per_group == gate.rows_per_group
    has_bias = bias is not None
    row = lambda i: (i, 0)
    if x_t:
        per_b = seq // tm
        in_specs = [pl.BlockSpec((None, k, tm), lambda i: (i // per_b, 0, i % per_b))]
    else:
        in_specs = [pl.BlockSpec((tm, k), row)]
    in_specs.append(pl.BlockSpec(memory_space=pltpu.HBM))
    args = [x, w]
    if has_bias:
        in_specs.append(pl.BlockSpec((1, d), lambda i: (0, 0)))
        args.append(bias.reshape(1, d))
    in_specs += [pl.BlockSpec((tm, d), row), gate.spec(gate_seg, d, tm),
                 pl.BlockSpec((1, d), lambda i: (0, 0))]
    args += [res, gate.arr, nxt[0].reshape(1, d)]
    if final:
        out_shape = jax.ShapeDtypeStruct((m, d), F32)
        out_specs = pl.BlockSpec((tm, d), row)
    else:
        _, nmod, seg_shift, seg_scale = nxt
        in_specs += [nmod.spec(seg_shift, d, tm), nmod.spec(seg_scale, d, tm)]
        args += [nmod.arr, nmod.arr]
        out_shape = [jax.ShapeDtypeStruct((m, d), F32), jax.ShapeDtypeStruct((m, d), BF16)]
        out_specs = [pl.BlockSpec((tm, d), row), pl.BlockSpec((tm, d), row)]
    return pl.pallas_call(
        functools.partial(_rowmm_body, has_bias=has_bias, x_t=x_t, final=final,
                          single_step=m == tm),
        out_shape=out_shape,
        grid=(m // tm,),
        in_specs=in_specs,
        out_specs=out_specs,
        scratch_shapes=[pltpu.VMEM(s, dt) for s, dt in scratch]
        + [pltpu.SemaphoreType.DMA((ROWMM_K_CHUNKS,))],
        compiler_params=pltpu.CompilerParams(
            dimension_semantics=("arbitrary",),
            vmem_limit_bytes=min(V7X_VMEM_CAP, max(32 * 2**20, vmem_need(tm) + 4 * 2**20))),
        name=name,
    )(*args)


def _conv_ln_swish(up_ref, pad, tt, w_ref, bdw_ref, lg_ref, lb_ref, acc_ref, o_ref):
    kw, d = w_ref.shape
    by_shift = [[(a, SUBLANES * a + s - pad) for a in range((pad + kw - 1) // SUBLANES + 1)
                 if 0 <= SUBLANES * a + s - pad < kw] for s in range(SUBLANES)]

    def strip(c, carry):
        cols = pl.ds(pl.multiple_of(c * LANES, LANES), LANES)
        z = bdw_ref[:, cols]
        for s, taps in enumerate(by_shift):
            n = tt + SUBLANES if s else tt
            q = None
            for a, k in taps:
                term = up_ref[pl.ds(SUBLANES * a, n), cols] * w_ref[pl.ds(k, 1), cols]
                q = term if q is None else q + term
            if q is not None:
                z = z + q[s:s + tt]
        acc_ref[:, cols] = z
        return carry

    lax.fori_loop(0, d // LANES, strip, 0)
    z = acc_ref[...]
    mu = jnp.mean(z, axis=-1, keepdims=True)
    zc = z - mu
    y = zc * lax.rsqrt(jnp.mean(zc * zc, axis=-1, keepdims=True) + EPS)
    y = y * lg_ref[...] + lb_ref[...]
    o_ref[...] = (y * _sigmoid(y)).astype(o_ref.dtype)


def _conv_prompt_body(main_ref, halo_ref, w_ref, bdw_ref, lg_ref, lb_ref, o_ref, up_ref, acc_ref,
                      *, halo, tt):
    kw = w_ref.shape[0]

    @pl.when(pl.program_id(1) == 0)
    def _():
        up_ref[0:halo, :] = jnp.zeros((halo, up_ref.shape[1]), F32)

    @pl.when(pl.program_id(1) > 0)
    def _():
        up_ref[0:halo, :] = halo_ref[...]

    up_ref[halo:halo + tt, :] = main_ref[...]
    _conv_ln_swish(up_ref, halo - (kw - 1), tt, w_ref, bdw_ref, lg_ref, lb_ref, acc_ref, o_ref)


def _conv_prompt_call(u, w_dw, b_dw, ln_g, ln_b, layer, tt=CONV_ROWS):
    b, t, d = u.shape
    kw = w_dw.shape[1]
    halo = -(-(kw - 1) // SUBLANES) * SUBLANES
    tt = _tile(t, tt, halo)
    hb = tt // halo
    vec = lambda: pl.BlockSpec((None, 1, d), lambda bi, ti: (layer, 0, 0))
    blocks = [((tt, d), F32), ((halo, d), F32), ((kw, d), F32), ((tt, d), BF16)]
    scratch = [((halo + tt, d), F32), ((tt, d), F32)]
    return pl.pallas_call(
        functools.partial(_conv_prompt_body, halo=halo, tt=tt),
        out_shape=jax.ShapeDtypeStruct((b, t, d), BF16),
        grid=(b, t // tt),
        in_specs=[
            pl.BlockSpec((None, tt, d), lambda bi, ti: (bi, ti, 0)),
            pl.BlockSpec((None, halo, d), lambda bi, ti: (bi, jnp.maximum(ti * hb - 1, 0), 0)),
            pl.BlockSpec((None, kw, d), lambda bi, ti: (layer, 0, 0)),
            vec(), vec(), vec(),
        ],
        out_specs=pl.BlockSpec((None, tt, d), lambda bi, ti: (bi, ti, 0)),
        scratch_shapes=[pltpu.VMEM(s, dt) for s, dt in scratch],
        compiler_params=pltpu.CompilerParams(
            dimension_semantics=("parallel", "arbitrary"),
            vmem_limit_bytes=_vmem_limit(blocks, scratch)),
        name="dwconv_ln_swish_prompt",
    )(u, u, w_dw, b_dw.reshape(-1, 1, d), ln_g.reshape(-1, 1, d), ln_b.reshape(-1, 1, d))


def _conv_sample_body(up_ref, w_ref, bdw_ref, lg_ref, lb_ref, o_ref, acc_ref, *, pad, tt):
    _conv_ln_swish(up_ref, pad, tt, w_ref, bdw_ref, lg_ref, lb_ref, acc_ref, o_ref)


def _conv_sample_call(up, pad, tt, w_dw, b_dw, ln_g, ln_b, layer):
    b, rows, d = up.shape
    kw = w_dw.shape[1]
    vec = lambda: pl.BlockSpec((None, 1, d), lambda bi: (layer, 0, 0))
    blocks = [((rows, d), F32), ((kw, d), F32), ((tt, d), BF16)]
    scratch = [((tt, d), F32)]
    return pl.pallas_call(
        functools.partial(_conv_sample_body, pad=pad, tt=tt),
        out_shape=jax.ShapeDtypeStruct((b, tt, d), BF16),
        grid=(b,),
        in_specs=[
            pl.BlockSpec((None, rows, d), lambda bi: (bi, 0, 0)),
            pl.BlockSpec((None, kw, d), lambda bi: (layer, 0, 0)),
            vec(), vec(), vec(),
        ],
        out_specs=pl.BlockSpec((None, tt, d), lambda bi: (bi, 0, 0)),
        scratch_shapes=[pltpu.VMEM(s, dt) for s, dt in scratch],
        compiler_params=pltpu.CompilerParams(
            dimension_semantics=("parallel",), vmem_limit_bytes=_vmem_limit(blocks, scratch)),
        name="dwconv_ln_swish_sample",
    )(up, w_dw, b_dw.reshape(-1, 1, d), ln_g.reshape(-1, 1, d), ln_b.reshape(-1, 1, d))


def _bucket_codes(n_q, n_k, k_off, n_buckets):
    rel = (np.arange(n_k) + k_off)[None, :] - np.arange(n_q)[:, None]
    nb = n_buckets // 2
    max_exact = nb // 2
    ret = np.where(rel > 0, nb, 0)
    n = np.abs(rel)
    nf = np.maximum(n, 1).astype(np.float32)
    large = max_exact + (np.log(nf / np.float32(max_exact))
                         / np.float32(math.log(MAX_DISTANCE / max_exact))
                         * np.float32(nb - max_exact)).astype(np.int32)
    large = np.minimum(large, nb - 1)
    return (ret + np.where(n < max_exact, n, large)).astype(np.int32)


def _bias_body(code_ref, table_ref, o_ref, *, present, group):
    n_cols = code_ref.shape[1]

    used = sorted({b for _, buckets in present for b in buckets})

    def head(hh, carry):
        value = {b: table_ref[hh, b] * LOG2E for b in used}
        for r0, buckets in present:
            rows = slice(r0, r0 + SUBLANES)
            code = code_ref[rows, :]
            out = jnp.full(code.shape, -jnp.inf, F32)
            for b in buckets:
                out = jnp.where(code == b, value[b], out)
            if group is None:
                o_ref[hh, rows, :] = out
            else:
                cols = pl.ds(pl.multiple_of((hh % group) * n_cols, LANES), n_cols)
                o_ref[hh // group, rows, cols] = out
        return carry

    lax.fori_loop(0, table_ref.shape[0], head, 0)


def _bias_call(code, table, group=None):
    n_heads, n_buckets = table.shape
    rows, cols = code.shape
    shape = (n_heads, rows, cols) if group is None else (n_heads // group, rows, group * cols)
    present = tuple(
        (r0, tuple(int(b) for b in np.unique(code[r0:r0 + SUBLANES]) if b < n_buckets))
        for r0 in range(0, rows, SUBLANES))
    return pl.pallas_call(
        functools.partial(_bias_body, present=present, group=group),
        out_shape=jax.ShapeDtypeStruct(shape, F32),
        in_specs=[pl.BlockSpec(memory_space=pltpu.VMEM), pl.BlockSpec(memory_space=pltpu.SMEM)],
        out_specs=pl.BlockSpec(memory_space=pltpu.VMEM),
        name="rel_bias",
    )(jnp.asarray(code), table)


def _unit_scores(qs, kh, bias):
    return lax.dot_general(qs, kh, (((1,), (1,)), ((), ())), preferred_element_type=F32) + bias


def _unit_output(s, vh, sink_col):
    sink_col = sink_col * LOG2E
    m = jnp.maximum(jnp.max(s, axis=-1, keepdims=True), sink_col)
    e = jnp.exp2(s - m)
    l = jnp.sum(e, axis=-1, keepdims=True) + jnp.exp2(sink_col - m)
    return jnp.dot(e.astype(BF16), vh, preferred_element_type=F32) * (1.0 / l)


def _stack_heads(q, h, group):
    hd = HEAD_DIM
    return jnp.concatenate(
        [q[:, (h * group + g) * hd:(h * group + g + 1) * hd] for g in range(group)], axis=0)


def _unstack_heads(o, group):
    tq = o.shape[0] // group
    return [o[g * tq:(g + 1) * tq, :] for g in range(group)]


ATTN_Q_COLS = 512
PAIR = 2 * CHUNK


def _pair_codes(n_buckets):
    code = _bucket_codes(PAIR, WINDOW + PAIR, -WINDOW, n_buckets).T
    key_chunk = np.arange(WINDOW + PAIR)[:, None] // CHUNK
    q_chunk = np.arange(PAIR)[None, :] // CHUNK
    visible = (key_chunk >= q_chunk) & (key_chunk <= q_chunk + WINDOW // CHUNK)
    return np.where(visible, code, n_buckets).astype(np.int32)


def _attn_prompt_body(qT_ref, kp_ref, kc_ref, vTp_ref, vTc_ref, bias_ref, sink_ref, oT_ref,
                      k_scr, vT_scr, *, kvh, group):
    i = pl.program_id(1)
    tq = qT_ref.shape[1]
    nk = WINDOW + PAIR
    hd = HEAD_DIM
    for h in range(kvh):
        k_scr[h, 0:WINDOW, :] = kp_ref[:, h * hd:(h + 1) * hd].astype(BF16)
        k_scr[h, WINDOW:WINDOW + tq, :] = kc_ref[:, h * hd:(h + 1) * hd].astype(BF16)
    vT_scr[:, 0:WINDOW] = vTp_ref[...].astype(BF16)
    vT_scr[:, WINDOW:WINDOW + tq] = vTc_ref[...].astype(BF16)
    row = lax.broadcasted_iota(jnp.int32, (nk, group * PAIR), 0)
    start_mask = jnp.where(row < jnp.where(i == 0, WINDOW, 0), -jnp.inf, 0.0).astype(F32)
    units = [(p * PAIR, h) for p in range(tq // PAIR) for h in range(kvh)]

    def scores(c0, h):
        qsT = jnp.concatenate(
            [qT_ref[hh * hd:(hh + 1) * hd, c0:c0 + PAIR] for hh in range(h * group, (h + 1) * group)],
            axis=1)
        s = jnp.dot(k_scr[h, c0:c0 + nk, :], qsT, preferred_element_type=F32) + bias_ref[h]
        return s + start_mask if c0 == 0 else s

    s_next = scores(*units[0])
    for n, (c0, h) in enumerate(units):
        s = s_next
        if n + 1 < len(units):
            s_next = scores(*units[n + 1])
        sink = sink_ref[h] * LOG2E
        m = jnp.maximum(jnp.max(s, axis=0, keepdims=True), sink)
        e = jnp.exp2(s - m)
        l = jnp.sum(e, axis=0, keepdims=True) + jnp.exp2(sink - m)
        oT = jnp.dot(vT_scr[h * hd:(h + 1) * hd, c0:c0 + nk], e.astype(BF16),
                     preferred_element_type=F32) * (1.0 / l)
        for g in range(group):
            hh = h * group + g
            oT_ref[hh * hd:(hh + 1) * hd, c0:c0 + PAIR] = (
                oT[:, g * PAIR:(g + 1) * PAIR].astype(oT_ref.dtype))


def _attn_prompt_call(qT, k, vT, bias, sink_row, kvh, group):
    b, dq, s = qT.shape
    dk = k.shape[2]
    tq = _tile(s, ATTN_Q_COLS, PAIR)
    per = tq // WINDOW
    prev = lambda i: jnp.maximum(i * per - 1, 0)
    blocks = [((dq, tq), BF16)] * 2 + [((WINDOW + tq, dk), F32)] * 2 + [
        (bias.shape, F32), (sink_row.shape[:1] + (SUBLANES, sink_row.shape[2]), F32)]
    scratch = [((kvh, WINDOW + tq, LANES), BF16), ((dk, WINDOW + tq), BF16)]
    return pl.pallas_call(
        functools.partial(_attn_prompt_body, kvh=kvh, group=group),
        out_shape=jax.ShapeDtypeStruct((b, dq, s), BF16),
        grid=(b, s // tq),
        in_specs=[
            pl.BlockSpec((None, dq, tq), lambda bi, i: (bi, 0, i)),
            pl.BlockSpec((None, WINDOW, dk), lambda bi, i: (bi, prev(i), 0)),
            pl.BlockSpec((None, tq, dk), lambda bi, i: (bi, i, 0)),
            pl.BlockSpec((None, dk, WINDOW), lambda bi, i: (bi, 0, prev(i))),
            pl.BlockSpec((None, dk, tq), lambda bi, i: (bi, 0, i)),
            pl.BlockSpec(bias.shape, lambda bi, i: (0, 0, 0)),
            pl.BlockSpec(sink_row.shape, lambda bi, i: (0, 0, 0)),
        ],
        out_specs=pl.BlockSpec((None, dq, tq), lambda bi, i: (bi, 0, i)),
        scratch_shapes=[pltpu.VMEM((kvh, WINDOW + tq, HEAD_DIM), BF16),
                        pltpu.VMEM((dk, WINDOW + tq), BF16)],
        compiler_params=pltpu.CompilerParams(
            dimension_semantics=("parallel", "arbitrary"),
            vmem_limit_bytes=_vmem_limit(blocks, scratch)),
        name="swa_sink_attention_prompt",
    )(qT, k, k, vT, vT, bias, sink_row)


ATTN_SAMPLE_BATCH = 4


def _attn_sample_body(q_ref, k_ref, v_ref, bias_ref, sink_ref, o_ref, *, kvh, group):
    hd = HEAD_DIM
    units = [(b, h) for b in range(q_ref.shape[0]) for h in range(kvh)]

    def scores(b, h):
        qs = _stack_heads(q_ref[b].astype(F32), h, group).astype(BF16)
        return _unit_scores(qs, k_ref[b, :, h * hd:(h + 1) * hd].astype(BF16), bias_ref[h])

    s_next = scores(*units[0])
    outs = []
    for n, (b, h) in enumerate(units):
        s = s_next
        if n + 1 < len(units):
            s_next = scores(*units[n + 1])
        o = _unit_output(s, v_ref[b, :, h * hd:(h + 1) * hd].astype(BF16), sink_ref[h])
        outs += _unstack_heads(o, group)
        if h == kvh - 1:
            o_ref[b] = jnp.concatenate(outs, axis=1).astype(o_ref.dtype)
            outs = []


def _attn_sample_call(q, k_all, v_all, bias, sink_col, kvh, group):
    b, t, dq = q.shape
    nk, dk = k_all.shape[1:]
    bb = math.gcd(b, ATTN_SAMPLE_BATCH)
    blocks = [((bb, t, dq), BF16)] * 2 + [((bb, nk, dk), F32)] * 2 + [
        (bias.shape[:2] + (2 * LANES,), F32), (sink_col.shape[:2] + (LANES,), F32)]
    return pl.pallas_call(
        functools.partial(_attn_sample_body, kvh=kvh, group=group),
        out_shape=jax.ShapeDtypeStruct((b, t, dq), BF16),
        grid=(b // bb,),
        in_specs=[
            pl.BlockSpec((bb, t, dq), lambda bi: (bi, 0, 0)),
            pl.BlockSpec((bb, nk, dk), lambda bi: (bi, 0, 0)),
            pl.BlockSpec((bb, nk, dk), lambda bi: (bi, 0, 0)),
            pl.BlockSpec(bias.shape, lambda bi: (0, 0, 0)),
            pl.BlockSpec(sink_col.shape, lambda bi: (0, 0, 0)),
        ],
        out_specs=pl.BlockSpec((bb, t, dq), lambda bi: (bi, 0, 0)),
        compiler_params=pltpu.CompilerParams(
            dimension_semantics=("parallel",), vmem_limit_bytes=_vmem_limit(blocks)),
        name="swa_sink_attention_sample",
    )(q, k_all, v_all, bias, sink_col)


def _trunks(xp3, xs3, mods_p, mods_s, state_conv, win_k, win_v, p):
    bp, tp, d = xp3.shape
    bs, ts, _ = xs3.shape
    mp, ms = bp * tp, bs * ts
    depth = p["w_mod"].shape[0]
    d_ff = p["w_gu"].shape[2] // 2
    n_heads = p["attn_sinks"].shape[1]
    dq = n_heads * HEAD_DIM
    kvh = (p["w_qkv"].shape[2] - dq) // (2 * HEAD_DIM)
    group = n_heads // kvh
    dkv = kvh * HEAD_DIM
    kw = p["w_dw"].shape[1]
    n_buckets = p["rel_bias_table"].shape[1]
    conv_p, k_p, v_p, conv_s, k_s, v_s = [], [], [], [], [], []

    xp, xs = xp3.reshape(mp, d), xs3.reshape(ms, d)
    hp = _norm_mod_call(xp, p["norm_mix"][0], mods_p[0], 0, 1, PANEL_ROWS)
    hs = _norm_mod_call(xs, p["norm_mix"][0], mods_s[0], 0, 1, ms)
    for i in range(depth):
        j = i // 2
        ffn_norm = lambda mods: (p["norm_ffn"][i], mods[i], 3, 4)
        if i % 2 == 0:
            u_p, u_s, wq = _mm_call(hp, p["w_pw1"], j, (0, d), d, x2=hs, cast=(p["w_pw2"], j),
                                    bias=p["b_pw1"], act="glu", name="pw1_glu")
            u_p, u_s = u_p.reshape(bp, tp, d), u_s.reshape(bs, ts, d)
            conv_w = (p["w_dw"], p["b_dw"], p["conv_ln_g"], p["conv_ln_b"], j)
            z_p = _conv_prompt_call(u_p, *conv_w)
            conv_p.append(u_p[:, tp - (kw - 1):])
            pad = (-(kw - 1 + ts)) % SUBLANES
            up = jnp.concatenate([jnp.zeros((bs, pad, d), F32), state_conv[j], u_s], axis=1)
            z_s = _conv_sample_call(up, pad, ts, *conv_w)
            conv_s.append(up[:, pad + ts:])
            xp, hp = _rowmm_call(z_p.reshape(mp, d), wq, xp, mods_p[i], 2, ffn_norm(mods_p),
                                 bias=p["b_pw2"][j], name="pw2_residual_norm")
            xs, hs = _rowmm_call(z_s.reshape(ms, d), wq, xs, mods_s[i], 2, ffn_norm(mods_s),
                                 bias=p["b_pw2"][j], name="pw2_residual_norm")
        else:
            sinks = p["attn_sinks"][j].reshape(kvh, group, 1)
            qT, q_s, wq = _mm_call(hp, p["w_qkv"], j, (0,), dq, x2=hs, cast=(p["w_o"], j),
                                   bias=p["b_qkv"], scale=LOG2E * HEAD_DIM ** -0.5,
                                   out_dtype=BF16, tn=Q_PANEL_COLS, name="q_proj",
                                   out_mode="t", seq=tp)
            (k, vT), kv_s = _mm_call(hp, p["w_qkv"], j, (dq,), 2 * dkv, x2=hs, bias=p["b_qkv"],
                                     tn=2 * dkv, name="kv_proj", out_mode="split_t", seq=tp)
            k = k.reshape(bp, tp, dkv)
            bias = _bias_call(_pair_codes(n_buckets), p["rel_bias_table"], group)
            sink_row = jnp.repeat(sinks, PAIR, axis=2).reshape(kvh, 1, group * PAIR)
            oT = _attn_prompt_call(qT, k, vT, bias, sink_row, kvh, group)
            k_p.append(k[:, tp - WINDOW:].reshape(bp, WINDOW, kvh, HEAD_DIM))
            v_p.append(jnp.swapaxes(vT[:, :, tp - WINDOW:], 1, 2)
                       .reshape(bp, WINDOW, kvh, HEAD_DIM))
            kv3 = kv_s.reshape(bs, ts, 2, kvh, HEAD_DIM)
            k_all = jnp.concatenate([win_k[j], kv3[:, :, 0]], axis=1)
            v_all = jnp.concatenate([win_v[j], kv3[:, :, 1]], axis=1)
            n_keys = WINDOW + ts
            bias = _bias_call(_bucket_codes(ts, n_keys, -WINDOW, n_buckets),
                              p["rel_bias_table"]).reshape(kvh, group * ts, n_keys)
            o_s = _attn_sample_call(q_s.reshape(bs, ts, dq), k_all.reshape(bs, n_keys, dkv),
                                    v_all.reshape(bs, n_keys, dkv), bias,
                                    jnp.repeat(sinks, ts, axis=1), kvh, group)
            k_s.append(k_all[:, ts:])
            v_s.append(v_all[:, ts:])
            xp, hp = _rowmm_call(oT, wq, xp, mods_p[i], 2, ffn_norm(mods_p), bias=p["b_o"][j],
                                 x_t=True, name="wo_residual_norm")
            xs, hs = _rowmm_call(o_s.reshape(ms, dq), wq, xs, mods_s[i], 2, ffn_norm(mods_s),
                                 bias=p["b_o"][j], name="wo_residual_norm")
        a_p, a_s, wq = _mm_call(hp, p["w_gu"], i, (0, d_ff), d_ff, x2=hs, cast=(p["w_down"], i),
                                act="swiglu", out_dtype=BF16, name="ffn_gate_up")
        if i + 1 < depth:
            nxt = lambda mods: (p["norm_mix"][i + 1], mods[i + 1], 0, 1)
            xp, hp = _rowmm_call(a_p, wq, xp, mods_p[i], 5, nxt(mods_p),
                                 name="ffn_down_residual_norm")
            xs, hs = _rowmm_call(a_s, wq, xs, mods_s[i], 5, nxt(mods_s),
                                 name="ffn_down_residual_norm")
        else:
            y_p = _rowmm_call(a_p, wq, xp, mods_p[i], 5, (p["norm_out"],),
                              name="ffn_down_residual_final_norm")
            y_s = _rowmm_call(a_s, wq, xs, mods_s[i], 5, (p["norm_out"],),
                              name="ffn_down_residual_final_norm")
    stack = jnp.stack
    return (y_p.reshape(bp, tp, d), y_s.reshape(bs, ts, d), stack(conv_p), stack(k_p), stack(v_p),
            stack(conv_s), stack(k_s), stack(v_s))


def kernel(x_prompt, x_sample, c_prompt, c_sample, state_conv, cache_win_k, cache_win_v, w_mod, b_mod, norm_mix, norm_ffn, w_pw1, b_pw1, w_dw, b_dw, conv_ln_g, conv_ln_b, w_pw2, b_pw2, w_qkv, b_qkv, w_o, b_o, attn_sinks, rel_bias_table, w_gu, w_down, norm_out):
    p = dict(w_mod=w_mod, b_mod=b_mod, norm_mix=norm_mix, norm_ffn=norm_ffn, w_pw1=w_pw1,
             b_pw1=b_pw1, w_dw=w_dw, b_dw=b_dw, conv_ln_g=conv_ln_g, conv_ln_b=conv_ln_b,
             w_pw2=w_pw2, b_pw2=b_pw2, w_qkv=w_qkv, b_qkv=b_qkv, w_o=w_o, b_o=b_o,
             attn_sinks=attn_sinks, rel_bias_table=rel_bias_table, w_gu=w_gu, w_down=w_down,
             norm_out=norm_out)
    bp, sp, d = x_prompt.shape
    bs, ts, _ = x_sample.shape
    depth = w_mod.shape[0]

    n_c = bp + bs
    c_all = jnp.concatenate(
        [c_prompt, c_sample, jnp.zeros((-n_c % BF16_SUBLANES, d), F32)], axis=0)
    mod_all = _mod_call(c_all, w_mod, b_mod)

    mods_p = [_Mod(mod_all[l, :bp].reshape(bp, 1, 6 * d), sp) for l in range(depth)]
    mods_s = [_Mod(mod_all[l, bp:n_c].reshape(bs, 1, 6 * d), ts) for l in range(depth)]

    return _trunks(x_prompt, x_sample, mods_p, mods_s, state_conv, cache_win_k, cache_win_v, p)
```

```python
import functools
import math

import numpy as np
import jax
import jax.numpy as jnp
from jax import lax
from jax.experimental import pallas as pl
from jax.experimental.pallas import tpu as pltpu

F32 = jnp.float32
BF16 = jnp.bfloat16

CHUNK = 64
WINDOW = 128
HEAD_DIM = 64
MAX_DISTANCE = 128
EPS = 1e-6
LOG2E = math.log2(math.e)

V7X_VMEM_BYTES = 64 * 2**20
V7X_VMEM_CAP = V7X_VMEM_BYTES - 8 * 2**20
SUBLANES = 8
LANES = 128
BF16_SUBLANES = 2 * SUBLANES

PANEL_ROWS = 1024
PANEL_COLS = 512
Q_PANEL_COLS = 1024
ROWMM_ROWS = 512
CONV_ROWS = 128
MOD_COLS = 1024


def _nbytes(shape, dtype):
    return math.prod(shape) * jnp.dtype(dtype).itemsize


def _vmem_limit(blocks, scratch=()):
    est = 2 * sum(_nbytes(s, d) for s, d in blocks) + sum(_nbytes(s, d) for s, d in scratch)
    return int(min(V7X_VMEM_CAP, max(32 * 2**20, 2 * est)))


def _tile(dim, pref, mult=LANES):
    if dim <= pref:
        return dim
    t = (pref // mult) * mult
    while t >= mult:
        if dim % t == 0:
            return t
        t -= mult
    return dim


def _sigmoid(x):
    return 1.0 / (1.0 + jnp.exp(-x))


def _mod_body(c_ref, w_ref, b_ref, o_ref):
    c = c_ref[...]
    a = (c * _sigmoid(c)).astype(BF16)
    o_ref[...] = jnp.dot(a, w_ref[...].astype(BF16), preferred_element_type=F32) + b_ref[...]


def _mod_call(c_all, w_mod, b_mod):
    depth, d, n = w_mod.shape
    mp = c_all.shape[0]
    tn = _tile(n, MOD_COLS)
    blocks = [((mp, d), F32), ((d, tn), F32), ((1, tn), F32), ((mp, tn), F32)]
    return pl.pallas_call(
        _mod_body,
        out_shape=jax.ShapeDtypeStruct((depth, mp, n), F32),
        grid=(depth, n // tn),
        in_specs=[
            pl.BlockSpec((mp, d), lambda l, j: (0, 0)),
            pl.BlockSpec((None, d, tn), lambda l, j: (l, 0, j)),
            pl.BlockSpec((None, 1, tn), lambda l, j: (l, 0, j)),
        ],
        out_specs=pl.BlockSpec((None, mp, tn), lambda l, j: (l, 0, j)),
        compiler_params=pltpu.CompilerParams(
            dimension_semantics=("arbitrary", "arbitrary"),
            vmem_limit_bytes=_vmem_limit(blocks, [((d, tn), BF16)])),
        name="adaln_mod",
    )(c_all, w_mod, b_mod.reshape(depth, 1, n))


class _Mod:
    def __init__(self, arr, rows_per_group):
        self.arr = arr
        self.rows_per_group = rows_per_group

    def tile_rows(self, total_rows, pref, mult):
        if self.rows_per_group >= pref:
            return _tile(self.rows_per_group, pref, mult)
        return _tile(total_rows, pref, max(mult, self.rows_per_group))

    def spec(self, seg, d, tm):
        if tm <= self.rows_per_group:
            per = self.rows_per_group // tm
            return pl.BlockSpec((1, 1, d), lambda i: (i // per, 0, seg))
        return pl.BlockSpec((tm // self.rows_per_group, 1, d), lambda i: (i, 0, seg))


def _grouped(v, m):
    return v.reshape(m.shape[0], v.shape[0] // m.shape[0], v.shape[1])


def _norm_mod_body(x_ref, g_ref, sh_ref, sc_ref, o_ref):
    x = x_ref[...]
    y = x * lax.rsqrt(jnp.mean(x * x, axis=-1, keepdims=True) + EPS) * g_ref[...]
    sc = sc_ref[...]
    o_ref[...] = (_grouped(y, sc) * (1.0 + sc) + sh_ref[...]).reshape(x.shape).astype(o_ref.dtype)


def _norm_mod_call(x, g, mod, seg_shift, seg_scale, tm):
    m, d = x.shape
    tm = mod.tile_rows(m, tm, SUBLANES)
    blocks = [((tm, d), F32), ((1, d), F32), ((tm, d), BF16)]
    return pl.pallas_call(
        _norm_mod_body,
        out_shape=jax.ShapeDtypeStruct((m, d), BF16),
        grid=(m // tm,),
        in_specs=[
            pl.BlockSpec((tm, d), lambda i: (i, 0)),
            pl.BlockSpec((1, d), lambda i: (0, 0)),
            mod.spec(seg_shift, d, tm),
            mod.spec(seg_scale, d, tm),
        ],
        out_specs=pl.BlockSpec((tm, d), lambda i: (i, 0)),
        compiler_params=pltpu.CompilerParams(
            dimension_semantics=("parallel",), vmem_limit_bytes=_vmem_limit(blocks)),
        name="rmsnorm_modulate",
    )(x, g.reshape(1, d), mod.arr, mod.arr)


def _mm_body(x_ref, *refs, layer, offs, has_bias, act, scale, out_mode, has_x2, has_cast):
    n_w = len(offs)
    if has_x2:
        x2_ref, refs = refs[0], refs[1:]
    w_hbm, refs = refs[0], refs[1:]
    bs = refs[:n_w] if has_bias else ()
    refs = refs[len(bs):]
    if has_cast:
        wc_ref, refs = refs[0], refs[1:]
    n_o = 2 if out_mode == "split_t" else 1
    o_refs = refs[:n_o]
    refs = refs[n_o:]
    if has_x2:
        o2_ref, refs = refs[0], refs[1:]
    if has_cast:
        oc_ref, refs = refs[0], refs[1:]
        oc_ref[...] = wc_ref[...].astype(BF16)
    wbs, (stage, sem) = refs[:n_w], refs[n_w:]
    tn = wbs[0].shape[1]

    def panel_copy(i, j, slot):
        cols = pl.ds(pl.multiple_of((offs[i] + j) * tn, tn), tn)
        return pltpu.make_async_copy(w_hbm.at[layer, :, cols], stage.at[i, slot], sem.at[i, slot])

    def compute(x):
        ps = [None] * n_w
        for i in ((1, 0) if act == "glu" else range(n_w)):
            p = jnp.dot(x, wbs[i][...], preferred_element_type=F32)
            if has_bias:
                p = p + bs[i][...]
            ps[i] = p
        if act == "glu":
            y = ps[0] * _sigmoid(ps[1])
        elif act == "swiglu":
            y = ps[0] * _sigmoid(ps[0]) * ps[1]
        else:
            y = ps[0]
        return y * scale if scale != 1.0 else y

    @pl.when(pl.program_id(1) == 0)
    def _():
        j = pl.program_id(0)
        slot = j % 2

        @pl.when(j == 0)
        def _():
            for i in range(n_w):
                panel_copy(i, 0, 0).start(priority=i % 2)

        for i in range(n_w):
            panel_copy(i, j, slot).wait()

        @pl.when(j + 1 < pl.num_programs(0))
        def _():
            for i in range(n_w):
                panel_copy(i, j + 1, 1 - slot).start(priority=1)

        for i in range(n_w):
            wbs[i][...] = stage[i, slot].astype(BF16)
        if has_x2:
            o2_ref[...] = compute(x2_ref[...]).astype(o2_ref.dtype)

    y = compute(x_ref[...])
    if out_mode == "plain":
        o_refs[0][...] = y.astype(o_refs[0].dtype)
    elif out_mode == "t":
        o_refs[0][...] = y.T.astype(o_refs[0].dtype)
    else:
        half = y.shape[1] // 2
        o_refs[0][...] = y[:, :half].astype(o_refs[0].dtype)
        o_refs[1][...] = y[:, half:].T.astype(o_refs[1].dtype)


def _mm_call(x, w, layer, col_starts, n_out, *, x2=None, cast=None, bias=None, act=None,
             scale=1.0, out_dtype=F32, tm=PANEL_ROWS, tn=PANEL_COLS, name="matmul",
             out_mode="plain", seq=None):
    m, k = x.shape
    if out_mode == "plain":
        tm = _tile(m, tm, SUBLANES)
    else:
        tm = _tile(seq, tm, LANES)
    tn = _tile(n_out, tn)
    n_w = len(col_starts)
    assert all(c % tn == 0 for c in col_starts) and m % tm == 0 and n_out % tn == 0
    has_bias = bias is not None
    per_b = None if seq is None else seq // tm

    in_specs = [pl.BlockSpec((tm, k), lambda j, i: (i, 0))]
    args = [x]
    blocks = [((tm, k), x.dtype), ((tm, tn), out_dtype)]
    if x2 is not None:
        m2 = x2.shape[0]
        in_specs.append(pl.BlockSpec((m2, k), lambda j, i: (0, 0)))
        args.append(x2)
        blocks += [((m2, k), x2.dtype), ((m2, tn), out_dtype)]
    in_specs.append(pl.BlockSpec(memory_space=pltpu.HBM))
    args.append(w)
    if has_bias:
        b3 = bias.reshape(bias.shape[0], 1, bias.shape[1])
        for c in col_starts:
            off = c // tn
            in_specs.append(pl.BlockSpec((None, 1, tn), lambda j, i, off=off: (layer, 0, off + j)))
            args.append(b3)
    n_i = m // tm
    if cast is not None:
        w_other, layer_other = cast
        kc, dc = w_other.shape[1:]
        slab = kc // ((n_out // tn) * n_i)
        assert slab * (n_out // tn) * n_i == kc and slab % BF16_SUBLANES == 0
        in_specs.append(pl.BlockSpec((None, slab, dc), lambda j, i: (layer_other, j * n_i + i, 0)))
        args.append(w_other)
        blocks += [((slab, dc), F32), ((slab, dc), BF16)]
    scratch = [((k, tn), BF16)] * n_w + [((n_w, 2, k, tn), F32)]
    body = functools.partial(_mm_body, layer=layer, offs=tuple(c // tn for c in col_starts),
                             has_bias=has_bias, act=act, scale=scale, out_mode=out_mode,
                             has_x2=x2 is not None, has_cast=cast is not None)
    if out_mode == "plain":
        out_shape = [jax.ShapeDtypeStruct((m, n_out), out_dtype)]
        out_specs = [pl.BlockSpec((tm, tn), lambda j, i: (i, j))]
    elif out_mode == "t":
        out_shape = [jax.ShapeDtypeStruct((m // seq, n_out, seq), out_dtype)]
        out_specs = [pl.BlockSpec((None, tn, tm), lambda j, i: (i // per_b, j, i % per_b))]
    else:
        assert tn == n_out
        half = n_out // 2
        out_shape = [jax.ShapeDtypeStruct((m, half), out_dtype),
                     jax.ShapeDtypeStruct((m // seq, half, seq), out_dtype)]
        out_specs = [pl.BlockSpec((tm, half), lambda j, i: (i, 0)),
                     pl.BlockSpec((None, half, tm), lambda j, i: (i // per_b, 0, i % per_b))]
    if x2 is not None:
        out_shape.append(jax.ShapeDtypeStruct((m2, n_out), out_dtype))
        out_specs.append(pl.BlockSpec((m2, tn), lambda j, i: (0, j)))
    if cast is not None:
        out_shape.append(jax.ShapeDtypeStruct((kc, dc), BF16))
        out_specs.append(pl.BlockSpec((slab, dc), lambda j, i: (j * n_i + i, 0)))
    outs = pl.pallas_call(
        body,
        out_shape=out_shape,
        grid=(n_out // tn, n_i),
        in_specs=in_specs,
        out_specs=out_specs,
        scratch_shapes=[pltpu.VMEM(s, d) for s, d in scratch]
        + [pltpu.SemaphoreType.DMA((n_w, 2))],
        compiler_params=pltpu.CompilerParams(
            dimension_semantics=("arbitrary", "arbitrary"),
            vmem_limit_bytes=_vmem_limit(blocks, scratch)),
        name=name,
    )(*args)
    n_primary = 2 if out_mode == "split_t" else 1
    primary = outs[0] if n_primary == 1 else outs[:n_primary]
    extras = list(outs[n_primary:])
    return (primary, *extras) if extras else primary


ROWMM_SUB_ROWS = 256
ROWMM_K_CHUNKS = 4
MXU_K = 256


def _rowmm_body(*refs, has_bias, x_t, final, single_step):
    x_ref, w_hbm = refs[:2]
    refs = refs[2:]
    if has_bias:
        b_ref, refs = refs[0], refs[1:]
    res_ref, gate_ref, ng_ref = refs[:3]
    refs = refs[3:]
    if not final:
        sh_ref, sc_ref = refs[:2]
        refs = refs[2:]
    n_o = 1 if final else 2
    o_refs = refs[:n_o]
    refs = refs[n_o:]
    w_res, sem = refs
    k = w_res.shape[0]
    tiles = k // MXU_K
    n_chunks = min(ROWMM_K_CHUNKS, tiles)
    bounds = [MXU_K * (tiles * c // n_chunks) for c in range(n_chunks + 1)]

    def chunk_copy(c):
        rows = pl.ds(bounds[c], bounds[c + 1] - bounds[c])
        return pltpu.make_async_copy(w_hbm.at[rows, :], w_res.at[rows, :], sem.at[c])

    tm = res_ref.shape[0]
    sub = min(tm, ROWMM_SUB_ROWS)

    def lhs(r0, k0=0, k1=k):
        return x_ref[k0:k1, r0:r0 + sub].T if x_t else x_ref[r0:r0 + sub, k0:k1]

    def groups_of(ref, r0):
        per = tm // ref.shape[0]
        return ref[...] if per >= tm else ref[r0 // per:(r0 + sub) // per]

    def finish(r0, y):
        if has_bias:
            y = y + b_ref[...]
        gate = groups_of(gate_ref, r0)
        xn = (_grouped(res_ref[r0:r0 + sub, :], gate) + gate * _grouped(y, gate)).reshape(y.shape)
        r = xn * lax.rsqrt(jnp.mean(xn * xn, axis=-1, keepdims=True) + EPS) * ng_ref[...]
        if final:
            o_refs[0][r0:r0 + sub, :] = r
        else:
            sc = groups_of(sc_ref, r0)
            o_refs[0][r0:r0 + sub, :] = xn
            o_refs[1][r0:r0 + sub, :] = (
                _grouped(r, sc) * (1.0 + sc) + groups_of(sh_ref, r0)
            ).reshape(y.shape).astype(o_refs[1].dtype)

    for c in range(n_chunks):
        @pl.when(pl.program_id(0) == 0)
        def _():
            chunk_copy(c).start()

    first = 0
    if single_step:
        y = None
        for c in range(n_chunks):
            chunk_copy(c).wait()
            part = jnp.dot(lhs(0, bounds[c], bounds[c + 1]), w_res[bounds[c]:bounds[c + 1], :],
                           preferred_element_type=F32)
            y = part if y is None else y + part
        finish(0, y)
        first = sub
    else:
        for c in range(n_chunks):
            @pl.when(pl.program_id(0) == 0)
            def _():
                chunk_copy(c).wait()

    for r0 in range(first, tm, sub):
        finish(r0, jnp.dot(lhs(r0), w_res[...], preferred_element_type=F32))


def _rowmm_call(x, w, res, gate, gate_seg, nxt, *, bias=None, x_t=False, tm=ROWMM_ROWS,
                name="rowmm"):
    if x_t:
        nb, k, seq = x.shape
        m = nb * seq
    else:
        m, k = x.shape
    d = w.shape[-1]
    final = len(nxt) == 1
    scratch = [((k, d), BF16)]
    n_mods = 1 if final else 3

    def vmem_need(rows):
        blocks = [((rows, k), BF16), ((rows, d), F32), ((rows, d), F32 if final else BF16)]
        blocks += [((rows, d), F32)] * (0 if final else 1)
        blocks += [((-(-rows // gate.rows_per_group) * SUBLANES, d), F32)] * n_mods
        temporaries = 4 * _nbytes((rows, d), F32)
        return (2 * sum(_nbytes(s, t) for s, t in blocks) + sum(_nbytes(s, t) for s, t in scratch)
                + temporaries)

    mult = LANES if x_t else SUBLANES
    tm = gate.tile_rows(m, tm, mult)
    rpg = gate.rows_per_group
    fits_groups = lambda t: t % mult == 0 and (rpg % t == 0 or t % rpg == 0)
    while vmem_need(tm) > V7X_VMEM_CAP and tm % 2 == 0 and fits_groups(tm // 2):
        tm //= 2
    assert final or nxt[1].rows_per_group == gate.rows_per_group
    has_bias = bias is not None
    row = lambda i: (i, 0)
    if x_t:
        per_b = seq // tm
        in_specs = [pl.BlockSpec((None, k, tm), lambda i: (i // per_b, 0, i % per_b))]
    else:
        in_specs = [pl.BlockSpec((tm, k), row)]
    in_specs.append(pl.BlockSpec(memory_space=pltpu.HBM))
    args = [x, w]
    if has_bias:
        in_specs.append(pl.BlockSpec((1, d), lambda i: (0, 0)))
        args.append(bias.reshape(1, d))
    in_specs += [pl.BlockSpec((tm, d), row), gate.spec(gate_seg, d, tm),
                 pl.BlockSpec((1, d), lambda i: (0, 0))]
    args += [res, gate.arr, nxt[0].reshape(1, d)]
    if final:
        out_shape = jax.ShapeDtypeStruct((m, d), F32)
        out_specs = pl.BlockSpec((tm, d), row)
    else:
        _, nmod, seg_shift, seg_scale = nxt
        in_specs += [nmod.spec(seg_shift, d, tm), nmod.spec(seg_scale, d, tm)]
        args += [nmod.arr, nmod.arr]
        out_shape = [jax.ShapeDtypeStruct((m, d), F32), jax.ShapeDtypeStruct((m, d), BF16)]
        out_specs = [pl.BlockSpec((tm, d), row), pl.BlockSpec((tm, d), row)]
    return pl.pallas_call(
        functools.partial(_rowmm_body, has_bias=has_bias, x_t=x_t, final=final,
                          single_step=m == tm),
        out_shape=out_shape,
        grid=(m // tm,),
        in_specs=in_specs,
        out_specs=out_specs,
        scratch_shapes=[pltpu.VMEM(s, dt) for s, dt in scratch]
        + [pltpu.SemaphoreType.DMA((ROWMM_K_CHUNKS,))],
        compiler_params=pltpu.CompilerParams(
            dimension_semantics=("arbitrary",),
            vmem_limit_bytes=min(V7X_VMEM_CAP, max(32 * 2**20, vmem_need(tm) + 4 * 2**20))),
        name=name,
    )(*args)


def _conv_ln_swish(up_ref, pad, tt, w_ref, bdw_ref, lg_ref, lb_ref, acc_ref, o_ref):
    kw, d = w_ref.shape
    by_shift = [[(a, SUBLANES * a + s - pad) for a in range((pad + kw - 1) // SUBLANES + 1)
                 if 0 <= SUBLANES * a + s - pad < kw] for s in range(SUBLANES)]

    def strip(c, carry):
        cols = pl.ds(pl.multiple_of(c * LANES, LANES), LANES)
        z = bdw_ref[:, cols]
        for s, taps in enumerate(by_shift):
            n = tt + SUBLANES if s else tt
            q = None
            for a, k in taps:
                term = up_ref[pl.ds(SUBLANES * a, n), cols] * w_ref[pl.ds(k, 1), cols]
                q = term if q is None else q + term
            if q is not None:
                z = z + q[s:s + tt]
        acc_ref[:, cols] = z
        return carry

    lax.fori_loop(0, d // LANES, strip, 0)
    z = acc_ref[...]
    mu = jnp.mean(z, axis=-1, keepdims=True)
    zc = z - mu
    y = zc * lax.rsqrt(jnp.mean(zc * zc, axis=-1, keepdims=True) + EPS)
    y = y * lg_ref[...] + lb_ref[...]
    o_ref[...] = (y * _sigmoid(y)).astype(o_ref.dtype)


def _conv_prompt_body(main_ref, halo_ref, w_ref, bdw_ref, lg_ref, lb_ref, o_ref, up_ref, acc_ref,
                      *, halo, tt):
    kw = w_ref.shape[0]

    @pl.when(pl.program_id(1) == 0)
    def _():
        up_ref[0:halo, :] = jnp.zeros((halo, up_ref.shape[1]), F32)

    @pl.when(pl.program_id(1) > 0)
    def _():
        up_ref[0:halo, :] = halo_ref[...]

    up_ref[halo:halo + tt, :] = main_ref[...]
    _conv_ln_swish(up_ref, halo - (kw - 1), tt, w_ref, bdw_ref, lg_ref, lb_ref, acc_ref, o_ref)


def _conv_prompt_call(u, w_dw, b_dw, ln_g, ln_b, layer, tt=CONV_ROWS):
    b, t, d = u.shape
    kw = w_dw.shape[1]
    halo = -(-(kw - 1) // SUBLANES) * SUBLANES
    tt = _tile(t, tt, halo)
    hb = tt // halo
    vec = lambda: pl.BlockSpec((None, 1, d), lambda bi, ti: (layer, 0, 0))
    blocks = [((tt, d), F32), ((halo, d), F32), ((kw, d), F32), ((tt, d), BF16)]
    scratch = [((halo + tt, d), F32), ((tt, d), F32)]
    return pl.pallas_call(
        functools.partial(_conv_prompt_body, halo=halo, tt=tt),
        out_shape=jax.ShapeDtypeStruct((b, t, d), BF16),
        grid=(b, t // tt),
        in_specs=[
            pl.BlockSpec((None, tt, d), lambda bi, ti: (bi, ti, 0)),
            pl.BlockSpec((None, halo, d), lambda bi, ti: (bi, jnp.maximum(ti * hb - 1, 0), 0)),
            pl.BlockSpec((None, kw, d), lambda bi, ti: (layer, 0, 0)),
            vec(), vec(), vec(),
        ],
        out_specs=pl.BlockSpec((None, tt, d), lambda bi, ti: (bi, ti, 0)),
        scratch_shapes=[pltpu.VMEM(s, dt) for s, dt in scratch],
        compiler_params=pltpu.CompilerParams(
            dimension_semantics=("parallel", "arbitrary"),
            vmem_limit_bytes=_vmem_limit(blocks, scratch)),
        name="dwconv_ln_swish_prompt",
    )(u, u, w_dw, b_dw.reshape(-1, 1, d), ln_g.reshape(-1, 1, d), ln_b.reshape(-1, 1, d))


def _conv_sample_body(up_ref, w_ref, bdw_ref, lg_ref, lb_ref, o_ref, acc_ref, *, pad, tt):
    _conv_ln_swish(up_ref, pad, tt, w_ref, bdw_ref, lg_ref, lb_ref, acc_ref, o_ref)


def _conv_sample_call(up, pad, tt, w_dw, b_dw, ln_g, ln_b, layer):
    b, rows, d = up.shape
    kw = w_dw.shape[1]
    vec = lambda: pl.BlockSpec((None, 1, d), lambda bi: (layer, 0, 0))
    blocks = [((rows, d), F32), ((kw, d), F32), ((tt, d), BF16)]
    scratch = [((tt, d), F32)]
    return pl.pallas_call(
        functools.partial(_conv_sample_body, pad=pad, tt=tt),
        out_shape=jax.ShapeDtypeStruct((b, tt, d), BF16),
        grid=(b,),
        in_specs=[
            pl.BlockSpec((None, rows, d), lambda bi: (bi, 0, 0)),
            pl.BlockSpec((None, kw, d), lambda bi: (layer, 0, 0)),
            vec(), vec(), vec(),
        ],
        out_specs=pl.BlockSpec((None, tt, d), lambda bi: (bi, 0, 0)),
        scratch_shapes=[pltpu.VMEM(s, dt) for s, dt in scratch],
        compiler_params=pltpu.CompilerParams(
            dimension_semantics=("parallel",), vmem_limit_bytes=_vmem_limit(blocks, scratch)),
        name="dwconv_ln_swish_sample",
    )(up, w_dw, b_dw.reshape(-1, 1, d), ln_g.reshape(-1, 1, d), ln_b.reshape(-1, 1, d))


def _bucket_codes(n_q, n_k, k_off, n_buckets):
    rel = (np.arange(n_k) + k_off)[None, :] - np.arange(n_q)[:, None]
    nb = n_buckets // 2
    max_exact = nb // 2
    ret = np.where(rel > 0, nb, 0)
    n = np.abs(rel)
    nf = np.maximum(n, 1).astype(np.float32)
    large = max_exact + (np.log(nf / np.float32(max_exact))
                         / np.float32(math.log(MAX_DISTANCE / max_exact))
                         * np.float32(nb - max_exact)).astype(np.int32)
    large = np.minimum(large, nb - 1)
    return (ret + np.where(n < max_exact, n, large)).astype(np.int32)


def _bias_body(code_ref, table_ref, o_ref, *, present, group):
    n_cols = code_ref.shape[1]

    used = sorted({b for _, buckets in present for b in buckets})

    def head(hh, carry):
        value = {b: table_ref[hh, b] * LOG2E for b in used}
        for r0, buckets in present:
            rows = slice(r0, r0 + SUBLANES)
            code = code_ref[rows, :]
            out = jnp.full(code.shape, -jnp.inf, F32)
            for b in buckets:
                out = jnp.where(code == b, value[b], out)
            if group is None:
                o_ref[hh, rows, :] = out
            else:
                cols = pl.ds(pl.multiple_of((hh % group) * n_cols, LANES), n_cols)
                o_ref[hh // group, rows, cols] = out
        return carry

    lax.fori_loop(0, table_ref.shape[0], head, 0)


def _bias_call(code, table, group=None):
    n_heads, n_buckets = table.shape
    rows, cols = code.shape
    shape = (n_heads, rows, cols) if group is None else (n_heads // group, rows, group * cols)
    present = tuple(
        (r0, tuple(int(b) for b in np.unique(code[r0:r0 + SUBLANES]) if b < n_buckets))
        for r0 in range(0, rows, SUBLANES))
    return pl.pallas_call(
        functools.partial(_bias_body, present=present, group=group),
        out_shape=jax.ShapeDtypeStruct(shape, F32),
        in_specs=[pl.BlockSpec(memory_space=pltpu.VMEM), pl.BlockSpec(memory_space=pltpu.SMEM)],
        out_specs=pl.BlockSpec(memory_space=pltpu.VMEM),
        name="rel_bias",
    )(jnp.asarray(code), table)


def _unit_scores(qs, kh, bias):
    return lax.dot_general(qs, kh, (((1,), (1,)), ((), ())), preferred_element_type=F32) + bias


def _unit_output(s, vh, sink_col):
    sink_col = sink_col * LOG2E
    m = jnp.maximum(jnp.max(s, axis=-1, keepdims=True), sink_col)
    e = jnp.exp2(s - m)
    l = jnp.sum(e, axis=-1, keepdims=True) + jnp.exp2(sink_col - m)
    return jnp.dot(e.astype(BF16), vh, preferred_element_type=F32) * (1.0 / l)


def _stack_heads(q, h, group):
    hd = HEAD_DIM
    return jnp.concatenate(
        [q[:, (h * group + g) * hd:(h * group + g + 1) * hd] for g in range(group)], axis=0)


def _unstack_heads(o, group):
    tq = o.shape[0] // group
    return [o[g * tq:(g + 1) * tq, :] for g in range(group)]


ATTN_Q_COLS = 512
PAIR = 2 * CHUNK


def _pair_codes(n_buckets):
    code = _bucket_codes(PAIR, WINDOW + PAIR, -WINDOW, n_buckets).T
    key_chunk = np.arange(WINDOW + PAIR)[:, None] // CHUNK
    q_chunk = np.arange(PAIR)[None, :] // CHUNK
    visible = (key_chunk >= q_chunk) & (key_chunk <= q_chunk + WINDOW // CHUNK)
    return np.where(visible, code, n_buckets).astype(np.int32)


def _attn_prompt_body(qT_ref, kp_ref, kc_ref, vTp_ref, vTc_ref, bias_ref, sink_ref, oT_ref,
                      k_scr, vT_scr, *, kvh, group):
    i = pl.program_id(1)
    tq = qT_ref.shape[1]
    nk = WINDOW + PAIR
    hd = HEAD_DIM
    for h in range(kvh):
        k_scr[h, 0:WINDOW, :] = kp_ref[:, h * hd:(h + 1) * hd].astype(BF16)
        k_scr[h, WINDOW:WINDOW + tq, :] = kc_ref[:, h * hd:(h + 1) * hd].astype(BF16)
    vT_scr[:, 0:WINDOW] = vTp_ref[...].astype(BF16)
    vT_scr[:, WINDOW:WINDOW + tq] = vTc_ref[...].astype(BF16)
    row = lax.broadcasted_iota(jnp.int32, (nk, group * PAIR), 0)
    start_mask = jnp.where(row < jnp.where(i == 0, WINDOW, 0), -jnp.inf, 0.0).astype(F32)
    units = [(p * PAIR, h) for p in range(tq // PAIR) for h in range(kvh)]

    def scores(c0, h):
        qsT = jnp.concatenate(
            [qT_ref[hh * hd:(hh + 1) * hd, c0:c0 + PAIR] for hh in range(h * group, (h + 1) * group)],
            axis=1)
        s = jnp.dot(k_scr[h, c0:c0 + nk, :], qsT, preferred_element_type=F32) + bias_ref[h]
        return s + start_mask if c0 == 0 else s

    s_next = scores(*units[0])
    for n, (c0, h) in enumerate(units):
        s = s_next
        if n + 1 < len(units):
            s_next = scores(*units[n + 1])
        sink = sink_ref[h] * LOG2E
        m = jnp.maximum(jnp.max(s, axis=0, keepdims=True), sink)
        e = jnp.exp2(s - m)
        l = jnp.sum(e, axis=0, keepdims=True) + jnp.exp2(sink - m)
        oT = jnp.dot(vT_scr[h * hd:(h + 1) * hd, c0:c0 + nk], e.astype(BF16),
                     preferred_element_type=F32) * (1.0 / l)
        for g in range(group):
            hh = h * group + g
            oT_ref[hh * hd:(hh + 1) * hd, c0:c0 + PAIR] = (
                oT[:, g * PAIR:(g + 1) * PAIR].astype(oT_ref.dtype))


def _attn_prompt_call(qT, k, vT, bias, sink_row, kvh, group):
    b, dq, s = qT.shape
    dk = k.shape[2]
    tq = _tile(s, ATTN_Q_COLS, PAIR)
    per = tq // WINDOW
    prev = lambda i: jnp.maximum(i * per - 1, 0)
    blocks = [((dq, tq), BF16)] * 2 + [((WINDOW + tq, dk), F32)] * 2 + [
        (bias.shape, F32), (sink_row.shape[:1] + (SUBLANES, sink_row.shape[2]), F32)]
    scratch = [((kvh, WINDOW + tq, LANES), BF16), ((dk, WINDOW + tq), BF16)]
    return pl.pallas_call(
        functools.partial(_attn_prompt_body, kvh=kvh, group=group),
        out_shape=jax.ShapeDtypeStruct((b, dq, s), BF16),
        grid=(b, s // tq),
        in_specs=[
            pl.BlockSpec((None, dq, tq), lambda bi, i: (bi, 0, i)),
            pl.BlockSpec((None, WINDOW, dk), lambda bi, i: (bi, prev(i), 0)),
            pl.BlockSpec((None, tq, dk), lambda bi, i: (bi, i, 0)),
            pl.BlockSpec((None, dk, WINDOW), lambda bi, i: (bi, 0, prev(i))),
            pl.BlockSpec((None, dk, tq), lambda bi, i: (bi, 0, i)),
            pl.BlockSpec(bias.shape, lambda bi, i: (0, 0, 0)),
            pl.BlockSpec(sink_row.shape, lambda bi, i: (0, 0, 0)),
        ],
        out_specs=pl.BlockSpec((None, dq, tq), lambda bi, i: (bi, 0, i)),
        scratch_shapes=[pltpu.VMEM((kvh, WINDOW + tq, HEAD_DIM), BF16),
                        pltpu.VMEM((dk, WINDOW + tq), BF16)],
        compiler_params=pltpu.CompilerParams(
            dimension_semantics=("parallel", "arbitrary"),
            vmem_limit_bytes=_vmem_limit(blocks, scratch)),
        name="swa_sink_attention_prompt",
    )(qT, k, k, vT, vT, bias, sink_row)


ATTN_SAMPLE_BATCH = 4


def _attn_sample_body(q_ref, k_ref, v_ref, bias_ref, sink_ref, o_ref, *, kvh, group):
    hd = HEAD_DIM
    units = [(b, h) for b in range(q_ref.shape[0]) for h in range(kvh)]

    def scores(b, h):
        qs = _stack_heads(q_ref[b].astype(F32), h, group).astype(BF16)
        return _unit_scores(qs, k_ref[b, :, h * hd:(h + 1) * hd].astype(BF16), bias_ref[h])

    s_next = scores(*units[0])
    outs = []
    for n, (b, h) in enumerate(units):
        s = s_next
        if n + 1 < len(units):
            s_next = scores(*units[n + 1])
        o = _unit_output(s, v_ref[b, :, h * hd:(h + 1) * hd].astype(BF16), sink_ref[h])
        outs += _unstack_heads(o, group)
        if h == kvh - 1:
            o_ref[b] = jnp.concatenate(outs, axis=1).astype(o_ref.dtype)
            outs = []


def _attn_sample_call(q, k_all, v_all, bias, sink_col, kvh, group):
    b, t, dq = q.shape
    nk, dk = k_all.shape[1:]
    bb = math.gcd(b, ATTN_SAMPLE_BATCH)
    blocks = [((bb, t, dq), BF16)] * 2 + [((bb, nk, dk), F32)] * 2 + [
        (bias.shape[:2] + (2 * LANES,), F32), (sink_col.shape[:2] + (LANES,), F32)]
    return pl.pallas_call(
        functools.partial(_attn_sample_body, kvh=kvh, group=group),
        out_shape=jax.ShapeDtypeStruct((b, t, dq), BF16),
        grid=(b // bb,),
        in_specs=[
            pl.BlockSpec((bb, t, dq), lambda bi: (bi, 0, 0)),
            pl.BlockSpec((bb, nk, dk), lambda bi: (bi, 0, 0)),
            pl.BlockSpec((bb, nk, dk), lambda bi: (bi, 0, 0)),
            pl.BlockSpec(bias.shape, lambda bi: (0, 0, 0)),
            pl.BlockSpec(sink_col.shape, lambda bi: (0, 0, 0)),
        ],
        out_specs=pl.BlockSpec((bb, t, dq), lambda bi: (bi, 0, 0)),
        compiler_params=pltpu.CompilerParams(
            dimension_semantics=("parallel",), vmem_limit_bytes=_vmem_limit(blocks)),
        name="swa_sink_attention_sample",
    )(q, k_all, v_all, bias, sink_col)


def _trunks(xp3, xs3, mods_p, mods_s, state_conv, win_k, win_v, p):
    bp, tp, d = xp3.shape
    bs, ts, _ = xs3.shape
    mp, ms = bp * tp, bs * ts
    depth = p["w_mod"].shape[0]
    d_ff = p["w_gu"].shape[2] // 2
    n_heads = p["attn_sinks"].shape[1]
    dq = n_heads * HEAD_DIM
    kvh = (p["w_qkv"].shape[2] - dq) // (2 * HEAD_DIM)
    group = n_heads // kvh
    dkv = kvh * HEAD_DIM
    kw = p["w_dw"].shape[1]
    n_buckets = p["rel_bias_table"].shape[1]
    conv_p, k_p, v_p, conv_s, k_s, v_s = [], [], [], [], [], []

    xp, xs = xp3.reshape(mp, d), xs3.reshape(ms, d)
    hp = _norm_mod_call(xp, p["norm_mix"][0], mods_p[0], 0, 1, PANEL_ROWS)
    hs = _norm_mod_call(xs, p["norm_mix"][0], mods_s[0], 0, 1, ms)
    for i in range(depth):
        j = i // 2
        ffn_norm = lambda mods: (p["norm_ffn"][i], mods[i], 3, 4)
        if i % 2 == 0:
            u_p, u_s, wq = _mm_call(hp, p["w_pw1"], j, (0, d), d, x2=hs, cast=(p["w_pw2"], j),
                                    bias=p["b_pw1"], act="glu", name="pw1_glu")
            u_p, u_s = u_p.reshape(bp, tp, d), u_s.reshape(bs, ts, d)
            conv_w = (p["w_dw"], p["b_dw"], p["conv_ln_g"], p["conv_ln_b"], j)
            z_p = _conv_prompt_call(u_p, *conv_w)
            conv_p.append(u_p[:, tp - (kw - 1):])
            pad = (-(kw - 1 + ts)) % SUBLANES
            up = jnp.concatenate([jnp.zeros((bs, pad, d), F32), state_conv[j], u_s], axis=1)
            z_s = _conv_sample_call(up, pad, ts, *conv_w)
            conv_s.append(up[:, pad + ts:])
            xp, hp = _rowmm_call(z_p.reshape(mp, d), wq, xp, mods_p[i], 2, ffn_norm(mods_p),
                                 bias=p["b_pw2"][j], name="pw2_residual_norm")
            xs, hs = _rowmm_call(z_s.reshape(ms, d), wq, xs, mods_s[i], 2, ffn_norm(mods_s),
                                 bias=p["b_pw2"][j], name="pw2_residual_norm")
        else:
            sinks = p["attn_sinks"][j].reshape(kvh, group, 1)
            qT, q_s, wq = _mm_call(hp, p["w_qkv"], j, (0,), dq, x2=hs, cast=(p["w_o"], j),
                                   bias=p["b_qkv"], scale=LOG2E * HEAD_DIM ** -0.5,
                                   out_dtype=BF16, tn=Q_PANEL_COLS, name="q_proj",
                                   out_mode="t", seq=tp)
            (k, vT), kv_s = _mm_call(hp, p["w_qkv"], j, (dq,), 2 * dkv, x2=hs, bias=p["b_qkv"],
                                     tn=2 * dkv, name="kv_proj", out_mode="split_t", seq=tp)
            k = k.reshape(bp, tp, dkv)
            bias = _bias_call(_pair_codes(n_buckets), p["rel_bias_table"], group)
            sink_row = jnp.repeat(sinks, PAIR, axis=2).reshape(kvh, 1, group * PAIR)
            oT = _attn_prompt_call(qT, k, vT, bias, sink_row, kvh, group)
            k_p.append(k[:, tp - WINDOW:].reshape(bp, WINDOW, kvh, HEAD_DIM))
            v_p.append(jnp.swapaxes(vT[:, :, tp - WINDOW:], 1, 2)
                       .reshape(bp, WINDOW, kvh, HEAD_DIM))
            kv3 = kv_s.reshape(bs, ts, 2, kvh, HEAD_DIM)
            k_all = jnp.concatenate([win_k[j], kv3[:, :, 0]], axis=1)
            v_all = jnp.concatenate([win_v[j], kv3[:, :, 1]], axis=1)
            n_keys = WINDOW + ts
            bias = _bias_call(_bucket_codes(ts, n_keys, -WINDOW, n_buckets),
                              p["rel_bias_table"]).reshape(kvh, group * ts, n_keys)
            o_s = _attn_sample_call(q_s.reshape(bs, ts, dq), k_all.reshape(bs, n_keys, dkv),
                                    v_all.reshape(bs, n_keys, dkv), bias,
                                    jnp.repeat(sinks, ts, axis=1), kvh, group)
            k_s.append(k_all[:, ts:])
            v_s.append(v_all[:, ts:])
            xp, hp = _rowmm_call(oT, wq, xp, mods_p[i], 2, ffn_norm(mods_p), bias=p["b_o"][j],
                                 x_t=True, name="wo_residual_norm")
            xs, hs = _rowmm_call(o_s.reshape(ms, dq), wq, xs, mods_s[i], 2, ffn_norm(mods_s),
                                 bias=p["b_o"][j], name="wo_residual_norm")
        a_p, a_s, wq = _mm_call(hp, p["w_gu"], i, (0, d_ff), d_ff, x2=hs, cast=(p["w_down"], i),
                                act="swiglu", out_dtype=BF16, name="ffn_gate_up")
        if i + 1 < depth:
            nxt = lambda mods: (p["norm_mix"][i + 1], mods[i + 1], 0, 1)
            xp, hp = _rowmm_call(a_p, wq, xp, mods_p[i], 5, nxt(mods_p),
                                 name="ffn_down_residual_norm")
            xs, hs = _rowmm_call(a_s, wq, xs, mods_s[i], 5, nxt(mods_s),
                                 name="ffn_down_residual_norm")
        else:
            y_p = _rowmm_call(a_p, wq, xp, mods_p[i], 5, (p["norm_out"],),
                              name="ffn_down_residual_final_norm")
            y_s = _rowmm_call(a_s, wq, xs, mods_s[i], 5, (p["norm_out"],),
                              name="ffn_down_residual_final_norm")
    stack = jnp.stack
    return (y_p.reshape(bp, tp, d), y_s.reshape(bs, ts, d), stack(conv_p), stack(k_p), stack(v_p),
            stack(conv_s), stack(k_s), stack(v_s))


def kernel(x_prompt, x_sample, c_prompt, c_sample, state_conv, cache_win_k, cache_win_v, w_mod, b_mod, norm_mix, norm_ffn, w_pw1, b_pw1, w_dw, b_dw, conv_ln_g, conv_ln_b, w_pw2, b_pw2, w_qkv, b_qkv, w_o, b_o, attn_sinks, rel_bias_table, w_gu, w_down, norm_out):
    p = dict(w_mod=w_mod, b_mod=b_mod, norm_mix=norm_mix, norm_ffn=norm_ffn, w_pw1=w_pw1,
             b_pw1=b_pw1, w_dw=w_dw, b_dw=b_dw, conv_ln_g=conv_ln_g, conv_ln_b=conv_ln_b,
             w_pw2=w_pw2, b_pw2=b_pw2, w_qkv=w_qkv, b_qkv=b_qkv, w_o=w_o, b_o=b_o,
             attn_sinks=attn_sinks, rel_bias_table=rel_bias_table, w_gu=w_gu, w_down=w_down,
             norm_out=norm_out)
    bp, sp, d = x_prompt.shape
    bs, ts, _ = x_sample.shape
    depth = w_mod.shape[0]

    n_c = bp + bs
    c_all = jnp.concatenate(
        [c_prompt, c_sample, jnp.zeros((-n_c % BF16_SUBLANES, d), F32)], axis=0)
    mod_all = _mod_call(c_all, w_mod, b_mod)

    mods_p = [_Mod(mod_all[l, :bp].reshape(bp, 1, 6 * d), sp) for l in range(depth)]
    mods_s = [_Mod(mod_all[l, bp:n_c].reshape(bs, 1, 6 * d), ts) for l in range(depth)]

    return _trunks(x_prompt, x_sample, mods_p, mods_s, state_conv, cache_win_k, cache_win_v, p)
```
